```python
import jax, jax.numpy as jnp
from jax import lax
import numpy as np

D_MODEL = 2048
BATCH = 8
SEQ = 4096
DEPTH = 1

N_META = 16
ATTN_HEADS = 8
HEAD_DIM = 128
ATTN_WIDTH = ATTN_HEADS * HEAD_DIM
CONV_GROUPS = 8
CONV_WIDTH = 1024
CONV_K = 3
N_BRANCH = 2
D_FF = 4 * D_MODEL
BLOCK_Q = 128
EPS = 1e-6
FGATE_BIAS = 3.0
COL_SIZES = (ATTN_WIDTH, ATTN_WIDTH, ATTN_WIDTH, ATTN_HEADS,
             CONV_WIDTH, CONV_WIDTH, CONV_WIDTH, N_BRANCH * D_MODEL)
IN_COLS = 3 * ATTN_WIDTH + ATTN_HEADS + 3 * CONV_WIDTH + N_BRANCH * D_MODEL

kernel_name = "fox_shortconv_gated_hybrid_block"


def rms_norm(x, g):
    xf = x.astype(jnp.float32)
    y = xf * lax.rsqrt(jnp.mean(xf * xf, axis=-1, keepdims=True) + EPS)
    return (y * g.astype(jnp.float32)).astype(x.dtype)


def split_offsets():
    offs, acc = [], 0
    for s in COL_SIZES[:-1]:
        acc += s
        offs.append(acc)
    return offs


def fox_block(qb, cq, qpos, k, v, ck, kpos):
    scale = HEAD_DIM ** -0.5
    s = jnp.einsum('bqhd,bkhd->bhqk', qb, k).astype(jnp.float32) * scale
    s = s + cq.transpose(0, 2, 1)[:, :, :, None] - ck.transpose(0, 2, 1)[:, :, None, :]
    mask = kpos[None, :] <= qpos[:, None]
    s = jnp.where(mask[None, None], s, -jnp.inf)
    p = jax.nn.softmax(s, axis=-1)
    return jnp.einsum('bhqk,bkhd->bqhd', p.astype(v.dtype), v)


def forgetting_attention(q, k, v, log_f):
    B, L, H, Dh = q.shape
    cum = jnp.cumsum(log_f, axis=1)
    pos = jnp.arange(L, dtype=jnp.int32)
    meta_out = fox_block(q[:, :N_META], cum[:, :N_META], pos[:N_META],
                         k[:, :N_META], v[:, :N_META], cum[:, :N_META], pos[:N_META])
    nb = (L - N_META) // BLOCK_Q
    qr = q[:, N_META:].reshape(B, nb, BLOCK_Q, H, Dh).transpose(1, 0, 2, 3, 4)
    cr = cum[:, N_META:].reshape(B, nb, BLOCK_Q, H).transpose(1, 0, 2, 3)
    pr = pos[N_META:].reshape(nb, BLOCK_Q)
    real = lax.map(lambda a: fox_block(a[0], a[1], a[2], k, v, cum, pos), (qr, cr, pr))
    real = real.transpose(1, 0, 2, 3, 4).reshape(B, L - N_META, H, Dh)
    return jnp.concatenate([meta_out, real], axis=1)


def short_conv(u, w):
    L = u.shape[1]
    up = jnp.pad(u, ((0, 0), (CONV_K - 1, 0), (0, 0)))
    y = w[0] * up[:, 0:L]
    for j in range(1, CONV_K):
        y = y + w[j] * up[:, j:j + L]
    return y


def _fwd_setup_inputs(seed: int = 0) -> dict:
    key = jax.random.key(seed)
    ks = jax.random.split(key, 16)
    f32 = jnp.float32
    nrm = lambda k, shape, scale: jax.random.normal(k, shape, f32) * scale
    x = jax.random.normal(ks[0], (BATCH, SEQ, D_MODEL), f32)
    meta_tokens = nrm(ks[1], (N_META, D_MODEL), 1.0)
    norm_mix = 1.0 + nrm(ks[2], (DEPTH, D_MODEL), 0.02)
    w_in = nrm(ks[3], (DEPTH, D_MODEL, IN_COLS), D_MODEL ** -0.5)
    b_fgate = FGATE_BIAS + nrm(ks[4], (DEPTH, ATTN_HEADS), 0.1)
    b_gate = nrm(ks[5], (DEPTH, N_BRANCH * D_MODEL), 0.01)
    q_norm = 1.0 + nrm(ks[6], (DEPTH, HEAD_DIM), 0.02)
    k_norm = 1.0 + nrm(ks[7], (DEPTH, HEAD_DIM), 0.02)
    conv_w = nrm(ks[8], (DEPTH, CONV_K, CONV_WIDTH), CONV_K ** -0.5)
    w_attn_out = nrm(ks[9], (DEPTH, ATTN_WIDTH, D_MODEL), ATTN_WIDTH ** -0.5)
    w_conv_out = nrm(ks[10], (DEPTH, CONV_WIDTH, D_MODEL), CONV_WIDTH ** -0.5)
    w_o = nrm(ks[11], (DEPTH, D_MODEL, D_MODEL), D_MODEL ** -0.5)
    norm_mlp = 1.0 + nrm(ks[12], (DEPTH, D_MODEL), 0.02)
    w_up = nrm(ks[13], (DEPTH, D_MODEL, D_FF), D_MODEL ** -0.5)
    w_down = nrm(ks[14], (DEPTH, D_FF, D_MODEL), D_FF ** -0.5)
    return {"x": x, "meta_tokens": meta_tokens, "norm_mix": norm_mix, "w_in": w_in,
            "b_fgate": b_fgate, "b_gate": b_gate, "q_norm": q_norm, "k_norm": k_norm,
            "conv_w": conv_w, "w_attn_out": w_attn_out, "w_conv_out": w_conv_out,
            "w_o": w_o, "norm_mlp": norm_mlp, "w_up": w_up, "w_down": w_down}


def _fwd_reference(x, meta_tokens, norm_mix, w_in, b_fgate, b_gate, q_norm, k_norm,
              conv_w, w_attn_out, w_conv_out, w_o, norm_mlp, w_up, w_down):
    B = x.shape[0]
    meta = jnp.broadcast_to(meta_tokens[None].astype(x.dtype), (B, N_META, D_MODEL))
    h = jnp.concatenate([meta, x], axis=1)
    L = h.shape[1]
    offs = split_offsets()
    for layer in range(DEPTH):
        xn = rms_norm(h, norm_mix[layer])
        proj = jnp.einsum('bld,dc->blc', xn, w_in[layer])
        q, k, v, fg, cb, cc, cx, gl = jnp.split(proj, offs, axis=-1)
        q = rms_norm(q.reshape(B, L, ATTN_HEADS, HEAD_DIM), q_norm[layer])
        k = rms_norm(k.reshape(B, L, ATTN_HEADS, HEAD_DIM), k_norm[layer])
        v = v.reshape(B, L, ATTN_HEADS, HEAD_DIM)
        log_f = jax.nn.log_sigmoid(fg.astype(jnp.float32) + b_fgate[layer].astype(jnp.float32))
        a = forgetting_attention(q, k, v, log_f).reshape(B, L, ATTN_WIDTH)
        a = jnp.einsum('blc,cd->bld', a, w_attn_out[layer])
        c = cb * short_conv(cc * cx, conv_w[layer])
        c = jnp.einsum('blc,cd->bld', c, w_conv_out[layer])
        g = jax.nn.sigmoid(gl.astype(jnp.float32) + b_gate[layer].astype(jnp.float32))
        g = g.astype(h.dtype).reshape(B, L, N_BRANCH, D_MODEL)
        merged = g[:, :, 0] * a + g[:, :, 1] * c
        h = h + jnp.einsum('bld,de->ble', merged, w_o[layer])
        hn = rms_norm(h, norm_mlp[layer])
        u = jnp.square(jax.nn.relu(jnp.einsum('bld,df->blf', hn, w_up[layer])))
        h = h + jnp.einsum('blf,fd->bld', u, w_down[layer])
    return h[:, N_META:]


import jax as _jax
import jax.numpy as _jnp

TWIN_FORMAT = 'train_step'
FWD_PARAMS = ['x', 'meta_tokens', 'norm_mix', 'w_in', 'b_fgate', 'b_gate', 'q_norm', 'k_norm', 'conv_w', 'w_attn_out', 'w_conv_out', 'w_o', 'norm_mlp', 'w_up', 'w_down']
TWIN_WEIGHTS = ['meta_tokens', 'norm_mix', 'w_in', 'b_fgate', 'b_gate', 'q_norm', 'k_norm', 'conv_w', 'w_attn_out', 'w_conv_out', 'w_o', 'norm_mlp', 'w_up', 'w_down']
TWIN_DIFF_INPUT = 'x'
TWIN_INPUTS = ['x', 'meta_tokens', 'norm_mix', 'w_in', 'b_fgate', 'b_gate', 'q_norm', 'k_norm', 'conv_w', 'w_attn_out', 'w_conv_out', 'w_o', 'norm_mlp', 'w_up', 'w_down', 'loss_target', 'm_meta_tokens', 'm_norm_mix', 'm_w_in', 'm_b_fgate', 'm_b_gate', 'm_q_norm', 'm_k_norm', 'm_conv_w', 'm_w_attn_out', 'm_w_conv_out', 'm_w_o', 'm_norm_mlp', 'm_w_up', 'm_w_down', 'v_meta_tokens', 'v_norm_mix', 'v_w_in', 'v_b_fgate', 'v_b_gate', 'v_q_norm', 'v_k_norm', 'v_conv_w', 'v_w_attn_out', 'v_w_conv_out', 'v_w_o', 'v_norm_mlp', 'v_w_up', 'v_w_down']
TWIN_OUTPUTS = ['loss', 'grad_x', 'grad_meta_tokens', 'grad_norm_mix', 'grad_w_in', 'grad_b_fgate', 'grad_b_gate', 'grad_q_norm', 'grad_k_norm', 'grad_conv_w', 'grad_w_attn_out', 'grad_w_conv_out', 'grad_w_o', 'grad_norm_mlp', 'grad_w_up', 'grad_w_down', 'delta_meta_tokens', 'delta_norm_mix', 'delta_w_in', 'delta_b_fgate', 'delta_b_gate', 'delta_q_norm', 'delta_k_norm', 'delta_conv_w', 'delta_w_attn_out', 'delta_w_conv_out', 'delta_w_o', 'delta_norm_mlp', 'delta_w_up', 'delta_w_down', 'new_m_meta_tokens', 'new_m_norm_mix', 'new_m_w_in', 'new_m_b_fgate', 'new_m_b_gate', 'new_m_q_norm', 'new_m_k_norm', 'new_m_conv_w', 'new_m_w_attn_out', 'new_m_w_conv_out', 'new_m_w_o', 'new_m_norm_mlp', 'new_m_w_up', 'new_m_w_down', 'new_v_meta_tokens', 'new_v_norm_mix', 'new_v_w_in', 'new_v_b_fgate', 'new_v_b_gate', 'new_v_q_norm', 'new_v_k_norm', 'new_v_conv_w', 'new_v_w_attn_out', 'new_v_w_conv_out', 'new_v_w_o', 'new_v_norm_mlp', 'new_v_w_up', 'new_v_w_down']
TWIN_LEAF_KINDS = {'loss': 'loss', 'grad_x': 'grad_x', 'grad_meta_tokens': 'grad_w', 'grad_norm_mix': 'grad_w', 'grad_w_in': 'grad_w', 'grad_b_fgate': 'grad_w', 'grad_b_gate': 'grad_w', 'grad_q_norm': 'grad_w', 'grad_k_norm': 'grad_w', 'grad_conv_w': 'grad_w', 'grad_w_attn_out': 'grad_w', 'grad_w_conv_out': 'grad_w', 'grad_w_o': 'grad_w', 'grad_norm_mlp': 'grad_w', 'grad_w_up': 'grad_w', 'grad_w_down': 'grad_w', 'delta_meta_tokens': 'delta_w', 'delta_norm_mix': 'delta_w', 'delta_w_in': 'delta_w', 'delta_b_fgate': 'delta_w', 'delta_b_gate': 'delta_w', 'delta_q_norm': 'delta_w', 'delta_k_norm': 'delta_w', 'delta_conv_w': 'delta_w', 'delta_w_attn_out': 'delta_w', 'delta_w_conv_out': 'delta_w', 'delta_w_o': 'delta_w', 'delta_norm_mlp': 'delta_w', 'delta_w_up': 'delta_w', 'delta_w_down': 'delta_w', 'new_m_meta_tokens': 'new_m', 'new_m_norm_mix': 'new_m', 'new_m_w_in': 'new_m', 'new_m_b_fgate': 'new_m', 'new_m_b_gate': 'new_m', 'new_m_q_norm': 'new_m', 'new_m_k_norm': 'new_m', 'new_m_conv_w': 'new_m', 'new_m_w_attn_out': 'new_m', 'new_m_w_conv_out': 'new_m', 'new_m_w_o': 'new_m', 'new_m_norm_mlp': 'new_m', 'new_m_w_up': 'new_m', 'new_m_w_down': 'new_m', 'new_v_meta_tokens': 'new_v', 'new_v_norm_mix': 'new_v', 'new_v_w_in': 'new_v', 'new_v_b_fgate': 'new_v', 'new_v_b_gate': 'new_v', 'new_v_q_norm': 'new_v', 'new_v_k_norm': 'new_v', 'new_v_conv_w': 'new_v', 'new_v_w_attn_out': 'new_v', 'new_v_w_conv_out': 'new_v', 'new_v_w_o': 'new_v', 'new_v_norm_mlp': 'new_v', 'new_v_w_up': 'new_v', 'new_v_w_down': 'new_v'}


def _forward(args):
    return _fwd_reference(*[args[k] for k in FWD_PARAMS])


def _output_shape():
    def fwd():
        inp = _fwd_setup_inputs(0)
        return _fwd_reference(*[inp[k] for k in FWD_PARAMS])
    out = _jax.eval_shape(fwd)
    return out.shape, out.dtype

N_MICROBATCH = 1
ADAM_LR = 0.001
ADAM_B1 = 0.9
ADAM_B2 = 0.999
ADAM_EPS = 1e-08
ADAM_WD = 0.01
ADAM_STEP = 10
PER_EXAMPLE_BATCH_AXIS = {'x': 0, 'loss_target': 0}
SHARED_INPUTS = []
_WEIGHT_DTYPES = {'meta_tokens': _jnp.float32, 'norm_mix': _jnp.float32, 'w_in': _jnp.float32, 'b_fgate': _jnp.float32, 'b_gate': _jnp.float32, 'q_norm': _jnp.float32, 'k_norm': _jnp.float32, 'conv_w': _jnp.float32, 'w_attn_out': _jnp.float32, 'w_conv_out': _jnp.float32, 'w_o': _jnp.float32, 'norm_mlp': _jnp.float32, 'w_up': _jnp.float32, 'w_down': _jnp.float32}
MOMENT_SCALE = {'meta_tokens': 8.423194e-03, 'norm_mix': 1.488311e+01, 'w_in': 2.097022e-01, 'b_fgate': 3.612330e+01, 'b_gate': 1.155169e+00, 'q_norm': 3.384128e+00, 'k_norm': 3.406761e+00, 'conv_w': 5.348499e+00, 'w_attn_out': 7.184830e-02, 'w_conv_out': 3.239291e-01, 'w_o': 3.367640e-01, 'norm_mlp': 4.798394e+01, 'w_up': 3.284868e-01, 'w_down': 3.985707e+00}


def _to_microbatches(a, axis):
    t = _jnp.moveaxis(a, axis, 0)
    t = t.reshape((N_MICROBATCH, t.shape[0] // N_MICROBATCH) + t.shape[1:])
    return _jnp.moveaxis(t, 1, axis + 1)


def setup_inputs(seed: int = 0) -> dict:
    inp = _fwd_setup_inputs(seed)
    key = _jax.random.fold_in(_jax.random.key(seed), 7919)
    shape, _ = _output_shape()
    out = dict(inp)
    out["loss_target"] = _jax.random.normal(_jax.random.fold_in(key, 0), shape, _jnp.float32)
    for i, name in enumerate(TWIN_WEIGHTS):
        w = inp[name].astype(_jnp.float32)
        if MOMENT_SCALE is None:
            s = _jnp.sqrt(_jnp.mean(_jnp.square(w)) + 1e-30)
        else:
            s = MOMENT_SCALE[name]
        km, kv = _jax.random.split(_jax.random.fold_in(key, i + 1))
        out[name] = w
        out["m_" + name] = s * _jax.random.normal(km, w.shape, _jnp.float32)
        out["v_" + name] = (s * s) * _jax.random.uniform(kv, w.shape, _jnp.float32, 0.5, 1.5)
    if N_MICROBATCH > 1:
        for name, axis in PER_EXAMPLE_BATCH_AXIS.items():
            out[name] = _to_microbatches(out[name], axis)
    return {'x': out['x'], 'meta_tokens': out['meta_tokens'], 'norm_mix': out['norm_mix'], 'w_in': out['w_in'], 'b_fgate': out['b_fgate'], 'b_gate': out['b_gate'], 'q_norm': out['q_norm'], 'k_norm': out['k_norm'], 'conv_w': out['conv_w'], 'w_attn_out': out['w_attn_out'], 'w_conv_out': out['w_conv_out'], 'w_o': out['w_o'], 'norm_mlp': out['norm_mlp'], 'w_up': out['w_up'], 'w_down': out['w_down'], 'loss_target': out['loss_target'], 'm_meta_tokens': out['m_meta_tokens'], 'm_norm_mix': out['m_norm_mix'], 'm_w_in': out['m_w_in'], 'm_b_fgate': out['m_b_fgate'], 'm_b_gate': out['m_b_gate'], 'm_q_norm': out['m_q_norm'], 'm_k_norm': out['m_k_norm'], 'm_conv_w': out['m_conv_w'], 'm_w_attn_out': out['m_w_attn_out'], 'm_w_conv_out': out['m_w_conv_out'], 'm_w_o': out['m_w_o'], 'm_norm_mlp': out['m_norm_mlp'], 'm_w_up': out['m_w_up'], 'm_w_down': out['m_w_down'], 'v_meta_tokens': out['v_meta_tokens'], 'v_norm_mix': out['v_norm_mix'], 'v_w_in': out['v_w_in'], 'v_b_fgate': out['v_b_fgate'], 'v_b_gate': out['v_b_gate'], 'v_q_norm': out['v_q_norm'], 'v_k_norm': out['v_k_norm'], 'v_conv_w': out['v_conv_w'], 'v_w_attn_out': out['v_w_attn_out'], 'v_w_conv_out': out['v_w_conv_out'], 'v_w_o': out['v_w_o'], 'v_norm_mlp': out['v_norm_mlp'], 'v_w_up': out['v_w_up'], 'v_w_down': out['v_w_down']}


def _loss(weights, diff, rest, loss_target):
    with _jax.named_scope("forward"):
        args = {**rest, TWIN_DIFF_INPUT: diff, **{k: w.astype(_WEIGHT_DTYPES[k]) for k, w in weights.items()}}
        y = _forward(args)
    with _jax.named_scope("loss_head"):
        err = _jnp.square(y.astype(_jnp.float32) - loss_target)
        return 0.5 * _jnp.sum(_jnp.mean(err, axis=-1)) if err.ndim else 0.5 * err


def _adamw(w, g, m, v):
    m = ADAM_B1 * m + (1.0 - ADAM_B1) * g
    v = ADAM_B2 * v + (1.0 - ADAM_B2) * _jnp.square(g)
    m_hat = m / (1.0 - ADAM_B1 ** ADAM_STEP)
    v_hat = v / (1.0 - ADAM_B2 ** ADAM_STEP)
    delta = -ADAM_LR * (m_hat / (_jnp.sqrt(v_hat) + ADAM_EPS) + ADAM_WD * w)
    return delta, m, v


def reference(x, meta_tokens, norm_mix, w_in, b_fgate, b_gate, q_norm, k_norm, conv_w, w_attn_out, w_conv_out, w_o, norm_mlp, w_up, w_down, loss_target, m_meta_tokens, m_norm_mix, m_w_in, m_b_fgate, m_b_gate, m_q_norm, m_k_norm, m_conv_w, m_w_attn_out, m_w_conv_out, m_w_o, m_norm_mlp, m_w_up, m_w_down, v_meta_tokens, v_norm_mix, v_w_in, v_b_fgate, v_b_gate, v_q_norm, v_k_norm, v_conv_w, v_w_attn_out, v_w_conv_out, v_w_o, v_norm_mlp, v_w_up, v_w_down):
    given = dict(x=x, meta_tokens=meta_tokens, norm_mix=norm_mix, w_in=w_in, b_fgate=b_fgate, b_gate=b_gate, q_norm=q_norm, k_norm=k_norm, conv_w=conv_w, w_attn_out=w_attn_out, w_conv_out=w_conv_out, w_o=w_o, norm_mlp=norm_mlp, w_up=w_up, w_down=w_down, loss_target=loss_target, m_meta_tokens=m_meta_tokens, m_norm_mix=m_norm_mix, m_w_in=m_w_in, m_b_fgate=m_b_fgate, m_b_gate=m_b_gate, m_q_norm=m_q_norm, m_k_norm=m_k_norm, m_conv_w=m_conv_w, m_w_attn_out=m_w_attn_out, m_w_conv_out=m_w_conv_out, m_w_o=m_w_o, m_norm_mlp=m_norm_mlp, m_w_up=m_w_up, m_w_down=m_w_down, v_meta_tokens=v_meta_tokens, v_norm_mix=v_norm_mix, v_w_in=v_w_in, v_b_fgate=v_b_fgate, v_b_gate=v_b_gate, v_q_norm=v_q_norm, v_k_norm=v_k_norm, v_conv_w=v_conv_w, v_w_attn_out=v_w_attn_out, v_w_conv_out=v_w_conv_out, v_w_o=v_w_o, v_norm_mlp=v_norm_mlp, v_w_up=v_w_up, v_w_down=v_w_down)
    weights = {n: given[n] for n in TWIN_WEIGHTS}
    shared = {n: given[n] for n in SHARED_INPUTS}
    per_example = {n: given[n] for n in ['x']}
    grad_fn = _jax.value_and_grad(_loss, argnums=(0, 1))

    def one_microbatch(ex, loss_target):
        ex = dict(ex)
        diff = ex.pop(TWIN_DIFF_INPUT)
        return grad_fn(weights, diff, {**shared, **ex}, loss_target)

    if N_MICROBATCH == 1:
        loss, (grad_w, grad_x) = one_microbatch(per_example, given["loss_target"])
    else:
        def body(carry, xs):
            loss_sum, grad_sum = carry
            l_k, (gw_k, gx_k) = one_microbatch(xs[0], xs[1])
            with _jax.named_scope("update"):
                return (loss_sum + l_k, _jax.tree.map(_jnp.add, grad_sum, gw_k)), gx_k

        init = (_jnp.zeros((), _jnp.float32), _jax.tree.map(_jnp.zeros_like, weights))
        (loss, grad_w), grad_x = _jax.lax.scan(body, init, (per_example, given["loss_target"]))
    with _jax.named_scope("update"):
        delta_w, new_m, new_v = {}, {}, {}
        for n in TWIN_WEIGHTS:
            delta_w[n], new_m[n], new_v[n] = _adamw(weights[n], grad_w[n], given["m_" + n], given["v_" + n])
    return (loss, grad_x, *[grad_w[n] for n in TWIN_WEIGHTS], *[delta_w[n] for n in TWIN_WEIGHTS],
            *[new_m[n] for n in TWIN_WEIGHTS], *[new_v[n] for n in TWIN_WEIGHTS])
```

```python
import functools

import jax
import jax.numpy as jnp
from jax import lax
from jax.experimental import pallas as pl
from jax.experimental.pallas import tpu as pltpu

F32 = jnp.float32
BF16 = jnp.bfloat16

N_DEV = 8
N_META = 16
HEAD_DIM = 128
LANES = 128
SUBLANES = 8
EPS = 1e-6
VMEM_LIMIT = 56 * 1024 * 1024

ADAM_LR = 0.001
ADAM_B1 = 0.9
ADAM_B2 = 0.999
ADAM_EPS = 1e-08
ADAM_WD = 0.01
ADAM_STEP = 10

MESH = pl.DeviceIdType.MESH
HBM_SPEC = pl.BlockSpec(memory_space=pltpu.HBM)
RELATIONS = tuple((r >> 2 & 1, r >> 1 & 1, r & 1) for r in range(1, N_DEV))


def _params(semantics=None):
    return pltpu.CompilerParams(dimension_semantics=semantics, vmem_limit_bytes=VMEM_LIMIT)


def _tile(n, prefs):
    for p in prefs:
        if n % p == 0:
            return p
    return n


def _sds(shape, dtype):
    return jax.ShapeDtypeStruct(shape, dtype)


def _my_place():
    return lax.axis_index("x"), lax.axis_index("y"), lax.axis_index("c")


def _flat(px, py, pc):
    return 4 * px + 2 * py + pc


def _all_gather(arrays, name):
    n = len(arrays)

    def body(*refs):
        srcs, outs = refs[:n], refs[n:2 * n]
        send_sems, recv_sems, local_sems = refs[2 * n:]
        x, y, c = _my_place()
        me, sibling = (x, y, c), (x, y, 1 - c)
        chips = [(1 - x, y), (x, 1 - y), (1 - x, 1 - y)]

        def copy(a, k, block, to, src=None):
            slot = outs[a].at[_flat(*block)]
            return pltpu.make_async_remote_copy(
                src_ref=slot if src is None else src, dst_ref=slot,
                send_sem=send_sems.at[a, k], recv_sem=recv_sems.at[a, k],
                device_id=to, device_id_type=MESH)

        started = []
        for a in range(n):
            mine = pltpu.make_async_copy(srcs[a], outs[a].at[_flat(*me)], local_sems.at[a])
            mine.start()
            started.append(mine)
        sends = []
        for a in range(n):
            for j, chip in enumerate(chips):
                sends.append(copy(a, 1 + j, me, (*chip, c), src=srcs[a]))
            sends.append(copy(a, 0, me, sibling, src=srcs[a]))
        for cp in sends:
            cp.start()
        for a in range(n):
            for j, chip in enumerate(chips):
                copy(a, 1 + j, (*chip, c), me).wait_recv()
                fwd = copy(a, 4 + j, (*chip, c), sibling)
                fwd.start()
                sends.append(fwd)
        for a in range(n):
            copy(a, 0, sibling, me).wait_recv()
            for j, chip in enumerate(chips):
                copy(a, 4 + j, (*chip, 1 - c), me).wait_recv()
        for cp in sends:
            cp.wait_send()
        for mine in started:
            mine.wait()

    return pl.pallas_call(
        body, name=name,
        out_shape=[_sds((N_DEV,) + a.shape, a.dtype) for a in arrays],
        in_specs=[HBM_SPEC] * n, out_specs=[HBM_SPEC] * n,
        scratch_shapes=[pltpu.SemaphoreType.DMA((n, 7)), pltpu.SemaphoreType.DMA((n, 7)),
                        pltpu.SemaphoreType.DMA((n,))],
    )(*arrays)


def _exchange(scatter, gather, name):
    ns, n = len(scatter), len(scatter) + len(gather)

    def body(*refs):
        srcs, outs = refs[:n], refs[n:2 * n]
        send_sems, recv_sems, local_sems = refs[2 * n:]
        x, y, c = _my_place()
        me = _flat(x, y, c)

        def peer_of(rel):
            return tuple(1 - p if r else p for p, r in zip((x, y, c), rel))

        def copy(a, k, rel):
            peer = peer_of(rel)
            src = srcs[a].at[_flat(*peer)] if a < ns else srcs[a]
            return pltpu.make_async_remote_copy(
                src_ref=src, dst_ref=outs[a].at[me],
                send_sem=send_sems.at[a, k], recv_sem=recv_sems.at[a, k],
                device_id=peer, device_id_type=MESH)

        def landed(a, k, rel):
            slot = outs[a].at[_flat(*peer_of(rel))]
            return pltpu.make_async_remote_copy(
                src_ref=slot, dst_ref=slot,
                send_sem=send_sems.at[a, k], recv_sem=recv_sems.at[a, k],
                device_id=peer_of(rel), device_id_type=MESH)

        own = []
        for a in range(n):
            src = srcs[a].at[me] if a < ns else srcs[a]
            cp = pltpu.make_async_copy(src, outs[a].at[me], local_sems.at[a])
            cp.start()
            own.append(cp)
        sends = [copy(a, k, rel) for a in range(n) for k, rel in enumerate(RELATIONS)]
        for cp in sends:
            cp.start()
        for a in range(n):
            for k, rel in enumerate(RELATIONS):
                landed(a, k, rel).wait_recv()
        for cp in sends:
            cp.wait_send()
        for cp in own:
            cp.wait()

    outs = [_sds(a.shape, a.dtype) for a in scatter] + [_sds((N_DEV,) + a.shape, a.dtype) for a in gather]
    return pl.pallas_call(
        body, name=name, out_shape=outs,
        in_specs=[HBM_SPEC] * n, out_specs=[HBM_SPEC] * n,
        scratch_shapes=[pltpu.SemaphoreType.DMA((n, 7)), pltpu.SemaphoreType.DMA((n, 7)),
                        pltpu.SemaphoreType.DMA((n,))],
    )(*scatter, *gather)


def _matmul(a, b, *, name, trans_b=False, extras=(), epilogue=None, out_dtypes=(F32,),
            tm=None, tn=None, tk=None):
    m, k = a.shape
    n = b.shape[0] if trans_b else b.shape[1]
    tm = tm or _tile(m, (1408, 1024, 512, 256, 128))
    tn = tn or _tile(n, (512, 256, 128))
    tk = tk or _tile(k, (2048, 1408, 1024, 512, 256, 128))
    nk = k // tk
    n_ex, n_out = len(extras), len(out_dtypes)
    dims = (((1,), (1,)), ((), ())) if trans_b else (((1,), (0,)), ((), ()))

    def body(*refs):
        a_ref, b_ref = refs[:2]
        ex_refs = refs[2:2 + n_ex]
        out_refs = refs[2 + n_ex:2 + n_ex + n_out]
        part = lax.dot_general(a_ref[...].astype(BF16), b_ref[...].astype(BF16), dims,
                               preferred_element_type=F32)

        def finish(acc):
            if epilogue is None:
                res = (acc,)
            else:
                res = epilogue(acc, pl.program_id(0), pl.program_id(1), *[e[...] for e in ex_refs])
            for o_ref, r in zip(out_refs, res):
                o_ref[...] = r.astype(o_ref.dtype)

        if nk == 1:
            finish(part)
        else:
            acc_ref = refs[-1]
            kk = pl.program_id(2)

            @pl.when(kk == 0)
            def _():
                acc_ref[...] = part

            @pl.when(kk > 0)
            def _():
                acc_ref[...] += part

            @pl.when(kk == nk - 1)
            def _():
                finish(acc_ref[...])

    in_specs = [pl.BlockSpec((tm, tk), lambda i, j, kk: (i, kk)),
                pl.BlockSpec((tn, tk), lambda i, j, kk: (j, kk)) if trans_b
                else pl.BlockSpec((tk, tn), lambda i, j, kk: (kk, j))]
    for e in extras:
        if e.shape[0] == 1:
            in_specs.append(pl.BlockSpec((1, tn), lambda i, j, kk: (0, j)))
        else:
            in_specs.append(pl.BlockSpec((tm, tn), lambda i, j, kk: (i, j)))
    res = pl.pallas_call(
        body, name=name, grid=(m // tm, n // tn, nk),
        in_specs=in_specs,
        out_specs=[pl.BlockSpec((tm, tn), lambda i, j, kk: (i, j))] * n_out,
        out_shape=[_sds((m, n), d) for d in out_dtypes],
        scratch_shapes=[pltpu.VMEM((tm, tn), F32)] if nk > 1 else [],
        compiler_params=_params(("parallel", "parallel", "arbitrary")),
    )(a, b, *extras)
    return res[0] if n_out == 1 else res


def _rstd(x):
    return lax.rsqrt(jnp.mean(x * x, axis=-1, keepdims=True) + EPS)


def _norm_bwd(x, dy, g):
    r = _rstd(x)
    u = dy * g
    dx = r * u - x * (r * r * r) * jnp.mean(u * x, axis=-1, keepdims=True)
    return dx, dy * (x * r)


def _rmsnorm_fwd(h, g, name):
    t, d = h.shape
    tr = _tile(t, (384, 256, 128))

    def body(h_ref, g_ref, o_ref):
        x = h_ref[...]
        o_ref[...] = ((x * _rstd(x)) * g_ref[...]).astype(o_ref.dtype)

    row = pl.BlockSpec((tr, d), lambda i: (i, 0))
    return pl.pallas_call(
        body, name=name, grid=(t // tr,),
        in_specs=[row, pl.BlockSpec((1, d), lambda i: (0, 0))], out_specs=row,
        out_shape=_sds((t, d), BF16), compiler_params=_params(("parallel",)),
    )(h, g)


def _rmsnorm_bwd(h, dy, g, res, name):
    t, d = h.shape
    tr = _tile(t, (384, 256, 128))

    def body(h_ref, dy_ref, g_ref, res_ref, dh_ref, dhb_ref, dg_ref):
        dx, dg_rows = _norm_bwd(h_ref[...], dy_ref[...], g_ref[...])
        dh = res_ref[...] + dx
        dh_ref[...] = dh
        dhb_ref[...] = dh.astype(BF16)

        @pl.when(pl.program_id(0) == 0)
        def _():
            dg_ref[...] = jnp.zeros_like(dg_ref)

        dg_ref[...] += jnp.sum(dg_rows, axis=0, keepdims=True)

    row = pl.BlockSpec((tr, d), lambda i: (i, 0))
    vec = pl.BlockSpec((1, d), lambda i: (0, 0))
    return pl.pallas_call(
        body, name=name, grid=(t // tr,),
        in_specs=[row, row, vec, row], out_specs=[row, row, vec],
        out_shape=[_sds((t, d), F32), _sds((t, d), BF16), _sds((1, d), F32)],
        compiler_params=_params(("arbitrary",)),
    )(h, dy, g, res)


def _qk_prep(proj, gq, gk, aw, name):
    t = proj.shape[0]
    heads = aw // HEAD_DIM
    tr = _tile(t, (384, 256, 128))

    def body(q_ref, k_ref, v_ref, gq_ref, gk_ref, qo_ref, ko_ref, vo_ref):
        for h in range(heads):
            sl = slice(h * HEAD_DIM, (h + 1) * HEAD_DIM)
            xq, xk = q_ref[:, sl], k_ref[:, sl]
            qo_ref[:, sl] = ((xq * _rstd(xq)) * gq_ref[...]).astype(BF16)
            ko_ref[:, sl] = ((xk * _rstd(xk)) * gk_ref[...]).astype(BF16)
        vo_ref[...] = v_ref[...].astype(BF16)

    vec = pl.BlockSpec((1, HEAD_DIM), lambda i: (0, 0))
    out = pl.BlockSpec((tr, aw), lambda i: (i, 0))
    return pl.pallas_call(
        body, name=name, grid=(t // tr,),
        in_specs=[pl.BlockSpec((tr, aw), lambda i: (i, 0)), pl.BlockSpec((tr, aw), lambda i: (i, 1)),
                  pl.BlockSpec((tr, aw), lambda i: (i, 2)), vec, vec],
        out_specs=[out, out, out], out_shape=[_sds((t, aw), BF16)] * 3,
        compiler_params=_params(("parallel",)),
    )(proj, proj, proj, gq, gk)


def _qk_bwd(dqn, dkn, proj, gq, gk, aw, name):
    t = proj.shape[0]
    heads = aw // HEAD_DIM
    tr = _tile(t, (384, 256, 128))

    def body(dq_ref, dk_ref, q_ref, k_ref, gq_ref, gk_ref, dqo_ref, dko_ref, dgq_ref, dgk_ref):
        @pl.when(pl.program_id(0) == 0)
        def _():
            dgq_ref[...] = jnp.zeros_like(dgq_ref)
            dgk_ref[...] = jnp.zeros_like(dgk_ref)

        for h in range(heads):
            sl = slice(h * HEAD_DIM, (h + 1) * HEAD_DIM)
            dx, dg_rows = _norm_bwd(q_ref[:, sl], dq_ref[:, sl], gq_ref[...])
            dqo_ref[:, sl] = dx.astype(BF16)
            dgq_ref[...] += jnp.sum(dg_rows, axis=0, keepdims=True)
            dx, dg_rows = _norm_bwd(k_ref[:, sl], dk_ref[:, sl], gk_ref[...])
            dko_ref[:, sl] = dx.astype(BF16)
            dgk_ref[...] += jnp.sum(dg_rows, axis=0, keepdims=True)

    vec = pl.BlockSpec((1, HEAD_DIM), lambda i: (0, 0))
    row = pl.BlockSpec((tr, aw), lambda i: (i, 0))
    return pl.pallas_call(
        body, name=name, grid=(t // tr,),
        in_specs=[row, row, row, pl.BlockSpec((tr, aw), lambda i: (i, 1)), vec, vec],
        out_specs=[row, row, vec, vec],
        out_shape=[_sds((t, aw), BF16), _sds((t, aw), BF16), _sds((1, HEAD_DIM), F32), _sds((1, HEAD_DIM), F32)],
        compiler_params=_params(("arbitrary",)),
    )(dqn, dkn, proj, proj, gq, gk)


def _triangle(lower):
    r = lax.broadcasted_iota(jnp.int32, (LANES, LANES), 0)
    c = lax.broadcasted_iota(jnp.int32, (LANES, LANES), 1)
    return ((c <= r) if lower else (c >= r)).astype(F32)


def _forget_fwd(fg, b, name):
    t = fg.shape[0]

    def body(fg_ref, b_ref, cum_ref, carry):
        @pl.when(pl.program_id(0) == 0)
        def _():
            carry[...] = jnp.zeros_like(carry)

        z = fg_ref[...] + b_ref[...]
        log_f = jnp.minimum(z, 0.0) - jnp.log1p(jnp.exp(-jnp.abs(z)))
        cs = jnp.dot(_triangle(True), log_f, precision=lax.Precision.HIGHEST,
                     preferred_element_type=F32) + carry[0:1, :]
        cum_ref[...] = cs
        carry[...] = jnp.broadcast_to(cs[LANES - 1:LANES, :], carry.shape)

    row = pl.BlockSpec((LANES, LANES), lambda i: (i, 0))
    return pl.pallas_call(
        body, name=name, grid=(t // LANES,),
        in_specs=[row, pl.BlockSpec((1, LANES), lambda i: (0, 0))], out_specs=row,
        out_shape=_sds((t, LANES), F32), scratch_shapes=[pltpu.VMEM((SUBLANES, LANES), F32)],
        compiler_params=_params(("arbitrary",)),
    )(fg, b)


def _forget_bwd(dcum, fg, b, name):
    t = fg.shape[0]
    nt = t // LANES

    def body(dc_ref, fg_ref, b_ref, dfg_ref, db_ref, carry):
        @pl.when(pl.program_id(0) == 0)
        def _():
            carry[...] = jnp.zeros_like(carry)
            db_ref[...] = jnp.zeros_like(db_ref)

        d_log_f = jnp.dot(_triangle(False), dc_ref[...], precision=lax.Precision.HIGHEST,
                          preferred_element_type=F32) + carry[0:1, :]
        carry[...] = jnp.broadcast_to(d_log_f[0:1, :], carry.shape)
        dz = d_log_f * jax.nn.sigmoid(-(fg_ref[...] + b_ref[...]))
        dfg_ref[...] = dz.astype(BF16)
        db_ref[...] += jnp.sum(dz, axis=0, keepdims=True)

    row = pl.BlockSpec((LANES, LANES), lambda i: (nt - 1 - i, 0))
    vec = pl.BlockSpec((1, LANES), lambda i: (0, 0))
    return pl.pallas_call(
        body, name=name, grid=(nt,),
        in_specs=[row, row, vec], out_specs=[row, vec],
        out_shape=[_sds((t, LANES), BF16), _sds((1, LANES), F32)],
        scratch_shapes=[pltpu.VMEM((SUBLANES, LANES), F32)],
        compiler_params=_params(("arbitrary",)),
    )(dcum, fg, b)


def _causal(qi, kj, tq):
    rows = qi * tq + lax.broadcasted_iota(jnp.int32, (tq, tq), 0)
    cols = kj * tq + lax.broadcasted_iota(jnp.int32, (tq, tq), 1)
    return cols <= rows


NT_DIMS = (((1,), (1,)), ((), ()))
TN_DIMS = (((0,), (0,)), ((), ()))


def _attn_fwd(q, k, v, cum_col, cum_row, name):
    t, aw = q.shape
    heads = aw // HEAD_DIM
    tq = _tile(t, (384, 256, 128))
    nq = t // tq
    scale = HEAD_DIM ** -0.5

    def body(q_ref, k_ref, v_ref, cq_ref, ck_ref, o_ref, of_ref, lse_ref, m_s, l_s, acc_s, res_s):
        qi, kj = pl.program_id(1), pl.program_id(2)

        @pl.when(kj == 0)
        def _():
            m_s[...] = jnp.full_like(m_s, -jnp.inf)
            l_s[...] = jnp.zeros_like(l_s)
            acc_s[...] = jnp.zeros_like(acc_s)
            res_s[...] = jnp.zeros_like(res_s)

        @pl.when(kj <= qi)
        def _():
            s = lax.dot_general(q_ref[...], k_ref[...], NT_DIMS, preferred_element_type=F32) * scale
            s = s + cq_ref[...] - ck_ref[...]
            s = jnp.where(_causal(qi, kj, tq), s, -jnp.inf)
            m_prev = m_s[...]
            m_new = jnp.maximum(m_prev, jnp.max(s, axis=-1, keepdims=True))
            alpha = jnp.exp(m_prev - m_new)
            p = jnp.exp(s - m_new)
            l_s[...] = alpha * l_s[...] + jnp.sum(p, axis=-1, keepdims=True)
            p_hi = p.astype(BF16)
            p_lo = (p - p_hi.astype(F32)).astype(BF16)
            acc_s[...] = alpha * acc_s[...] + jnp.dot(p_hi, v_ref[...], preferred_element_type=F32)
            res_s[...] = alpha * res_s[...] + jnp.dot(p_lo, v_ref[...], preferred_element_type=F32)
            m_s[...] = m_new

        @pl.when(kj == qi)
        def _():
            o_ref[...] = acc_s[...] / l_s[...]
            of_ref[...] = (acc_s[...] + res_s[...]) / l_s[...]
            lse_ref[...] = m_s[...] + jnp.log(l_s[...])

    q_spec = pl.BlockSpec((tq, HEAD_DIM), lambda h, i, j: (i, h))
    kv_spec = pl.BlockSpec((tq, HEAD_DIM), lambda h, i, j: (jnp.minimum(i, j), h))
    col = pl.BlockSpec((None, tq, 1), lambda h, i, j: (h, i, 0))
    return pl.pallas_call(
        body, name=name, grid=(heads, nq, nq),
        in_specs=[q_spec, kv_spec, kv_spec, col,
                  pl.BlockSpec((None, 1, tq), lambda h, i, j: (h, 0, jnp.minimum(i, j)))],
        out_specs=[q_spec, q_spec, col],
        out_shape=[_sds((t, aw), F32), _sds((t, aw), F32), _sds((heads, t, 1), F32)],
        scratch_shapes=[pltpu.VMEM((tq, 1), F32), pltpu.VMEM((tq, 1), F32), pltpu.VMEM((tq, HEAD_DIM), F32),
                        pltpu.VMEM((tq, HEAD_DIM), F32)],
        compiler_params=_params(("parallel", "parallel", "arbitrary")),
    )(q, k, v, cum_col, cum_row)


def _attn_stats(do, o, cum_col, lse, name):
    t, aw = o.shape
    heads = aw // HEAD_DIM
    tr = _tile(t, (384, 256, 128))

    def body(do_ref, o_ref, cq_ref, lse_ref, delta_ref, crow_ref):
        for h in range(heads):
            sl = slice(h * HEAD_DIM, (h + 1) * HEAD_DIM)
            do_seen = do_ref[:, sl].astype(BF16).astype(F32)
            delta_ref[h] = jnp.sum(do_seen * o_ref[:, sl], axis=-1, keepdims=True)
        crow_ref[...] = cq_ref[...] - lse_ref[...]

    row = pl.BlockSpec((tr, aw), lambda i: (i, 0))
    col = pl.BlockSpec((heads, tr, 1), lambda i: (0, i, 0))
    return pl.pallas_call(
        body, name=name, grid=(t // tr,),
        in_specs=[row, row, col, col], out_specs=[col, col],
        out_shape=[_sds((heads, t, 1), F32)] * 2, compiler_params=_params(("parallel",)),
    )(do, o, cum_col, lse)


def _attn_bwd(q, k, v, do, crow, delta, cum_row, name):
    t, aw = q.shape
    heads = aw // HEAD_DIM
    tq = _tile(t, (384, 256, 128))
    nq = t // tq
    scale = HEAD_DIM ** -0.5

    def body(q_ref, k_ref, v_ref, do_ref, crow_ref, delta_ref, ck_ref,
             dq_ref, dk_ref, dv_ref, dck_ref, dk_s, dv_s, dck_s):
        kj, qi = pl.program_id(1), pl.program_id(2)

        @pl.when((kj == 0) & (qi == 0))
        def _():
            dq_ref[...] = jnp.zeros_like(dq_ref)

        @pl.when(qi == 0)
        def _():
            dk_s[...] = jnp.zeros_like(dk_s)
            dv_s[...] = jnp.zeros_like(dv_s)
            dck_s[...] = jnp.zeros_like(dck_s)

        @pl.when(qi >= kj)
        def _():
            qv, kv, dov = q_ref[...], k_ref[...], do_ref[...].astype(BF16)
            s = lax.dot_general(qv, kv, NT_DIMS, preferred_element_type=F32) * scale
            s = s + crow_ref[...] - ck_ref[...]
            p = jnp.where(_causal(qi, kj, tq), jnp.exp(s), 0.0)
            dp = lax.dot_general(dov, v_ref[...], NT_DIMS, preferred_element_type=F32)
            ds = p * (dp - delta_ref[...])
            dsb = ds.astype(BF16)
            dv_s[...] += lax.dot_general(p.astype(BF16), dov, TN_DIMS, preferred_element_type=F32)
            dk_s[...] += lax.dot_general(dsb, qv, TN_DIMS, preferred_element_type=F32)
            rows = pl.ds(pl.multiple_of(qi * tq, tq), tq)
            dq_ref[rows, :] += jnp.dot(dsb, kv, preferred_element_type=F32) * scale
            dck_s[...] += jnp.sum(ds, axis=0, keepdims=True)

        @pl.when(qi == nq - 1)
        def _():
            dk_ref[...] = dk_s[...] * scale
            dv_ref[...] = dv_s[...].astype(dv_ref.dtype)
            dck_ref[...] = -dck_s[...]

    q_spec = pl.BlockSpec((tq, HEAD_DIM), lambda h, j, i: (jnp.maximum(i, j), h))
    k_spec = pl.BlockSpec((tq, HEAD_DIM), lambda h, j, i: (j, h))
    col = pl.BlockSpec((None, tq, 1), lambda h, j, i: (h, jnp.maximum(i, j), 0))
    row = pl.BlockSpec((None, 1, tq), lambda h, j, i: (h, 0, j))
    return pl.pallas_call(
        body, name=name, grid=(heads, nq, nq),
        in_specs=[q_spec, k_spec, k_spec, q_spec, col, col, row],
        out_specs=[pl.BlockSpec((t, HEAD_DIM), lambda h, j, i: (0, h)), k_spec, k_spec, row],
        out_shape=[_sds((t, aw), F32), _sds((t, aw), F32), _sds((t, aw), BF16), _sds((heads, 1, t), F32)],
        scratch_shapes=[pltpu.VMEM((tq, HEAD_DIM), F32), pltpu.VMEM((tq, HEAD_DIM), F32), pltpu.VMEM((1, tq), F32)],
        compiler_params=_params(("parallel", "arbitrary", "arbitrary")),
    )(q, k, v, do, crow, delta, cum_row)


def _shift_down(u, by):
    rows = lax.broadcasted_iota(jnp.int32, u.shape, 0)
    return jnp.where(rows >= by, pltpu.roll(u, by, 0), 0.0)


def _shift_up(u, by):
    t = u.shape[0]
    rows = lax.broadcasted_iota(jnp.int32, u.shape, 0)
    return jnp.where(rows < t - by, pltpu.roll(u, t - by, 0), 0.0)


def _conv_specs(t, off_b, cw_width):
    nb = cw_width // LANES
    base = off_b // LANES
    return [pl.BlockSpec((t, LANES), lambda j, s=s: (0, base + s * nb + j)) for s in range(3)]


def _conv_fwd(proj, cw, off_b, name):
    t = proj.shape[0]
    width = cw.shape[1]

    def body(cb_ref, cc_ref, cx_ref, w_ref, o_ref):
        u = cc_ref[...] * cx_ref[...]
        y = w_ref[0:1, :] * _shift_down(u, 2) + w_ref[1:2, :] * _shift_down(u, 1) + w_ref[2:3, :] * u
        o_ref[...] = (cb_ref[...] * y).astype(BF16)

    return pl.pallas_call(
        body, name=name, grid=(width // LANES,),
        in_specs=_conv_specs(t, off_b, width) + [pl.BlockSpec((SUBLANES, LANES), lambda j: (0, j))],
        out_specs=pl.BlockSpec((t, LANES), lambda j: (0, j)),
        out_shape=_sds((t, width), BF16), compiler_params=_params(("parallel",)),
    )(proj, proj, proj, cw)


def _conv_bwd(dcp, proj, cw, off_b, name):
    t = proj.shape[0]
    width = cw.shape[1]

    def body(d_ref, cb_ref, cc_ref, cx_ref, w_ref, dcb_ref, dcc_ref, dcx_ref, dw_ref):
        cc, cx = cc_ref[...], cx_ref[...]
        u = cc * cx
        u1, u2 = _shift_down(u, 1), _shift_down(u, 2)
        w0, w1, w2 = w_ref[0:1, :], w_ref[1:2, :], w_ref[2:3, :]
        d = d_ref[...]
        dcb_ref[...] = (d * (w0 * u2 + w1 * u1 + w2 * u)).astype(BF16)
        dy = d * cb_ref[...]
        du = w2 * dy + w1 * _shift_up(dy, 1) + w0 * _shift_up(dy, 2)
        dcc_ref[...] = (du * cx).astype(BF16)
        dcx_ref[...] = (du * cc).astype(BF16)
        dw = [jnp.sum(dy * s, axis=0, keepdims=True) for s in (u2, u1, u)]
        dw_ref[...] = jnp.concatenate(dw + [jnp.zeros((SUBLANES - 3, LANES), F32)], axis=0)

    col = pl.BlockSpec((t, LANES), lambda j: (0, j))
    wspec = pl.BlockSpec((SUBLANES, LANES), lambda j: (0, j))
    return pl.pallas_call(
        body, name=name, grid=(width // LANES,),
        in_specs=[col] + _conv_specs(t, off_b, width) + [wspec],
        out_specs=[col, col, col, wspec],
        out_shape=[_sds((t, width), BF16)] * 3 + [_sds((SUBLANES, width), F32)],
        compiler_params=_params(("parallel",)),
    )(dcp, proj, proj, proj, cw)


def _gate_specs(t, d, off_g, tr, tc, rows_first):
    nb = d // tc
    base = off_g // tc
    if rows_first:
        tile = lambda s: pl.BlockSpec((tr, tc), lambda i, j: (i, base + s * nb + j))
        vec = lambda s: pl.BlockSpec((1, tc), lambda i, j: (0, s * nb + j))
        plain = pl.BlockSpec((tr, tc), lambda i, j: (i, j))
    else:
        tile = lambda s: pl.BlockSpec((tr, tc), lambda j, i: (i, base + s * nb + j))
        vec = lambda s: pl.BlockSpec((1, tc), lambda j, i: (0, s * nb + j))
        plain = pl.BlockSpec((tr, tc), lambda j, i: (i, j))
    return tile, vec, plain


def _gate_fwd(a, c, proj, bg, off_g, name):
    t, d = a.shape
    tr, tc = _tile(t, (384, 256, 128)), _tile(d, (512, 256, 128))
    tile, vec, plain = _gate_specs(t, d, off_g, tr, tc, True)

    def body(a_ref, c_ref, g0_ref, g1_ref, b0_ref, b1_ref, o_ref):
        g0 = jax.nn.sigmoid(g0_ref[...] + b0_ref[...])
        g1 = jax.nn.sigmoid(g1_ref[...] + b1_ref[...])
        o_ref[...] = (g0 * a_ref[...] + g1 * c_ref[...]).astype(BF16)

    return pl.pallas_call(
        body, name=name, grid=(t // tr, d // tc),
        in_specs=[plain, plain, tile(0), tile(1), vec(0), vec(1)], out_specs=plain,
        out_shape=_sds((t, d), BF16), compiler_params=_params(("parallel", "parallel")),
    )(a, c, proj, proj, bg, bg)


def _gate_bwd(dm, a, c, proj, bg, off_g, name):
    t, d = a.shape
    tr, tc = _tile(t, (384, 256, 128)), _tile(d, (512, 256, 128))
    tile, vec, plain = _gate_specs(t, d, off_g, tr, tc, False)

    def body(dm_ref, a_ref, c_ref, g0_ref, g1_ref, b0_ref, b1_ref,
             da_ref, dc_ref, dg0_ref, dg1_ref, db0_ref, db1_ref):
        @pl.when(pl.program_id(1) == 0)
        def _():
            db0_ref[...] = jnp.zeros_like(db0_ref)
            db1_ref[...] = jnp.zeros_like(db1_ref)

        dm = dm_ref[...]
        g0 = jax.nn.sigmoid(g0_ref[...] + b0_ref[...])
        g1 = jax.nn.sigmoid(g1_ref[...] + b1_ref[...])
        da_ref[...] = (dm * g0).astype(BF16)
        dc_ref[...] = (dm * g1).astype(BF16)
        dz0 = dm * a_ref[...] * (g0 * (1.0 - g0))
        dz1 = dm * c_ref[...] * (g1 * (1.0 - g1))
        dg0_ref[...] = dz0.astype(BF16)
        dg1_ref[...] = dz1.astype(BF16)
        db0_ref[...] += jnp.sum(dz0, axis=0, keepdims=True)
        db1_ref[...] += jnp.sum(dz1, axis=0, keepdims=True)

    bvec = pl.BlockSpec((1, tc), lambda j, i: (0, j))
    return pl.pallas_call(
        body, name=name, grid=(d // tc, t // tr),
        in_specs=[plain, plain, plain, tile(0), tile(1), vec(0), vec(1)],
        out_specs=[plain] * 4 + [bvec, bvec],
        out_shape=[_sds((t, d), BF16)] * 4 + [_sds((1, d), F32)] * 2,
        compiler_params=_params(("parallel", "arbitrary")),
    )(dm, a, c, proj, proj, bg, bg)


def _sum_squares(x, name):
    t, d = x.shape
    tr = _tile(t, (384, 256, 128))

    def body(x_ref, o_ref):
        @pl.when(pl.program_id(0) == 0)
        def _():
            o_ref[...] = jnp.zeros_like(o_ref)

        v = x_ref[...]
        o_ref[...] += jnp.sum(jnp.sum(v * v, axis=0, keepdims=True), axis=1, keepdims=True)

    return pl.pallas_call(
        body, name=name, grid=(t // tr,),
        in_specs=[pl.BlockSpec((tr, d), lambda i: (i, 0))],
        out_specs=pl.BlockSpec((1, LANES), lambda i: (0, 0)),
        out_shape=_sds((1, LANES), F32), compiler_params=_params(("arbitrary",)),
    )(x)


def _row_tile(r, c):
    return r if r * c <= 128 * 1024 else _tile(r, (128, 64, 32, 16))


def _sum_parts(parts, name):
    n, r, c = parts.shape
    tr = _row_tile(r, c)

    def body(p_ref, o_ref):
        acc = p_ref[0].astype(F32)
        for i in range(1, n):
            acc = acc + p_ref[i].astype(F32)
        o_ref[...] = acc

    return pl.pallas_call(
        body, name=name, grid=(r // tr,),
        in_specs=[pl.BlockSpec((n, tr, c), lambda i: (0, i, 0))],
        out_specs=pl.BlockSpec((tr, c), lambda i: (i, 0)),
        out_shape=_sds((r, c), F32), compiler_params=_params(("parallel",)),
    )(parts)


def _adamw(parts, w, m, v, name):
    n, r, c = parts.shape
    tr = _row_tile(r, c)

    def body(p_ref, w_ref, m_ref, v_ref, g_ref, d_ref, nm_ref, nv_ref):
        g = p_ref[0].astype(F32)
        for i in range(1, n):
            g = g + p_ref[i].astype(F32)
        nm = ADAM_B1 * m_ref[...] + (1.0 - ADAM_B1) * g
        nv = ADAM_B2 * v_ref[...] + (1.0 - ADAM_B2) * (g * g)
        m_hat = nm / (1.0 - ADAM_B1 ** ADAM_STEP)
        v_hat = nv / (1.0 - ADAM_B2 ** ADAM_STEP)
        g_ref[...] = g
        d_ref[...] = -ADAM_LR * (m_hat / (jnp.sqrt(v_hat) + ADAM_EPS) + ADAM_WD * w_ref[...])
        nm_ref[...] = nm
        nv_ref[...] = nv

    row = pl.BlockSpec((tr, c), lambda i: (i, 0))
    return pl.pallas_call(
        body, name=name, grid=(r // tr,),
        in_specs=[pl.BlockSpec((n, tr, c), lambda i: (0, i, 0)), row, row, row],
        out_specs=[row] * 4, out_shape=[_sds((r, c), F32)] * 4,
        compiler_params=_params(("parallel",)),
    )(parts, w, m, v)


def _pad_lanes(a, width=LANES):
    return jnp.pad(a, ((0, 0), (0, width - a.shape[1])))


def _rows_of(a):
    flat = a.reshape(-1)
    n = -(-flat.shape[0] // LANES) * LANES
    return jnp.pad(flat, (0, n - flat.shape[0])).reshape(-1, LANES)


def _columns_to_slots(full, n_rows):
    return full.reshape(n_rows, N_DEV, -1).transpose(1, 0, 2)


def _slots_to_columns(slots):
    return slots.transpose(1, 0, 2).reshape(slots.shape[1], -1)


def kernel(x, meta_tokens, norm_mix, w_in, b_fgate, b_gate, q_norm, k_norm, conv_w, w_attn_out, w_conv_out, w_o, norm_mlp, w_up, w_down, loss_target, m_meta_tokens, m_norm_mix, m_w_in, m_b_fgate, m_b_gate, m_q_norm, m_k_norm, m_conv_w, m_w_attn_out, m_w_conv_out, m_w_o, m_norm_mlp, m_w_up, m_w_down, v_meta_tokens, v_norm_mix, v_w_in, v_b_fgate, v_b_gate, v_q_norm, v_k_norm, v_conv_w, v_w_attn_out, v_w_conv_out, v_w_o, v_norm_mlp, v_w_up, v_w_down):
    seq, d = x.shape[1], x.shape[2]
    heads = b_fgate.shape[1]
    aw = heads * HEAD_DIM
    cwid = conv_w.shape[2] * N_DEV
    dff = w_up.shape[2] * N_DEV
    n_valid = N_META + seq
    t = -(-n_valid // LANES) * LANES
    me = _flat(*_my_place())
    off_cb, off_gl = 3 * aw, 3 * aw + 3 * cwid

    conv_shard = jnp.pad(conv_w[0], ((0, SUBLANES - conv_w.shape[1]), (0, 0)))
    g_in, g_ao, g_co, g_o, g_up, g_down, g_meta, g_cw = _all_gather(
        [w_in[0].astype(BF16), w_attn_out[0].astype(BF16), w_conv_out[0].astype(BF16), w_o[0].astype(BF16),
         w_up[0].astype(BF16), w_down[0].astype(BF16), meta_tokens, conv_shard], "gather_weights")
    w_in_full = _slots_to_columns(g_in)
    w_main = jnp.concatenate([w_in_full[:, :3 * aw], w_in_full[:, 3 * aw + heads:]], axis=1)
    w_fg = _pad_lanes(w_in_full[:, 3 * aw:3 * aw + heads])
    w_ao, w_co, w_up_f = _slots_to_columns(g_ao), _slots_to_columns(g_co), _slots_to_columns(g_up)
    w_o_f, w_down_f = g_o.reshape(d, d), g_down.reshape(dff, d)
    meta_full, cw_full = _slots_to_columns(g_meta), _slots_to_columns(g_cw)

    pad_rows = t - n_valid
    h0 = jnp.concatenate([meta_full, x[0], jnp.zeros((pad_rows, d), F32)], axis=0)
    target = jnp.concatenate([jnp.zeros((N_META, d), F32), loss_target[0], jnp.zeros((pad_rows, d), F32)], axis=0)
    b_f = _pad_lanes(b_fgate)

    xn = _rmsnorm_fwd(h0, norm_mix, "norm_mix_fwd")
    proj = _matmul(xn, w_main, name="in_proj")
    fg = _matmul(xn, w_fg, name="in_proj_fgate")
    qn, kn, vb = _qk_prep(proj, q_norm, k_norm, aw, "qk_norm_fwd")
    cum = _forget_fwd(fg, b_f, "forget_cumsum")
    cum_heads = cum[:, :heads].T
    cum_col, cum_row = cum_heads[:, :, None], cum_heads[:, None, :]
    o, o_fine, lse = _attn_fwd(qn, kn, vb, cum_col, cum_row, "attention_fwd")
    a = _matmul(o, w_ao, name="attn_out_proj")
    cpre = _conv_fwd(proj, cw_full, off_cb, "short_conv_fwd")
    c = _matmul(cpre, w_co, name="conv_out_proj")
    merged = _gate_fwd(a, c, proj, b_gate, off_gl, "gate_merge_fwd")
    h1 = _matmul(merged, w_o_f, name="out_proj", extras=(h0,), epilogue=lambda acc, i, j, r: (r + acc,))
    hn = _rmsnorm_fwd(h1, norm_mlp, "norm_mlp_fwd")
    z, u = _matmul(hn, w_up_f, name="mlp_up", out_dtypes=(F32, BF16),
                   epilogue=lambda acc, i, j: (acc, jnp.square(jnp.maximum(acc, 0.0))))

    tm_down = _tile(t, (1408, 1024, 512, 256, 128))

    def loss_grad(acc, i, j, h1_tile, tgt_tile):
        rows = i * tm_down + lax.broadcasted_iota(jnp.int32, acc.shape, 0)
        valid = (rows >= N_META) & (rows < n_valid)
        dy = jnp.where(valid, ((h1_tile + acc) - tgt_tile) / d, 0.0)
        return dy, dy

    dh2, dh2b = _matmul(u, w_down_f, name="mlp_down_loss", extras=(h1, target), epilogue=loss_grad,
                        out_dtypes=(F32, BF16), tm=tm_down)
    loss_part = _sum_squares(dh2, "loss_sum") * (0.5 * d)

    dz = _matmul(dh2b, w_down_f, name="mlp_down_bwd", trans_b=True, extras=(z,), out_dtypes=(BF16,),
                 epilogue=lambda acc, i, j, zt: (acc * (2.0 * jnp.maximum(zt, 0.0)),))
    dw_down = _matmul(u.T, dh2b, name="mlp_down_wgrad")
    dw_up = _matmul(hn.T, dz, name="mlp_up_wgrad")
    dhn = _matmul(dz, w_up_f, name="mlp_up_bwd", trans_b=True)
    dh1, dh1b, dg_mlp = _rmsnorm_bwd(h1, dhn, norm_mlp, dh2, "norm_mlp_bwd")
    dmerged = _matmul(dh1b, w_o_f, name="out_proj_bwd", trans_b=True)
    dw_o = _matmul(merged.T, dh1b, name="out_proj_wgrad")
    da, dc, dgl0, dgl1, dbg0, dbg1 = _gate_bwd(dmerged, a, c, proj, b_gate, off_gl, "gate_merge_bwd")
    do = _matmul(da, w_ao, name="attn_out_bwd", trans_b=True)
    dw_ao = _matmul(o.astype(BF16).T, da, name="attn_out_wgrad")
    dcp = _matmul(dc, w_co, name="conv_out_bwd", trans_b=True)
    dw_co = _matmul(cpre.T, dc, name="conv_out_wgrad")
    dcb, dcc, dcx, dcw = _conv_bwd(dcp, proj, cw_full, off_cb, "short_conv_bwd")
    delta, crow = _attn_stats(do, o_fine, cum_col, lse, "attention_stats")
    dqn, dkn, dv, dck = _attn_bwd(qn, kn, vb, do, crow, delta, cum_row, "attention_bwd")
    dq_raw, dk_raw, dg_q, dg_k = _qk_bwd(dqn, dkn, proj, q_norm, k_norm, aw, "qk_norm_bwd")
    dcum = _pad_lanes(dck.reshape(heads, t).T)
    dfg, db_f = _forget_bwd(dcum, fg, b_f, "forget_bwd")
    dproj = jnp.concatenate([dq_raw, dk_raw, dv, dcb, dcc, dcx, dgl0, dgl1], axis=1)
    xn_t = xn.T
    dw_main = _matmul(xn_t, dproj, name="in_proj_wgrad")
    dw_fg = _matmul(xn_t, dfg, name="in_proj_fgate_wgrad")
    dxn_fg = _matmul(dfg, w_fg, name="in_proj_fgate_bwd", trans_b=True)
    dxn = _matmul(dproj, w_main, name="in_proj_bwd", trans_b=True, extras=(dxn_fg,),
                  epilogue=lambda acc, i, j, r: (r + acc,))
    dh0, _, dg_mix = _rmsnorm_bwd(h0, dxn, norm_mix, dh1, "norm_mix_bwd")

    dw_in = jnp.concatenate([dw_main[:, :3 * aw], dw_fg[:, :heads], dw_main[:, 3 * aw:]], axis=1)
    small = [dg_mix, dbg0, dbg1, dg_mlp, dg_q, dg_k, db_f, loss_part, dcw, dh0[:N_META]]
    small_rows = [_rows_of(s) for s in small]
    pack = jnp.concatenate(small_rows, axis=0)
    pack = jnp.pad(pack, ((0, -pack.shape[0] % SUBLANES), (0, 0)))
    big = [_columns_to_slots(dw_in, d), _columns_to_slots(dw_ao, aw), _columns_to_slots(dw_co, cwid),
           dw_o.reshape(N_DEV, d // N_DEV, d), _columns_to_slots(dw_up, d), dw_down.reshape(N_DEV, dff // N_DEV, d)]
    *landed, pack_all = _exchange([b.astype(BF16) for b in big], [pack], "exchange_gradients")

    names = ["w_in", "w_attn_out", "w_conv_out", "w_o", "w_up", "w_down"]
    shards = {"w_in": (w_in, m_w_in, v_w_in), "w_attn_out": (w_attn_out, m_w_attn_out, v_w_attn_out),
              "w_conv_out": (w_conv_out, m_w_conv_out, v_w_conv_out), "w_o": (w_o, m_w_o, v_w_o),
              "w_up": (w_up, m_w_up, v_w_up), "w_down": (w_down, m_w_down, v_w_down)}
    out = {}
    for nm, parts in zip(names, landed):
        w_, m_, v_ = shards[nm]
        res = _adamw(parts, w_[0], m_[0], v_[0], "adamw_" + nm)
        out[nm] = [r[None] for r in res]

    total = _sum_parts(pack_all, "sum_small")
    pieces, at = [], 0
    for s, rows in zip(small, small_rows):
        n_el = 1
        for dim in s.shape:
            n_el *= dim
        pieces.append(total[at:at + rows.shape[0]].reshape(-1)[:n_el].reshape(s.shape))
        at += rows.shape[0]
    g_mix, g_bg0, g_bg1, g_mlp, g_q, g_k, g_bf, loss_row, g_cw_full, g_meta_full = pieces
    loss = loss_row[0, 0]
    cshard = conv_w.shape[2]
    g_small = {
        "norm_mix": g_mix, "b_gate": jnp.concatenate([g_bg0, g_bg1], axis=1), "norm_mlp": g_mlp,
        "q_norm": g_q, "k_norm": g_k, "b_fgate": g_bf[:, :heads],
        "conv_w": lax.dynamic_slice_in_dim(g_cw_full[:conv_w.shape[1]], me * cshard, cshard, axis=1)[None],
        "meta_tokens": lax.dynamic_slice_in_dim(g_meta_full, me * (d // N_DEV), d // N_DEV, axis=1),
    }
    small_w = {"norm_mix": (norm_mix, m_norm_mix, v_norm_mix), "b_gate": (b_gate, m_b_gate, v_b_gate),
               "norm_mlp": (norm_mlp, m_norm_mlp, v_norm_mlp), "q_norm": (q_norm, m_q_norm, v_q_norm),
               "k_norm": (k_norm, m_k_norm, v_k_norm), "b_fgate": (b_fgate, m_b_fgate, v_b_fgate),
               "conv_w": (conv_w, m_conv_w, v_conv_w), "meta_tokens": (meta_tokens, m_meta_tokens, v_meta_tokens)}
    order = list(small_w)
    packed = []
    for idx in range(4):
        cols = [g_small[nm] if idx == 0 else small_w[nm][idx - 1] for nm in order]
        rows = jnp.concatenate([_rows_of(c_) for c_ in cols], axis=0)
        packed.append(jnp.pad(rows, ((0, -rows.shape[0] % SUBLANES), (0, 0))))
    res = _adamw(packed[0][None], packed[1], packed[2], packed[3], "adamw_small")
    at = 0
    for nm in order:
        shape = small_w[nm][0].shape
        n_el = 1
        for dim in shape:
            n_el *= dim
        n_rows = -(-n_el // LANES)
        out[nm] = [r[at:at + n_rows].reshape(-1)[:n_el].reshape(shape) for r in res]
        at += n_rows

    weights = ["meta_tokens", "norm_mix", "w_in", "b_fgate", "b_gate", "q_norm", "k_norm", "conv_w",
               "w_attn_out", "w_conv_out", "w_o", "norm_mlp", "w_up", "w_down"]
    grad_x = dh0[N_META:n_valid][None]
    return (loss, grad_x, *[out[nm][0] for nm in weights], *[out[nm][1] for nm in weights],
            *[out[nm][2] for nm in weights], *[out[nm][3] for nm in weights])
```

```python
import functools

import jax
import jax.numpy as jnp
from jax import lax
from jax.experimental import pallas as pl
from jax.experimental.pallas import tpu as pltpu

F32 = jnp.float32
BF16 = jnp.bfloat16

N_DEV = 8
N_META = 16
HEAD_DIM = 128
LANES = 128
SUBLANES = 8
EPS = 1e-6
VMEM_LIMIT = 56 * 1024 * 1024

ADAM_LR = 0.001
ADAM_B1 = 0.9
ADAM_B2 = 0.999
ADAM_EPS = 1e-08
ADAM_WD = 0.01
ADAM_STEP = 10

MESH = pl.DeviceIdType.MESH
HBM_SPEC = pl.BlockSpec(memory_space=pltpu.HBM)
RELATIONS = tuple((r >> 2 & 1, r >> 1 & 1, r & 1) for r in range(1, N_DEV))


def _params(semantics=None):
    return pltpu.CompilerParams(dimension_semantics=semantics, vmem_limit_bytes=VMEM_LIMIT)


def _tile(n, prefs):
    for p in prefs:
        if n % p == 0:
            return p
    return n


def _sds(shape, dtype):
    return jax.ShapeDtypeStruct(shape, dtype)


def _my_place():
    return lax.axis_index("x"), lax.axis_index("y"), lax.axis_index("c")


def _flat(px, py, pc):
    return 4 * px + 2 * py + pc


class _Gather:
    def __init__(self, arrays):
        self.operands = list(arrays)
        self.n = len(arrays)
        self.out_shape = [_sds((N_DEV,) + a.shape, a.dtype) for a in arrays]

    def _copy(self, srcs, outs, sems, a, k, block, to, from_src=False):
        slot = outs[a].at[_flat(*block)]
        return pltpu.make_async_remote_copy(
            src_ref=srcs[a] if from_src else slot, dst_ref=slot,
            send_sem=sems[0].at[a, k], recv_sem=sems[1].at[a, k],
            device_id=to, device_id_type=MESH)

    def _places(self):
        x, y, c = _my_place()
        return (x, y, c), (x, y, 1 - c), [(1 - x, y), (x, 1 - y), (1 - x, 1 - y)], c

    def start(self, srcs, outs, sems):
        me, sibling, chips, c = self._places()
        for a in range(self.n):
            pltpu.make_async_copy(srcs[a], outs[a].at[_flat(*me)], sems[2].at[a]).start()
            for j, chip in enumerate(chips):
                self._copy(srcs, outs, sems, a, 1 + j, me, (*chip, c), from_src=True).start()
            self._copy(srcs, outs, sems, a, 0, me, sibling, from_src=True).start()

    def mid(self, srcs, outs, sems):
        me, sibling, chips, c = self._places()
        for a in range(self.n):
            for j, chip in enumerate(chips):
                self._copy(srcs, outs, sems, a, 1 + j, (*chip, c), me).wait_recv()
                self._copy(srcs, outs, sems, a, 4 + j, (*chip, c), sibling).start()

    def finish(self, srcs, outs, sems):
        me, sibling, chips, c = self._places()
        for a in range(self.n):
            self._copy(srcs, outs, sems, a, 0, sibling, me).wait_recv()
            for j, chip in enumerate(chips):
                self._copy(srcs, outs, sems, a, 4 + j, (*chip, 1 - c), me).wait_recv()
            for k in range(7):
                self._copy(srcs, outs, sems, a, k, me, sibling).wait_send()
            pltpu.make_async_copy(srcs[a], outs[a].at[_flat(*me)], sems[2].at[a]).wait()


class _Scatter:
    def __init__(self, scatter, gather=()):
        self.operands = list(scatter) + list(gather)
        self.ns, self.n = len(scatter), len(scatter) + len(gather)
        self.out_shape = ([_sds(a.shape, a.dtype) for a in scatter]
                          + [_sds((N_DEV,) + a.shape, a.dtype) for a in gather])

    def _peer(self, rel):
        return tuple(1 - p if r else p for p, r in zip(_my_place(), rel))

    def _src(self, srcs, a, place):
        return srcs[a].at[_flat(*place)] if a < self.ns else srcs[a]

    def _send(self, srcs, outs, sems, a, k, rel):
        peer = self._peer(rel)
        return pltpu.make_async_remote_copy(
            src_ref=self._src(srcs, a, peer), dst_ref=outs[a].at[_flat(*_my_place())],
            send_sem=sems[0].at[a, k], recv_sem=sems[1].at[a, k],
            device_id=peer, device_id_type=MESH)

    def _landed(self, outs, sems, a, k, rel):
        peer = self._peer(rel)
        slot = outs[a].at[_flat(*peer)]
        return pltpu.make_async_remote_copy(
            src_ref=slot, dst_ref=slot, send_sem=sems[0].at[a, k], recv_sem=sems[1].at[a, k],
            device_id=peer, device_id_type=MESH)

    def _own(self, srcs, outs, sems, a):
        me = _my_place()
        return pltpu.make_async_copy(self._src(srcs, a, me), outs[a].at[_flat(*me)], sems[2].at[a])

    def start(self, srcs, outs, sems):
        for a in range(self.n):
            self._own(srcs, outs, sems, a).start()
            for k, rel in enumerate(RELATIONS):
                self._send(srcs, outs, sems, a, k, rel).start()

    def mid(self, srcs, outs, sems):
        pass

    def finish(self, srcs, outs, sems):
        for a in range(self.n):
            for k, rel in enumerate(RELATIONS):
                self._landed(outs, sems, a, k, rel).wait_recv()
            for k, rel in enumerate(RELATIONS):
                self._send(srcs, outs, sems, a, k, rel).wait_send()
            self._own(srcs, outs, sems, a).wait()


def _job_sems(job):
    return [pltpu.SemaphoreType.DMA((job.n, 7)), pltpu.SemaphoreType.DMA((job.n, 7)),
            pltpu.SemaphoreType.DMA((job.n,))]


def _run_job(job, name):
    n = job.n

    def body(*refs):
        srcs, outs, sems = refs[:n], refs[n:2 * n], refs[2 * n:]
        job.start(srcs, outs, sems)
        job.mid(srcs, outs, sems)
        job.finish(srcs, outs, sems)

    return pl.pallas_call(
        body, name=name, out_shape=job.out_shape,
        in_specs=[HBM_SPEC] * n, out_specs=[HBM_SPEC] * n, scratch_shapes=_job_sems(job),
    )(*job.operands)


def _call(body, *, name, grid, in_specs, out_specs, out_shape, scratch_shapes, semantics,
          operands, job=None, mid_at=0.5):
    if job is None:
        res = pl.pallas_call(
            body, name=name, grid=grid, in_specs=in_specs, out_specs=out_specs, out_shape=out_shape,
            scratch_shapes=scratch_shapes, compiler_params=_params(semantics))(*operands)
        return res, []
    n_in, n_out, n_scr = len(in_specs), len(out_specs), len(scratch_shapes)
    total = 1
    for g in grid:
        total *= g
    mid_step = min(int(total * mid_at), total - 1)

    def carried(*refs):
        c_in, j_in = refs[:n_in], refs[n_in:n_in + job.n]
        o0 = n_in + job.n
        c_out, j_out = refs[o0:o0 + n_out], refs[o0 + n_out:o0 + n_out + job.n]
        s0 = o0 + n_out + job.n
        c_scr, sems = refs[s0:s0 + n_scr], refs[s0 + n_scr:]
        step = pl.program_id(0)
        for ax in range(1, len(grid)):
            step = step * grid[ax] + pl.program_id(ax)

        @pl.when(step == 0)
        def _():
            job.start(j_in, j_out, sems)

        body(*c_in, *c_out, *c_scr)

        @pl.when(step == mid_step)
        def _():
            job.mid(j_in, j_out, sems)

        @pl.when(step == total - 1)
        def _():
            job.finish(j_in, j_out, sems)

    res = pl.pallas_call(
        carried, name=name, grid=grid,
        in_specs=list(in_specs) + [HBM_SPEC] * job.n,
        out_specs=list(out_specs) + [HBM_SPEC] * job.n,
        out_shape=list(out_shape) + job.out_shape,
        scratch_shapes=list(scratch_shapes) + _job_sems(job),
        compiler_params=_params(("arbitrary",) * len(grid)),
    )(*operands, *job.operands)
    return list(res[:n_out]), list(res[n_out:])


def _matmul(a, b, *, name, trans_b=False, extras=(), epilogue=None, out_dtypes=(F32,),
            tm=None, tn=None, tk=None, rows=None, job=None, mid_at=0.5):
    m, k = a.shape
    n = b.shape[0] if trans_b else b.shape[1]
    tm = tm or _tile(m, (1408, 1024, 512, 256, 128))
    tn = tn or _tile(n, (512, 256, 128))
    tk = tk or _tile(k, (2048, 1408, 1024, 512, 256, 128))
    nk = k // tk
    row0, n_rows = rows or (0, m // tm)
    m = n_rows * tm
    n_ex, n_out = len(extras), len(out_dtypes)
    dims = (((1,), (1,)), ((), ())) if trans_b else (((1,), (0,)), ((), ()))

    def body(*refs):
        a_ref, b_ref = refs[:2]
        ex_refs = refs[2:2 + n_ex]
        out_refs = refs[2 + n_ex:2 + n_ex + n_out]
        part = lax.dot_general(a_ref[...].astype(BF16), b_ref[...].astype(BF16), dims,
                               preferred_element_type=F32)

        def finish(acc):
            if epilogue is None:
                res = (acc,)
            else:
                res = epilogue(acc, pl.program_id(0), pl.program_id(1), *[e[...] for e in ex_refs])
            for o_ref, r in zip(out_refs, res):
                o_ref[...] = r.astype(o_ref.dtype)

        if nk == 1:
            finish(part)
        else:
            acc_ref = refs[-1]
            kk = pl.program_id(2)

            @pl.when(kk == 0)
            def _():
                acc_ref[...] = part

            @pl.when(kk > 0)
            def _():
                acc_ref[...] += part

            @pl.when(kk == nk - 1)
            def _():
                finish(acc_ref[...])

    in_specs = [pl.BlockSpec((tm, tk), lambda i, j, kk: (row0 + i, kk)),
                pl.BlockSpec((tn, tk), lambda i, j, kk: (j, kk)) if trans_b
                else pl.BlockSpec((tk, tn), lambda i, j, kk: (kk, j))]
    for e in extras:
        if e.shape[0] == 1:
            in_specs.append(pl.BlockSpec((1, tn), lambda i, j, kk: (0, j)))
        else:
            in_specs.append(pl.BlockSpec((tm, tn), lambda i, j, kk: (i, j)))
    res, moved = _call(
        body, name=name, grid=(n_rows, n // tn, nk),
        in_specs=in_specs,
        out_specs=[pl.BlockSpec((tm, tn), lambda i, j, kk: (i, j))] * n_out,
        out_shape=[_sds((m, n), d) for d in out_dtypes],
        scratch_shapes=[pltpu.VMEM((tm, tn), F32)] if nk > 1 else [],
        semantics=("parallel", "parallel", "arbitrary"),
        operands=(a, b, *extras), job=job, mid_at=mid_at)
    res = res[0] if n_out == 1 else tuple(res)
    return res if job is None else (res, moved)


def _rstd(x):
    return lax.rsqrt(jnp.mean(x * x, axis=-1, keepdims=True) + EPS)


def _norm_bwd(x, dy, g):
    r = _rstd(x)
    u = dy * g
    dx = r * u - x * (r * r * r) * jnp.mean(u * x, axis=-1, keepdims=True)
    return dx, dy * (x * r)


def _rmsnorm_fwd(h, g, name):
    t, d = h.shape
    tr = _tile(t, (384, 256, 128))

    def body(h_ref, g_ref, o_ref):
        x = h_ref[...]
        o_ref[...] = ((x * _rstd(x)) * g_ref[...]).astype(o_ref.dtype)

    row = pl.BlockSpec((tr, d), lambda i: (i, 0))
    return pl.pallas_call(
        body, name=name, grid=(t // tr,),
        in_specs=[row, pl.BlockSpec((1, d), lambda i: (0, 0))], out_specs=row,
        out_shape=_sds((t, d), BF16), compiler_params=_params(("parallel",)),
    )(h, g)


def _rmsnorm_bwd(h, dy, g, res, name):
    t, d = h.shape
    tr = _tile(t, (384, 256, 128))

    def body(h_ref, dy_ref, g_ref, res_ref, dh_ref, dhb_ref, dg_ref):
        dx, dg_rows = _norm_bwd(h_ref[...], dy_ref[...], g_ref[...])
        dh = res_ref[...] + dx
        dh_ref[...] = dh
        dhb_ref[...] = dh.astype(BF16)

        @pl.when(pl.program_id(0) == 0)
        def _():
            dg_ref[...] = jnp.zeros_like(dg_ref)

        dg_ref[...] += jnp.sum(dg_rows, axis=0, keepdims=True)

    row = pl.BlockSpec((tr, d), lambda i: (i, 0))
    vec = pl.BlockSpec((1, d), lambda i: (0, 0))
    return pl.pallas_call(
        body, name=name, grid=(t // tr,),
        in_specs=[row, row, vec, row], out_specs=[row, row, vec],
        out_shape=[_sds((t, d), F32), _sds((t, d), BF16), _sds((1, d), F32)],
        compiler_params=_params(("arbitrary",)),
    )(h, dy, g, res)


def _qk_prep(proj, gq, gk, aw, name):
    t = proj.shape[0]
    heads = aw // HEAD_DIM
    tr = _tile(t, (384, 256, 128))

    def body(q_ref, k_ref, v_ref, gq_ref, gk_ref, qo_ref, ko_ref, vo_ref):
        for h in range(heads):
            sl = slice(h * HEAD_DIM, (h + 1) * HEAD_DIM)
            xq, xk = q_ref[:, sl], k_ref[:, sl]
            qo_ref[:, sl] = ((xq * _rstd(xq)) * gq_ref[...]).astype(BF16)
            ko_ref[:, sl] = ((xk * _rstd(xk)) * gk_ref[...]).astype(BF16)
        vo_ref[...] = v_ref[...].astype(BF16)

    vec = pl.BlockSpec((1, HEAD_DIM), lambda i: (0, 0))
    out = pl.BlockSpec((tr, aw), lambda i: (i, 0))
    return pl.pallas_call(
        body, name=name, grid=(t // tr,),
        in_specs=[pl.BlockSpec((tr, aw), lambda i: (i, 0)), pl.BlockSpec((tr, aw), lambda i: (i, 1)),
                  pl.BlockSpec((tr, aw), lambda i: (i, 2)), vec, vec],
        out_specs=[out, out, out], out_shape=[_sds((t, aw), BF16)] * 3,
        compiler_params=_params(("parallel",)),
    )(proj, proj, proj, gq, gk)


def _qk_bwd(dqn, dkn, proj, gq, gk, aw, name):
    t = proj.shape[0]
    heads = aw // HEAD_DIM
    tr = _tile(t, (384, 256, 128))

    def body(dq_ref, dk_ref, q_ref, k_ref, gq_ref, gk_ref, dqo_ref, dko_ref, dgq_ref, dgk_ref):
        @pl.when(pl.program_id(0) == 0)
        def _():
            dgq_ref[...] = jnp.zeros_like(dgq_ref)
            dgk_ref[...] = jnp.zeros_like(dgk_ref)

        for h in range(heads):
            sl = slice(h * HEAD_DIM, (h + 1) * HEAD_DIM)
            dx, dg_rows = _norm_bwd(q_ref[:, sl], dq_ref[:, sl], gq_ref[...])
            dqo_ref[:, sl] = dx.astype(BF16)
            dgq_ref[...] += jnp.sum(dg_rows, axis=0, keepdims=True)
            dx, dg_rows = _norm_bwd(k_ref[:, sl], dk_ref[:, sl], gk_ref[...])
            dko_ref[:, sl] = dx.astype(BF16)
            dgk_ref[...] += jnp.sum(dg_rows, axis=0, keepdims=True)

    vec = pl.BlockSpec((1, HEAD_DIM), lambda i: (0, 0))
    row = pl.BlockSpec((tr, aw), lambda i: (i, 0))
    return pl.pallas_call(
        body, name=name, grid=(t // tr,),
        in_specs=[row, row, row, pl.BlockSpec((tr, aw), lambda i: (i, 1)), vec, vec],
        out_specs=[row, row, vec, vec],
        out_shape=[_sds((t, aw), BF16), _sds((t, aw), BF16), _sds((1, HEAD_DIM), F32), _sds((1, HEAD_DIM), F32)],
        compiler_params=_params(("arbitrary",)),
    )(dqn, dkn, proj, proj, gq, gk)


def _triangle(lower):
    r = lax.broadcasted_iota(jnp.int32, (LANES, LANES), 0)
    c = lax.broadcasted_iota(jnp.int32, (LANES, LANES), 1)
    return ((c <= r) if lower else (c >= r)).astype(F32)


def _forget_fwd(fg, b, name):
    t = fg.shape[0]

    def body(fg_ref, b_ref, cum_ref, carry):
        @pl.when(pl.program_id(0) == 0)
        def _():
            carry[...] = jnp.zeros_like(carry)

        z = fg_ref[...] + b_ref[...]
        log_f = jnp.minimum(z, 0.0) - jnp.log1p(jnp.exp(-jnp.abs(z)))
        cs = jnp.dot(_triangle(True), log_f, precision=lax.Precision.HIGHEST,
                     preferred_element_type=F32) + carry[0:1, :]
        cum_ref[...] = cs
        carry[...] = jnp.broadcast_to(cs[LANES - 1:LANES, :], carry.shape)

    row = pl.BlockSpec((LANES, LANES), lambda i: (i, 0))
    return pl.pallas_call(
        body, name=name, grid=(t // LANES,),
        in_specs=[row, pl.BlockSpec((1, LANES), lambda i: (0, 0))], out_specs=row,
        out_shape=_sds((t, LANES), F32), scratch_shapes=[pltpu.VMEM((SUBLANES, LANES), F32)],
        compiler_params=_params(("arbitrary",)),
    )(fg, b)


def _forget_bwd(dcum, fg, b, name):
    t = fg.shape[0]
    nt = t // LANES

    def body(dc_ref, fg_ref, b_ref, dfg_ref, db_ref, carry):
        @pl.when(pl.program_id(0) == 0)
        def _():
            carry[...] = jnp.zeros_like(carry)
            db_ref[...] = jnp.zeros_like(db_ref)

        d_log_f = jnp.dot(_triangle(False), dc_ref[...], precision=lax.Precision.HIGHEST,
                          preferred_element_type=F32) + carry[0:1, :]
        carry[...] = jnp.broadcast_to(d_log_f[0:1, :], carry.shape)
        dz = d_log_f * jax.nn.sigmoid(-(fg_ref[...] + b_ref[...]))
        dfg_ref[...] = dz.astype(BF16)
        db_ref[...] += jnp.sum(dz, axis=0, keepdims=True)

    row = pl.BlockSpec((LANES, LANES), lambda i: (nt - 1 - i, 0))
    vec = pl.BlockSpec((1, LANES), lambda i: (0, 0))
    return pl.pallas_call(
        body, name=name, grid=(nt,),
        in_specs=[row, row, vec], out_specs=[row, vec],
        out_shape=[_sds((t, LANES), BF16), _sds((1, LANES), F32)],
        scratch_shapes=[pltpu.VMEM((SUBLANES, LANES), F32)],
        compiler_params=_params(("arbitrary",)),
    )(dcum, fg, b)


def _causal(qi, kj, tq):
    rows = qi * tq + lax.broadcasted_iota(jnp.int32, (tq, tq), 0)
    cols = kj * tq + lax.broadcasted_iota(jnp.int32, (tq, tq), 1)
    return cols <= rows


NT_DIMS = (((1,), (1,)), ((), ()))
TN_DIMS = (((0,), (0,)), ((), ()))


def _attn_fwd(q, k, v, cum_col, cum_row, name, job=None, mid_at=0.5):
    t, aw = q.shape
    heads = aw // HEAD_DIM
    tq = _tile(t, (384, 256, 128))
    nq = t // tq
    scale = HEAD_DIM ** -0.5

    def body(q_ref, k_ref, v_ref, cq_ref, ck_ref, o_ref, of_ref, lse_ref, m_s, l_s, acc_s, res_s):
        qi, kj = pl.program_id(1), pl.program_id(2)

        @pl.when(kj == 0)
        def _():
            m_s[...] = jnp.full_like(m_s, -jnp.inf)
            l_s[...] = jnp.zeros_like(l_s)
            acc_s[...] = jnp.zeros_like(acc_s)
            res_s[...] = jnp.zeros_like(res_s)

        @pl.when(kj <= qi)
        def _():
            s = lax.dot_general(q_ref[...], k_ref[...], NT_DIMS, preferred_element_type=F32) * scale
            s = s + cq_ref[...] - ck_ref[...]
            s = jnp.where(_causal(qi, kj, tq), s, -jnp.inf)
            m_prev = m_s[...]
            m_new = jnp.maximum(m_prev, jnp.max(s, axis=-1, keepdims=True))
            alpha = jnp.exp(m_prev - m_new)
            p = jnp.exp(s - m_new)
            l_s[...] = alpha * l_s[...] + jnp.sum(p, axis=-1, keepdims=True)
            p_hi = p.astype(BF16)
            p_lo = (p - p_hi.astype(F32)).astype(BF16)
            acc_s[...] = alpha * acc_s[...] + jnp.dot(p_hi, v_ref[...], preferred_element_type=F32)
            res_s[...] = alpha * res_s[...] + jnp.dot(p_lo, v_ref[...], preferred_element_type=F32)
            m_s[...] = m_new

        @pl.when(kj == qi)
        def _():
            o_ref[...] = acc_s[...] / l_s[...]
            of_ref[...] = (acc_s[...] + res_s[...]) / l_s[...]
            lse_ref[...] = m_s[...] + jnp.log(l_s[...])

    q_spec = pl.BlockSpec((tq, HEAD_DIM), lambda h, i, j: (i, h))
    kv_spec = pl.BlockSpec((tq, HEAD_DIM), lambda h, i, j: (jnp.minimum(i, j), h))
    col = pl.BlockSpec((None, tq, 1), lambda h, i, j: (h, i, 0))
    return _call(
        body, name=name, grid=(heads, nq, nq),
        in_specs=[q_spec, kv_spec, kv_spec, col,
                  pl.BlockSpec((None, 1, tq), lambda h, i, j: (h, 0, jnp.minimum(i, j)))],
        out_specs=[q_spec, q_spec, col],
        out_shape=[_sds((t, aw), F32), _sds((t, aw), F32), _sds((heads, t, 1), F32)],
        scratch_shapes=[pltpu.VMEM((tq, 1), F32), pltpu.VMEM((tq, 1), F32), pltpu.VMEM((tq, HEAD_DIM), F32),
                        pltpu.VMEM((tq, HEAD_DIM), F32)],
        semantics=("parallel", "parallel", "arbitrary"),
        operands=(q, k, v, cum_col, cum_row), job=job, mid_at=mid_at)


def _attn_stats(do, o, cum_col, lse, name):
    t, aw = o.shape
    heads = aw // HEAD_DIM
    tr = _tile(t, (384, 256, 128))

    def body(do_ref, o_ref, cq_ref, lse_ref, delta_ref, crow_ref):
        for h in range(heads):
            sl = slice(h * HEAD_DIM, (h + 1) * HEAD_DIM)
            do_seen = do_ref[:, sl].astype(BF16).astype(F32)
            delta_ref[h] = jnp.sum(do_seen * o_ref[:, sl], axis=-1, keepdims=True)
        crow_ref[...] = cq_ref[...] - lse_ref[...]

    row = pl.BlockSpec((tr, aw), lambda i: (i, 0))
    col = pl.BlockSpec((heads, tr, 1), lambda i: (0, i, 0))
    return pl.pallas_call(
        body, name=name, grid=(t // tr,),
        in_specs=[row, row, col, col], out_specs=[col, col],
        out_shape=[_sds((heads, t, 1), F32)] * 2, compiler_params=_params(("parallel",)),
    )(do, o, cum_col, lse)


def _attn_bwd(q, k, v, do, crow, delta, cum_row, name, job=None):
    t, aw = q.shape
    heads = aw // HEAD_DIM
    tq = _tile(t, (384, 256, 128))
    nq = t // tq
    scale = HEAD_DIM ** -0.5

    def body(q_ref, k_ref, v_ref, do_ref, crow_ref, delta_ref, ck_ref,
             dq_ref, dk_ref, dv_ref, dck_ref, dk_s, dv_s, dck_s):
        kj, qi = pl.program_id(1), pl.program_id(2)

        @pl.when((kj == 0) & (qi == 0))
        def _():
            dq_ref[...] = jnp.zeros_like(dq_ref)

        @pl.when(qi == 0)
        def _():
            dk_s[...] = jnp.zeros_like(dk_s)
            dv_s[...] = jnp.zeros_like(dv_s)
            dck_s[...] = jnp.zeros_like(dck_s)

        @pl.when(qi >= kj)
        def _():
            qv, kv, dov = q_ref[...], k_ref[...], do_ref[...].astype(BF16)
            s = lax.dot_general(qv, kv, NT_DIMS, preferred_element_type=F32) * scale
            s = s + crow_ref[...] - ck_ref[...]
            p = jnp.where(_causal(qi, kj, tq), jnp.exp(s), 0.0)
            dp = lax.dot_general(dov, v_ref[...], NT_DIMS, preferred_element_type=F32)
            ds = p * (dp - delta_ref[...])
            dsb = ds.astype(BF16)
            dv_s[...] += lax.dot_general(p.astype(BF16), dov, TN_DIMS, preferred_element_type=F32)
            dk_s[...] += lax.dot_general(dsb, qv, TN_DIMS, preferred_element_type=F32)
            rows = pl.ds(pl.multiple_of(qi * tq, tq), tq)
            dq_ref[rows, :] += jnp.dot(dsb, kv, preferred_element_type=F32) * scale
            dck_s[...] += jnp.sum(ds, axis=0, keepdims=True)

        @pl.when(qi == nq - 1)
        def _():
            dk_ref[...] = dk_s[...] * scale
            dv_ref[...] = dv_s[...].astype(dv_ref.dtype)
            dck_ref[...] = -dck_s[...]

    q_spec = pl.BlockSpec((tq, HEAD_DIM), lambda h, j, i: (jnp.maximum(i, j), h))
    k_spec = pl.BlockSpec((tq, HEAD_DIM), lambda h, j, i: (j, h))
    col = pl.BlockSpec((None, tq, 1), lambda h, j, i: (h, jnp.maximum(i, j), 0))
    row = pl.BlockSpec((None, 1, tq), lambda h, j, i: (h, 0, j))
    return _call(
        body, name=name, grid=(heads, nq, nq),
        in_specs=[q_spec, k_spec, k_spec, q_spec, col, col, row],
        out_specs=[pl.BlockSpec((t, HEAD_DIM), lambda h, j, i: (0, h)), k_spec, k_spec, row],
        out_shape=[_sds((t, aw), F32), _sds((t, aw), F32), _sds((t, aw), BF16), _sds((heads, 1, t), F32)],
        scratch_shapes=[pltpu.VMEM((tq, HEAD_DIM), F32), pltpu.VMEM((tq, HEAD_DIM), F32), pltpu.VMEM((1, tq), F32)],
        semantics=("parallel", "arbitrary", "arbitrary"),
        operands=(q, k, v, do, crow, delta, cum_row), job=job)


def _shift_down(u, by):
    rows = lax.broadcasted_iota(jnp.int32, u.shape, 0)
    return jnp.where(rows >= by, pltpu.roll(u, by, 0), 0.0)


def _shift_up(u, by):
    t = u.shape[0]
    rows = lax.broadcasted_iota(jnp.int32, u.shape, 0)
    return jnp.where(rows < t - by, pltpu.roll(u, t - by, 0), 0.0)


def _conv_specs(t, off_b, cw_width):
    nb = cw_width // LANES
    base = off_b // LANES
    return [pl.BlockSpec((t, LANES), lambda j, s=s: (0, base + s * nb + j)) for s in range(3)]


def _conv_fwd(proj, cw, off_b, name):
    t = proj.shape[0]
    width = cw.shape[1]

    def body(cb_ref, cc_ref, cx_ref, w_ref, o_ref):
        u = cc_ref[...] * cx_ref[...]
        y = w_ref[0:1, :] * _shift_down(u, 2) + w_ref[1:2, :] * _shift_down(u, 1) + w_ref[2:3, :] * u
        o_ref[...] = (cb_ref[...] * y).astype(BF16)

    return pl.pallas_call(
        body, name=name, grid=(width // LANES,),
        in_specs=_conv_specs(t, off_b, width) + [pl.BlockSpec((SUBLANES, LANES), lambda j: (0, j))],
        out_specs=pl.BlockSpec((t, LANES), lambda j: (0, j)),
        out_shape=_sds((t, width), BF16), compiler_params=_params(("parallel",)),
    )(proj, proj, proj, cw)


def _conv_bwd(dcp, proj, cw, off_b, name):
    t = proj.shape[0]
    width = cw.shape[1]

    def body(d_ref, cb_ref, cc_ref, cx_ref, w_ref, dcb_ref, dcc_ref, dcx_ref, dw_ref):
        cc, cx = cc_ref[...], cx_ref[...]
        u = cc * cx
        u1, u2 = _shift_down(u, 1), _shift_down(u, 2)
        w0, w1, w2 = w_ref[0:1, :], w_ref[1:2, :], w_ref[2:3, :]
        d = d_ref[...]
        dcb_ref[...] = (d * (w0 * u2 + w1 * u1 + w2 * u)).astype(BF16)
        dy = d * cb_ref[...]
        du = w2 * dy + w1 * _shift_up(dy, 1) + w0 * _shift_up(dy, 2)
        dcc_ref[...] = (du * cx).astype(BF16)
        dcx_ref[...] = (du * cc).astype(BF16)
        dw = [jnp.sum(dy * s, axis=0, keepdims=True) for s in (u2, u1, u)]
        dw_ref[...] = jnp.concatenate(dw + [jnp.zeros((SUBLANES - 3, LANES), F32)], axis=0)

    col = pl.BlockSpec((t, LANES), lambda j: (0, j))
    wspec = pl.BlockSpec((SUBLANES, LANES), lambda j: (0, j))
    return pl.pallas_call(
        body, name=name, grid=(width // LANES,),
        in_specs=[col] + _conv_specs(t, off_b, width) + [wspec],
        out_specs=[col, col, col, wspec],
        out_shape=[_sds((t, width), BF16)] * 3 + [_sds((SUBLANES, width), F32)],
        compiler_params=_params(("parallel",)),
    )(dcp, proj, proj, proj, cw)


def _gate_specs(t, d, off_g, tr, tc, rows_first):
    nb = d // tc
    base = off_g // tc
    if rows_first:
        tile = lambda s: pl.BlockSpec((tr, tc), lambda i, j: (i, base + s * nb + j))
        vec = lambda s: pl.BlockSpec((1, tc), lambda i, j: (0, s * nb + j))
        plain = pl.BlockSpec((tr, tc), lambda i, j: (i, j))
    else:
        tile = lambda s: pl.BlockSpec((tr, tc), lambda j, i: (i, base + s * nb + j))
        vec = lambda s: pl.BlockSpec((1, tc), lambda j, i: (0, s * nb + j))
        plain = pl.BlockSpec((tr, tc), lambda j, i: (i, j))
    return tile, vec, plain


def _gate_fwd(a, c, proj, bg, off_g, name):
    t, d = a.shape
    tr, tc = _tile(t, (384, 256, 128)), _tile(d, (512, 256, 128))
    tile, vec, plain = _gate_specs(t, d, off_g, tr, tc, True)

    def body(a_ref, c_ref, g0_ref, g1_ref, b0_ref, b1_ref, o_ref):
        g0 = jax.nn.sigmoid(g0_ref[...] + b0_ref[...])
        g1 = jax.nn.sigmoid(g1_ref[...] + b1_ref[...])
        o_ref[...] = (g0 * a_ref[...] + g1 * c_ref[...]).astype(BF16)

    return pl.pallas_call(
        body, name=name, grid=(t // tr, d // tc),
        in_specs=[plain, plain, tile(0), tile(1), vec(0), vec(1)], out_specs=plain,
        out_shape=_sds((t, d), BF16), compiler_params=_params(("parallel", "parallel")),
    )(a, c, proj, proj, bg, bg)


def _gate_bwd(dm, a, c, proj, bg, off_g, name):
    t, d = a.shape
    tr, tc = _tile(t, (384, 256, 128)), _tile(d, (512, 256, 128))
    tile, vec, plain = _gate_specs(t, d, off_g, tr, tc, False)

    def body(dm_ref, a_ref, c_ref, g0_ref, g1_ref, b0_ref, b1_ref,
             da_ref, dc_ref, dg0_ref, dg1_ref, db0_ref, db1_ref):
        @pl.when(pl.program_id(1) == 0)
        def _():
            db0_ref[...] = jnp.zeros_like(db0_ref)
            db1_ref[...] = jnp.zeros_like(db1_ref)

        dm = dm_ref[...]
        g0 = jax.nn.sigmoid(g0_ref[...] + b0_ref[...])
        g1 = jax.nn.sigmoid(g1_ref[...] + b1_ref[...])
        da_ref[...] = (dm * g0).astype(BF16)
        dc_ref[...] = (dm * g1).astype(BF16)
        dz0 = dm * a_ref[...] * (g0 * (1.0 - g0))
        dz1 = dm * c_ref[...] * (g1 * (1.0 - g1))
        dg0_ref[...] = dz0.astype(BF16)
        dg1_ref[...] = dz1.astype(BF16)
        db0_ref[...] += jnp.sum(dz0, axis=0, keepdims=True)
        db1_ref[...] += jnp.sum(dz1, axis=0, keepdims=True)

    bvec = pl.BlockSpec((1, tc), lambda j, i: (0, j))
    return pl.pallas_call(
        body, name=name, grid=(d // tc, t // tr),
        in_specs=[plain, plain, plain, tile(0), tile(1), vec(0), vec(1)],
        out_specs=[plain] * 4 + [bvec, bvec],
        out_shape=[_sds((t, d), BF16)] * 4 + [_sds((1, d), F32)] * 2,
        compiler_params=_params(("parallel", "arbitrary")),
    )(dm, a, c, proj, proj, bg, bg)


def _sum_squares(x, name):
    t, d = x.shape
    tr = _tile(t, (384, 256, 128))

    def body(x_ref, o_ref):
        @pl.when(pl.program_id(0) == 0)
        def _():
            o_ref[...] = jnp.zeros_like(o_ref)

        v = x_ref[...]
        o_ref[...] += jnp.sum(jnp.sum(v * v, axis=0, keepdims=True), axis=1, keepdims=True)

    return pl.pallas_call(
        body, name=name, grid=(t // tr,),
        in_specs=[pl.BlockSpec((tr, d), lambda i: (i, 0))],
        out_specs=pl.BlockSpec((1, LANES), lambda i: (0, 0)),
        out_shape=_sds((1, LANES), F32), compiler_params=_params(("arbitrary",)),
    )(x)


def _row_tile(r, c):
    return r if r * c <= 128 * 1024 else _tile(r, (128, 64, 32, 16))


def _sum_parts(parts, name):
    n, r, c = parts.shape
    tr = _row_tile(r, c)

    def body(p_ref, o_ref):
        acc = p_ref[0].astype(F32)
        for i in range(1, n):
            acc = acc + p_ref[i].astype(F32)
        o_ref[...] = acc

    return pl.pallas_call(
        body, name=name, grid=(r // tr,),
        in_specs=[pl.BlockSpec((n, tr, c), lambda i: (0, i, 0))],
        out_specs=pl.BlockSpec((tr, c), lambda i: (i, 0)),
        out_shape=_sds((r, c), F32), compiler_params=_params(("parallel",)),
    )(parts)


def _adamw(chunks, w, m, v, name):
    n, rc, c = chunks[0].shape
    r = rc * len(chunks)
    tr = _row_tile(rc, c)
    per = rc // tr

    def body(*refs):
        p_refs = refs[:len(chunks)]
        w_ref, m_ref, v_ref, g_ref, d_ref, nm_ref, nv_ref = refs[len(chunks):]
        i = pl.program_id(0)

        def update(p_ref):
            g = p_ref[0].astype(F32)
            for s in range(1, n):
                g = g + p_ref[s].astype(F32)
            nm = ADAM_B1 * m_ref[...] + (1.0 - ADAM_B1) * g
            nv = ADAM_B2 * v_ref[...] + (1.0 - ADAM_B2) * (g * g)
            m_hat = nm / (1.0 - ADAM_B1 ** ADAM_STEP)
            v_hat = nv / (1.0 - ADAM_B2 ** ADAM_STEP)
            g_ref[...] = g
            d_ref[...] = -ADAM_LR * (m_hat / (jnp.sqrt(v_hat) + ADAM_EPS) + ADAM_WD * w_ref[...])
            nm_ref[...] = nm
            nv_ref[...] = nv

        if len(chunks) == 1:
            update(p_refs[0])
        else:
            for ci, p_ref in enumerate(p_refs):
                pl.when((i >= ci * per) & (i < (ci + 1) * per))(functools.partial(update, p_ref))

    row = pl.BlockSpec((tr, c), lambda i: (i, 0))
    part_specs = [pl.BlockSpec((n, tr, c), lambda i, ci=ci: (0, jnp.clip(i - ci * per, 0, per - 1), 0))
                  for ci in range(len(chunks))]
    return pl.pallas_call(
        body, name=name, grid=(r // tr,),
        in_specs=part_specs + [row, row, row],
        out_specs=[row] * 4, out_shape=[_sds((r, c), F32)] * 4,
        compiler_params=_params(("parallel",)),
    )(*chunks, w, m, v)


def _pad_lanes(a, width=LANES):
    return jnp.pad(a, ((0, 0), (0, width - a.shape[1])))


def _rows_of(a):
    flat = a.reshape(-1)
    n = -(-flat.shape[0] // LANES) * LANES
    return jnp.pad(flat, (0, n - flat.shape[0])).reshape(-1, LANES)


def _columns_to_slots(full, n_rows):
    return full.reshape(n_rows, N_DEV, -1).transpose(1, 0, 2)


def _slots_to_columns(slots):
    return slots.transpose(1, 0, 2).reshape(slots.shape[1], -1)


def kernel(x, meta_tokens, norm_mix, w_in, b_fgate, b_gate, q_norm, k_norm, conv_w, w_attn_out, w_conv_out, w_o, norm_mlp, w_up, w_down, loss_target, m_meta_tokens, m_norm_mix, m_w_in, m_b_fgate, m_b_gate, m_q_norm, m_k_norm, m_conv_w, m_w_attn_out, m_w_conv_out, m_w_o, m_norm_mlp, m_w_up, m_w_down, v_meta_tokens, v_norm_mix, v_w_in, v_b_fgate, v_b_gate, v_q_norm, v_k_norm, v_conv_w, v_w_attn_out, v_w_conv_out, v_w_o, v_norm_mlp, v_w_up, v_w_down):
    seq, d = x.shape[1], x.shape[2]
    heads = b_fgate.shape[1]
    aw = heads * HEAD_DIM
    cwid = conv_w.shape[2] * N_DEV
    dff = w_up.shape[2] * N_DEV
    n_valid = N_META + seq
    t = -(-n_valid // LANES) * LANES
    me = _flat(*_my_place())
    off_cb, off_gl = 3 * aw, 3 * aw + 3 * cwid

    conv_shard = jnp.pad(conv_w[0], ((0, SUBLANES - conv_w.shape[1]), (0, 0)))
    g_in, g_meta, g_cw = _run_job(_Gather([w_in[0].astype(BF16), meta_tokens, conv_shard]), "gather_first")
    w_in_full = _slots_to_columns(g_in)
    w_main = jnp.concatenate([w_in_full[:, :3 * aw], w_in_full[:, 3 * aw + heads:]], axis=1)
    w_fg = _pad_lanes(w_in_full[:, 3 * aw:3 * aw + heads])
    meta_full, cw_full = _slots_to_columns(g_meta), _slots_to_columns(g_cw)

    pad_rows = t - n_valid
    h0 = jnp.concatenate([meta_full, x[0], jnp.zeros((pad_rows, d), F32)], axis=0)
    target = jnp.concatenate([jnp.zeros((N_META, d), F32), loss_target[0], jnp.zeros((pad_rows, d), F32)], axis=0)
    b_f = _pad_lanes(b_fgate)

    xn = _rmsnorm_fwd(h0, norm_mix, "norm_mix_fwd")
    proj, (g_ao, g_co, g_o) = _matmul(
        xn, w_main, name="in_proj", mid_at=0.6,
        job=_Gather([w_attn_out[0].astype(BF16), w_conv_out[0].astype(BF16), w_o[0].astype(BF16)]))
    w_ao, w_co, w_o_f = _slots_to_columns(g_ao), _slots_to_columns(g_co), g_o.reshape(d, d)
    fg = _matmul(xn, w_fg, name="in_proj_fgate")
    qn, kn, vb = _qk_prep(proj, q_norm, k_norm, aw, "qk_norm_fwd")
    cum = _forget_fwd(fg, b_f, "forget_cumsum")
    cum_heads = cum[:, :heads].T
    cum_col, cum_row = cum_heads[:, :, None], cum_heads[:, None, :]
    (o, o_fine, lse), (g_up, g_down) = _attn_fwd(
        qn, kn, vb, cum_col, cum_row, "attention_fwd", mid_at=0.75,
        job=_Gather([w_up[0].astype(BF16), w_down[0].astype(BF16)]))
    w_up_f, w_down_f = _slots_to_columns(g_up), g_down.reshape(dff, d)
    a = _matmul(o, w_ao, name="attn_out_proj")
    cpre = _conv_fwd(proj, cw_full, off_cb, "short_conv_fwd")
    c = _matmul(cpre, w_co, name="conv_out_proj")
    merged = _gate_fwd(a, c, proj, b_gate, off_gl, "gate_merge_fwd")
    h1 = _matmul(merged, w_o_f, name="out_proj", extras=(h0,), epilogue=lambda acc, i, j, r: (r + acc,))
    hn = _rmsnorm_fwd(h1, norm_mlp, "norm_mlp_fwd")
    z, u = _matmul(hn, w_up_f, name="mlp_up", out_dtypes=(F32, BF16),
                   epilogue=lambda acc, i, j: (acc, jnp.square(jnp.maximum(acc, 0.0))))

    tm_down = _tile(t, (1408, 1024, 512, 256, 128))

    def loss_grad(acc, i, j, h1_tile, tgt_tile):
        rows = i * tm_down + lax.broadcasted_iota(jnp.int32, acc.shape, 0)
        valid = (rows >= N_META) & (rows < n_valid)
        dy = jnp.where(valid, ((h1_tile + acc) - tgt_tile) / d, 0.0)
        return dy, dy

    dh2, dh2b = _matmul(u, w_down_f, name="mlp_down_loss", extras=(h1, target), epilogue=loss_grad,
                        out_dtypes=(F32, BF16), tm=tm_down)
    loss_part = _sum_squares(dh2, "loss_sum") * (0.5 * d)

    wide = lambda n_cols: _tile(n_cols, (1024, 512, 256, 128))
    dw_down = _matmul(u.T, dh2b, name="mlp_down_wgrad", tn=wide(d))
    s_down = dw_down.reshape(N_DEV, dff // N_DEV, d)
    half_down = dff // N_DEV // 2
    dz, l_down0 = _matmul(dh2b, w_down_f, name="mlp_down_bwd", trans_b=True, extras=(z,), out_dtypes=(BF16,),
                          epilogue=lambda acc, i, j, zt: (acc * (2.0 * jnp.maximum(zt, 0.0)),),
                          job=_Scatter([s_down[:, :half_down].astype(BF16)]))
    dw_up, l_down1 = _matmul(hn.T, dz, name="mlp_up_wgrad", tn=wide(dff),
                             job=_Scatter([s_down[:, half_down:].astype(BF16)]))
    s_up = _columns_to_slots(dw_up, d)
    dhn, l_up0 = _matmul(dz, w_up_f, name="mlp_up_bwd", trans_b=True, tn=wide(d),
                         job=_Scatter([s_up[:, :d // 2].astype(BF16)]))
    dh1, dh1b, dg_mlp = _rmsnorm_bwd(h1, dhn, norm_mlp, dh2, "norm_mlp_bwd")
    dmerged = _matmul(dh1b, w_o_f, name="out_proj_bwd", trans_b=True)
    dw_o = _matmul(merged.T, dh1b, name="out_proj_wgrad", tn=wide(d))
    da, dc, dgl0, dgl1, dbg0, dbg1 = _gate_bwd(dmerged, a, c, proj, b_gate, off_gl, "gate_merge_bwd")
    do = _matmul(da, w_ao, name="attn_out_bwd", trans_b=True)
    dw_ao = _matmul(o.astype(BF16).T, da, name="attn_out_wgrad", tn=wide(d))
    dcp = _matmul(dc, w_co, name="conv_out_bwd", trans_b=True)
    dw_co = _matmul(cpre.T, dc, name="conv_out_wgrad", tn=wide(d))
    dcb, dcc, dcx, dcw = _conv_bwd(dcp, proj, cw_full, off_cb, "short_conv_bwd")
    delta, crow = _attn_stats(do, o_fine, cum_col, lse, "attention_stats")
    leaving = [s_up[:, d // 2:], dw_o.reshape(N_DEV, d // N_DEV, d), _columns_to_slots(dw_ao, aw),
               _columns_to_slots(dw_co, cwid)]
    (dqn, dkn, dv, dck), (l_up1, l_o, l_ao, l_co) = _attn_bwd(
        qn, kn, vb, do, crow, delta, cum_row, "attention_bwd", job=_Scatter([s.astype(BF16) for s in leaving]))
    dq_raw, dk_raw, dg_q, dg_k = _qk_bwd(dqn, dkn, proj, q_norm, k_norm, aw, "qk_norm_bwd")
    dcum = _pad_lanes(dck.reshape(heads, t).T)
    dfg, db_f = _forget_bwd(dcum, fg, b_f, "forget_bwd")
    dproj = jnp.concatenate([dq_raw, dk_raw, dv, dcb, dcc, dcx, dgl0, dgl1], axis=1)
    xn_t = xn.T
    dw_fg = _matmul(xn_t, dfg, name="in_proj_fgate_wgrad")
    rows_in = d // 2

    def in_slots(dw_rows, first):
        full = jnp.concatenate([dw_rows[:, :3 * aw], dw_fg[first:first + rows_in, :heads], dw_rows[:, 3 * aw:]],
                               axis=1)
        return _columns_to_slots(full, rows_in).astype(BF16)

    dw_rows0 = _matmul(xn_t, dproj, name="in_proj_wgrad_0", tm=rows_in, tn=wide(dproj.shape[1]), rows=(0, 1))
    dw_rows1, l_in0 = _matmul(xn_t, dproj, name="in_proj_wgrad_1", tm=rows_in, tn=wide(dproj.shape[1]),
                              rows=(1, 1), job=_Scatter([in_slots(dw_rows0, 0)]))
    dxn_fg = _matmul(dfg, w_fg, name="in_proj_fgate_bwd", trans_b=True)
    dxn, l_in1 = _matmul(dproj, w_main, name="in_proj_bwd", trans_b=True, extras=(dxn_fg,),
                         epilogue=lambda acc, i, j, r: (r + acc,), job=_Scatter([in_slots(dw_rows1, rows_in)]))
    dh0, _, dg_mix = _rmsnorm_bwd(h0, dxn, norm_mix, dh1, "norm_mix_bwd")

    small = [dg_mix, dbg0, dbg1, dg_mlp, dg_q, dg_k, db_f, loss_part, dcw, dh0[:N_META]]
    small_rows = [_rows_of(s) for s in small]
    pack = jnp.concatenate(small_rows, axis=0)
    pack = jnp.pad(pack, ((0, -pack.shape[0] % SUBLANES), (0, 0)))
    (pack_all,) = _run_job(_Scatter([], [pack]), "gather_small")

    landed = {"w_in": l_in0 + l_in1, "w_attn_out": [l_ao], "w_conv_out": [l_co], "w_o": [l_o],
              "w_up": l_up0 + [l_up1], "w_down": l_down0 + l_down1}
    shards = {"w_in": (w_in, m_w_in, v_w_in), "w_attn_out": (w_attn_out, m_w_attn_out, v_w_attn_out),
              "w_conv_out": (w_conv_out, m_w_conv_out, v_w_conv_out), "w_o": (w_o, m_w_o, v_w_o),
              "w_up": (w_up, m_w_up, v_w_up), "w_down": (w_down, m_w_down, v_w_down)}
    out = {}
    for nm, chunks in landed.items():
        w_, m_, v_ = shards[nm]
        res = _adamw(list(chunks), w_[0], m_[0], v_[0], "adamw_" + nm)
        out[nm] = [r[None] for r in res]

    total = _sum_parts(pack_all, "sum_small")
    pieces, at = [], 0
    for s, rows in zip(small, small_rows):
        n_el = 1
        for dim in s.shape:
            n_el *= dim
        pieces.append(total[at:at + rows.shape[0]].reshape(-1)[:n_el].reshape(s.shape))
        at += rows.shape[0]
    g_mix, g_bg0, g_bg1, g_mlp, g_q, g_k, g_bf, loss_row, g_cw_full, g_meta_full = pieces
    loss = loss_row[0, 0]
    cshard = conv_w.shape[2]
    g_small = {
        "norm_mix": g_mix, "b_gate": jnp.concatenate([g_bg0, g_bg1], axis=1), "norm_mlp": g_mlp,
        "q_norm": g_q, "k_norm": g_k, "b_fgate": g_bf[:, :heads],
        "conv_w": lax.dynamic_slice_in_dim(g_cw_full[:conv_w.shape[1]], me * cshard, cshard, axis=1)[None],
        "meta_tokens": lax.dynamic_slice_in_dim(g_meta_full, me * (d // N_DEV), d // N_DEV, axis=1),
    }
    small_w = {"norm_mix": (norm_mix, m_norm_mix, v_norm_mix), "b_gate": (b_gate, m_b_gate, v_b_gate),
               "norm_mlp": (norm_mlp, m_norm_mlp, v_norm_mlp), "q_norm": (q_norm, m_q_norm, v_q_norm),
               "k_norm": (k_norm, m_k_norm, v_k_norm), "b_fgate": (b_fgate, m_b_fgate, v_b_fgate),
               "conv_w": (conv_w, m_conv_w, v_conv_w), "meta_tokens": (meta_tokens, m_meta_tokens, v_meta_tokens)}
    order = list(small_w)
    packed = []
    for idx in range(4):
        cols = [g_small[nm] if idx == 0 else small_w[nm][idx - 1] for nm in order]
        rows = jnp.concatenate([_rows_of(c_) for c_ in cols], axis=0)
        packed.append(jnp.pad(rows, ((0, -rows.shape[0] % SUBLANES), (0, 0))))
    res = _adamw([packed[0][None]], packed[1], packed[2], packed[3], "adamw_small")
    at = 0
    for nm in order:
        shape = small_w[nm][0].shape
        n_el = 1
        for dim in shape:
            n_el *= dim
        n_rows = -(-n_el // LANES)
        out[nm] = [r[at:at + n_rows].reshape(-1)[:n_el].reshape(shape) for r in res]
        at += n_rows

    weights = ["meta_tokens", "norm_mix", "w_in", "b_fgate", "b_gate", "q_norm", "k_norm", "conv_w",
               "w_attn_out", "w_conv_out", "w_o", "norm_mlp", "w_up", "w_down"]
    grad_x = dh0[N_META:n_valid][None]
    return (loss, grad_x, *[out[nm][0] for nm in weights], *[out[nm][1] for nm in weights],
            *[out[nm][2] for nm in weights], *[out[nm][3] for nm in weights])
```

```python
import functools

import jax
import jax.numpy as jnp
from jax import lax
from jax.experimental import pallas as pl
from jax.experimental.pallas import tpu as pltpu

F32 = jnp.float32
BF16 = jnp.bfloat16

N_DEV = 8
N_META = 16
HEAD_DIM = 128
LANES = 128
SUBLANES = 8
EPS = 1e-6
VMEM_LIMIT = 56 * 1024 * 1024

ADAM_LR = 0.001
ADAM_B1 = 0.9
ADAM_B2 = 0.999
ADAM_EPS = 1e-08
ADAM_WD = 0.01
ADAM_STEP = 10

MESH = pl.DeviceIdType.MESH
HBM_SPEC = pl.BlockSpec(memory_space=pltpu.HBM)
RELATIONS = tuple((r >> 2 & 1, r >> 1 & 1, r & 1) for r in range(1, N_DEV))


def _params(semantics=None):
    return pltpu.CompilerParams(dimension_semantics=semantics, vmem_limit_bytes=VMEM_LIMIT)


def _tile(n, prefs):
    for p in prefs:
        if n % p == 0:
            return p
    return n


def _sds(shape, dtype):
    return jax.ShapeDtypeStruct(shape, dtype)


def _my_place():
    return lax.axis_index("x"), lax.axis_index("y"), lax.axis_index("c")


def _flat(px, py, pc):
    return 4 * px + 2 * py + pc


class _Gather:
    def __init__(self, arrays):
        self.operands = list(arrays)
        self.n = len(arrays)
        self.out_shape = [_sds((N_DEV,) + a.shape, a.dtype) for a in arrays]

    def _copy(self, srcs, outs, sems, a, k, block, to, from_src=False):
        slot = outs[a].at[_flat(*block)]
        return pltpu.make_async_remote_copy(
            src_ref=srcs[a] if from_src else slot, dst_ref=slot,
            send_sem=sems[0].at[a, k], recv_sem=sems[1].at[a, k],
            device_id=to, device_id_type=MESH)

    def _places(self):
        x, y, c = _my_place()
        return (x, y, c), (x, y, 1 - c), [(1 - x, y), (x, 1 - y), (1 - x, 1 - y)], c

    def start(self, srcs, outs, sems):
        me, sibling, chips, c = self._places()
        for a in range(self.n):
            pltpu.make_async_copy(srcs[a], outs[a].at[_flat(*me)], sems[2].at[a]).start()
            for j, chip in enumerate(chips):
                self._copy(srcs, outs, sems, a, 1 + j, me, (*chip, c), from_src=True).start()
            self._copy(srcs, outs, sems, a, 0, me, sibling, from_src=True).start()

    def mid(self, srcs, outs, sems):
        me, sibling, chips, c = self._places()
        for a in range(self.n):
            for j, chip in enumerate(chips):
                self._copy(srcs, outs, sems, a, 1 + j, (*chip, c), me).wait_recv()
                self._copy(srcs, outs, sems, a, 4 + j, (*chip, c), sibling).start()

    def finish(self, srcs, outs, sems):
        me, sibling, chips, c = self._places()
        for a in range(self.n):
            self._copy(srcs, outs, sems, a, 0, sibling, me).wait_recv()
            for j, chip in enumerate(chips):
                self._copy(srcs, outs, sems, a, 4 + j, (*chip, 1 - c), me).wait_recv()
            for k in range(7):
                self._copy(srcs, outs, sems, a, k, me, sibling).wait_send()
            pltpu.make_async_copy(srcs[a], outs[a].at[_flat(*me)], sems[2].at[a]).wait()


class _Scatter:
    def __init__(self, scatter, gather=()):
        scatter = [s if isinstance(s, tuple) else (s, 0, s.shape[1]) for s in scatter]
        self.ranges = [(lo, cnt) for _, lo, cnt in scatter]
        self.operands = [s[0] for s in scatter] + list(gather)
        self.ns, self.n = len(scatter), len(scatter) + len(gather)
        self.out_shape = ([_sds((N_DEV, cnt, arr.shape[2]), arr.dtype) for arr, _, cnt in scatter]
                          + [_sds((N_DEV,) + a.shape, a.dtype) for a in gather])

    def _peer(self, rel):
        return tuple(1 - p if r else p for p, r in zip(_my_place(), rel))

    def _src(self, srcs, a, place):
        if a >= self.ns:
            return srcs[a]
        lo, cnt = self.ranges[a]
        return srcs[a].at[_flat(*place), pl.ds(lo, cnt)]

    def _send(self, srcs, outs, sems, a, k, rel):
        peer = self._peer(rel)
        return pltpu.make_async_remote_copy(
            src_ref=self._src(srcs, a, peer), dst_ref=outs[a].at[_flat(*_my_place())],
            send_sem=sems[0].at[a, k], recv_sem=sems[1].at[a, k],
            device_id=peer, device_id_type=MESH)

    def _landed(self, outs, sems, a, k, rel):
        peer = self._peer(rel)
        slot = outs[a].at[_flat(*peer)]
        return pltpu.make_async_remote_copy(
            src_ref=slot, dst_ref=slot, send_sem=sems[0].at[a, k], recv_sem=sems[1].at[a, k],
            device_id=peer, device_id_type=MESH)

    def _own(self, srcs, outs, sems, a):
        me = _my_place()
        return pltpu.make_async_copy(self._src(srcs, a, me), outs[a].at[_flat(*me)], sems[2].at[a])

    def start(self, srcs, outs, sems):
        for a in range(self.n):
            self._own(srcs, outs, sems, a).start()
            for k, rel in enumerate(RELATIONS):
                self._send(srcs, outs, sems, a, k, rel).start()

    def mid(self, srcs, outs, sems):
        pass

    def finish(self, srcs, outs, sems):
        for a in range(self.n):
            for k, rel in enumerate(RELATIONS):
                self._landed(outs, sems, a, k, rel).wait_recv()
            for k, rel in enumerate(RELATIONS):
                self._send(srcs, outs, sems, a, k, rel).wait_send()
            self._own(srcs, outs, sems, a).wait()


def _job_sems(job):
    return [pltpu.SemaphoreType.DMA((job.n, 7)), pltpu.SemaphoreType.DMA((job.n, 7)),
            pltpu.SemaphoreType.DMA((job.n,))]


def _run_job(job, name):
    n = job.n

    def body(*refs):
        srcs, outs, sems = refs[:n], refs[n:2 * n], refs[2 * n:]
        job.start(srcs, outs, sems)
        job.mid(srcs, outs, sems)
        job.finish(srcs, outs, sems)

    return pl.pallas_call(
        body, name=name, out_shape=job.out_shape,
        in_specs=[HBM_SPEC] * n, out_specs=[HBM_SPEC] * n, scratch_shapes=_job_sems(job),
    )(*job.operands)


def _call(body, *, name, grid, in_specs, out_specs, out_shape, scratch_shapes, semantics,
          operands, job=None, mid_at=0.5):
    if job is None:
        res = pl.pallas_call(
            body, name=name, grid=grid, in_specs=in_specs, out_specs=out_specs, out_shape=out_shape,
            scratch_shapes=scratch_shapes, compiler_params=_params(semantics))(*operands)
        return res, []
    n_in, n_out, n_scr = len(in_specs), len(out_specs), len(scratch_shapes)
    total = 1
    for g in grid:
        total *= g
    mid_step = min(int(total * mid_at), total - 1)

    def carried(*refs):
        c_in, j_in = refs[:n_in], refs[n_in:n_in + job.n]
        o0 = n_in + job.n
        c_out, j_out = refs[o0:o0 + n_out], refs[o0 + n_out:o0 + n_out + job.n]
        s0 = o0 + n_out + job.n
        c_scr, sems = refs[s0:s0 + n_scr], refs[s0 + n_scr:]
        step = pl.program_id(0)
        for ax in range(1, len(grid)):
            step = step * grid[ax] + pl.program_id(ax)

        @pl.when(step == 0)
        def _():
            job.start(j_in, j_out, sems)

        body(*c_in, *c_out, *c_scr)

        @pl.when(step == mid_step)
        def _():
            job.mid(j_in, j_out, sems)

        @pl.when(step == total - 1)
        def _():
            job.finish(j_in, j_out, sems)

    res = pl.pallas_call(
        carried, name=name, grid=grid,
        in_specs=list(in_specs) + [HBM_SPEC] * job.n,
        out_specs=list(out_specs) + [HBM_SPEC] * job.n,
        out_shape=list(out_shape) + job.out_shape,
        scratch_shapes=list(scratch_shapes) + _job_sems(job),
        compiler_params=_params(("arbitrary",) * len(grid)),
    )(*operands, *job.operands)
    return list(res[:n_out]), list(res[n_out:])


def _matmul(a, b, *, name, trans_b=False, extras=(), epilogue=None, out_dtypes=(F32,),
            tm=None, tn=None, tk=None, rows=None, trans_a=False, slots=False, job=None, mid_at=0.5):
    k, m = a.shape if trans_a else a.shape[::-1]
    n = b.shape[0] if trans_b else b.shape[1]
    tm = tm or _tile(m, (1408, 1024, 512, 256, 128))
    tn = tn or _tile(n // N_DEV if slots else n, (512, 256, 128))
    tk = tk or _tile(k, (2048, 1408, 1024, 512, 256, 128))
    nk = k // tk
    row0, n_rows = rows or (0, m // tm)
    m = n_rows * tm
    n_ex, n_out = len(extras), len(out_dtypes)
    dims = (((0,) if trans_a else (1,), (1,) if trans_b else (0,)), ((), ()))

    def body(*refs):
        a_ref, b_ref = refs[:2]
        ex_refs = refs[2:2 + n_ex]
        out_refs = refs[2 + n_ex:2 + n_ex + n_out]
        part = lax.dot_general(a_ref[...].astype(BF16), b_ref[...].astype(BF16), dims,
                               preferred_element_type=F32)

        def finish(acc):
            if epilogue is None:
                res = (acc,)
            else:
                res = epilogue(acc, pl.program_id(0), pl.program_id(1), *[e[...] for e in ex_refs])
            for o_ref, r in zip(out_refs, res):
                o_ref[...] = r.astype(o_ref.dtype)

        if nk == 1:
            finish(part)
        else:
            acc_ref = refs[-1]
            kk = pl.program_id(2)

            @pl.when(kk == 0)
            def _():
                acc_ref[...] = part

            @pl.when(kk > 0)
            def _():
                acc_ref[...] += part

            @pl.when(kk == nk - 1)
            def _():
                finish(acc_ref[...])

    in_specs = [pl.BlockSpec((tk, tm), lambda i, j, kk: (kk, row0 + i)) if trans_a
                else pl.BlockSpec((tm, tk), lambda i, j, kk: (row0 + i, kk)),
                pl.BlockSpec((tn, tk), lambda i, j, kk: (j, kk)) if trans_b
                else pl.BlockSpec((tk, tn), lambda i, j, kk: (kk, j))]
    for e in extras:
        if e.shape[0] == 1:
            in_specs.append(pl.BlockSpec((1, tn), lambda i, j, kk: (0, j)))
        else:
            in_specs.append(pl.BlockSpec((tm, tn), lambda i, j, kk: (i, j)))
    if slots:
        per_slot = n // N_DEV // tn
        out_spec = pl.BlockSpec((None, tm, tn), lambda i, j, kk: (j // per_slot, i, j % per_slot))
        out_shape = [_sds((N_DEV, m, n // N_DEV), d) for d in out_dtypes]
    else:
        out_spec = pl.BlockSpec((tm, tn), lambda i, j, kk: (i, j))
        out_shape = [_sds((m, n), d) for d in out_dtypes]
    res, moved = _call(
        body, name=name, grid=(n_rows, n // tn, nk),
        in_specs=in_specs,
        out_specs=[out_spec] * n_out,
        out_shape=out_shape,
        scratch_shapes=[pltpu.VMEM((tm, tn), F32)] if nk > 1 else [],
        semantics=("parallel", "parallel", "arbitrary"),
        operands=(a, b, *extras), job=job, mid_at=mid_at)
    res = res[0] if n_out == 1 else tuple(res)
    return res if job is None else (res, moved)


def _rstd(x):
    return lax.rsqrt(jnp.mean(x * x, axis=-1, keepdims=True) + EPS)


def _norm_bwd(x, dy, g):
    r = _rstd(x)
    u = dy * g
    dx = r * u - x * (r * r * r) * jnp.mean(u * x, axis=-1, keepdims=True)
    return dx, dy * (x * r)


def _rmsnorm_fwd(h, g, name):
    t, d = h.shape
    tr = _tile(t, (384, 256, 128))

    def body(h_ref, g_ref, o_ref):
        x = h_ref[...]
        o_ref[...] = ((x * _rstd(x)) * g_ref[...]).astype(o_ref.dtype)

    row = pl.BlockSpec((tr, d), lambda i: (i, 0))
    return pl.pallas_call(
        body, name=name, grid=(t // tr,),
        in_specs=[row, pl.BlockSpec((1, d), lambda i: (0, 0))], out_specs=row,
        out_shape=_sds((t, d), BF16), compiler_params=_params(("parallel",)),
    )(h, g)


def _rmsnorm_bwd(h, dy, g, res, name):
    t, d = h.shape
    tr = _tile(t, (384, 256, 128))

    def body(h_ref, dy_ref, g_ref, res_ref, dh_ref, dhb_ref, dg_ref):
        dx, dg_rows = _norm_bwd(h_ref[...], dy_ref[...], g_ref[...])
        dh = res_ref[...] + dx
        dh_ref[...] = dh
        dhb_ref[...] = dh.astype(BF16)

        @pl.when(pl.program_id(0) == 0)
        def _():
            dg_ref[...] = jnp.zeros_like(dg_ref)

        dg_ref[...] += jnp.sum(dg_rows, axis=0, keepdims=True)

    row = pl.BlockSpec((tr, d), lambda i: (i, 0))
    vec = pl.BlockSpec((1, d), lambda i: (0, 0))
    return pl.pallas_call(
        body, name=name, grid=(t // tr,),
        in_specs=[row, row, vec, row], out_specs=[row, row, vec],
        out_shape=[_sds((t, d), F32), _sds((t, d), BF16), _sds((1, d), F32)],
        compiler_params=_params(("arbitrary",)),
    )(h, dy, g, res)


def _qk_prep(proj, gq, gk, aw, name):
    t = proj.shape[0]
    heads = aw // HEAD_DIM
    tr = _tile(t, (384, 256, 128))

    def body(q_ref, k_ref, v_ref, gq_ref, gk_ref, qo_ref, ko_ref, vo_ref):
        for h in range(heads):
            sl = slice(h * HEAD_DIM, (h + 1) * HEAD_DIM)
            xq, xk = q_ref[:, sl], k_ref[:, sl]
            qo_ref[:, sl] = ((xq * _rstd(xq)) * gq_ref[...]).astype(BF16)
            ko_ref[:, sl] = ((xk * _rstd(xk)) * gk_ref[...]).astype(BF16)
        vo_ref[...] = v_ref[...].astype(BF16)

    vec = pl.BlockSpec((1, HEAD_DIM), lambda i: (0, 0))
    out = pl.BlockSpec((tr, aw), lambda i: (i, 0))
    return pl.pallas_call(
        body, name=name, grid=(t // tr,),
        in_specs=[pl.BlockSpec((tr, aw), lambda i: (i, 0)), pl.BlockSpec((tr, aw), lambda i: (i, 1)),
                  pl.BlockSpec((tr, aw), lambda i: (i, 2)), vec, vec],
        out_specs=[out, out, out], out_shape=[_sds((t, aw), BF16)] * 3,
        compiler_params=_params(("parallel",)),
    )(proj, proj, proj, gq, gk)


def _qk_bwd(dqn, dkn, proj, gq, gk, aw, name):
    t = proj.shape[0]
    heads = aw // HEAD_DIM
    tr = _tile(t, (384, 256, 128))

    def body(dq_ref, dk_ref, q_ref, k_ref, gq_ref, gk_ref, dqo_ref, dko_ref, dgq_ref, dgk_ref):
        @pl.when(pl.program_id(0) == 0)
        def _():
            dgq_ref[...] = jnp.zeros_like(dgq_ref)
            dgk_ref[...] = jnp.zeros_like(dgk_ref)

        for h in range(heads):
            sl = slice(h * HEAD_DIM, (h + 1) * HEAD_DIM)
            dx, dg_rows = _norm_bwd(q_ref[:, sl], dq_ref[:, sl], gq_ref[...])
            dqo_ref[:, sl] = dx.astype(BF16)
            dgq_ref[...] += jnp.sum(dg_rows, axis=0, keepdims=True)
            dx, dg_rows = _norm_bwd(k_ref[:, sl], dk_ref[:, sl], gk_ref[...])
            dko_ref[:, sl] = dx.astype(BF16)
            dgk_ref[...] += jnp.sum(dg_rows, axis=0, keepdims=True)

    vec = pl.BlockSpec((1, HEAD_DIM), lambda i: (0, 0))
    row = pl.BlockSpec((tr, aw), lambda i: (i, 0))
    return pl.pallas_call(
        body, name=name, grid=(t // tr,),
        in_specs=[row, row, row, pl.BlockSpec((tr, aw), lambda i: (i, 1)), vec, vec],
        out_specs=[row, row, vec, vec],
        out_shape=[_sds((t, aw), BF16), _sds((t, aw), BF16), _sds((1, HEAD_DIM), F32), _sds((1, HEAD_DIM), F32)],
        compiler_params=_params(("arbitrary",)),
    )(dqn, dkn, proj, proj, gq, gk)


def _triangle(lower):
    r = lax.broadcasted_iota(jnp.int32, (LANES, LANES), 0)
    c = lax.broadcasted_iota(jnp.int32, (LANES, LANES), 1)
    return ((c <= r) if lower else (c >= r)).astype(F32)


def _forget_fwd(fg, b, name):
    t = fg.shape[0]

    def body(fg_ref, b_ref, cum_ref, carry):
        @pl.when(pl.program_id(0) == 0)
        def _():
            carry[...] = jnp.zeros_like(carry)

        z = fg_ref[...] + b_ref[...]
        log_f = jnp.minimum(z, 0.0) - jnp.log1p(jnp.exp(-jnp.abs(z)))
        cs = jnp.dot(_triangle(True), log_f, precision=lax.Precision.HIGHEST,
                     preferred_element_type=F32) + carry[0:1, :]
        cum_ref[...] = cs
        carry[...] = jnp.broadcast_to(cs[LANES - 1:LANES, :], carry.shape)

    row = pl.BlockSpec((LANES, LANES), lambda i: (i, 0))
    return pl.pallas_call(
        body, name=name, grid=(t // LANES,),
        in_specs=[row, pl.BlockSpec((1, LANES), lambda i: (0, 0))], out_specs=row,
        out_shape=_sds((t, LANES), F32), scratch_shapes=[pltpu.VMEM((SUBLANES, LANES), F32)],
        compiler_params=_params(("arbitrary",)),
    )(fg, b)


def _forget_bwd(dcum, fg, b, name):
    t = fg.shape[0]
    nt = t // LANES

    def body(dc_ref, fg_ref, b_ref, dfg_ref, db_ref, carry):
        @pl.when(pl.program_id(0) == 0)
        def _():
            carry[...] = jnp.zeros_like(carry)
            db_ref[...] = jnp.zeros_like(db_ref)

        d_log_f = jnp.dot(_triangle(False), dc_ref[...], precision=lax.Precision.HIGHEST,
                          preferred_element_type=F32) + carry[0:1, :]
        carry[...] = jnp.broadcast_to(d_log_f[0:1, :], carry.shape)
        dz = d_log_f * jax.nn.sigmoid(-(fg_ref[...] + b_ref[...]))
        dfg_ref[...] = dz.astype(BF16)
        db_ref[...] += jnp.sum(dz, axis=0, keepdims=True)

    row = pl.BlockSpec((LANES, LANES), lambda i: (nt - 1 - i, 0))
    vec = pl.BlockSpec((1, LANES), lambda i: (0, 0))
    return pl.pallas_call(
        body, name=name, grid=(nt,),
        in_specs=[row, row, vec], out_specs=[row, vec],
        out_shape=[_sds((t, LANES), BF16), _sds((1, LANES), F32)],
        scratch_shapes=[pltpu.VMEM((SUBLANES, LANES), F32)],
        compiler_params=_params(("arbitrary",)),
    )(dcum, fg, b)


def _causal(qi, kj, tq):
    rows = qi * tq + lax.broadcasted_iota(jnp.int32, (tq, tq), 0)
    cols = kj * tq + lax.broadcasted_iota(jnp.int32, (tq, tq), 1)
    return cols <= rows


NT_DIMS = (((1,), (1,)), ((), ()))
TN_DIMS = (((0,), (0,)), ((), ()))


def _attn_fwd(q, k, v, cum_col, cum_row, name, job=None, mid_at=0.5):
    t, aw = q.shape
    heads = aw // HEAD_DIM
    tq = _tile(t, (384, 256, 128))
    nq = t // tq
    scale = HEAD_DIM ** -0.5

    def body(q_ref, k_ref, v_ref, cq_ref, ck_ref, o_ref, of_ref, lse_ref, m_s, l_s, acc_s, res_s):
        qi, kj = pl.program_id(1), pl.program_id(2)

        @pl.when(kj == 0)
        def _():
            m_s[...] = jnp.full_like(m_s, -jnp.inf)
            l_s[...] = jnp.zeros_like(l_s)
            acc_s[...] = jnp.zeros_like(acc_s)
            res_s[...] = jnp.zeros_like(res_s)

        @pl.when(kj <= qi)
        def _():
            s = lax.dot_general(q_ref[...], k_ref[...], NT_DIMS, preferred_element_type=F32) * scale
            s = s + cq_ref[...] - ck_ref[...]
            s = jnp.where(_causal(qi, kj, tq), s, -jnp.inf)
            m_prev = m_s[...]
            m_new = jnp.maximum(m_prev, jnp.max(s, axis=-1, keepdims=True))
            alpha = jnp.exp(m_prev - m_new)
            p = jnp.exp(s - m_new)
            l_s[...] = alpha * l_s[...] + jnp.sum(p, axis=-1, keepdims=True)
            p_hi = p.astype(BF16)
            p_lo = (p - p_hi.astype(F32)).astype(BF16)
            acc_s[...] = alpha * acc_s[...] + jnp.dot(p_hi, v_ref[...], preferred_element_type=F32)
            res_s[...] = alpha * res_s[...] + jnp.dot(p_lo, v_ref[...], preferred_element_type=F32)
            m_s[...] = m_new

        @pl.when(kj == qi)
        def _():
            o_ref[...] = acc_s[...] / l_s[...]
            of_ref[...] = (acc_s[...] + res_s[...]) / l_s[...]
            lse_ref[...] = m_s[...] + jnp.log(l_s[...])

    q_spec = pl.BlockSpec((tq, HEAD_DIM), lambda h, i, j: (i, h))
    kv_spec = pl.BlockSpec((tq, HEAD_DIM), lambda h, i, j: (jnp.minimum(i, j), h))
    col = pl.BlockSpec((None, tq, 1), lambda h, i, j: (h, i, 0))
    return _call(
        body, name=name, grid=(heads, nq, nq),
        in_specs=[q_spec, kv_spec, kv_spec, col,
                  pl.BlockSpec((None, 1, tq), lambda h, i, j: (h, 0, jnp.minimum(i, j)))],
        out_specs=[q_spec, q_spec, col],
        out_shape=[_sds((t, aw), F32), _sds((t, aw), F32), _sds((heads, t, 1), F32)],
        scratch_shapes=[pltpu.VMEM((tq, 1), F32), pltpu.VMEM((tq, 1), F32), pltpu.VMEM((tq, HEAD_DIM), F32),
                        pltpu.VMEM((tq, HEAD_DIM), F32)],
        semantics=("parallel", "parallel", "arbitrary"),
        operands=(q, k, v, cum_col, cum_row), job=job, mid_at=mid_at)


def _attn_stats(do, o, cum_col, lse, name):
    t, aw = o.shape
    heads = aw // HEAD_DIM
    tr = _tile(t, (384, 256, 128))

    def body(do_ref, o_ref, cq_ref, lse_ref, delta_ref, crow_ref):
        for h in range(heads):
            sl = slice(h * HEAD_DIM, (h + 1) * HEAD_DIM)
            do_seen = do_ref[:, sl].astype(BF16).astype(F32)
            delta_ref[h] = jnp.sum(do_seen * o_ref[:, sl], axis=-1, keepdims=True)
        crow_ref[...] = cq_ref[...] - lse_ref[...]

    row = pl.BlockSpec((tr, aw), lambda i: (i, 0))
    col = pl.BlockSpec((heads, tr, 1), lambda i: (0, i, 0))
    return pl.pallas_call(
        body, name=name, grid=(t // tr,),
        in_specs=[row, row, col, col], out_specs=[col, col],
        out_shape=[_sds((heads, t, 1), F32)] * 2, compiler_params=_params(("parallel",)),
    )(do, o, cum_col, lse)


def _attn_bwd(q, k, v, do, crow, delta, cum_row, name, job=None):
    t, aw = q.shape
    heads = aw // HEAD_DIM
    tq = _tile(t, (384, 256, 128))
    nq = t // tq
    scale = HEAD_DIM ** -0.5

    def body(q_ref, k_ref, v_ref, do_ref, crow_ref, delta_ref, ck_ref,
             dq_ref, dk_ref, dv_ref, dck_ref, dk_s, dv_s, dck_s):
        kj, qi = pl.program_id(1), pl.program_id(2)

        @pl.when((kj == 0) & (qi == 0))
        def _():
            dq_ref[...] = jnp.zeros_like(dq_ref)

        @pl.when(qi == 0)
        def _():
            dk_s[...] = jnp.zeros_like(dk_s)
            dv_s[...] = jnp.zeros_like(dv_s)
            dck_s[...] = jnp.zeros_like(dck_s)

        @pl.when(qi >= kj)
        def _():
            qv, kv, dov = q_ref[...], k_ref[...], do_ref[...].astype(BF16)
            s = lax.dot_general(qv, kv, NT_DIMS, preferred_element_type=F32) * scale
            s = s + crow_ref[...] - ck_ref[...]
            p = jnp.where(_causal(qi, kj, tq), jnp.exp(s), 0.0)
            dp = lax.dot_general(dov, v_ref[...], NT_DIMS, preferred_element_type=F32)
            ds = p * (dp - delta_ref[...])
            dsb = ds.astype(BF16)
            dv_s[...] += lax.dot_general(p.astype(BF16), dov, TN_DIMS, preferred_element_type=F32)
            dk_s[...] += lax.dot_general(dsb, qv, TN_DIMS, preferred_element_type=F32)
            rows = pl.ds(pl.multiple_of(qi * tq, tq), tq)
            dq_ref[rows, :] += jnp.dot(dsb, kv, preferred_element_type=F32) * scale
            dck_s[...] += jnp.sum(ds, axis=0, keepdims=True)

        @pl.when(qi == nq - 1)
        def _():
            dk_ref[...] = dk_s[...] * scale
            dv_ref[...] = dv_s[...].astype(dv_ref.dtype)
            dck_ref[...] = -dck_s[...]

    q_spec = pl.BlockSpec((tq, HEAD_DIM), lambda h, j, i: (jnp.maximum(i, j), h))
    k_spec = pl.BlockSpec((tq, HEAD_DIM), lambda h, j, i: (j, h))
    col = pl.BlockSpec((None, tq, 1), lambda h, j, i: (h, jnp.maximum(i, j), 0))
    row = pl.BlockSpec((None, 1, tq), lambda h, j, i: (h, 0, j))
    return _call(
        body, name=name, grid=(heads, nq, nq),
        in_specs=[q_spec, k_spec, k_spec, q_spec, col, col, row],
        out_specs=[pl.BlockSpec((t, HEAD_DIM), lambda h, j, i: (0, h)), k_spec, k_spec, row],
        out_shape=[_sds((t, aw), F32), _sds((t, aw), F32), _sds((t, aw), BF16), _sds((heads, 1, t), F32)],
        scratch_shapes=[pltpu.VMEM((tq, HEAD_DIM), F32), pltpu.VMEM((tq, HEAD_DIM), F32), pltpu.VMEM((1, tq), F32)],
        semantics=("parallel", "arbitrary", "arbitrary"),
        operands=(q, k, v, do, crow, delta, cum_row), job=job)


def _attn_tile():
    return (384, 256, 128)


def _attn_fwd(q, k, v, cum_row, name, job=None, mid_at=0.5):
    t, aw = q.shape
    heads = aw // HEAD_DIM
    tq = _tile(t, _attn_tile())
    nq = t // tq
    scale = HEAD_DIM ** -0.5

    def body(q_ref, k_ref, v_ref, ck_ref, o_ref, of_ref, lse_ref):
        qi = pl.program_id(1)
        qv = q_ref[...]

        def tile(kj, carry, masked):
            m_prev, l_prev, acc, res = carry
            ks = pl.ds(pl.multiple_of(kj * tq, tq), tq)
            s = lax.dot_general(qv, k_ref[ks, :], NT_DIMS, preferred_element_type=F32) * scale - ck_ref[kj]
            if masked:
                s = jnp.where(_causal(0, 0, tq), s, -jnp.inf)
            m_new = jnp.maximum(m_prev, jnp.max(s, axis=-1, keepdims=True))
            alpha = jnp.exp(m_prev - m_new)
            p = jnp.exp(s - m_new)
            p_hi = p.astype(BF16)
            p_lo = (p - p_hi.astype(F32)).astype(BF16)
            vv = v_ref[ks, :]
            return (m_new, alpha * l_prev + jnp.sum(p, axis=-1, keepdims=True),
                    alpha * acc + jnp.dot(p_hi, vv, preferred_element_type=F32),
                    alpha * res + jnp.dot(p_lo, vv, preferred_element_type=F32))

        init = (jnp.full((tq, 1), -jnp.inf, F32), jnp.zeros((tq, 1), F32),
                jnp.zeros((tq, HEAD_DIM), F32), jnp.zeros((tq, HEAD_DIM), F32))
        carry = lax.fori_loop(0, qi, lambda kj, c: tile(kj, c, False), init)
        m_fin, l_fin, acc, res = tile(qi, carry, True)
        o_ref[...] = acc / l_fin
        of_ref[...] = (acc + res) / l_fin
        lse_ref[...] = m_fin + jnp.log(l_fin)

    q_spec = pl.BlockSpec((tq, HEAD_DIM), lambda h, i: (i, h))
    head = pl.BlockSpec((t, HEAD_DIM), lambda h, i: (0, h))
    return _call(
        body, name=name, grid=(heads, nq),
        in_specs=[q_spec, head, head, pl.BlockSpec((None, nq, 1, tq), lambda h, i: (h, 0, 0, 0))],
        out_specs=[q_spec, q_spec, pl.BlockSpec((None, tq, 1), lambda h, i: (h, i, 0))],
        out_shape=[_sds((t, aw), F32), _sds((t, aw), F32), _sds((heads, t, 1), F32)],
        scratch_shapes=[], semantics=("parallel", "arbitrary"),
        operands=(q, k, v, cum_row), job=job, mid_at=mid_at)


def _attn_stats(do, o, name):
    t, aw = o.shape
    heads = aw // HEAD_DIM
    tr = _tile(t, (384, 256, 128))

    def body(do_ref, o_ref, delta_ref):
        for h in range(heads):
            sl = slice(h * HEAD_DIM, (h + 1) * HEAD_DIM)
            do_seen = do_ref[:, sl].astype(BF16).astype(F32)
            delta_ref[h] = jnp.sum(do_seen * o_ref[:, sl], axis=-1, keepdims=True)

    row = pl.BlockSpec((tr, aw), lambda i: (i, 0))
    return pl.pallas_call(
        body, name=name, grid=(t // tr,),
        in_specs=[row, row], out_specs=pl.BlockSpec((heads, tr, 1), lambda i: (0, i, 0)),
        out_shape=_sds((heads, t, 1), F32), compiler_params=_params(("parallel",)),
    )(do, o)


def _attn_bwd(q, k, v, do, lse, delta, cum_row, name, job=None):
    t, aw = q.shape
    heads = aw // HEAD_DIM
    tq = _tile(t, _attn_tile())
    nq = t // tq
    scale = HEAD_DIM ** -0.5

    def body(q_ref, k_ref, v_ref, do_ref, lse_ref, delta_ref, ck_ref, dq_ref, dk_ref, dv_ref, dck_ref):
        kj = pl.program_id(1)

        @pl.when(kj == 0)
        def _():
            dq_ref[...] = jnp.zeros_like(dq_ref)

        kv, vv, ck = k_ref[...], v_ref[...], ck_ref[...]

        def tile(qi, carry, masked):
            dk_acc, dv_acc, dck_acc = carry
            rows = pl.ds(pl.multiple_of(qi * tq, tq), tq)
            qv, dov = q_ref[rows, :], do_ref[rows, :].astype(BF16)
            s = lax.dot_general(qv, kv, NT_DIMS, preferred_element_type=F32) * scale - ck - lse_ref[rows, :]
            p = jnp.exp(s)
            if masked:
                p = jnp.where(_causal(0, 0, tq), p, 0.0)
            dp = lax.dot_general(dov, vv, NT_DIMS, preferred_element_type=F32)
            ds = p * (dp - delta_ref[rows, :])
            dsb = ds.astype(BF16)
            dq_ref[rows, :] += jnp.dot(dsb, kv, preferred_element_type=F32) * scale
            return (dk_acc + lax.dot_general(dsb, qv, TN_DIMS, preferred_element_type=F32),
                    dv_acc + lax.dot_general(p.astype(BF16), dov, TN_DIMS, preferred_element_type=F32),
                    dck_acc + jnp.sum(ds, axis=0, keepdims=True))

        init = (jnp.zeros((tq, HEAD_DIM), F32), jnp.zeros((tq, HEAD_DIM), F32), jnp.zeros((1, tq), F32))
        carry = tile(kj, init, True)
        dk_acc, dv_acc, dck_acc = lax.fori_loop(kj + 1, nq, lambda qi, c: tile(qi, c, False), carry)
        dk_ref[...] = dk_acc * scale
        dv_ref[...] = dv_acc.astype(dv_ref.dtype)
        dck_ref[...] = -dck_acc

    head = pl.BlockSpec((t, HEAD_DIM), lambda h, j: (0, h))
    k_spec = pl.BlockSpec((tq, HEAD_DIM), lambda h, j: (j, h))
    col = pl.BlockSpec((None, t, 1), lambda h, j: (h, 0, 0))
    row = pl.BlockSpec((None, 1, tq), lambda h, j: (h, 0, j))
    return _call(
        body, name=name, grid=(heads, nq),
        in_specs=[head, k_spec, k_spec, head, col, col, row],
        out_specs=[head, k_spec, k_spec, row],
        out_shape=[_sds((t, aw), F32), _sds((t, aw), F32), _sds((t, aw), BF16), _sds((heads, 1, t), F32)],
        scratch_shapes=[], semantics=("parallel", "arbitrary"),
        operands=(q, k, v, do, lse, delta, cum_row), job=job)


def _shift_down(u, by):
    rows = lax.broadcasted_iota(jnp.int32, u.shape, 0)
    return jnp.where(rows >= by, pltpu.roll(u, by, 0), 0.0)


def _shift_up(u, by):
    t = u.shape[0]
    rows = lax.broadcasted_iota(jnp.int32, u.shape, 0)
    return jnp.where(rows < t - by, pltpu.roll(u, t - by, 0), 0.0)


def _conv_specs(t, off_b, cw_width):
    nb = cw_width // LANES
    base = off_b // LANES
    return [pl.BlockSpec((t, LANES), lambda j, s=s: (0, base + s * nb + j)) for s in range(3)]


def _conv_fwd(proj, cw, off_b, name):
    t = proj.shape[0]
    width = cw.shape[1]

    def body(cb_ref, cc_ref, cx_ref, w_ref, o_ref):
        u = cc_ref[...] * cx_ref[...]
        y = w_ref[0:1, :] * _shift_down(u, 2) + w_ref[1:2, :] * _shift_down(u, 1) + w_ref[2:3, :] * u
        o_ref[...] = (cb_ref[...] * y).astype(BF16)

    return pl.pallas_call(
        body, name=name, grid=(width // LANES,),
        in_specs=_conv_specs(t, off_b, width) + [pl.BlockSpec((SUBLANES, LANES), lambda j: (0, j))],
        out_specs=pl.BlockSpec((t, LANES), lambda j: (0, j)),
        out_shape=_sds((t, width), BF16), compiler_params=_params(("parallel",)),
    )(proj, proj, proj, cw)


def _conv_bwd(dcp, proj, cw, off_b, name):
    t = proj.shape[0]
    width = cw.shape[1]

    def body(d_ref, cb_ref, cc_ref, cx_ref, w_ref, dcb_ref, dcc_ref, dcx_ref, dw_ref):
        cc, cx = cc_ref[...], cx_ref[...]
        u = cc * cx
        u1, u2 = _shift_down(u, 1), _shift_down(u, 2)
        w0, w1, w2 = w_ref[0:1, :], w_ref[1:2, :], w_ref[2:3, :]
        d = d_ref[...]
        dcb_ref[...] = (d * (w0 * u2 + w1 * u1 + w2 * u)).astype(BF16)
        dy = d * cb_ref[...]
        du = w2 * dy + w1 * _shift_up(dy, 1) + w0 * _shift_up(dy, 2)
        dcc_ref[...] = (du * cx).astype(BF16)
        dcx_ref[...] = (du * cc).astype(BF16)
        dw = [jnp.sum(dy * s, axis=0, keepdims=True) for s in (u2, u1, u)]
        dw_ref[...] = jnp.concatenate(dw + [jnp.zeros((SUBLANES - 3, LANES), F32)], axis=0)

    col = pl.BlockSpec((t, LANES), lambda j: (0, j))
    wspec = pl.BlockSpec((SUBLANES, LANES), lambda j: (0, j))
    return pl.pallas_call(
        body, name=name, grid=(width // LANES,),
        in_specs=[col] + _conv_specs(t, off_b, width) + [wspec],
        out_specs=[col, col, col, wspec],
        out_shape=[_sds((t, width), BF16)] * 3 + [_sds((SUBLANES, width), F32)],
        compiler_params=_params(("parallel",)),
    )(dcp, proj, proj, proj, cw)


def _gate_specs(t, d, off_g, tr, tc, rows_first):
    nb = d // tc
    base = off_g // tc
    if rows_first:
        tile = lambda s: pl.BlockSpec((tr, tc), lambda i, j: (i, base + s * nb + j))
        vec = lambda s: pl.BlockSpec((1, tc), lambda i, j: (0, s * nb + j))
        plain = pl.BlockSpec((tr, tc), lambda i, j: (i, j))
    else:
        tile = lambda s: pl.BlockSpec((tr, tc), lambda j, i: (i, base + s * nb + j))
        vec = lambda s: pl.BlockSpec((1, tc), lambda j, i: (0, s * nb + j))
        plain = pl.BlockSpec((tr, tc), lambda j, i: (i, j))
    return tile, vec, plain


def _gate_fwd(a, c, proj, bg, off_g, name):
    t, d = a.shape
    tr, tc = _tile(t, (384, 256, 128)), _tile(d, (512, 256, 128))
    tile, vec, plain = _gate_specs(t, d, off_g, tr, tc, True)

    def body(a_ref, c_ref, g0_ref, g1_ref, b0_ref, b1_ref, o_ref):
        g0 = jax.nn.sigmoid(g0_ref[...] + b0_ref[...])
        g1 = jax.nn.sigmoid(g1_ref[...] + b1_ref[...])
        o_ref[...] = (g0 * a_ref[...] + g1 * c_ref[...]).astype(BF16)

    return pl.pallas_call(
        body, name=name, grid=(t // tr, d // tc),
        in_specs=[plain, plain, tile(0), tile(1), vec(0), vec(1)], out_specs=plain,
        out_shape=_sds((t, d), BF16), compiler_params=_params(("parallel", "parallel")),
    )(a, c, proj, proj, bg, bg)


def _gate_bwd(dm, a, c, proj, bg, off_g, name):
    t, d = a.shape
    tr, tc = _tile(t, (384, 256, 128)), _tile(d, (512, 256, 128))
    tile, vec, plain = _gate_specs(t, d, off_g, tr, tc, False)

    def body(dm_ref, a_ref, c_ref, g0_ref, g1_ref, b0_ref, b1_ref,
             da_ref, dc_ref, dg0_ref, dg1_ref, db0_ref, db1_ref):
        @pl.when(pl.program_id(1) == 0)
        def _():
            db0_ref[...] = jnp.zeros_like(db0_ref)
            db1_ref[...] = jnp.zeros_like(db1_ref)

        dm = dm_ref[...]
        g0 = jax.nn.sigmoid(g0_ref[...] + b0_ref[...])
        g1 = jax.nn.sigmoid(g1_ref[...] + b1_ref[...])
        da_ref[...] = (dm * g0).astype(BF16)
        dc_ref[...] = (dm * g1).astype(BF16)
        dz0 = dm * a_ref[...] * (g0 * (1.0 - g0))
        dz1 = dm * c_ref[...] * (g1 * (1.0 - g1))
        dg0_ref[...] = dz0.astype(BF16)
        dg1_ref[...] = dz1.astype(BF16)
        db0_ref[...] += jnp.sum(dz0, axis=0, keepdims=True)
        db1_ref[...] += jnp.sum(dz1, axis=0, keepdims=True)

    bvec = pl.BlockSpec((1, tc), lambda j, i: (0, j))
    return pl.pallas_call(
        body, name=name, grid=(d // tc, t // tr),
        in_specs=[plain, plain, plain, tile(0), tile(1), vec(0), vec(1)],
        out_specs=[plain] * 4 + [bvec, bvec],
        out_shape=[_sds((t, d), BF16)] * 4 + [_sds((1, d), F32)] * 2,
        compiler_params=_params(("parallel", "arbitrary")),
    )(dm, a, c, proj, proj, bg, bg)


def _sum_squares(x, name):
    t, d = x.shape
    tr = _tile(t, (384, 256, 128))

    def body(x_ref, o_ref):
        @pl.when(pl.program_id(0) == 0)
        def _():
            o_ref[...] = jnp.zeros_like(o_ref)

        v = x_ref[...]
        o_ref[...] += jnp.sum(jnp.sum(v * v, axis=0, keepdims=True), axis=1, keepdims=True)

    return pl.pallas_call(
        body, name=name, grid=(t // tr,),
        in_specs=[pl.BlockSpec((tr, d), lambda i: (i, 0))],
        out_specs=pl.BlockSpec((1, LANES), lambda i: (0, 0)),
        out_shape=_sds((1, LANES), F32), compiler_params=_params(("arbitrary",)),
    )(x)


def _row_tile(r, c):
    return r if r * c <= 128 * 1024 else _tile(r, (128, 64, 32, 16))


def _sum_parts(parts, name):
    n, r, c = parts.shape
    tr = _row_tile(r, c)

    def body(p_ref, o_ref):
        acc = p_ref[0].astype(F32)
        for i in range(1, n):
            acc = acc + p_ref[i].astype(F32)
        o_ref[...] = acc

    return pl.pallas_call(
        body, name=name, grid=(r // tr,),
        in_specs=[pl.BlockSpec((n, tr, c), lambda i: (0, i, 0))],
        out_specs=pl.BlockSpec((tr, c), lambda i: (i, 0)),
        out_shape=_sds((r, c), F32), compiler_params=_params(("parallel",)),
    )(parts)


def _adamw(chunks, w, m, v, name):
    n, rc, c = chunks[0].shape
    r = rc * len(chunks)
    tr = _row_tile(rc, c)
    per = rc // tr

    def body(*refs):
        p_refs = refs[:len(chunks)]
        w_ref, m_ref, v_ref, g_ref, d_ref, nm_ref, nv_ref = refs[len(chunks):]
        i = pl.program_id(0)

        def update(p_ref):
            g = p_ref[0].astype(F32)
            for s in range(1, n):
                g = g + p_ref[s].astype(F32)
            nm = ADAM_B1 * m_ref[...] + (1.0 - ADAM_B1) * g
            nv = ADAM_B2 * v_ref[...] + (1.0 - ADAM_B2) * (g * g)
            m_hat = nm / (1.0 - ADAM_B1 ** ADAM_STEP)
            v_hat = nv / (1.0 - ADAM_B2 ** ADAM_STEP)
            g_ref[...] = g
            d_ref[...] = -ADAM_LR * (m_hat / (jnp.sqrt(v_hat) + ADAM_EPS) + ADAM_WD * w_ref[...])
            nm_ref[...] = nm
            nv_ref[...] = nv

        if len(chunks) == 1:
            update(p_refs[0])
        else:
            for ci, p_ref in enumerate(p_refs):
                pl.when((i >= ci * per) & (i < (ci + 1) * per))(functools.partial(update, p_ref))

    row = pl.BlockSpec((tr, c), lambda i: (i, 0))
    part_specs = [pl.BlockSpec((n, tr, c), lambda i, ci=ci: (0, jnp.clip(i - ci * per, 0, per - 1), 0))
                  for ci in range(len(chunks))]
    return pl.pallas_call(
        body, name=name, grid=(r // tr,),
        in_specs=part_specs + [row, row, row],
        out_specs=[row] * 4, out_shape=[_sds((r, c), F32)] * 4,
        compiler_params=_params(("parallel",)),
    )(*chunks, w, m, v)


def _pad_lanes(a, width=LANES):
    return jnp.pad(a, ((0, 0), (0, width - a.shape[1])))


def _rows_of(a):
    flat = a.reshape(-1)
    n = -(-flat.shape[0] // LANES) * LANES
    return jnp.pad(flat, (0, n - flat.shape[0])).reshape(-1, LANES)


def _columns_to_slots(full, n_rows):
    return full.reshape(n_rows, N_DEV, -1).transpose(1, 0, 2)


def _slots_to_columns(slots):
    return slots.transpose(1, 0, 2).reshape(slots.shape[1], -1)


def kernel(x, meta_tokens, norm_mix, w_in, b_fgate, b_gate, q_norm, k_norm, conv_w, w_attn_out, w_conv_out, w_o, norm_mlp, w_up, w_down, loss_target, m_meta_tokens, m_norm_mix, m_w_in, m_b_fgate, m_b_gate, m_q_norm, m_k_norm, m_conv_w, m_w_attn_out, m_w_conv_out, m_w_o, m_norm_mlp, m_w_up, m_w_down, v_meta_tokens, v_norm_mix, v_w_in, v_b_fgate, v_b_gate, v_q_norm, v_k_norm, v_conv_w, v_w_attn_out, v_w_conv_out, v_w_o, v_norm_mlp, v_w_up, v_w_down):
    seq, d = x.shape[1], x.shape[2]
    heads = b_fgate.shape[1]
    aw = heads * HEAD_DIM
    cwid = conv_w.shape[2] * N_DEV
    dff = w_up.shape[2] * N_DEV
    n_valid = N_META + seq
    t = -(-n_valid // LANES) * LANES
    me = _flat(*_my_place())
    off_cb, off_gl = 3 * aw, 3 * aw + 3 * cwid

    conv_shard = jnp.pad(conv_w[0], ((0, SUBLANES - conv_w.shape[1]), (0, 0)))
    g_in, g_meta, g_cw = _run_job(_Gather([w_in[0].astype(BF16), meta_tokens, conv_shard]), "gather_first")
    w_in_full = _slots_to_columns(g_in)
    w_main = jnp.concatenate([w_in_full[:, :3 * aw], w_in_full[:, 3 * aw + heads:]], axis=1)
    w_fg = _pad_lanes(w_in_full[:, 3 * aw:3 * aw + heads])
    meta_full, cw_full = _slots_to_columns(g_meta), _slots_to_columns(g_cw)

    pad_rows = t - n_valid
    h0 = jnp.concatenate([meta_full, x[0], jnp.zeros((pad_rows, d), F32)], axis=0)
    target = jnp.concatenate([jnp.zeros((N_META, d), F32), loss_target[0], jnp.zeros((pad_rows, d), F32)], axis=0)
    b_f = _pad_lanes(b_fgate)

    xn = _rmsnorm_fwd(h0, norm_mix, "norm_mix_fwd")
    proj, (g_ao, g_co, g_o) = _matmul(
        xn, w_main, name="in_proj", mid_at=0.6,
        job=_Gather([w_attn_out[0].astype(BF16), w_conv_out[0].astype(BF16), w_o[0].astype(BF16)]))
    w_ao, w_co, w_o_f = _slots_to_columns(g_ao), _slots_to_columns(g_co), g_o.reshape(d, d)
    fg = _matmul(xn, w_fg, name="in_proj_fgate")
    qn, kn, vb = _qk_prep(proj, q_norm, k_norm, aw, "qk_norm_fwd")
    cum = _forget_fwd(fg, b_f, "forget_cumsum")
    cum_heads = cum[:, :heads].T
    cum_row = cum_heads[:, None, :]
    t_attn = _tile(t, _attn_tile())
    (o, o_fine, lse), (g_up, g_down) = _attn_fwd(
        qn, kn, vb, cum_heads.reshape(heads, t // t_attn, 1, t_attn), "attention_fwd", mid_at=0.75,
        job=_Gather([w_up[0].astype(BF16), w_down[0].astype(BF16)]))
    w_up_f, w_down_f = _slots_to_columns(g_up), g_down.reshape(dff, d)
    a = _matmul(o, w_ao, name="attn_out_proj")
    cpre = _conv_fwd(proj, cw_full, off_cb, "short_conv_fwd")
    c = _matmul(cpre, w_co, name="conv_out_proj")
    merged = _gate_fwd(a, c, proj, b_gate, off_gl, "gate_merge_fwd")
    h1 = _matmul(merged, w_o_f, name="out_proj", extras=(h0,), epilogue=lambda acc, i, j, r: (r + acc,))
    hn = _rmsnorm_fwd(h1, norm_mlp, "norm_mlp_fwd")
    z, u = _matmul(hn, w_up_f, name="mlp_up", out_dtypes=(F32, BF16),
                   epilogue=lambda acc, i, j: (acc, jnp.square(jnp.maximum(acc, 0.0))))

    tm_down = _tile(t, (1408, 1024, 512, 256, 128))

    def loss_grad(acc, i, j, h1_tile, tgt_tile):
        rows = i * tm_down + lax.broadcasted_iota(jnp.int32, acc.shape, 0)
        valid = (rows >= N_META) & (rows < n_valid)
        dy = jnp.where(valid, ((h1_tile + acc) - tgt_tile) / d, 0.0)
        return dy, dy

    dh2, dh2b = _matmul(u, w_down_f, name="mlp_down_loss", extras=(h1, target), epilogue=loss_grad,
                        out_dtypes=(F32, BF16), tm=tm_down)
    loss_part = _sum_squares(dh2, "loss_sum") * (0.5 * d)

    wide = lambda n_cols: _tile(n_cols, (1024, 512, 256, 128))
    dw_down = _matmul(u, dh2b, name="mlp_down_wgrad", trans_a=True, tn=wide(d), out_dtypes=(BF16,))
    s_down = dw_down.reshape(N_DEV, dff // N_DEV, d)
    half_down = dff // N_DEV // 2
    dz, l_down0 = _matmul(dh2b, w_down_f, name="mlp_down_bwd", trans_b=True, extras=(z,), out_dtypes=(BF16,),
                          epilogue=lambda acc, i, j, zt: (acc * (2.0 * jnp.maximum(zt, 0.0)),),
                          job=_Scatter([(s_down, 0, half_down)]))
    s_up, l_down1 = _matmul(hn, dz, name="mlp_up_wgrad", trans_a=True, slots=True, out_dtypes=(BF16,),
                            tn=wide(dff // N_DEV), job=_Scatter([(s_down, half_down, half_down)]))
    dhn, l_up0 = _matmul(dz, w_up_f, name="mlp_up_bwd", trans_b=True, tn=wide(d),
                         job=_Scatter([(s_up, 0, d // 2)]))
    dh1, dh1b, dg_mlp = _rmsnorm_bwd(h1, dhn, norm_mlp, dh2, "norm_mlp_bwd")
    dmerged = _matmul(dh1b, w_o_f, name="out_proj_bwd", trans_b=True)
    dw_o = _matmul(merged, dh1b, name="out_proj_wgrad", trans_a=True, tn=wide(d), out_dtypes=(BF16,))
    da, dc, dgl0, dgl1, dbg0, dbg1 = _gate_bwd(dmerged, a, c, proj, b_gate, off_gl, "gate_merge_bwd")
    do = _matmul(da, w_ao, name="attn_out_bwd", trans_b=True)
    s_ao = _matmul(o, da, name="attn_out_wgrad", trans_a=True, slots=True, out_dtypes=(BF16,))
    dcp = _matmul(dc, w_co, name="conv_out_bwd", trans_b=True)
    s_co = _matmul(cpre, dc, name="conv_out_wgrad", trans_a=True, slots=True, out_dtypes=(BF16,))
    dcb, dcc, dcx, dcw = _conv_bwd(dcp, proj, cw_full, off_cb, "short_conv_bwd")
    delta = _attn_stats(do, o_fine, "attention_stats")
    (dqn, dkn, dv, dck), (l_up1, l_o, l_ao, l_co) = _attn_bwd(
        qn, kn, vb, do, lse, delta, cum_row, "attention_bwd",
        job=_Scatter([(s_up, d // 2, d // 2), dw_o.reshape(N_DEV, d // N_DEV, d), s_ao, s_co]))
    dq_raw, dk_raw, dg_q, dg_k = _qk_bwd(dqn, dkn, proj, q_norm, k_norm, aw, "qk_norm_bwd")
    dcum = _pad_lanes(dck.reshape(heads, t).T)
    dfg, db_f = _forget_bwd(dcum, fg, b_f, "forget_bwd")
    dproj = jnp.concatenate([dq_raw, dk_raw, dv, dcb, dcc, dcx, dgl0, dgl1], axis=1)
    dw_fg = _matmul(xn, dfg, name="in_proj_fgate_wgrad", trans_a=True, out_dtypes=(BF16,))
    rows_in = d // 2

    def in_slots(dw_rows, first):
        full = jnp.concatenate([dw_rows[:, :3 * aw], dw_fg[first:first + rows_in, :heads], dw_rows[:, 3 * aw:]],
                               axis=1)
        return _columns_to_slots(full, rows_in)

    dw_rows0 = _matmul(xn, dproj, name="in_proj_wgrad_0", trans_a=True, tm=rows_in, tn=wide(dproj.shape[1]),
                       rows=(0, 1), out_dtypes=(BF16,))
    dw_rows1, l_in0 = _matmul(xn, dproj, name="in_proj_wgrad_1", trans_a=True, tm=rows_in,
                              tn=wide(dproj.shape[1]), rows=(1, 1), out_dtypes=(BF16,),
                              job=_Scatter([in_slots(dw_rows0, 0)]))
    dxn_fg = _matmul(dfg, w_fg, name="in_proj_fgate_bwd", trans_b=True)
    dxn, l_in1 = _matmul(dproj, w_main, name="in_proj_bwd", trans_b=True, extras=(dxn_fg,),
                         epilogue=lambda acc, i, j, r: (r + acc,), job=_Scatter([in_slots(dw_rows1, rows_in)]))
    dh0, _, dg_mix = _rmsnorm_bwd(h0, dxn, norm_mix, dh1, "norm_mix_bwd")

    small = [dg_mix, dbg0, dbg1, dg_mlp, dg_q, dg_k, db_f, loss_part, dcw, dh0[:N_META]]
    small_rows = [_rows_of(s) for s in small]
    pack = jnp.concatenate(small_rows, axis=0)
    pack = jnp.pad(pack, ((0, -pack.shape[0] % SUBLANES), (0, 0)))
    (pack_all,) = _run_job(_Scatter([], [pack]), "gather_small")

    landed = {"w_in": l_in0 + l_in1, "w_attn_out": [l_ao], "w_conv_out": [l_co], "w_o": [l_o],
              "w_up": l_up0 + [l_up1], "w_down": l_down0 + l_down1}
    shards = {"w_in": (w_in, m_w_in, v_w_in), "w_attn_out": (w_attn_out, m_w_attn_out, v_w_attn_out),
              "w_conv_out": (w_conv_out, m_w_conv_out, v_w_conv_out), "w_o": (w_o, m_w_o, v_w_o),
              "w_up": (w_up, m_w_up, v_w_up), "w_down": (w_down, m_w_down, v_w_down)}
    out = {}
    for nm, chunks in landed.items():
        w_, m_, v_ = shards[nm]
        res = _adamw(list(chunks), w_[0], m_[0], v_[0], "adamw_" + nm)
        out[nm] = [r[None] for r in res]

    total = _sum_parts(pack_all, "sum_small")
    pieces, at = [], 0
    for s, rows in zip(small, small_rows):
        n_el = 1
        for dim in s.shape:
            n_el *= dim
        pieces.append(total[at:at + rows.shape[0]].reshape(-1)[:n_el].reshape(s.shape))
        at += rows.shape[0]
    g_mix, g_bg0, g_bg1, g_mlp, g_q, g_k, g_bf, loss_row, g_cw_full, g_meta_full = pieces
    loss = loss_row[0, 0]
    cshard = conv_w.shape[2]
    g_small = {
        "norm_mix": g_mix, "b_gate": jnp.concatenate([g_bg0, g_bg1], axis=1), "norm_mlp": g_mlp,
        "q_norm": g_q, "k_norm": g_k, "b_fgate": g_bf[:, :heads],
        "conv_w": lax.dynamic_slice_in_dim(g_cw_full[:conv_w.shape[1]], me * cshard, cshard, axis=1)[None],
        "meta_tokens": lax.dynamic_slice_in_dim(g_meta_full, me * (d // N_DEV), d // N_DEV, axis=1),
    }
    small_w = {"norm_mix": (norm_mix, m_norm_mix, v_norm_mix), "b_gate": (b_gate, m_b_gate, v_b_gate),
               "norm_mlp": (norm_mlp, m_norm_mlp, v_norm_mlp), "q_norm": (q_norm, m_q_norm, v_q_norm),
               "k_norm": (k_norm, m_k_norm, v_k_norm), "b_fgate": (b_fgate, m_b_fgate, v_b_fgate),
               "conv_w": (conv_w, m_conv_w, v_conv_w), "meta_tokens": (meta_tokens, m_meta_tokens, v_meta_tokens)}
    order = list(small_w)
    packed = []
    for idx in range(4):
        cols = [g_small[nm] if idx == 0 else small_w[nm][idx - 1] for nm in order]
        rows = jnp.concatenate([_rows_of(c_) for c_ in cols], axis=0)
        packed.append(jnp.pad(rows, ((0, -rows.shape[0] % SUBLANES), (0, 0))))
    res = _adamw([packed[0][None]], packed[1], packed[2], packed[3], "adamw_small")
    at = 0
    for nm in order:
        shape = small_w[nm][0].shape
        n_el = 1
        for dim in shape:
            n_el *= dim
        n_rows = -(-n_el // LANES)
        out[nm] = [r[at:at + n_rows].reshape(-1)[:n_el].reshape(shape) for r in res]
        at += n_rows

    weights = ["meta_tokens", "norm_mix", "w_in", "b_fgate", "b_gate", "q_norm", "k_norm", "conv_w",
               "w_attn_out", "w_conv_out", "w_o", "norm_mlp", "w_up", "w_down"]
    grad_x = dh0[N_META:n_valid][None]
    return (loss, grad_x, *[out[nm][0] for nm in weights], *[out[nm][1] for nm in weights],
            *[out[nm][2] for nm in weights], *[out[nm][3] for nm in weights])
```

```python
import functools

import jax
import jax.numpy as jnp
from jax import lax
from jax.experimental import pallas as pl
from jax.experimental.pallas import tpu as pltpu

F32 = jnp.float32
BF16 = jnp.bfloat16

N_DEV = 8
N_META = 16
HEAD_DIM = 128
LANES = 128
SUBLANES = 8
EPS = 1e-6
VMEM_LIMIT = 56 * 1024 * 1024

ADAM_LR = 0.001
ADAM_B1 = 0.9
ADAM_B2 = 0.999
ADAM_EPS = 1e-08
ADAM_WD = 0.01
ADAM_STEP = 10

MESH = pl.DeviceIdType.MESH
HBM_SPEC = pl.BlockSpec(memory_space=pltpu.HBM)
RELATIONS = tuple((r >> 2 & 1, r >> 1 & 1, r & 1) for r in range(1, N_DEV))


def _params(semantics=None):
    return pltpu.CompilerParams(dimension_semantics=semantics, vmem_limit_bytes=VMEM_LIMIT)


def _tile(n, prefs):
    for p in prefs:
        if n % p == 0:
            return p
    return n


def _sds(shape, dtype):
    return jax.ShapeDtypeStruct(shape, dtype)


def _my_place():
    return lax.axis_index("x"), lax.axis_index("y"), lax.axis_index("c")


def _flat(px, py, pc):
    return 4 * px + 2 * py + pc


class _Gather:
    def __init__(self, arrays):
        self.operands = list(arrays)
        self.n = len(arrays)
        self.out_shape = [_sds((N_DEV,) + a.shape, a.dtype) for a in arrays]

    def _copy(self, srcs, outs, sems, a, k, block, to, from_src=False):
        slot = outs[a].at[_flat(*block)]
        return pltpu.make_async_remote_copy(
            src_ref=srcs[a] if from_src else slot, dst_ref=slot,
            send_sem=sems[0].at[a, k], recv_sem=sems[1].at[a, k],
            device_id=to, device_id_type=MESH)

    def _places(self):
        x, y, c = _my_place()
        return (x, y, c), (x, y, 1 - c), [(1 - x, y), (x, 1 - y), (1 - x, 1 - y)], c

    def start(self, srcs, outs, sems):
        me, sibling, chips, c = self._places()
        for a in range(self.n):
            pltpu.make_async_copy(srcs[a], outs[a].at[_flat(*me)], sems[2].at[a]).start()
            for j, chip in enumerate(chips):
                self._copy(srcs, outs, sems, a, 1 + j, me, (*chip, c), from_src=True).start()
            self._copy(srcs, outs, sems, a, 0, me, sibling, from_src=True).start()

    def mid(self, srcs, outs, sems):
        me, sibling, chips, c = self._places()
        for a in range(self.n):
            for j, chip in enumerate(chips):
                self._copy(srcs, outs, sems, a, 1 + j, (*chip, c), me).wait_recv()
                self._copy(srcs, outs, sems, a, 4 + j, (*chip, c), sibling).start()

    def finish(self, srcs, outs, sems):
        me, sibling, chips, c = self._places()
        for a in range(self.n):
            self._copy(srcs, outs, sems, a, 0, sibling, me).wait_recv()
            for j, chip in enumerate(chips):
                self._copy(srcs, outs, sems, a, 4 + j, (*chip, 1 - c), me).wait_recv()
            for k in range(7):
                self._copy(srcs, outs, sems, a, k, me, sibling).wait_send()
            pltpu.make_async_copy(srcs[a], outs[a].at[_flat(*me)], sems[2].at[a]).wait()


class _Scatter:
    def __init__(self, scatter, gather=()):
        scatter = [s if isinstance(s, tuple) else (s, 0, s.shape[1]) for s in scatter]
        self.ranges = [(lo, cnt) for _, lo, cnt in scatter]
        self.operands = [s[0] for s in scatter] + list(gather)
        self.ns, self.n = len(scatter), len(scatter) + len(gather)
        self.out_shape = ([_sds((N_DEV, cnt, arr.shape[2]), arr.dtype) for arr, _, cnt in scatter]
                          + [_sds((N_DEV,) + a.shape, a.dtype) for a in gather])

    def _peer(self, rel):
        return tuple(1 - p if r else p for p, r in zip(_my_place(), rel))

    def _src(self, srcs, a, place):
        if a >= self.ns:
            return srcs[a]
        lo, cnt = self.ranges[a]
        return srcs[a].at[_flat(*place), pl.ds(lo, cnt)]

    def _send(self, srcs, outs, sems, a, k, rel):
        peer = self._peer(rel)
        return pltpu.make_async_remote_copy(
            src_ref=self._src(srcs, a, peer), dst_ref=outs[a].at[_flat(*_my_place())],
            send_sem=sems[0].at[a, k], recv_sem=sems[1].at[a, k],
            device_id=peer, device_id_type=MESH)

    def _landed(self, outs, sems, a, k, rel):
        peer = self._peer(rel)
        slot = outs[a].at[_flat(*peer)]
        return pltpu.make_async_remote_copy(
            src_ref=slot, dst_ref=slot, send_sem=sems[0].at[a, k], recv_sem=sems[1].at[a, k],
            device_id=peer, device_id_type=MESH)

    def _own(self, srcs, outs, sems, a):
        me = _my_place()
        return pltpu.make_async_copy(self._src(srcs, a, me), outs[a].at[_flat(*me)], sems[2].at[a])

    def start(self, srcs, outs, sems):
        for a in range(self.n):
            self._own(srcs, outs, sems, a).start()
            for k, rel in enumerate(RELATIONS):
                self._send(srcs, outs, sems, a, k, rel).start()

    def mid(self, srcs, outs, sems):
        pass

    def finish(self, srcs, outs, sems):
        for a in range(self.n):
            for k, rel in enumerate(RELATIONS):
                self._landed(outs, sems, a, k, rel).wait_recv()
            for k, rel in enumerate(RELATIONS):
                self._send(srcs, outs, sems, a, k, rel).wait_send()
            self._own(srcs, outs, sems, a).wait()


def _job_sems(job):
    return [pltpu.SemaphoreType.DMA((job.n, 7)), pltpu.SemaphoreType.DMA((job.n, 7)),
            pltpu.SemaphoreType.DMA((job.n,))]


def _run_job(job, name):
    n = job.n

    def body(*refs):
        srcs, outs, sems = refs[:n], refs[n:2 * n], refs[2 * n:]
        job.start(srcs, outs, sems)
        job.mid(srcs, outs, sems)
        job.finish(srcs, outs, sems)

    return pl.pallas_call(
        body, name=name, out_shape=job.out_shape,
        in_specs=[HBM_SPEC] * n, out_specs=[HBM_SPEC] * n, scratch_shapes=_job_sems(job),
    )(*job.operands)


def _call(body, *, name, grid, in_specs, out_specs, out_shape, scratch_shapes, semantics,
          operands, job=None, mid_at=0.5):
    if job is None:
        res = pl.pallas_call(
            body, name=name, grid=grid, in_specs=in_specs, out_specs=out_specs, out_shape=out_shape,
            scratch_shapes=scratch_shapes, compiler_params=_params(semantics))(*operands)
        return res, []
    n_in, n_out, n_scr = len(in_specs), len(out_specs), len(scratch_shapes)
    total = 1
    for g in grid:
        total *= g
    mid_step = min(int(total * mid_at), total - 1)

    def carried(*refs):
        c_in, j_in = refs[:n_in], refs[n_in:n_in + job.n]
        o0 = n_in + job.n
        c_out, j_out = refs[o0:o0 + n_out], refs[o0 + n_out:o0 + n_out + job.n]
        s0 = o0 + n_out + job.n
        c_scr, sems = refs[s0:s0 + n_scr], refs[s0 + n_scr:]
        step = pl.program_id(0)
        for ax in range(1, len(grid)):
            step = step * grid[ax] + pl.program_id(ax)

        @pl.when(step == 0)
        def _():
            job.start(j_in, j_out, sems)

        body(*c_in, *c_out, *c_scr)

        @pl.when(step == mid_step)
        def _():
            job.mid(j_in, j_out, sems)

        @pl.when(step == total - 1)
        def _():
            job.finish(j_in, j_out, sems)

    res = pl.pallas_call(
        carried, name=name, grid=grid,
        in_specs=list(in_specs) + [HBM_SPEC] * job.n,
        out_specs=list(out_specs) + [HBM_SPEC] * job.n,
        out_shape=list(out_shape) + job.out_shape,
        scratch_shapes=list(scratch_shapes) + _job_sems(job),
        compiler_params=_params(("arbitrary",) * len(grid)),
    )(*operands, *job.operands)
    return list(res[:n_out]), list(res[n_out:])


def _matmul(a, b, *, name, trans_b=False, extras=(), epilogue=None, out_dtypes=(F32,),
            tm=None, tn=None, tk=None, rows=None, cols=None, trans_a=False, slots=False, job=None,
            mid_at=0.5):
    k, m = a.shape if trans_a else a.shape[::-1]
    n = b.shape[0] if trans_b else b.shape[1]
    tm = tm or _tile(m, (1408, 1024, 512, 256, 128))
    tn = tn or _tile(n // N_DEV if slots else n, (512, 256, 128))
    tk = tk or _tile(k, (2048, 1408, 1024, 512, 256, 128))
    nk = k // tk
    row0, n_rows = rows or (0, m // tm)
    m = n_rows * tm
    col0, n_cols = cols or (0, n // tn)
    n = n_cols * tn
    n_ex, n_out = len(extras), len(out_dtypes)
    dims = (((0,) if trans_a else (1,), (1,) if trans_b else (0,)), ((), ()))

    def body(*refs):
        a_ref, b_ref = refs[:2]
        ex_refs = refs[2:2 + n_ex]
        out_refs = refs[2 + n_ex:2 + n_ex + n_out]
        part = lax.dot_general(a_ref[...].astype(BF16), b_ref[...].astype(BF16), dims,
                               preferred_element_type=F32)

        def finish(acc):
            if epilogue is None:
                res = (acc,)
            else:
                res = epilogue(acc, pl.program_id(0), pl.program_id(1), *[e[...] for e in ex_refs])
            for o_ref, r in zip(out_refs, res):
                o_ref[...] = r.astype(o_ref.dtype)

        if nk == 1:
            finish(part)
        else:
            acc_ref = refs[-1]
            kk = pl.program_id(2)

            @pl.when(kk == 0)
            def _():
                acc_ref[...] = part

            @pl.when(kk > 0)
            def _():
                acc_ref[...] += part

            @pl.when(kk == nk - 1)
            def _():
                finish(acc_ref[...])

    in_specs = [pl.BlockSpec((tk, tm), lambda i, j, kk: (kk, row0 + i)) if trans_a
                else pl.BlockSpec((tm, tk), lambda i, j, kk: (row0 + i, kk)),
                pl.BlockSpec((tn, tk), lambda i, j, kk: (col0 + j, kk)) if trans_b
                else pl.BlockSpec((tk, tn), lambda i, j, kk: (kk, col0 + j))]
    for e in extras:
        if e.shape[0] == 1:
            in_specs.append(pl.BlockSpec((1, tn), lambda i, j, kk: (0, j)))
        else:
            in_specs.append(pl.BlockSpec((tm, tn), lambda i, j, kk: (i, j)))
    if slots:
        per_slot = n // N_DEV // tn
        out_spec = pl.BlockSpec((None, tm, tn), lambda i, j, kk: (j // per_slot, i, j % per_slot))
        out_shape = [_sds((N_DEV, m, n // N_DEV), d) for d in out_dtypes]
    else:
        out_spec = pl.BlockSpec((tm, tn), lambda i, j, kk: (i, j))
        out_shape = [_sds((m, n), d) for d in out_dtypes]
    res, moved = _call(
        body, name=name, grid=(n_rows, n // tn, nk),
        in_specs=in_specs,
        out_specs=[out_spec] * n_out,
        out_shape=out_shape,
        scratch_shapes=[pltpu.VMEM((tm, tn), F32)] if nk > 1 else [],
        semantics=("parallel", "parallel", "arbitrary"),
        operands=(a, b, *extras), job=job, mid_at=mid_at)
    res = res[0] if n_out == 1 else tuple(res)
    return res if job is None else (res, moved)


def _rstd(x):
    return lax.rsqrt(jnp.mean(x * x, axis=-1, keepdims=True) + EPS)


def _norm_bwd(x, dy, g):
    r = _rstd(x)
    u = dy * g
    dx = r * u - x * (r * r * r) * jnp.mean(u * x, axis=-1, keepdims=True)
    return dx, dy * (x * r)


def _rmsnorm_fwd(h, g, name):
    t, d = h.shape
    tr = _tile(t, (384, 256, 128))

    def body(h_ref, g_ref, o_ref):
        x = h_ref[...]
        o_ref[...] = ((x * _rstd(x)) * g_ref[...]).astype(o_ref.dtype)

    row = pl.BlockSpec((tr, d), lambda i: (i, 0))
    return pl.pallas_call(
        body, name=name, grid=(t // tr,),
        in_specs=[row, pl.BlockSpec((1, d), lambda i: (0, 0))], out_specs=row,
        out_shape=_sds((t, d), BF16), compiler_params=_params(("parallel",)),
    )(h, g)


def _rmsnorm_bwd(h, dy, g, res, name):
    t, d = h.shape
    tr = _tile(t, (384, 256, 128))

    def body(h_ref, dy_ref, g_ref, res_ref, dh_ref, dhb_ref, dg_ref):
        dx, dg_rows = _norm_bwd(h_ref[...], dy_ref[...], g_ref[...])
        dh = res_ref[...] + dx
        dh_ref[...] = dh
        dhb_ref[...] = dh.astype(BF16)

        @pl.when(pl.program_id(0) == 0)
        def _():
            dg_ref[...] = jnp.zeros_like(dg_ref)

        dg_ref[...] += jnp.sum(dg_rows, axis=0, keepdims=True)

    row = pl.BlockSpec((tr, d), lambda i: (i, 0))
    vec = pl.BlockSpec((1, d), lambda i: (0, 0))
    return pl.pallas_call(
        body, name=name, grid=(t // tr,),
        in_specs=[row, row, vec, row], out_specs=[row, row, vec],
        out_shape=[_sds((t, d), F32), _sds((t, d), BF16), _sds((1, d), F32)],
        compiler_params=_params(("arbitrary",)),
    )(h, dy, g, res)


def _qk_prep(proj, gq, gk, aw, name):
    t = proj.shape[0]
    heads = aw // HEAD_DIM
    tr = _tile(t, (384, 256, 128))

    def body(q_ref, k_ref, v_ref, gq_ref, gk_ref, qo_ref, ko_ref, vo_ref):
        for h in range(heads):
            sl = slice(h * HEAD_DIM, (h + 1) * HEAD_DIM)
            xq, xk = q_ref[:, sl], k_ref[:, sl]
            qo_ref[:, sl] = ((xq * _rstd(xq)) * gq_ref[...]).astype(BF16)
            ko_ref[:, sl] = ((xk * _rstd(xk)) * gk_ref[...]).astype(BF16)
        vo_ref[...] = v_ref[...].astype(BF16)

    vec = pl.BlockSpec((1, HEAD_DIM), lambda i: (0, 0))
    out = pl.BlockSpec((tr, aw), lambda i: (i, 0))
    return pl.pallas_call(
        body, name=name, grid=(t // tr,),
        in_specs=[pl.BlockSpec((tr, aw), lambda i: (i, 0)), pl.BlockSpec((tr, aw), lambda i: (i, 1)),
                  pl.BlockSpec((tr, aw), lambda i: (i, 2)), vec, vec],
        out_specs=[out, out, out], out_shape=[_sds((t, aw), BF16)] * 3,
        compiler_params=_params(("parallel",)),
    )(proj, proj, proj, gq, gk)


def _qk_bwd(dqn, dkn, proj, gq, gk, aw, name):
    t = proj.shape[0]
    heads = aw // HEAD_DIM
    tr = _tile(t, (384, 256, 128))

    def body(dq_ref, dk_ref, q_ref, k_ref, gq_ref, gk_ref, dqo_ref, dko_ref, dgq_ref, dgk_ref):
        @pl.when(pl.program_id(0) == 0)
        def _():
            dgq_ref[...] = jnp.zeros_like(dgq_ref)
            dgk_ref[...] = jnp.zeros_like(dgk_ref)

        for h in range(heads):
            sl = slice(h * HEAD_DIM, (h + 1) * HEAD_DIM)
            dx, dg_rows = _norm_bwd(q_ref[:, sl], dq_ref[:, sl], gq_ref[...])
            dqo_ref[:, sl] = dx.astype(BF16)
            dgq_ref[...] += jnp.sum(dg_rows, axis=0, keepdims=True)
            dx, dg_rows = _norm_bwd(k_ref[:, sl], dk_ref[:, sl], gk_ref[...])
            dko_ref[:, sl] = dx.astype(BF16)
            dgk_ref[...] += jnp.sum(dg_rows, axis=0, keepdims=True)

    vec = pl.BlockSpec((1, HEAD_DIM), lambda i: (0, 0))
    row = pl.BlockSpec((tr, aw), lambda i: (i, 0))
    return pl.pallas_call(
        body, name=name, grid=(t // tr,),
        in_specs=[row, row, row, pl.BlockSpec((tr, aw), lambda i: (i, 1)), vec, vec],
        out_specs=[row, row, vec, vec],
        out_shape=[_sds((t, aw), BF16), _sds((t, aw), BF16), _sds((1, HEAD_DIM), F32), _sds((1, HEAD_DIM), F32)],
        compiler_params=_params(("arbitrary",)),
    )(dqn, dkn, proj, proj, gq, gk)


def _triangle(lower):
    r = lax.broadcasted_iota(jnp.int32, (LANES, LANES), 0)
    c = lax.broadcasted_iota(jnp.int32, (LANES, LANES), 1)
    return ((c <= r) if lower else (c >= r)).astype(F32)


def _forget_fwd(fg, b, name):
    t = fg.shape[0]

    def body(fg_ref, b_ref, cum_ref, carry):
        @pl.when(pl.program_id(0) == 0)
        def _():
            carry[...] = jnp.zeros_like(carry)

        z = fg_ref[...] + b_ref[...]
        log_f = jnp.minimum(z, 0.0) - jnp.log1p(jnp.exp(-jnp.abs(z)))
        cs = jnp.dot(_triangle(True), log_f, precision=lax.Precision.HIGHEST,
                     preferred_element_type=F32) + carry[0:1, :]
        cum_ref[...] = cs
        carry[...] = jnp.broadcast_to(cs[LANES - 1:LANES, :], carry.shape)

    row = pl.BlockSpec((LANES, LANES), lambda i: (i, 0))
    return pl.pallas_call(
        body, name=name, grid=(t // LANES,),
        in_specs=[row, pl.BlockSpec((1, LANES), lambda i: (0, 0))], out_specs=row,
        out_shape=_sds((t, LANES), F32), scratch_shapes=[pltpu.VMEM((SUBLANES, LANES), F32)],
        compiler_params=_params(("arbitrary",)),
    )(fg, b)


def _forget_bwd(dcum, fg, b, name):
    t = fg.shape[0]
    nt = t // LANES

    def body(dc_ref, fg_ref, b_ref, dfg_ref, db_ref, carry):
        @pl.when(pl.program_id(0) == 0)
        def _():
            carry[...] = jnp.zeros_like(carry)
            db_ref[...] = jnp.zeros_like(db_ref)

        d_log_f = jnp.dot(_triangle(False), dc_ref[...], precision=lax.Precision.HIGHEST,
                          preferred_element_type=F32) + carry[0:1, :]
        carry[...] = jnp.broadcast_to(d_log_f[0:1, :], carry.shape)
        dz = d_log_f * jax.nn.sigmoid(-(fg_ref[...] + b_ref[...]))
        dfg_ref[...] = dz.astype(BF16)
        db_ref[...] += jnp.sum(dz, axis=0, keepdims=True)

    row = pl.BlockSpec((LANES, LANES), lambda i: (nt - 1 - i, 0))
    vec = pl.BlockSpec((1, LANES), lambda i: (0, 0))
    return pl.pallas_call(
        body, name=name, grid=(nt,),
        in_specs=[row, row, vec], out_specs=[row, vec],
        out_shape=[_sds((t, LANES), BF16), _sds((1, LANES), F32)],
        scratch_shapes=[pltpu.VMEM((SUBLANES, LANES), F32)],
        compiler_params=_params(("arbitrary",)),
    )(dcum, fg, b)


def _causal(qi, kj, tq):
    rows = qi * tq + lax.broadcasted_iota(jnp.int32, (tq, tq), 0)
    cols = kj * tq + lax.broadcasted_iota(jnp.int32, (tq, tq), 1)
    return cols <= rows


NT_DIMS = (((1,), (1,)), ((), ()))
TN_DIMS = (((0,), (0,)), ((), ()))


def _attn_fwd(q, k, v, cum_col, cum_row, name, job=None, mid_at=0.5):
    t, aw = q.shape
    heads = aw // HEAD_DIM
    tq = _tile(t, (384, 256, 128))
    nq = t // tq
    scale = HEAD_DIM ** -0.5

    def body(q_ref, k_ref, v_ref, cq_ref, ck_ref, o_ref, of_ref, lse_ref, m_s, l_s, acc_s, res_s):
        qi, kj = pl.program_id(1), pl.program_id(2)

        @pl.when(kj == 0)
        def _():
            m_s[...] = jnp.full_like(m_s, -jnp.inf)
            l_s[...] = jnp.zeros_like(l_s)
            acc_s[...] = jnp.zeros_like(acc_s)
            res_s[...] = jnp.zeros_like(res_s)

        @pl.when(kj <= qi)
        def _():
            s = lax.dot_general(q_ref[...], k_ref[...], NT_DIMS, preferred_element_type=F32) * scale
            s = s + cq_ref[...] - ck_ref[...]
            s = jnp.where(_causal(qi, kj, tq), s, -jnp.inf)
            m_prev = m_s[...]
            m_new = jnp.maximum(m_prev, jnp.max(s, axis=-1, keepdims=True))
            alpha = jnp.exp(m_prev - m_new)
            p = jnp.exp(s - m_new)
            l_s[...] = alpha * l_s[...] + jnp.sum(p, axis=-1, keepdims=True)
            p_hi = p.astype(BF16)
            p_lo = (p - p_hi.astype(F32)).astype(BF16)
            acc_s[...] = alpha * acc_s[...] + jnp.dot(p_hi, v_ref[...], preferred_element_type=F32)
            res_s[...] = alpha * res_s[...] + jnp.dot(p_lo, v_ref[...], preferred_element_type=F32)
            m_s[...] = m_new

        @pl.when(kj == qi)
        def _():
            o_ref[...] = acc_s[...] / l_s[...]
            of_ref[...] = (acc_s[...] + res_s[...]) / l_s[...]
            lse_ref[...] = m_s[...] + jnp.log(l_s[...])

    q_spec = pl.BlockSpec((tq, HEAD_DIM), lambda h, i, j: (i, h))
    kv_spec = pl.BlockSpec((tq, HEAD_DIM), lambda h, i, j: (jnp.minimum(i, j), h))
    col = pl.BlockSpec((None, tq, 1), lambda h, i, j: (h, i, 0))
    return _call(
        body, name=name, grid=(heads, nq, nq),
        in_specs=[q_spec, kv_spec, kv_spec, col,
                  pl.BlockSpec((None, 1, tq), lambda h, i, j: (h, 0, jnp.minimum(i, j)))],
        out_specs=[q_spec, q_spec, col],
        out_shape=[_sds((t, aw), F32), _sds((t, aw), F32), _sds((heads, t, 1), F32)],
        scratch_shapes=[pltpu.VMEM((tq, 1), F32), pltpu.VMEM((tq, 1), F32), pltpu.VMEM((tq, HEAD_DIM), F32),
                        pltpu.VMEM((tq, HEAD_DIM), F32)],
        semantics=("parallel", "parallel", "arbitrary"),
        operands=(q, k, v, cum_col, cum_row), job=job, mid_at=mid_at)


def _attn_stats(do, o, cum_col, lse, name):
    t, aw = o.shape
    heads = aw // HEAD_DIM
    tr = _tile(t, (384, 256, 128))

    def body(do_ref, o_ref, cq_ref, lse_ref, delta_ref, crow_ref):
        for h in range(heads):
            sl = slice(h * HEAD_DIM, (h + 1) * HEAD_DIM)
            do_seen = do_ref[:, sl].astype(BF16).astype(F32)
            delta_ref[h] = jnp.sum(do_seen * o_ref[:, sl], axis=-1, keepdims=True)
        crow_ref[...] = cq_ref[...] - lse_ref[...]

    row = pl.BlockSpec((tr, aw), lambda i: (i, 0))
    col = pl.BlockSpec((heads, tr, 1), lambda i: (0, i, 0))
    return pl.pallas_call(
        body, name=name, grid=(t // tr,),
        in_specs=[row, row, col, col], out_specs=[col, col],
        out_shape=[_sds((heads, t, 1), F32)] * 2, compiler_params=_params(("parallel",)),
    )(do, o, cum_col, lse)


def _attn_bwd(q, k, v, do, crow, delta, cum_row, name, job=None):
    t, aw = q.shape
    heads = aw // HEAD_DIM
    tq = _tile(t, (384, 256, 128))
    nq = t // tq
    scale = HEAD_DIM ** -0.5

    def body(q_ref, k_ref, v_ref, do_ref, crow_ref, delta_ref, ck_ref,
             dq_ref, dk_ref, dv_ref, dck_ref, dk_s, dv_s, dck_s):
        kj, qi = pl.program_id(1), pl.program_id(2)

        @pl.when((kj == 0) & (qi == 0))
        def _():
            dq_ref[...] = jnp.zeros_like(dq_ref)

        @pl.when(qi == 0)
        def _():
            dk_s[...] = jnp.zeros_like(dk_s)
            dv_s[...] = jnp.zeros_like(dv_s)
            dck_s[...] = jnp.zeros_like(dck_s)

        @pl.when(qi >= kj)
        def _():
            qv, kv, dov = q_ref[...], k_ref[...], do_ref[...].astype(BF16)
            s = lax.dot_general(qv, kv, NT_DIMS, preferred_element_type=F32) * scale
            s = s + crow_ref[...] - ck_ref[...]
            p = jnp.where(_causal(qi, kj, tq), jnp.exp(s), 0.0)
            dp = lax.dot_general(dov, v_ref[...], NT_DIMS, preferred_element_type=F32)
            ds = p * (dp - delta_ref[...])
            dsb = ds.astype(BF16)
            dv_s[...] += lax.dot_general(p.astype(BF16), dov, TN_DIMS, preferred_element_type=F32)
            dk_s[...] += lax.dot_general(dsb, qv, TN_DIMS, preferred_element_type=F32)
            rows = pl.ds(pl.multiple_of(qi * tq, tq), tq)
            dq_ref[rows, :] += jnp.dot(dsb, kv, preferred_element_type=F32) * scale
            dck_s[...] += jnp.sum(ds, axis=0, keepdims=True)

        @pl.when(qi == nq - 1)
        def _():
            dk_ref[...] = dk_s[...] * scale
            dv_ref[...] = dv_s[...].astype(dv_ref.dtype)
            dck_ref[...] = -dck_s[...]

    q_spec = pl.BlockSpec((tq, HEAD_DIM), lambda h, j, i: (jnp.maximum(i, j), h))
    k_spec = pl.BlockSpec((tq, HEAD_DIM), lambda h, j, i: (j, h))
    col = pl.BlockSpec((None, tq, 1), lambda h, j, i: (h, jnp.maximum(i, j), 0))
    row = pl.BlockSpec((None, 1, tq), lambda h, j, i: (h, 0, j))
    return _call(
        body, name=name, grid=(heads, nq, nq),
        in_specs=[q_spec, k_spec, k_spec, q_spec, col, col, row],
        out_specs=[pl.BlockSpec((t, HEAD_DIM), lambda h, j, i: (0, h)), k_spec, k_spec, row],
        out_shape=[_sds((t, aw), F32), _sds((t, aw), F32), _sds((t, aw), BF16), _sds((heads, 1, t), F32)],
        scratch_shapes=[pltpu.VMEM((tq, HEAD_DIM), F32), pltpu.VMEM((tq, HEAD_DIM), F32), pltpu.VMEM((1, tq), F32)],
        semantics=("parallel", "arbitrary", "arbitrary"),
        operands=(q, k, v, do, crow, delta, cum_row), job=job)


def _attn_tile():
    return (384, 256, 128)


def _attn_fwd(q, k, v, cum_row, name, job=None, mid_at=0.5):
    t, aw = q.shape
    heads = aw // HEAD_DIM
    tq = _tile(t, _attn_tile())
    nq = t // tq
    scale = HEAD_DIM ** -0.5

    def body(q_ref, k_ref, v_ref, ck_ref, o_ref, of_ref, lse_ref):
        qi = pl.program_id(1)
        qv = q_ref[...]

        def tile(kj, carry, masked):
            m_prev, l_prev, acc, res = carry
            ks = pl.ds(pl.multiple_of(kj * tq, tq), tq)
            s = lax.dot_general(qv, k_ref[ks, :], NT_DIMS, preferred_element_type=F32) * scale - ck_ref[kj]
            if masked:
                s = jnp.where(_causal(0, 0, tq), s, -jnp.inf)
            m_new = jnp.maximum(m_prev, jnp.max(s, axis=-1, keepdims=True))
            alpha = jnp.exp(m_prev - m_new)
            p = jnp.exp(s - m_new)
            p_hi = p.astype(BF16)
            p_lo = (p - p_hi.astype(F32)).astype(BF16)
            vv = v_ref[ks, :]
            return (m_new, alpha * l_prev + jnp.sum(p, axis=-1, keepdims=True),
                    alpha * acc + jnp.dot(p_hi, vv, preferred_element_type=F32),
                    alpha * res + jnp.dot(p_lo, vv, preferred_element_type=F32))

        init = (jnp.full((tq, 1), -jnp.inf, F32), jnp.zeros((tq, 1), F32),
                jnp.zeros((tq, HEAD_DIM), F32), jnp.zeros((tq, HEAD_DIM), F32))
        carry = lax.fori_loop(0, qi, lambda kj, c: tile(kj, c, False), init)
        m_fin, l_fin, acc, res = tile(qi, carry, True)
        o_ref[...] = acc / l_fin
        of_ref[...] = (acc + res) / l_fin
        lse_ref[...] = m_fin + jnp.log(l_fin)

    q_spec = pl.BlockSpec((tq, HEAD_DIM), lambda h, i: (i, h))
    head = pl.BlockSpec((t, HEAD_DIM), lambda h, i: (0, h))
    return _call(
        body, name=name, grid=(heads, nq),
        in_specs=[q_spec, head, head, pl.BlockSpec((None, nq, 1, tq), lambda h, i: (h, 0, 0, 0))],
        out_specs=[q_spec, q_spec, pl.BlockSpec((None, tq, 1), lambda h, i: (h, i, 0))],
        out_shape=[_sds((t, aw), F32), _sds((t, aw), F32), _sds((heads, t, 1), F32)],
        scratch_shapes=[], semantics=("parallel", "arbitrary"),
        operands=(q, k, v, cum_row), job=job, mid_at=mid_at)


def _attn_stats(do, o, name):
    t, aw = o.shape
    heads = aw // HEAD_DIM
    tr = _tile(t, (384, 256, 128))

    def body(do_ref, o_ref, delta_ref):
        for h in range(heads):
            sl = slice(h * HEAD_DIM, (h + 1) * HEAD_DIM)
            do_seen = do_ref[:, sl].astype(BF16).astype(F32)
            delta_ref[h] = jnp.sum(do_seen * o_ref[:, sl], axis=-1, keepdims=True)

    row = pl.BlockSpec((tr, aw), lambda i: (i, 0))
    return pl.pallas_call(
        body, name=name, grid=(t // tr,),
        in_specs=[row, row], out_specs=pl.BlockSpec((heads, tr, 1), lambda i: (0, i, 0)),
        out_shape=_sds((heads, t, 1), F32), compiler_params=_params(("parallel",)),
    )(do, o)


def _attn_bwd(q, k, v, do, lse, delta, cum_row, name, job=None):
    t, aw = q.shape
    heads = aw // HEAD_DIM
    tq = _tile(t, _attn_tile())
    nq = t // tq
    scale = HEAD_DIM ** -0.5

    def body(q_ref, k_ref, v_ref, do_ref, lse_ref, delta_ref, ck_ref, dq_ref, dk_ref, dv_ref, dck_ref):
        kj = pl.program_id(1)

        @pl.when(kj == 0)
        def _():
            dq_ref[...] = jnp.zeros_like(dq_ref)

        kv, vv, ck = k_ref[...], v_ref[...], ck_ref[...]

        def tile(qi, carry, masked):
            dk_acc, dv_acc, dck_acc = carry
            rows = pl.ds(pl.multiple_of(qi * tq, tq), tq)
            qv, dov = q_ref[rows, :], do_ref[rows, :].astype(BF16)
            s = lax.dot_general(qv, kv, NT_DIMS, preferred_element_type=F32) * scale - ck - lse_ref[rows, :]
            p = jnp.exp(s)
            if masked:
                p = jnp.where(_causal(0, 0, tq), p, 0.0)
            dp = lax.dot_general(dov, vv, NT_DIMS, preferred_element_type=F32)
            ds = p * (dp - delta_ref[rows, :])
            dsb = ds.astype(BF16)
            dq_ref[rows, :] += jnp.dot(dsb, kv, preferred_element_type=F32) * scale
            return (dk_acc + lax.dot_general(dsb, qv, TN_DIMS, preferred_element_type=F32),
                    dv_acc + lax.dot_general(p.astype(BF16), dov, TN_DIMS, preferred_element_type=F32),
                    dck_acc + jnp.sum(ds, axis=0, keepdims=True))

        init = (jnp.zeros((tq, HEAD_DIM), F32), jnp.zeros((tq, HEAD_DIM), F32), jnp.zeros((1, tq), F32))
        carry = tile(kj, init, True)
        dk_acc, dv_acc, dck_acc = lax.fori_loop(kj + 1, nq, lambda qi, c: tile(qi, c, False), carry)
        dk_ref[...] = dk_acc * scale
        dv_ref[...] = dv_acc.astype(dv_ref.dtype)
        dck_ref[...] = -dck_acc

    head = pl.BlockSpec((t, HEAD_DIM), lambda h, j: (0, h))
    k_spec = pl.BlockSpec((tq, HEAD_DIM), lambda h, j: (j, h))
    col = pl.BlockSpec((None, t, 1), lambda h, j: (h, 0, 0))
    row = pl.BlockSpec((None, 1, tq), lambda h, j: (h, 0, j))
    return _call(
        body, name=name, grid=(heads, nq),
        in_specs=[head, k_spec, k_spec, head, col, col, row],
        out_specs=[head, k_spec, k_spec, row],
        out_shape=[_sds((t, aw), F32), _sds((t, aw), F32), _sds((t, aw), BF16), _sds((heads, 1, t), F32)],
        scratch_shapes=[], semantics=("parallel", "arbitrary"),
        operands=(q, k, v, do, lse, delta, cum_row), job=job)


def _shift_down(u, by):
    rows = lax.broadcasted_iota(jnp.int32, u.shape, 0)
    return jnp.where(rows >= by, pltpu.roll(u, by, 0), 0.0)


def _shift_up(u, by):
    t = u.shape[0]
    rows = lax.broadcasted_iota(jnp.int32, u.shape, 0)
    return jnp.where(rows < t - by, pltpu.roll(u, t - by, 0), 0.0)


def _conv_specs(t, off_b, cw_width):
    nb = cw_width // LANES
    base = off_b // LANES
    return [pl.BlockSpec((t, LANES), lambda j, s=s: (0, base + s * nb + j)) for s in range(3)]


def _conv_fwd(proj, cw, off_b, name):
    t = proj.shape[0]
    width = cw.shape[1]

    def body(cb_ref, cc_ref, cx_ref, w_ref, o_ref):
        u = cc_ref[...] * cx_ref[...]
        y = w_ref[0:1, :] * _shift_down(u, 2) + w_ref[1:2, :] * _shift_down(u, 1) + w_ref[2:3, :] * u
        o_ref[...] = (cb_ref[...] * y).astype(BF16)

    return pl.pallas_call(
        body, name=name, grid=(width // LANES,),
        in_specs=_conv_specs(t, off_b, width) + [pl.BlockSpec((SUBLANES, LANES), lambda j: (0, j))],
        out_specs=pl.BlockSpec((t, LANES), lambda j: (0, j)),
        out_shape=_sds((t, width), BF16), compiler_params=_params(("parallel",)),
    )(proj, proj, proj, cw)


def _conv_bwd(dcp, proj, cw, off_b, name):
    t = proj.shape[0]
    width = cw.shape[1]

    def body(d_ref, cb_ref, cc_ref, cx_ref, w_ref, dcb_ref, dcc_ref, dcx_ref, dw_ref):
        cc, cx = cc_ref[...], cx_ref[...]
        u = cc * cx
        u1, u2 = _shift_down(u, 1), _shift_down(u, 2)
        w0, w1, w2 = w_ref[0:1, :], w_ref[1:2, :], w_ref[2:3, :]
        d = d_ref[...]
        dcb_ref[...] = (d * (w0 * u2 + w1 * u1 + w2 * u)).astype(BF16)
        dy = d * cb_ref[...]
        du = w2 * dy + w1 * _shift_up(dy, 1) + w0 * _shift_up(dy, 2)
        dcc_ref[...] = (du * cx).astype(BF16)
        dcx_ref[...] = (du * cc).astype(BF16)
        dw = [jnp.sum(dy * s, axis=0, keepdims=True) for s in (u2, u1, u)]
        dw_ref[...] = jnp.concatenate(dw + [jnp.zeros((SUBLANES - 3, LANES), F32)], axis=0)

    col = pl.BlockSpec((t, LANES), lambda j: (0, j))
    wspec = pl.BlockSpec((SUBLANES, LANES), lambda j: (0, j))
    return pl.pallas_call(
        body, name=name, grid=(width // LANES,),
        in_specs=[col] + _conv_specs(t, off_b, width) + [wspec],
        out_specs=[col, col, col, wspec],
        out_shape=[_sds((t, width), BF16)] * 3 + [_sds((SUBLANES, width), F32)],
        compiler_params=_params(("parallel",)),
    )(dcp, proj, proj, proj, cw)


def _gate_specs(t, d, off_g, tr, tc, rows_first):
    nb = d // tc
    base = off_g // tc
    if rows_first:
        tile = lambda s: pl.BlockSpec((tr, tc), lambda i, j: (i, base + s * nb + j))
        vec = lambda s: pl.BlockSpec((1, tc), lambda i, j: (0, s * nb + j))
        plain = pl.BlockSpec((tr, tc), lambda i, j: (i, j))
    else:
        tile = lambda s: pl.BlockSpec((tr, tc), lambda j, i: (i, base + s * nb + j))
        vec = lambda s: pl.BlockSpec((1, tc), lambda j, i: (0, s * nb + j))
        plain = pl.BlockSpec((tr, tc), lambda j, i: (i, j))
    return tile, vec, plain


def _gate_fwd(a, c, proj, bg, off_g, name):
    t, d = a.shape
    tr, tc = _tile(t, (384, 256, 128)), _tile(d, (512, 256, 128))
    tile, vec, plain = _gate_specs(t, d, off_g, tr, tc, True)

    def body(a_ref, c_ref, g0_ref, g1_ref, b0_ref, b1_ref, o_ref):
        g0 = jax.nn.sigmoid(g0_ref[...] + b0_ref[...])
        g1 = jax.nn.sigmoid(g1_ref[...] + b1_ref[...])
        o_ref[...] = (g0 * a_ref[...] + g1 * c_ref[...]).astype(BF16)

    return pl.pallas_call(
        body, name=name, grid=(t // tr, d // tc),
        in_specs=[plain, plain, tile(0), tile(1), vec(0), vec(1)], out_specs=plain,
        out_shape=_sds((t, d), BF16), compiler_params=_params(("parallel", "parallel")),
    )(a, c, proj, proj, bg, bg)


def _gate_bwd(dm, a, c, proj, bg, off_g, name):
    t, d = a.shape
    tr, tc = _tile(t, (384, 256, 128)), _tile(d, (512, 256, 128))
    tile, vec, plain = _gate_specs(t, d, off_g, tr, tc, False)

    def body(dm_ref, a_ref, c_ref, g0_ref, g1_ref, b0_ref, b1_ref,
             da_ref, dc_ref, dg0_ref, dg1_ref, db0_ref, db1_ref):
        @pl.when(pl.program_id(1) == 0)
        def _():
            db0_ref[...] = jnp.zeros_like(db0_ref)
            db1_ref[...] = jnp.zeros_like(db1_ref)

        dm = dm_ref[...]
        g0 = jax.nn.sigmoid(g0_ref[...] + b0_ref[...])
        g1 = jax.nn.sigmoid(g1_ref[...] + b1_ref[...])
        da_ref[...] = (dm * g0).astype(BF16)
        dc_ref[...] = (dm * g1).astype(BF16)
        dz0 = dm * a_ref[...] * (g0 * (1.0 - g0))
        dz1 = dm * c_ref[...] * (g1 * (1.0 - g1))
        dg0_ref[...] = dz0.astype(BF16)
        dg1_ref[...] = dz1.astype(BF16)
        db0_ref[...] += jnp.sum(dz0, axis=0, keepdims=True)
        db1_ref[...] += jnp.sum(dz1, axis=0, keepdims=True)

    bvec = pl.BlockSpec((1, tc), lambda j, i: (0, j))
    return pl.pallas_call(
        body, name=name, grid=(d // tc, t // tr),
        in_specs=[plain, plain, plain, tile(0), tile(1), vec(0), vec(1)],
        out_specs=[plain] * 4 + [bvec, bvec],
        out_shape=[_sds((t, d), BF16)] * 4 + [_sds((1, d), F32)] * 2,
        compiler_params=_params(("parallel", "arbitrary")),
    )(dm, a, c, proj, proj, bg, bg)


def _sum_squares(x, name):
    t, d = x.shape
    tr = _tile(t, (384, 256, 128))

    def body(x_ref, o_ref):
        @pl.when(pl.program_id(0) == 0)
        def _():
            o_ref[...] = jnp.zeros_like(o_ref)

        v = x_ref[...]
        o_ref[...] += jnp.sum(jnp.sum(v * v, axis=0, keepdims=True), axis=1, keepdims=True)

    return pl.pallas_call(
        body, name=name, grid=(t // tr,),
        in_specs=[pl.BlockSpec((tr, d), lambda i: (i, 0))],
        out_specs=pl.BlockSpec((1, LANES), lambda i: (0, 0)),
        out_shape=_sds((1, LANES), F32), compiler_params=_params(("arbitrary",)),
    )(x)


def _row_tile(r, c):
    return r if r * c <= 128 * 1024 else _tile(r, (128, 64, 32, 16))


def _sum_parts(parts, name):
    n, r, c = parts.shape
    tr = _row_tile(r, c)

    def body(p_ref, o_ref):
        acc = p_ref[0].astype(F32)
        for i in range(1, n):
            acc = acc + p_ref[i].astype(F32)
        o_ref[...] = acc

    return pl.pallas_call(
        body, name=name, grid=(r // tr,),
        in_specs=[pl.BlockSpec((n, tr, c), lambda i: (0, i, 0))],
        out_specs=pl.BlockSpec((tr, c), lambda i: (i, 0)),
        out_shape=_sds((r, c), F32), compiler_params=_params(("parallel",)),
    )(parts)


def _adamw(chunks, w, m, v, name):
    n, rc, c = chunks[0].shape
    r = rc * len(chunks)
    tr = _row_tile(rc, c)
    per = rc // tr

    def body(*refs):
        p_refs = refs[:len(chunks)]
        w_ref, m_ref, v_ref, g_ref, d_ref, nm_ref, nv_ref = refs[len(chunks):]
        i = pl.program_id(0)

        def update(p_ref):
            g = p_ref[0].astype(F32)
            for s in range(1, n):
                g = g + p_ref[s].astype(F32)
            nm = ADAM_B1 * m_ref[...] + (1.0 - ADAM_B1) * g
            nv = ADAM_B2 * v_ref[...] + (1.0 - ADAM_B2) * (g * g)
            m_hat = nm / (1.0 - ADAM_B1 ** ADAM_STEP)
            v_hat = nv / (1.0 - ADAM_B2 ** ADAM_STEP)
            g_ref[...] = g
            d_ref[...] = -ADAM_LR * (m_hat / (jnp.sqrt(v_hat) + ADAM_EPS) + ADAM_WD * w_ref[...])
            nm_ref[...] = nm
            nv_ref[...] = nv

        if len(chunks) == 1:
            update(p_refs[0])
        else:
            for ci, p_ref in enumerate(p_refs):
                pl.when((i >= ci * per) & (i < (ci + 1) * per))(functools.partial(update, p_ref))

    row = pl.BlockSpec((tr, c), lambda i: (i, 0))
    part_specs = [pl.BlockSpec((n, tr, c), lambda i, ci=ci: (0, jnp.clip(i - ci * per, 0, per - 1), 0))
                  for ci in range(len(chunks))]
    return pl.pallas_call(
        body, name=name, grid=(r // tr,),
        in_specs=part_specs + [row, row, row],
        out_specs=[row] * 4, out_shape=[_sds((r, c), F32)] * 4,
        compiler_params=_params(("parallel",)),
    )(*chunks, w, m, v)


def _adamw_cols(chunks, w, m, v, name):
    n, r, _ = chunks[0].shape
    widths = [ch.shape[2] for ch in chunks]
    tc = min([LANES] + widths)
    firsts = [sum(widths[:ci]) // tc for ci in range(len(chunks) + 1)]

    def body(*refs):
        p_refs = refs[:len(chunks)]
        w_ref, m_ref, v_ref, g_ref, d_ref, nm_ref, nv_ref = refs[len(chunks):]
        j = pl.program_id(0)

        def update(p_ref):
            g = p_ref[0].astype(F32)
            for s in range(1, n):
                g = g + p_ref[s].astype(F32)
            nm = ADAM_B1 * m_ref[...] + (1.0 - ADAM_B1) * g
            nv = ADAM_B2 * v_ref[...] + (1.0 - ADAM_B2) * (g * g)
            m_hat = nm / (1.0 - ADAM_B1 ** ADAM_STEP)
            v_hat = nv / (1.0 - ADAM_B2 ** ADAM_STEP)
            g_ref[...] = g
            d_ref[...] = -ADAM_LR * (m_hat / (jnp.sqrt(v_hat) + ADAM_EPS) + ADAM_WD * w_ref[...])
            nm_ref[...] = nm
            nv_ref[...] = nv

        for ci, p_ref in enumerate(p_refs):
            pl.when((j >= firsts[ci]) & (j < firsts[ci + 1]))(functools.partial(update, p_ref))

    col = pl.BlockSpec((r, tc), lambda j: (0, j))
    part_specs = [pl.BlockSpec((n, r, tc),
                               lambda j, lo=firsts[ci], hi=firsts[ci + 1]: (0, 0, jnp.clip(j - lo, 0, hi - lo - 1)))
                  for ci in range(len(chunks))]
    return pl.pallas_call(
        body, name=name, grid=(firsts[-1],),
        in_specs=part_specs + [col, col, col],
        out_specs=[col] * 4, out_shape=[_sds((r, firsts[-1] * tc), F32)] * 4,
        compiler_params=_params(("parallel",)),
    )(*chunks, w, m, v)


def _pad_lanes(a, width=LANES):
    return jnp.pad(a, ((0, 0), (0, width - a.shape[1])))


def _rows_of(a):
    flat = a.reshape(-1)
    n = -(-flat.shape[0] // LANES) * LANES
    return jnp.pad(flat, (0, n - flat.shape[0])).reshape(-1, LANES)


def _columns_to_slots(full, n_rows):
    return full.reshape(n_rows, N_DEV, -1).transpose(1, 0, 2)


def _slots_to_columns(slots):
    return slots.transpose(1, 0, 2).reshape(slots.shape[1], -1)


def kernel(x, meta_tokens, norm_mix, w_in, b_fgate, b_gate, q_norm, k_norm, conv_w, w_attn_out, w_conv_out, w_o, norm_mlp, w_up, w_down, loss_target, m_meta_tokens, m_norm_mix, m_w_in, m_b_fgate, m_b_gate, m_q_norm, m_k_norm, m_conv_w, m_w_attn_out, m_w_conv_out, m_w_o, m_norm_mlp, m_w_up, m_w_down, v_meta_tokens, v_norm_mix, v_w_in, v_b_fgate, v_b_gate, v_q_norm, v_k_norm, v_conv_w, v_w_attn_out, v_w_conv_out, v_w_o, v_norm_mlp, v_w_up, v_w_down):
    seq, d = x.shape[1], x.shape[2]
    heads = b_fgate.shape[1]
    aw = heads * HEAD_DIM
    cwid = conv_w.shape[2] * N_DEV
    dff = w_up.shape[2] * N_DEV
    n_valid = N_META + seq
    t = -(-n_valid // LANES) * LANES
    me = _flat(*_my_place())
    off_cb, off_gl = 3 * aw, 3 * aw + 3 * cwid

    conv_shard = jnp.pad(conv_w[0], ((0, SUBLANES - conv_w.shape[1]), (0, 0)))
    w_in_t, m_in_t, v_in_t = (jnp.swapaxes(p, 1, 2)[0] for p in (w_in, m_w_in, v_w_in))
    g_in, g_meta, g_cw = _run_job(_Gather([w_in_t.astype(BF16), meta_tokens, conv_shard]), "gather_first")
    w_all_t = g_in.reshape(-1, d)
    w_main_t = jnp.concatenate([w_all_t[:3 * aw], w_all_t[3 * aw + heads:]], axis=0)
    w_fg_t = jnp.pad(w_all_t[3 * aw:3 * aw + heads], ((0, LANES - heads), (0, 0)))
    meta_full, cw_full = _slots_to_columns(g_meta), _slots_to_columns(g_cw)

    pad_rows = t - n_valid
    h0 = jnp.concatenate([meta_full, x[0], jnp.zeros((pad_rows, d), F32)], axis=0)
    target = jnp.concatenate([jnp.zeros((N_META, d), F32), loss_target[0], jnp.zeros((pad_rows, d), F32)], axis=0)
    b_f = _pad_lanes(b_fgate)

    xn = _rmsnorm_fwd(h0, norm_mix, "norm_mix_fwd")
    proj, (g_ao, g_co, g_o) = _matmul(
        xn, w_main_t, name="in_proj", trans_b=True, mid_at=0.6,
        job=_Gather([w_attn_out[0].astype(BF16), w_conv_out[0].astype(BF16), w_o[0].astype(BF16)]))
    w_ao, w_co, w_o_f = _slots_to_columns(g_ao), _slots_to_columns(g_co), g_o.reshape(d, d)
    fg = _matmul(xn, w_fg_t, name="in_proj_fgate", trans_b=True)
    qn, kn, vb = _qk_prep(proj, q_norm, k_norm, aw, "qk_norm_fwd")
    cum = _forget_fwd(fg, b_f, "forget_cumsum")
    cum_heads = cum[:, :heads].T
    cum_row = cum_heads[:, None, :]
    t_attn = _tile(t, _attn_tile())
    (o, o_fine, lse), (g_up,) = _attn_fwd(
        qn, kn, vb, cum_heads.reshape(heads, t // t_attn, 1, t_attn), "attention_fwd", mid_at=0.7,
        job=_Gather([w_up[0].astype(BF16)]))
    w_up_f = _slots_to_columns(g_up)
    a = _matmul(o, w_ao, name="attn_out_proj")
    cpre = _conv_fwd(proj, cw_full, off_cb, "short_conv_fwd")
    c = _matmul(cpre, w_co, name="conv_out_proj")
    merged = _gate_fwd(a, c, proj, b_gate, off_gl, "gate_merge_fwd")
    h1 = _matmul(merged, w_o_f, name="out_proj", extras=(h0,), epilogue=lambda acc, i, j, r: (r + acc,))
    hn = _rmsnorm_fwd(h1, norm_mlp, "norm_mlp_fwd")
    (z, u), (g_down,) = _matmul(hn, w_up_f, name="mlp_up", out_dtypes=(F32, BF16), mid_at=0.8,
                                epilogue=lambda acc, i, j: (acc, jnp.square(jnp.maximum(acc, 0.0))),
                                job=_Gather([w_down[0].astype(BF16)]))
    w_down_f = g_down.reshape(dff, d)

    tm_down = _tile(t, (1408, 1024, 512, 256, 128))

    def loss_grad(acc, i, j, h1_tile, tgt_tile):
        rows = i * tm_down + lax.broadcasted_iota(jnp.int32, acc.shape, 0)
        valid = (rows >= N_META) & (rows < n_valid)
        dy = jnp.where(valid, ((h1_tile + acc) - tgt_tile) / d, 0.0)
        return dy, dy

    dh2, dh2b = _matmul(u, w_down_f, name="mlp_down_loss", extras=(h1, target), epilogue=loss_grad,
                        out_dtypes=(F32, BF16), tm=tm_down)
    loss_part = _sum_squares(dh2, "loss_sum") * (0.5 * d)

    wide = lambda n_cols: _tile(n_cols, (1024, 512, 256, 128))
    dw_down = _matmul(u, dh2b, name="mlp_down_wgrad", trans_a=True, tn=wide(d), out_dtypes=(BF16,))
    s_down = dw_down.reshape(N_DEV, dff // N_DEV, d)
    half_down = dff // N_DEV // 2
    dz, l_down0 = _matmul(dh2b, w_down_f, name="mlp_down_bwd", trans_b=True, extras=(z,), out_dtypes=(BF16,),
                          epilogue=lambda acc, i, j, zt: (acc * (2.0 * jnp.maximum(zt, 0.0)),),
                          job=_Scatter([(s_down, 0, half_down)]))
    s_up, l_down1 = _matmul(hn, dz, name="mlp_up_wgrad", trans_a=True, slots=True, out_dtypes=(BF16,),
                            tn=wide(dff // N_DEV), job=_Scatter([(s_down, half_down, half_down)]))
    dhn, l_up0 = _matmul(dz, w_up_f, name="mlp_up_bwd", trans_b=True, tn=wide(d),
                         job=_Scatter([(s_up, 0, d // 2)]))
    dh1, dh1b, dg_mlp = _rmsnorm_bwd(h1, dhn, norm_mlp, dh2, "norm_mlp_bwd")
    dmerged = _matmul(dh1b, w_o_f, name="out_proj_bwd", trans_b=True)
    dw_o = _matmul(merged, dh1b, name="out_proj_wgrad", trans_a=True, tn=wide(d), out_dtypes=(BF16,))
    da, dc, dgl0, dgl1, dbg0, dbg1 = _gate_bwd(dmerged, a, c, proj, b_gate, off_gl, "gate_merge_bwd")
    do = _matmul(da, w_ao, name="attn_out_bwd", trans_b=True)
    s_ao = _matmul(o, da, name="attn_out_wgrad", trans_a=True, slots=True, out_dtypes=(BF16,))
    dcp = _matmul(dc, w_co, name="conv_out_bwd", trans_b=True)
    s_co = _matmul(cpre, dc, name="conv_out_wgrad", trans_a=True, slots=True, out_dtypes=(BF16,))
    dcb, dcc, dcx, dcw = _conv_bwd(dcp, proj, cw_full, off_cb, "short_conv_bwd")
    delta = _attn_stats(do, o_fine, "attention_stats")
    (dqn, dkn, dv, dck), (l_up1, l_o, l_ao, l_co) = _attn_bwd(
        qn, kn, vb, do, lse, delta, cum_row, "attention_bwd",
        job=_Scatter([(s_up, d // 2, d // 2), dw_o.reshape(N_DEV, d // N_DEV, d), s_ao, s_co]))
    dq_raw, dk_raw, dg_q, dg_k = _qk_bwd(dqn, dkn, proj, q_norm, k_norm, aw, "qk_norm_bwd")
    dcum = _pad_lanes(dck.reshape(heads, t).T)
    dfg, db_f = _forget_bwd(dcum, fg, b_f, "forget_bwd")
    dproj = jnp.concatenate([dq_raw, dk_raw, dv, dcb, dcc, dcx, dgl0, dgl1], axis=1)
    dwt_fg = _matmul(dfg, xn, name="in_proj_fgate_wgrad", trans_a=True, out_dtypes=(BF16,))
    quarter = d // 4

    def in_slots(dwt, first):
        width = dwt.shape[1]
        full = jnp.concatenate([dwt[:3 * aw], dwt_fg[:heads, first:first + width], dwt[3 * aw:]], axis=0)
        return full.reshape(N_DEV, -1, width)

    def in_wgrad(idx, first_block, n_blocks, job):
        return _matmul(dproj, xn, name="in_proj_wgrad_%d" % idx, trans_a=True, tn=quarter,
                       cols=(first_block, n_blocks), out_dtypes=(BF16,), job=job)

    dwt0 = in_wgrad(0, 0, 1, None)
    dwt1, l_in0 = in_wgrad(1, 1, 1, _Scatter([in_slots(dwt0, 0)]))
    dwt2, l_in1 = in_wgrad(2, 2, 2, _Scatter([in_slots(dwt1, quarter)]))
    dxn_fg = _matmul(dfg, w_fg_t, name="in_proj_fgate_bwd")
    dxn, l_in2 = _matmul(dproj, w_main_t, name="in_proj_bwd", extras=(dxn_fg,),
                         epilogue=lambda acc, i, j, r: (r + acc,), job=_Scatter([in_slots(dwt2, 2 * quarter)]))
    dh0, _, dg_mix = _rmsnorm_bwd(h0, dxn, norm_mix, dh1, "norm_mix_bwd")

    small = [dg_mix, dbg0, dbg1, dg_mlp, dg_q, dg_k, db_f, loss_part, dcw, dh0[:N_META]]
    small_rows = [_rows_of(s) for s in small]
    pack = jnp.concatenate(small_rows, axis=0)
    pack = jnp.pad(pack, ((0, -pack.shape[0] % SUBLANES), (0, 0)))
    (pack_all,) = _run_job(_Scatter([], [pack]), "gather_small")

    landed = {"w_attn_out": [l_ao], "w_conv_out": [l_co], "w_o": [l_o],
              "w_up": l_up0 + [l_up1], "w_down": l_down0 + l_down1}
    shards = {"w_attn_out": (w_attn_out, m_w_attn_out, v_w_attn_out),
              "w_conv_out": (w_conv_out, m_w_conv_out, v_w_conv_out), "w_o": (w_o, m_w_o, v_w_o),
              "w_up": (w_up, m_w_up, v_w_up), "w_down": (w_down, m_w_down, v_w_down)}
    out = {}
    for nm, chunks in landed.items():
        w_, m_, v_ = shards[nm]
        res = _adamw(list(chunks), w_[0], m_[0], v_[0], "adamw_" + nm)
        out[nm] = [r[None] for r in res]
    res = _adamw_cols(l_in0 + l_in1 + l_in2, w_in_t, m_in_t, v_in_t, "adamw_w_in")
    out["w_in"] = [r.T[None] for r in res]

    total = _sum_parts(pack_all, "sum_small")
    pieces, at = [], 0
    for s, rows in zip(small, small_rows):
        n_el = 1
        for dim in s.shape:
            n_el *= dim
        pieces.append(total[at:at + rows.shape[0]].reshape(-1)[:n_el].reshape(s.shape))
        at += rows.shape[0]
    g_mix, g_bg0, g_bg1, g_mlp, g_q, g_k, g_bf, loss_row, g_cw_full, g_meta_full = pieces
    loss = loss_row[0, 0]
    cshard = conv_w.shape[2]
    g_small = {
        "norm_mix": g_mix, "b_gate": jnp.concatenate([g_bg0, g_bg1], axis=1), "norm_mlp": g_mlp,
        "q_norm": g_q, "k_norm": g_k, "b_fgate": g_bf[:, :heads],
        "conv_w": lax.dynamic_slice_in_dim(g_cw_full[:conv_w.shape[1]], me * cshard, cshard, axis=1)[None],
        "meta_tokens": lax.dynamic_slice_in_dim(g_meta_full, me * (d // N_DEV), d // N_DEV, axis=1),
    }
    small_w = {"norm_mix": (norm_mix, m_norm_mix, v_norm_mix), "b_gate": (b_gate, m_b_gate, v_b_gate),
               "norm_mlp": (norm_mlp, m_norm_mlp, v_norm_mlp), "q_norm": (q_norm, m_q_norm, v_q_norm),
               "k_norm": (k_norm, m_k_norm, v_k_norm), "b_fgate": (b_fgate, m_b_fgate, v_b_fgate),
               "conv_w": (conv_w, m_conv_w, v_conv_w), "meta_tokens": (meta_tokens, m_meta_tokens, v_meta_tokens)}
    order = list(small_w)
    packed = []
    for idx in range(4):
        cols = [g_small[nm] if idx == 0 else small_w[nm][idx - 1] for nm in order]
        rows = jnp.concatenate([_rows_of(c_) for c_ in cols], axis=0)
        packed.append(jnp.pad(rows, ((0, -rows.shape[0] % SUBLANES), (0, 0))))
    res = _adamw([packed[0][None]], packed[1], packed[2], packed[3], "adamw_small")
    at = 0
    for nm in order:
        shape = small_w[nm][0].shape
        n_el = 1
        for dim in shape:
            n_el *= dim
        n_rows = -(-n_el // LANES)
        out[nm] = [r[at:at + n_rows].reshape(-1)[:n_el].reshape(shape) for r in res]
        at += n_rows

    weights = ["meta_tokens", "norm_mix", "w_in", "b_fgate", "b_gate", "q_norm", "k_norm", "conv_w",
               "w_attn_out", "w_conv_out", "w_o", "norm_mlp", "w_up", "w_down"]
    grad_x = dh0[N_META:n_valid][None]
    return (loss, grad_x, *[out[nm][0] for nm in weights], *[out[nm][1] for nm in weights],
            *[out[nm][2] for nm in weights], *[out[nm][3] for nm in weights])
```

```python
import functools

import jax
import jax.numpy as jnp
from jax import lax
from jax.experimental import pallas as pl
from jax.experimental.pallas import tpu as pltpu

F32 = jnp.float32
BF16 = jnp.bfloat16

N_DEV = 8
N_META = 16
HEAD_DIM = 128
LANES = 128
SUBLANES = 8
EPS = 1e-6
VMEM_LIMIT = 56 * 1024 * 1024

ADAM_LR = 0.001
ADAM_B1 = 0.9
ADAM_B2 = 0.999
ADAM_EPS = 1e-08
ADAM_WD = 0.01
ADAM_STEP = 10

MESH = pl.DeviceIdType.MESH
HBM_SPEC = pl.BlockSpec(memory_space=pltpu.HBM)
RELATIONS = tuple((r >> 2 & 1, r >> 1 & 1, r & 1) for r in range(1, N_DEV))


def _params(semantics=None):
    return pltpu.CompilerParams(dimension_semantics=semantics, vmem_limit_bytes=VMEM_LIMIT)


def _tile(n, prefs):
    for p in prefs:
        if n % p == 0:
            return p
    return n


def _sds(shape, dtype):
    return jax.ShapeDtypeStruct(shape, dtype)


def _my_place():
    return lax.axis_index("x"), lax.axis_index("y"), lax.axis_index("c")


def _flat(px, py, pc):
    return 4 * px + 2 * py + pc


class _Gather:
    def __init__(self, arrays):
        self.operands = list(arrays)
        self.n = len(arrays)
        self.out_shape = [_sds((N_DEV,) + a.shape, a.dtype) for a in arrays]

    def _copy(self, srcs, outs, sems, a, k, block, to, from_src=False):
        slot = outs[a].at[_flat(*block)]
        return pltpu.make_async_remote_copy(
            src_ref=srcs[a] if from_src else slot, dst_ref=slot,
            send_sem=sems[0].at[a, k], recv_sem=sems[1].at[a, k],
            device_id=to, device_id_type=MESH)

    def _places(self):
        x, y, c = _my_place()
        return (x, y, c), (x, y, 1 - c), [(1 - x, y), (x, 1 - y), (1 - x, 1 - y)], c

    def start(self, srcs, outs, sems):
        me, sibling, chips, c = self._places()
        for a in range(self.n):
            pltpu.make_async_copy(srcs[a], outs[a].at[_flat(*me)], sems[2].at[a]).start()
            for j, chip in enumerate(chips):
                self._copy(srcs, outs, sems, a, 1 + j, me, (*chip, c), from_src=True).start()
            self._copy(srcs, outs, sems, a, 0, me, sibling, from_src=True).start()

    def mid(self, srcs, outs, sems):
        me, sibling, chips, c = self._places()
        for a in range(self.n):
            for j, chip in enumerate(chips):
                self._copy(srcs, outs, sems, a, 1 + j, (*chip, c), me).wait_recv()
                self._copy(srcs, outs, sems, a, 4 + j, (*chip, c), sibling).start()

    def finish(self, srcs, outs, sems):
        me, sibling, chips, c = self._places()
        for a in range(self.n):
            self._copy(srcs, outs, sems, a, 0, sibling, me).wait_recv()
            for j, chip in enumerate(chips):
                self._copy(srcs, outs, sems, a, 4 + j, (*chip, 1 - c), me).wait_recv()
            for k in range(7):
                self._copy(srcs, outs, sems, a, k, me, sibling).wait_send()
            pltpu.make_async_copy(srcs[a], outs[a].at[_flat(*me)], sems[2].at[a]).wait()


class _Scatter:
    def __init__(self, scatter, gather=()):
        scatter = [s if isinstance(s, tuple) else (s, 0, s.shape[1]) for s in scatter]
        self.ranges = [(lo, cnt) for _, lo, cnt in scatter]
        self.operands = [s[0] for s in scatter] + list(gather)
        self.ns, self.n = len(scatter), len(scatter) + len(gather)
        self.out_shape = ([_sds((N_DEV, cnt, arr.shape[2]), arr.dtype) for arr, _, cnt in scatter]
                          + [_sds((N_DEV,) + a.shape, a.dtype) for a in gather])

    def _peer(self, rel):
        return tuple(1 - p if r else p for p, r in zip(_my_place(), rel))

    def _src(self, srcs, a, place):
        if a >= self.ns:
            return srcs[a]
        lo, cnt = self.ranges[a]
        return srcs[a].at[_flat(*place), pl.ds(lo, cnt)]

    def _send(self, srcs, outs, sems, a, k, rel):
        peer = self._peer(rel)
        return pltpu.make_async_remote_copy(
            src_ref=self._src(srcs, a, peer), dst_ref=outs[a].at[_flat(*_my_place())],
            send_sem=sems[0].at[a, k], recv_sem=sems[1].at[a, k],
            device_id=peer, device_id_type=MESH)

    def _landed(self, outs, sems, a, k, rel):
        peer = self._peer(rel)
        slot = outs[a].at[_flat(*peer)]
        return pltpu.make_async_remote_copy(
            src_ref=slot, dst_ref=slot, send_sem=sems[0].at[a, k], recv_sem=sems[1].at[a, k],
            device_id=peer, device_id_type=MESH)

    def _own(self, srcs, outs, sems, a):
        me = _my_place()
        return pltpu.make_async_copy(self._src(srcs, a, me), outs[a].at[_flat(*me)], sems[2].at[a])

    def start(self, srcs, outs, sems):
        for a in range(self.n):
            self._own(srcs, outs, sems, a).start()
            for k, rel in enumerate(RELATIONS):
                self._send(srcs, outs, sems, a, k, rel).start()

    def mid(self, srcs, outs, sems):
        pass

    def finish(self, srcs, outs, sems):
        for a in range(self.n):
            for k, rel in enumerate(RELATIONS):
                self._landed(outs, sems, a, k, rel).wait_recv()
            for k, rel in enumerate(RELATIONS):
                self._send(srcs, outs, sems, a, k, rel).wait_send()
            self._own(srcs, outs, sems, a).wait()


def _job_sems(job):
    return [pltpu.SemaphoreType.DMA((job.n, 7)), pltpu.SemaphoreType.DMA((job.n, 7)),
            pltpu.SemaphoreType.DMA((job.n,))]


def _run_job(job, name):
    n = job.n

    def body(*refs):
        srcs, outs, sems = refs[:n], refs[n:2 * n], refs[2 * n:]
        job.start(srcs, outs, sems)
        job.mid(srcs, outs, sems)
        job.finish(srcs, outs, sems)

    return pl.pallas_call(
        body, name=name, out_shape=job.out_shape,
        in_specs=[HBM_SPEC] * n, out_specs=[HBM_SPEC] * n, scratch_shapes=_job_sems(job),
    )(*job.operands)


def _call(body, *, name, grid, in_specs, out_specs, out_shape, scratch_shapes, semantics,
          operands, job=None, mid_at=0.5):
    if job is None:
        res = pl.pallas_call(
            body, name=name, grid=grid, in_specs=in_specs, out_specs=out_specs, out_shape=out_shape,
            scratch_shapes=scratch_shapes, compiler_params=_params(semantics))(*operands)
        return res, []
    n_in, n_out, n_scr = len(in_specs), len(out_specs), len(scratch_shapes)
    total = 1
    for g in grid:
        total *= g
    mid_step = min(int(total * mid_at), total - 1)

    def carried(*refs):
        c_in, j_in = refs[:n_in], refs[n_in:n_in + job.n]
        o0 = n_in + job.n
        c_out, j_out = refs[o0:o0 + n_out], refs[o0 + n_out:o0 + n_out + job.n]
        s0 = o0 + n_out + job.n
        c_scr, sems = refs[s0:s0 + n_scr], refs[s0 + n_scr:]
        step = pl.program_id(0)
        for ax in range(1, len(grid)):
            step = step * grid[ax] + pl.program_id(ax)

        @pl.when(step == 0)
        def _():
            job.start(j_in, j_out, sems)

        body(*c_in, *c_out, *c_scr)

        @pl.when(step == mid_step)
        def _():
            job.mid(j_in, j_out, sems)

        @pl.when(step == total - 1)
        def _():
            job.finish(j_in, j_out, sems)

    res = pl.pallas_call(
        carried, name=name, grid=grid,
        in_specs=list(in_specs) + [HBM_SPEC] * job.n,
        out_specs=list(out_specs) + [HBM_SPEC] * job.n,
        out_shape=list(out_shape) + job.out_shape,
        scratch_shapes=list(scratch_shapes) + _job_sems(job),
        compiler_params=_params(("arbitrary",) * len(grid)),
    )(*operands, *job.operands)
    return list(res[:n_out]), list(res[n_out:])


def _matmul(a, b, *, name, trans_b=False, extras=(), epilogue=None, out_dtypes=(F32,),
            tm=None, tn=None, tk=None, rows=None, cols=None, trans_a=False, slots=False, job=None,
            mid_at=0.5):
    k, m = a.shape if trans_a else a.shape[::-1]
    n = b.shape[0] if trans_b else b.shape[1]
    tm = tm or _tile(m, (1408, 1024, 512, 256, 128))
    tn = tn or _tile(n // N_DEV if slots else n, (512, 256, 128))
    tk = tk or _tile(k, (2048, 1408, 1024, 512, 256, 128))
    nk = k // tk
    row0, n_rows = rows or (0, m // tm)
    m = n_rows * tm
    col0, n_cols = cols or (0, n // tn)
    n = n_cols * tn
    n_ex, n_out = len(extras), len(out_dtypes)
    dims = (((0,) if trans_a else (1,), (1,) if trans_b else (0,)), ((), ()))

    def body(*refs):
        a_ref, b_ref = refs[:2]
        ex_refs = refs[2:2 + n_ex]
        out_refs = refs[2 + n_ex:2 + n_ex + n_out]
        part = lax.dot_general(a_ref[...].astype(BF16), b_ref[...].astype(BF16), dims,
                               preferred_element_type=F32)

        def finish(acc):
            if epilogue is None:
                res = (acc,)
            else:
                res = epilogue(acc, pl.program_id(0), pl.program_id(1), *[e[...] for e in ex_refs])
            for o_ref, r in zip(out_refs, res):
                o_ref[...] = r.astype(o_ref.dtype)

        if nk == 1:
            finish(part)
        else:
            acc_ref = refs[-1]
            kk = pl.program_id(2)

            @pl.when(kk == 0)
            def _():
                acc_ref[...] = part

            @pl.when(kk > 0)
            def _():
                acc_ref[...] += part

            @pl.when(kk == nk - 1)
            def _():
                finish(acc_ref[...])

    in_specs = [pl.BlockSpec((tk, tm), lambda i, j, kk: (kk, row0 + i)) if trans_a
                else pl.BlockSpec((tm, tk), lambda i, j, kk: (row0 + i, kk)),
                pl.BlockSpec((tn, tk), lambda i, j, kk: (col0 + j, kk)) if trans_b
                else pl.BlockSpec((tk, tn), lambda i, j, kk: (kk, col0 + j))]
    for e in extras:
        if e.shape[0] == 1:
            in_specs.append(pl.BlockSpec((1, tn), lambda i, j, kk: (0, j)))
        else:
            in_specs.append(pl.BlockSpec((tm, tn), lambda i, j, kk: (i, j)))
    if slots:
        per_slot = n // N_DEV // tn
        out_spec = pl.BlockSpec((None, tm, tn), lambda i, j, kk: (j // per_slot, i, j % per_slot))
        out_shape = [_sds((N_DEV, m, n // N_DEV), d) for d in out_dtypes]
    else:
        out_spec = pl.BlockSpec((tm, tn), lambda i, j, kk: (i, j))
        out_shape = [_sds((m, n), d) for d in out_dtypes]
    res, moved = _call(
        body, name=name, grid=(n_rows, n // tn, nk),
        in_specs=in_specs,
        out_specs=[out_spec] * n_out,
        out_shape=out_shape,
        scratch_shapes=[pltpu.VMEM((tm, tn), F32)] if nk > 1 else [],
        semantics=("parallel", "parallel", "arbitrary"),
        operands=(a, b, *extras), job=job, mid_at=mid_at)
    res = res[0] if n_out == 1 else tuple(res)
    return res if job is None else (res, moved)


def _rstd(x):
    return lax.rsqrt(jnp.mean(x * x, axis=-1, keepdims=True) + EPS)


def _norm_bwd(x, dy, g):
    r = _rstd(x)
    u = dy * g
    dx = r * u - x * (r * r * r) * jnp.mean(u * x, axis=-1, keepdims=True)
    return dx, dy * (x * r)


def _rmsnorm_fwd(h, g, name):
    t, d = h.shape
    tr = _tile(t, (384, 256, 128))

    def body(h_ref, g_ref, o_ref):
        x = h_ref[...]
        o_ref[...] = ((x * _rstd(x)) * g_ref[...]).astype(o_ref.dtype)

    row = pl.BlockSpec((tr, d), lambda i: (i, 0))
    return pl.pallas_call(
        body, name=name, grid=(t // tr,),
        in_specs=[row, pl.BlockSpec((1, d), lambda i: (0, 0))], out_specs=row,
        out_shape=_sds((t, d), BF16), compiler_params=_params(("parallel",)),
    )(h, g)


def _rmsnorm_bwd(h, dy, g, res, name):
    t, d = h.shape
    tr = _tile(t, (384, 256, 128))

    def body(h_ref, dy_ref, g_ref, res_ref, dh_ref, dhb_ref, dg_ref):
        dx, dg_rows = _norm_bwd(h_ref[...], dy_ref[...], g_ref[...])
        dh = res_ref[...] + dx
        dh_ref[...] = dh
        dhb_ref[...] = dh.astype(BF16)

        @pl.when(pl.program_id(0) == 0)
        def _():
            dg_ref[...] = jnp.zeros_like(dg_ref)

        dg_ref[...] += jnp.sum(dg_rows, axis=0, keepdims=True)

    row = pl.BlockSpec((tr, d), lambda i: (i, 0))
    vec = pl.BlockSpec((1, d), lambda i: (0, 0))
    return pl.pallas_call(
        body, name=name, grid=(t // tr,),
        in_specs=[row, row, vec, row], out_specs=[row, row, vec],
        out_shape=[_sds((t, d), F32), _sds((t, d), BF16), _sds((1, d), F32)],
        compiler_params=_params(("arbitrary",)),
    )(h, dy, g, res)


def _qk_prep(proj, gq, gk, aw, name):
    t = proj.shape[0]
    heads = aw // HEAD_DIM
    tr = _tile(t, (384, 256, 128))

    def body(q_ref, k_ref, v_ref, gq_ref, gk_ref, qo_ref, ko_ref, vo_ref):
        for h in range(heads):
            sl = slice(h * HEAD_DIM, (h + 1) * HEAD_DIM)
            xq, xk = q_ref[:, sl].astype(F32), k_ref[:, sl].astype(F32)
            qo_ref[:, sl] = ((xq * _rstd(xq)) * gq_ref[...]).astype(BF16)
            ko_ref[:, sl] = ((xk * _rstd(xk)) * gk_ref[...]).astype(BF16)
        vo_ref[...] = v_ref[...].astype(BF16)

    vec = pl.BlockSpec((1, HEAD_DIM), lambda i: (0, 0))
    out = pl.BlockSpec((tr, aw), lambda i: (i, 0))
    return pl.pallas_call(
        body, name=name, grid=(t // tr,),
        in_specs=[pl.BlockSpec((tr, aw), lambda i: (i, 0)), pl.BlockSpec((tr, aw), lambda i: (i, 1)),
                  pl.BlockSpec((tr, aw), lambda i: (i, 2)), vec, vec],
        out_specs=[out, out, out], out_shape=[_sds((t, aw), BF16)] * 3,
        compiler_params=_params(("parallel",)),
    )(proj, proj, proj, gq, gk)


def _qk_bwd(dqn, dkn, proj, gq, gk, aw, name):
    t = proj.shape[0]
    heads = aw // HEAD_DIM
    tr = _tile(t, (384, 256, 128))

    def body(dq_ref, dk_ref, q_ref, k_ref, gq_ref, gk_ref, dqo_ref, dko_ref, dgq_ref, dgk_ref):
        @pl.when(pl.program_id(0) == 0)
        def _():
            dgq_ref[...] = jnp.zeros_like(dgq_ref)
            dgk_ref[...] = jnp.zeros_like(dgk_ref)

        for h in range(heads):
            sl = slice(h * HEAD_DIM, (h + 1) * HEAD_DIM)
            dx, dg_rows = _norm_bwd(q_ref[:, sl].astype(F32), dq_ref[:, sl], gq_ref[...])
            dqo_ref[:, sl] = dx.astype(BF16)
            dgq_ref[...] += jnp.sum(dg_rows, axis=0, keepdims=True)
            dx, dg_rows = _norm_bwd(k_ref[:, sl].astype(F32), dk_ref[:, sl], gk_ref[...])
            dko_ref[:, sl] = dx.astype(BF16)
            dgk_ref[...] += jnp.sum(dg_rows, axis=0, keepdims=True)

    vec = pl.BlockSpec((1, HEAD_DIM), lambda i: (0, 0))
    row = pl.BlockSpec((tr, aw), lambda i: (i, 0))
    return pl.pallas_call(
        body, name=name, grid=(t // tr,),
        in_specs=[row, row, row, pl.BlockSpec((tr, aw), lambda i: (i, 1)), vec, vec],
        out_specs=[row, row, vec, vec],
        out_shape=[_sds((t, aw), BF16), _sds((t, aw), BF16), _sds((1, HEAD_DIM), F32), _sds((1, HEAD_DIM), F32)],
        compiler_params=_params(("arbitrary",)),
    )(dqn, dkn, proj, proj, gq, gk)


def _triangle(lower):
    r = lax.broadcasted_iota(jnp.int32, (LANES, LANES), 0)
    c = lax.broadcasted_iota(jnp.int32, (LANES, LANES), 1)
    return ((c <= r) if lower else (c >= r)).astype(F32)


def _forget_fwd(fg, b, name):
    t = fg.shape[0]

    def body(fg_ref, b_ref, cum_ref, carry):
        @pl.when(pl.program_id(0) == 0)
        def _():
            carry[...] = jnp.zeros_like(carry)

        z = fg_ref[...] + b_ref[...]
        log_f = jnp.minimum(z, 0.0) - jnp.log1p(jnp.exp(-jnp.abs(z)))
        cs = jnp.dot(_triangle(True), log_f, precision=lax.Precision.HIGHEST,
                     preferred_element_type=F32) + carry[0:1, :]
        cum_ref[...] = cs
        carry[...] = jnp.broadcast_to(cs[LANES - 1:LANES, :], carry.shape)

    row = pl.BlockSpec((LANES, LANES), lambda i: (i, 0))
    return pl.pallas_call(
        body, name=name, grid=(t // LANES,),
        in_specs=[row, pl.BlockSpec((1, LANES), lambda i: (0, 0))], out_specs=row,
        out_shape=_sds((t, LANES), F32), scratch_shapes=[pltpu.VMEM((SUBLANES, LANES), F32)],
        compiler_params=_params(("arbitrary",)),
    )(fg, b)


def _forget_bwd(dcum, fg, b, name):
    t = fg.shape[0]
    nt = t // LANES

    def body(dc_ref, fg_ref, b_ref, dfg_ref, db_ref, carry):
        @pl.when(pl.program_id(0) == 0)
        def _():
            carry[...] = jnp.zeros_like(carry)
            db_ref[...] = jnp.zeros_like(db_ref)

        d_log_f = jnp.dot(_triangle(False), dc_ref[...], precision=lax.Precision.HIGHEST,
                          preferred_element_type=F32) + carry[0:1, :]
        carry[...] = jnp.broadcast_to(d_log_f[0:1, :], carry.shape)
        dz = d_log_f * jax.nn.sigmoid(-(fg_ref[...] + b_ref[...]))
        dfg_ref[...] = dz.astype(BF16)
        db_ref[...] += jnp.sum(dz, axis=0, keepdims=True)

    row = pl.BlockSpec((LANES, LANES), lambda i: (nt - 1 - i, 0))
    vec = pl.BlockSpec((1, LANES), lambda i: (0, 0))
    return pl.pallas_call(
        body, name=name, grid=(nt,),
        in_specs=[row, row, vec], out_specs=[row, vec],
        out_shape=[_sds((t, LANES), BF16), _sds((1, LANES), F32)],
        scratch_shapes=[pltpu.VMEM((SUBLANES, LANES), F32)],
        compiler_params=_params(("arbitrary",)),
    )(dcum, fg, b)


def _causal(qi, kj, tq):
    rows = qi * tq + lax.broadcasted_iota(jnp.int32, (tq, tq), 0)
    cols = kj * tq + lax.broadcasted_iota(jnp.int32, (tq, tq), 1)
    return cols <= rows


NT_DIMS = (((1,), (1,)), ((), ()))
TN_DIMS = (((0,), (0,)), ((), ()))


def _attn_tile():
    return (384, 256, 128)


def _attn_fwd(q, k, v, cum_row, name, job=None, mid_at=0.5):
    t, aw = q.shape
    heads = aw // HEAD_DIM
    tq = _tile(t, _attn_tile())
    nq = t // tq
    scale = HEAD_DIM ** -0.5

    def body(q_ref, k_ref, v_ref, ck_ref, o_ref, of_ref, lse_ref):
        qi = pl.program_id(1)
        qv = q_ref[...]

        def tile(kj, carry, masked):
            m_prev, l_prev, acc, res = carry
            ks = pl.ds(pl.multiple_of(kj * tq, tq), tq)
            s = lax.dot_general(qv, k_ref[ks, :], NT_DIMS, preferred_element_type=F32) * scale - ck_ref[kj]
            if masked:
                s = jnp.where(_causal(0, 0, tq), s, -jnp.inf)
            m_new = jnp.maximum(m_prev, jnp.max(s, axis=-1, keepdims=True))
            alpha = jnp.exp(m_prev - m_new)
            p = jnp.exp(s - m_new)
            p_hi = p.astype(BF16)
            p_lo = (p - p_hi.astype(F32)).astype(BF16)
            vv = v_ref[ks, :]
            return (m_new, alpha * l_prev + jnp.sum(p, axis=-1, keepdims=True),
                    alpha * acc + jnp.dot(p_hi, vv, preferred_element_type=F32),
                    alpha * res + jnp.dot(p_lo, vv, preferred_element_type=F32))

        init = (jnp.full((tq, 1), -jnp.inf, F32), jnp.zeros((tq, 1), F32),
                jnp.zeros((tq, HEAD_DIM), F32), jnp.zeros((tq, HEAD_DIM), F32))
        carry = lax.fori_loop(0, qi, lambda kj, c: tile(kj, c, False), init)
        m_fin, l_fin, acc, res = tile(qi, carry, True)
        o_ref[...] = (acc / l_fin).astype(o_ref.dtype)
        of_ref[...] = (acc + res) / l_fin
        lse_ref[...] = m_fin + jnp.log(l_fin)

    q_spec = pl.BlockSpec((tq, HEAD_DIM), lambda h, i: (i, h))
    head = pl.BlockSpec((t, HEAD_DIM), lambda h, i: (0, h))
    return _call(
        body, name=name, grid=(heads, nq),
        in_specs=[q_spec, head, head, pl.BlockSpec((None, nq, 1, tq), lambda h, i: (h, 0, 0, 0))],
        out_specs=[q_spec, q_spec, pl.BlockSpec((None, tq, 1), lambda h, i: (h, i, 0))],
        out_shape=[_sds((t, aw), BF16), _sds((t, aw), F32), _sds((heads, t, 1), F32)],
        scratch_shapes=[], semantics=("parallel", "arbitrary"),
        operands=(q, k, v, cum_row), job=job, mid_at=mid_at)


def _attn_stats(do, o, name):
    t, aw = o.shape
    heads = aw // HEAD_DIM
    tr = _tile(t, (384, 256, 128))

    def body(do_ref, o_ref, delta_ref):
        for h in range(heads):
            sl = slice(h * HEAD_DIM, (h + 1) * HEAD_DIM)
            do_seen = do_ref[:, sl].astype(BF16).astype(F32)
            delta_ref[h] = jnp.sum(do_seen * o_ref[:, sl], axis=-1, keepdims=True)

    row = pl.BlockSpec((tr, aw), lambda i: (i, 0))
    return pl.pallas_call(
        body, name=name, grid=(t // tr,),
        in_specs=[row, row], out_specs=pl.BlockSpec((heads, tr, 1), lambda i: (0, i, 0)),
        out_shape=_sds((heads, t, 1), F32), compiler_params=_params(("parallel",)),
    )(do, o)


def _attn_bwd(q, k, v, do, lse, delta, cum_row, name, job=None):
    t, aw = q.shape
    heads = aw // HEAD_DIM
    tq = _tile(t, _attn_tile())
    nq = t // tq
    scale = HEAD_DIM ** -0.5

    def body(q_ref, k_ref, v_ref, do_ref, lse_ref, delta_ref, ck_ref, dq_ref, dk_ref, dv_ref, dck_ref):
        kj = pl.program_id(1)

        @pl.when(kj == 0)
        def _():
            dq_ref[...] = jnp.zeros_like(dq_ref)

        kv, vv, ck = k_ref[...], v_ref[...], ck_ref[...]

        def tile(qi, carry, masked):
            dk_acc, dv_acc, dck_acc = carry
            rows = pl.ds(pl.multiple_of(qi * tq, tq), tq)
            qv, dov = q_ref[rows, :], do_ref[rows, :].astype(BF16)
            s = lax.dot_general(qv, kv, NT_DIMS, preferred_element_type=F32) * scale - ck - lse_ref[rows, :]
            p = jnp.exp(s)
            if masked:
                p = jnp.where(_causal(0, 0, tq), p, 0.0)
            dp = lax.dot_general(dov, vv, NT_DIMS, preferred_element_type=F32)
            ds = p * (dp - delta_ref[rows, :])
            dsb = ds.astype(BF16)
            dq_ref[rows, :] += jnp.dot(dsb, kv, preferred_element_type=F32) * scale
            return (dk_acc + lax.dot_general(dsb, qv, TN_DIMS, preferred_element_type=F32),
                    dv_acc + lax.dot_general(p.astype(BF16), dov, TN_DIMS, preferred_element_type=F32),
                    dck_acc + jnp.sum(ds, axis=0, keepdims=True))

        init = (jnp.zeros((tq, HEAD_DIM), F32), jnp.zeros((tq, HEAD_DIM), F32), jnp.zeros((1, tq), F32))
        carry = tile(kj, init, True)
        dk_acc, dv_acc, dck_acc = lax.fori_loop(kj + 1, nq, lambda qi, c: tile(qi, c, False), carry)
        dk_ref[...] = dk_acc * scale
        dv_ref[...] = dv_acc.astype(dv_ref.dtype)
        dck_ref[...] = -dck_acc

    head = pl.BlockSpec((t, HEAD_DIM), lambda h, j: (0, h))
    k_spec = pl.BlockSpec((tq, HEAD_DIM), lambda h, j: (j, h))
    col = pl.BlockSpec((None, t, 1), lambda h, j: (h, 0, 0))
    row = pl.BlockSpec((None, 1, tq), lambda h, j: (h, 0, j))
    return _call(
        body, name=name, grid=(heads, nq),
        in_specs=[head, k_spec, k_spec, head, col, col, row],
        out_specs=[head, k_spec, k_spec, row],
        out_shape=[_sds((t, aw), F32), _sds((t, aw), F32), _sds((t, aw), BF16), _sds((heads, 1, t), F32)],
        scratch_shapes=[], semantics=("parallel", "arbitrary"),
        operands=(q, k, v, do, lse, delta, cum_row), job=job)


def _shift_down(u, by):
    rows = lax.broadcasted_iota(jnp.int32, u.shape, 0)
    return jnp.where(rows >= by, pltpu.roll(u, by, 0), 0.0)


def _shift_up(u, by):
    t = u.shape[0]
    rows = lax.broadcasted_iota(jnp.int32, u.shape, 0)
    return jnp.where(rows < t - by, pltpu.roll(u, t - by, 0), 0.0)


def _conv_specs(t, off_b, cw_width):
    nb = cw_width // LANES
    base = off_b // LANES
    return [pl.BlockSpec((t, LANES), lambda j, s=s: (0, base + s * nb + j)) for s in range(3)]


def _conv_fwd(proj, cw, off_b, name):
    t = proj.shape[0]
    width = cw.shape[1]

    def body(cb_ref, cc_ref, cx_ref, w_ref, o_ref):
        u = cc_ref[...].astype(F32) * cx_ref[...]
        y = w_ref[0:1, :] * _shift_down(u, 2) + w_ref[1:2, :] * _shift_down(u, 1) + w_ref[2:3, :] * u
        o_ref[...] = (cb_ref[...] * y).astype(BF16)

    return pl.pallas_call(
        body, name=name, grid=(width // LANES,),
        in_specs=_conv_specs(t, off_b, width) + [pl.BlockSpec((SUBLANES, LANES), lambda j: (0, j))],
        out_specs=pl.BlockSpec((t, LANES), lambda j: (0, j)),
        out_shape=_sds((t, width), BF16), compiler_params=_params(("parallel",)),
    )(proj, proj, proj, cw)


def _conv_bwd(dcp, proj, cw, off_b, name):
    t = proj.shape[0]
    width = cw.shape[1]

    def body(d_ref, cb_ref, cc_ref, cx_ref, w_ref, dcb_ref, dcc_ref, dcx_ref, dw_ref):
        cc, cx = cc_ref[...].astype(F32), cx_ref[...].astype(F32)
        u = cc * cx
        u1, u2 = _shift_down(u, 1), _shift_down(u, 2)
        w0, w1, w2 = w_ref[0:1, :], w_ref[1:2, :], w_ref[2:3, :]
        d = d_ref[...]
        dcb_ref[...] = (d * (w0 * u2 + w1 * u1 + w2 * u)).astype(BF16)
        dy = d * cb_ref[...]
        du = w2 * dy + w1 * _shift_up(dy, 1) + w0 * _shift_up(dy, 2)
        dcc_ref[...] = (du * cx).astype(BF16)
        dcx_ref[...] = (du * cc).astype(BF16)
        dw = [jnp.sum(dy * s, axis=0, keepdims=True) for s in (u2, u1, u)]
        dw_ref[...] = jnp.concatenate(dw + [jnp.zeros((SUBLANES - 3, LANES), F32)], axis=0)

    col = pl.BlockSpec((t, LANES), lambda j: (0, j))
    wspec = pl.BlockSpec((SUBLANES, LANES), lambda j: (0, j))
    return pl.pallas_call(
        body, name=name, grid=(width // LANES,),
        in_specs=[col] + _conv_specs(t, off_b, width) + [wspec],
        out_specs=[col, col, col, wspec],
        out_shape=[_sds((t, width), BF16)] * 3 + [_sds((SUBLANES, width), F32)],
        compiler_params=_params(("parallel",)),
    )(dcp, proj, proj, proj, cw)


def _gate_specs(t, d, off_g, tr, tc, rows_first):
    nb = d // tc
    base = off_g // tc
    if rows_first:
        tile = lambda s: pl.BlockSpec((tr, tc), lambda i, j: (i, base + s * nb + j))
        vec = lambda s: pl.BlockSpec((1, tc), lambda i, j: (0, s * nb + j))
        plain = pl.BlockSpec((tr, tc), lambda i, j: (i, j))
    else:
        tile = lambda s: pl.BlockSpec((tr, tc), lambda j, i: (i, base + s * nb + j))
        vec = lambda s: pl.BlockSpec((1, tc), lambda j, i: (0, s * nb + j))
        plain = pl.BlockSpec((tr, tc), lambda j, i: (i, j))
    return tile, vec, plain


def _gate_fwd(a, c, proj, bg, off_g, name):
    t, d = a.shape
    tr, tc = _tile(t, (384, 256, 128)), _tile(d, (512, 256, 128))
    tile, vec, plain = _gate_specs(t, d, off_g, tr, tc, True)

    def body(a_ref, c_ref, g0_ref, g1_ref, b0_ref, b1_ref, o_ref):
        g0 = jax.nn.sigmoid(g0_ref[...] + b0_ref[...])
        g1 = jax.nn.sigmoid(g1_ref[...] + b1_ref[...])
        o_ref[...] = (g0 * a_ref[...] + g1 * c_ref[...]).astype(BF16)

    return pl.pallas_call(
        body, name=name, grid=(t // tr, d // tc),
        in_specs=[plain, plain, tile(0), tile(1), vec(0), vec(1)], out_specs=plain,
        out_shape=_sds((t, d), BF16), compiler_params=_params(("parallel", "parallel")),
    )(a, c, proj, proj, bg, bg)


def _gate_bwd(dm, a, c, proj, bg, off_g, name):
    t, d = a.shape
    tr, tc = _tile(t, (384, 256, 128)), _tile(d, (512, 256, 128))
    tile, vec, plain = _gate_specs(t, d, off_g, tr, tc, False)

    def body(dm_ref, a_ref, c_ref, g0_ref, g1_ref, b0_ref, b1_ref,
             da_ref, dc_ref, dg0_ref, dg1_ref, db0_ref, db1_ref):
        @pl.when(pl.program_id(1) == 0)
        def _():
            db0_ref[...] = jnp.zeros_like(db0_ref)
            db1_ref[...] = jnp.zeros_like(db1_ref)

        dm = dm_ref[...]
        g0 = jax.nn.sigmoid(g0_ref[...] + b0_ref[...])
        g1 = jax.nn.sigmoid(g1_ref[...] + b1_ref[...])
        da_ref[...] = (dm * g0).astype(BF16)
        dc_ref[...] = (dm * g1).astype(BF16)
        dz0 = dm * a_ref[...] * (g0 * (1.0 - g0))
        dz1 = dm * c_ref[...] * (g1 * (1.0 - g1))
        dg0_ref[...] = dz0.astype(BF16)
        dg1_ref[...] = dz1.astype(BF16)
        db0_ref[...] += jnp.sum(dz0, axis=0, keepdims=True)
        db1_ref[...] += jnp.sum(dz1, axis=0, keepdims=True)

    bvec = pl.BlockSpec((1, tc), lambda j, i: (0, j))
    return pl.pallas_call(
        body, name=name, grid=(d // tc, t // tr),
        in_specs=[plain, plain, plain, tile(0), tile(1), vec(0), vec(1)],
        out_specs=[plain] * 4 + [bvec, bvec],
        out_shape=[_sds((t, d), BF16)] * 4 + [_sds((1, d), F32)] * 2,
        compiler_params=_params(("parallel", "arbitrary")),
    )(dm, a, c, proj, proj, bg, bg)


def _sum_squares(x, name):
    t, d = x.shape
    tr = _tile(t, (384, 256, 128))

    def body(x_ref, o_ref):
        @pl.when(pl.program_id(0) == 0)
        def _():
            o_ref[...] = jnp.zeros_like(o_ref)

        v = x_ref[...]
        o_ref[...] += jnp.sum(jnp.sum(v * v, axis=0, keepdims=True), axis=1, keepdims=True)

    return pl.pallas_call(
        body, name=name, grid=(t // tr,),
        in_specs=[pl.BlockSpec((tr, d), lambda i: (i, 0))],
        out_specs=pl.BlockSpec((1, LANES), lambda i: (0, 0)),
        out_shape=_sds((1, LANES), F32), compiler_params=_params(("arbitrary",)),
    )(x)


def _row_tile(r, c):
    return r if r * c <= 128 * 1024 else _tile(r, (128, 64, 32, 16))


def _sum_parts(parts, name):
    n, r, c = parts.shape
    tr = _row_tile(r, c)

    def body(p_ref, o_ref):
        acc = p_ref[0].astype(F32)
        for i in range(1, n):
            acc = acc + p_ref[i].astype(F32)
        o_ref[...] = acc

    return pl.pallas_call(
        body, name=name, grid=(r // tr,),
        in_specs=[pl.BlockSpec((n, tr, c), lambda i: (0, i, 0))],
        out_specs=pl.BlockSpec((tr, c), lambda i: (i, 0)),
        out_shape=_sds((r, c), F32), compiler_params=_params(("parallel",)),
    )(parts)


def _adamw(chunks, w, m, v, name):
    n, rc, c = chunks[0].shape
    r = rc * len(chunks)
    tr = _row_tile(rc, c)
    per = rc // tr

    def body(*refs):
        p_refs = refs[:len(chunks)]
        w_ref, m_ref, v_ref, g_ref, d_ref, nm_ref, nv_ref = refs[len(chunks):]
        i = pl.program_id(0)

        def update(p_ref):
            g = p_ref[0].astype(F32)
            for s in range(1, n):
                g = g + p_ref[s].astype(F32)
            nm = ADAM_B1 * m_ref[...] + (1.0 - ADAM_B1) * g
            nv = ADAM_B2 * v_ref[...] + (1.0 - ADAM_B2) * (g * g)
            m_hat = nm / (1.0 - ADAM_B1 ** ADAM_STEP)
            v_hat = nv / (1.0 - ADAM_B2 ** ADAM_STEP)
            g_ref[...] = g
            d_ref[...] = -ADAM_LR * (m_hat / (jnp.sqrt(v_hat) + ADAM_EPS) + ADAM_WD * w_ref[...])
            nm_ref[...] = nm
            nv_ref[...] = nv

        if len(chunks) == 1:
            update(p_refs[0])
        else:
            for ci, p_ref in enumerate(p_refs):
                pl.when((i >= ci * per) & (i < (ci + 1) * per))(functools.partial(update, p_ref))

    row = pl.BlockSpec((tr, c), lambda i: (i, 0))
    part_specs = [pl.BlockSpec((n, tr, c), lambda i, ci=ci: (0, jnp.clip(i - ci * per, 0, per - 1), 0))
                  for ci in range(len(chunks))]
    return pl.pallas_call(
        body, name=name, grid=(r // tr,),
        in_specs=part_specs + [row, row, row],
        out_specs=[row] * 4, out_shape=[_sds((r, c), F32)] * 4,
        compiler_params=_params(("parallel",)),
    )(*chunks, w, m, v)


def _adamw_cols(chunks, w, m, v, name):
    n, r, _ = chunks[0].shape
    widths = [ch.shape[2] for ch in chunks]
    tc = min([LANES] + widths)
    firsts = [sum(widths[:ci]) // tc for ci in range(len(chunks) + 1)]

    def body(*refs):
        p_refs = refs[:len(chunks)]
        w_ref, m_ref, v_ref, g_ref, d_ref, nm_ref, nv_ref = refs[len(chunks):]
        j = pl.program_id(0)

        def update(p_ref):
            g = p_ref[0].astype(F32)
            for s in range(1, n):
                g = g + p_ref[s].astype(F32)
            nm = ADAM_B1 * m_ref[...] + (1.0 - ADAM_B1) * g
            nv = ADAM_B2 * v_ref[...] + (1.0 - ADAM_B2) * (g * g)
            m_hat = nm / (1.0 - ADAM_B1 ** ADAM_STEP)
            v_hat = nv / (1.0 - ADAM_B2 ** ADAM_STEP)
            g_ref[...] = g
            d_ref[...] = -ADAM_LR * (m_hat / (jnp.sqrt(v_hat) + ADAM_EPS) + ADAM_WD * w_ref[...])
            nm_ref[...] = nm
            nv_ref[...] = nv

        for ci, p_ref in enumerate(p_refs):
            pl.when((j >= firsts[ci]) & (j < firsts[ci + 1]))(functools.partial(update, p_ref))

    col = pl.BlockSpec((r, tc), lambda j: (0, j))
    part_specs = [pl.BlockSpec((n, r, tc),
                               lambda j, lo=firsts[ci], hi=firsts[ci + 1]: (0, 0, jnp.clip(j - lo, 0, hi - lo - 1)))
                  for ci in range(len(chunks))]
    return pl.pallas_call(
        body, name=name, grid=(firsts[-1],),
        in_specs=part_specs + [col, col, col],
        out_specs=[col] * 4, out_shape=[_sds((r, firsts[-1] * tc), F32)] * 4,
        compiler_params=_params(("parallel",)),
    )(*chunks, w, m, v)


def _pad_lanes(a, width=LANES):
    return jnp.pad(a, ((0, 0), (0, width - a.shape[1])))


def _rows_of(a):
    flat = a.reshape(-1)
    n = -(-flat.shape[0] // LANES) * LANES
    return jnp.pad(flat, (0, n - flat.shape[0])).reshape(-1, LANES)


def _columns_to_slots(full, n_rows):
    return full.reshape(n_rows, N_DEV, -1).transpose(1, 0, 2)


def _slots_to_columns(slots):
    return slots.transpose(1, 0, 2).reshape(slots.shape[1], -1)


def _stacked_rows(slots, lo, hi):
    r = slots.shape[1]
    out = []
    while lo < hi:
        j, off = divmod(lo, r)
        n = min(hi - lo, r - off)
        out.append(slots[j, off:off + n])
        lo += n
    return out


def kernel(x, meta_tokens, norm_mix, w_in, b_fgate, b_gate, q_norm, k_norm, conv_w, w_attn_out, w_conv_out, w_o, norm_mlp, w_up, w_down, loss_target, m_meta_tokens, m_norm_mix, m_w_in, m_b_fgate, m_b_gate, m_q_norm, m_k_norm, m_conv_w, m_w_attn_out, m_w_conv_out, m_w_o, m_norm_mlp, m_w_up, m_w_down, v_meta_tokens, v_norm_mix, v_w_in, v_b_fgate, v_b_gate, v_q_norm, v_k_norm, v_conv_w, v_w_attn_out, v_w_conv_out, v_w_o, v_norm_mlp, v_w_up, v_w_down):
    seq, d = x.shape[1], x.shape[2]
    heads = b_fgate.shape[1]
    aw = heads * HEAD_DIM
    cwid = conv_w.shape[2] * N_DEV
    dff = w_up.shape[2] * N_DEV
    n_valid = N_META + seq
    t = -(-n_valid // LANES) * LANES
    me = _flat(*_my_place())
    off_cb, off_gl = 3 * aw, 3 * aw + 3 * cwid

    conv_shard = jnp.pad(conv_w[0], ((0, SUBLANES - conv_w.shape[1]), (0, 0)))
    w_in_t, m_in_t, v_in_t = (jnp.swapaxes(p, 1, 2)[0] for p in (w_in, m_w_in, v_w_in))
    g_in, g_meta, g_cw = _run_job(_Gather([w_in_t.astype(BF16), meta_tokens, conv_shard]), "gather_first")
    n_in = N_DEV * g_in.shape[1]
    w_main_t = jnp.concatenate(_stacked_rows(g_in, 0, 3 * aw) + _stacked_rows(g_in, 3 * aw + heads, n_in), axis=0)
    w_fg_t = jnp.pad(jnp.concatenate(_stacked_rows(g_in, 3 * aw, 3 * aw + heads), axis=0),
                     ((0, LANES - heads), (0, 0)))
    meta_full, cw_full = _slots_to_columns(g_meta), _slots_to_columns(g_cw)

    pad_rows = t - n_valid
    h0 = jnp.concatenate([meta_full, x[0], jnp.zeros((pad_rows, d), F32)], axis=0)
    target = jnp.concatenate([jnp.zeros((N_META, d), F32), loss_target[0], jnp.zeros((pad_rows, d), F32)], axis=0)
    b_f = _pad_lanes(b_fgate)

    xn = _rmsnorm_fwd(h0, norm_mix, "norm_mix_fwd")
    proj, (g_ao, g_co, g_o) = _matmul(
        xn, w_main_t, name="in_proj", trans_b=True, mid_at=0.6, out_dtypes=(BF16,),
        job=_Gather([w_attn_out[0].astype(BF16), w_conv_out[0].astype(BF16), w_o[0].astype(BF16)]))
    w_ao, w_co, w_o_f = _slots_to_columns(g_ao), _slots_to_columns(g_co), g_o.reshape(d, d)
    fg = _matmul(xn, w_fg_t, name="in_proj_fgate", trans_b=True)
    qn, kn, vb = _qk_prep(proj, q_norm, k_norm, aw, "qk_norm_fwd")
    cum = _forget_fwd(fg, b_f, "forget_cumsum")
    cum_heads = cum[:, :heads].T
    cum_row = cum_heads[:, None, :]
    t_attn = _tile(t, _attn_tile())
    (o, o_fine, lse), (g_up,) = _attn_fwd(
        qn, kn, vb, cum_heads.reshape(heads, t // t_attn, 1, t_attn), "attention_fwd", mid_at=0.7,
        job=_Gather([w_up[0].astype(BF16)]))
    w_up_f = _slots_to_columns(g_up)
    a = _matmul(o, w_ao, name="attn_out_proj", out_dtypes=(BF16,))
    cpre = _conv_fwd(proj, cw_full, off_cb, "short_conv_fwd")
    c = _matmul(cpre, w_co, name="conv_out_proj", out_dtypes=(BF16,))
    merged = _gate_fwd(a, c, proj, b_gate, off_gl, "gate_merge_fwd")
    h1 = _matmul(merged, w_o_f, name="out_proj", extras=(h0,), epilogue=lambda acc, i, j, r: (r + acc,))
    hn = _rmsnorm_fwd(h1, norm_mlp, "norm_mlp_fwd")
    (z, u), (g_down,) = _matmul(hn, w_up_f, name="mlp_up", out_dtypes=(F32, BF16), mid_at=1.0,
                                epilogue=lambda acc, i, j: (acc, jnp.square(jnp.maximum(acc, 0.0))),
                                job=_Gather([w_down[0].astype(BF16)]))
    w_down_f = g_down.reshape(dff, d)

    tm_down = _tile(t, (1408, 1024, 512, 256, 128))

    def loss_grad(acc, i, j, h1_tile, tgt_tile):
        rows = i * tm_down + lax.broadcasted_iota(jnp.int32, acc.shape, 0)
        valid = (rows >= N_META) & (rows < n_valid)
        dy = jnp.where(valid, ((h1_tile + acc) - tgt_tile) / d, 0.0)
        return dy, dy

    dh2, dh2b = _matmul(u, w_down_f, name="mlp_down_loss", extras=(h1, target), epilogue=loss_grad,
                        out_dtypes=(F32, BF16), tm=tm_down)
    loss_part = _sum_squares(dh2, "loss_sum") * (0.5 * d)

    wide = lambda n_cols: _tile(n_cols, (1024, 512, 256, 128))
    dw_down = _matmul(u, dh2b, name="mlp_down_wgrad", trans_a=True, tn=wide(d), out_dtypes=(BF16,))
    s_down = dw_down.reshape(N_DEV, dff // N_DEV, d)
    half_down = dff // N_DEV // 2
    dz, l_down0 = _matmul(dh2b, w_down_f, name="mlp_down_bwd", trans_b=True, extras=(z,), out_dtypes=(BF16,),
                          epilogue=lambda acc, i, j, zt: (acc * (2.0 * jnp.maximum(zt, 0.0)),),
                          job=_Scatter([(s_down, 0, half_down)]))
    s_up, l_down1 = _matmul(hn, dz, name="mlp_up_wgrad", trans_a=True, slots=True, out_dtypes=(BF16,),
                            tn=wide(dff // N_DEV), job=_Scatter([(s_down, half_down, half_down)]))
    dhn, l_up0 = _matmul(dz, w_up_f, name="mlp_up_bwd", trans_b=True, tn=wide(d),
                         job=_Scatter([(s_up, 0, d // 2)]))
    dh1, dh1b, dg_mlp = _rmsnorm_bwd(h1, dhn, norm_mlp, dh2, "norm_mlp_bwd")
    dmerged = _matmul(dh1b, w_o_f, name="out_proj_bwd", trans_b=True)
    dw_o = _matmul(merged, dh1b, name="out_proj_wgrad", trans_a=True, tn=wide(d), out_dtypes=(BF16,))
    da, dc, dgl0, dgl1, dbg0, dbg1 = _gate_bwd(dmerged, a, c, proj, b_gate, off_gl, "gate_merge_bwd")
    do = _matmul(da, w_ao, name="attn_out_bwd", trans_b=True)
    s_ao = _matmul(o, da, name="attn_out_wgrad", trans_a=True, slots=True, out_dtypes=(BF16,))
    dcp = _matmul(dc, w_co, name="conv_out_bwd", trans_b=True)
    s_co = _matmul(cpre, dc, name="conv_out_wgrad", trans_a=True, slots=True, out_dtypes=(BF16,))
    dcb, dcc, dcx, dcw = _conv_bwd(dcp, proj, cw_full, off_cb, "short_conv_bwd")
    delta = _attn_stats(do, o_fine, "attention_stats")
    (dqn, dkn, dv, dck), (l_up1, l_ao, l_co) = _attn_bwd(
        qn, kn, vb, do, lse, delta, cum_row, "attention_bwd",
        job=_Scatter([(s_up, d // 2, d // 2), s_ao, s_co]))
    dq_raw, dk_raw, dg_q, dg_k = _qk_bwd(dqn, dkn, proj, q_norm, k_norm, aw, "qk_norm_bwd")
    dcum = _pad_lanes(dck.reshape(heads, t).T)
    dfg, db_f = _forget_bwd(dcum, fg, b_f, "forget_bwd")
    dproj = jnp.concatenate([dq_raw, dk_raw, dv, dcb, dcc, dcx, dgl0, dgl1], axis=1)
    dwt_fg = _matmul(dfg, xn, name="in_proj_fgate_wgrad", trans_a=True, out_dtypes=(BF16,))
    quarter = d // 4

    def in_slots(dwt, first):
        width = dwt.shape[1]
        parts = ((0, 3 * aw, dwt, 0), (3 * aw, 3 * aw + heads, dwt_fg[:heads, first:first + width], 3 * aw),
                 (3 * aw + heads, n_in, dwt, heads))
        slots = []
        for j in range(N_DEV):
            lo, hi = j * n_in // N_DEV, (j + 1) * n_in // N_DEV
            rows = [src[max(lo, a) - shift:min(hi, b) - shift] for a, b, src, shift in parts
                    if max(lo, a) < min(hi, b)]
            slots.append(rows[0] if len(rows) == 1 else jnp.concatenate(rows, axis=0))
        return jnp.stack(slots)

    def in_wgrad(idx, first_block, n_blocks, job):
        return _matmul(dproj, xn, name="in_proj_wgrad_%d" % idx, trans_a=True, tn=quarter,
                       cols=(first_block, n_blocks), out_dtypes=(BF16,), job=job)

    dwt0, (l_o,) = in_wgrad(0, 0, 1, _Scatter([dw_o.reshape(N_DEV, d // N_DEV, d)]))
    dwt1, l_in0 = in_wgrad(1, 1, 1, _Scatter([in_slots(dwt0, 0)]))
    dwt2, l_in1 = in_wgrad(2, 2, 2, _Scatter([in_slots(dwt1, quarter)]))
    dxn_fg = _matmul(dfg, w_fg_t, name="in_proj_fgate_bwd")
    dxn, l_in2 = _matmul(dproj, w_main_t, name="in_proj_bwd", extras=(dxn_fg,),
                         epilogue=lambda acc, i, j, r: (r + acc,), job=_Scatter([in_slots(dwt2, 2 * quarter)]))
    dh0, _, dg_mix = _rmsnorm_bwd(h0, dxn, norm_mix, dh1, "norm_mix_bwd")

    small = [dg_mix, dbg0, dbg1, dg_mlp, dg_q, dg_k, db_f, loss_part, dcw, dh0[:N_META]]
    small_rows = [_rows_of(s) for s in small]
    pack = jnp.concatenate(small_rows, axis=0)
    pack = jnp.pad(pack, ((0, -pack.shape[0] % SUBLANES), (0, 0)))
    (pack_all,) = _run_job(_Scatter([], [pack]), "gather_small")

    landed = {"w_attn_out": [l_ao], "w_conv_out": [l_co], "w_o": [l_o],
              "w_up": l_up0 + [l_up1], "w_down": l_down0 + l_down1}
    shards = {"w_attn_out": (w_attn_out, m_w_attn_out, v_w_attn_out),
              "w_conv_out": (w_conv_out, m_w_conv_out, v_w_conv_out), "w_o": (w_o, m_w_o, v_w_o),
              "w_up": (w_up, m_w_up, v_w_up), "w_down": (w_down, m_w_down, v_w_down)}
    out = {}
    for nm, chunks in landed.items():
        w_, m_, v_ = shards[nm]
        res = _adamw(list(chunks), w_[0], m_[0], v_[0], "adamw_" + nm)
        out[nm] = [r[None] for r in res]
    res = _adamw_cols(l_in0 + l_in1 + l_in2, w_in_t, m_in_t, v_in_t, "adamw_w_in")
    out["w_in"] = [r.T[None] for r in res]

    total = _sum_parts(pack_all, "sum_small")
    pieces, at = [], 0
    for s, rows in zip(small, small_rows):
        n_el = 1
        for dim in s.shape:
            n_el *= dim
        pieces.append(total[at:at + rows.shape[0]].reshape(-1)[:n_el].reshape(s.shape))
        at += rows.shape[0]
    g_mix, g_bg0, g_bg1, g_mlp, g_q, g_k, g_bf, loss_row, g_cw_full, g_meta_full = pieces
    loss = loss_row[0, 0]
    cshard = conv_w.shape[2]
    g_small = {
        "norm_mix": g_mix, "b_gate": jnp.concatenate([g_bg0, g_bg1], axis=1), "norm_mlp": g_mlp,
        "q_norm": g_q, "k_norm": g_k, "b_fgate": g_bf[:, :heads],
        "conv_w": lax.dynamic_slice_in_dim(g_cw_full[:conv_w.shape[1]], me * cshard, cshard, axis=1)[None],
        "meta_tokens": lax.dynamic_slice_in_dim(g_meta_full, me * (d // N_DEV), d // N_DEV, axis=1),
    }
    small_w = {"norm_mix": (norm_mix, m_norm_mix, v_norm_mix), "b_gate": (b_gate, m_b_gate, v_b_gate),
               "norm_mlp": (norm_mlp, m_norm_mlp, v_norm_mlp), "q_norm": (q_norm, m_q_norm, v_q_norm),
               "k_norm": (k_norm, m_k_norm, v_k_norm), "b_fgate": (b_fgate, m_b_fgate, v_b_fgate),
               "conv_w": (conv_w, m_conv_w, v_conv_w), "meta_tokens": (meta_tokens, m_meta_tokens, v_meta_tokens)}
    order = list(small_w)
    packed = []
    for idx in range(4):
        cols = [g_small[nm] if idx == 0 else small_w[nm][idx - 1] for nm in order]
        rows = jnp.concatenate([_rows_of(c_) for c_ in cols], axis=0)
        packed.append(jnp.pad(rows, ((0, -rows.shape[0] % SUBLANES), (0, 0))))
    res = _adamw([packed[0][None]], packed[1], packed[2], packed[3], "adamw_small")
    at = 0
    for nm in order:
        shape = small_w[nm][0].shape
        n_el = 1
        for dim in shape:
            n_el *= dim
        n_rows = -(-n_el // LANES)
        out[nm] = [r[at:at + n_rows].reshape(-1)[:n_el].reshape(shape) for r in res]
        at += n_rows

    weights = ["meta_tokens", "norm_mix", "w_in", "b_fgate", "b_gate", "q_norm", "k_norm", "conv_w",
               "w_attn_out", "w_conv_out", "w_o", "norm_mlp", "w_up", "w_down"]
    grad_x = dh0[N_META:n_valid][None]
    return (loss, grad_x, *[out[nm][0] for nm in weights], *[out[nm][1] for nm in weights],
            *[out[nm][2] for nm in weights], *[out[nm][3] for nm in weights])
```

```python
import functools

import jax
import jax.numpy as jnp
from jax import lax
from jax.experimental import pallas as pl
from jax.experimental.pallas import tpu as pltpu

F32 = jnp.float32
BF16 = jnp.bfloat16

N_DEV = 8
N_META = 16
HEAD_DIM = 128
LANES = 128
SUBLANES = 8
EPS = 1e-6
VMEM_LIMIT = 56 * 1024 * 1024

ADAM_LR = 0.001
ADAM_B1 = 0.9
ADAM_B2 = 0.999
ADAM_EPS = 1e-08
ADAM_WD = 0.01
ADAM_STEP = 10

MESH = pl.DeviceIdType.MESH
HBM_SPEC = pl.BlockSpec(memory_space=pltpu.HBM)
RELATIONS = tuple((r >> 2 & 1, r >> 1 & 1, r & 1) for r in range(1, N_DEV))


def _params(semantics=None):
    return pltpu.CompilerParams(dimension_semantics=semantics, vmem_limit_bytes=VMEM_LIMIT)


def _tile(n, prefs):
    for p in prefs:
        if n % p == 0:
            return p
    return n


def _sds(shape, dtype):
    return jax.ShapeDtypeStruct(shape, dtype)


def _my_place():
    return lax.axis_index("x"), lax.axis_index("y"), lax.axis_index("c")


def _flat(px, py, pc):
    return 4 * px + 2 * py + pc


class _Gather:
    def __init__(self, arrays):
        self.operands = list(arrays)
        self.n = len(arrays)
        self.out_shape = [_sds((N_DEV,) + a.shape, a.dtype) for a in arrays]

    def _copy(self, srcs, outs, sems, a, k, block, to, from_src=False):
        slot = outs[a].at[_flat(*block)]
        return pltpu.make_async_remote_copy(
            src_ref=srcs[a] if from_src else slot, dst_ref=slot,
            send_sem=sems[0].at[a, k], recv_sem=sems[1].at[a, k],
            device_id=to, device_id_type=MESH)

    def _places(self):
        x, y, c = _my_place()
        return (x, y, c), (x, y, 1 - c), [(1 - x, y), (x, 1 - y), (1 - x, 1 - y)], c

    def start(self, srcs, outs, sems):
        me, sibling, chips, c = self._places()
        for a in range(self.n):
            pltpu.make_async_copy(srcs[a], outs[a].at[_flat(*me)], sems[2].at[a]).start()
            for j, chip in enumerate(chips):
                self._copy(srcs, outs, sems, a, 1 + j, me, (*chip, c), from_src=True).start()
            self._copy(srcs, outs, sems, a, 0, me, sibling, from_src=True).start()

    def mid(self, srcs, outs, sems):
        me, sibling, chips, c = self._places()
        for a in range(self.n):
            for j, chip in enumerate(chips):
                self._copy(srcs, outs, sems, a, 1 + j, (*chip, c), me).wait_recv()
                self._copy(srcs, outs, sems, a, 4 + j, (*chip, c), sibling).start()

    def finish(self, srcs, outs, sems):
        me, sibling, chips, c = self._places()
        for a in range(self.n):
            self._copy(srcs, outs, sems, a, 0, sibling, me).wait_recv()
            for j, chip in enumerate(chips):
                self._copy(srcs, outs, sems, a, 4 + j, (*chip, 1 - c), me).wait_recv()
            for k in range(7):
                self._copy(srcs, outs, sems, a, k, me, sibling).wait_send()
            pltpu.make_async_copy(srcs[a], outs[a].at[_flat(*me)], sems[2].at[a]).wait()


class _Scatter:
    def __init__(self, scatter, gather=()):
        scatter = [s if isinstance(s, tuple) else (s, 0, s.shape[1]) for s in scatter]
        self.ranges = [(lo, cnt) for _, lo, cnt in scatter]
        self.operands = [s[0] for s in scatter] + list(gather)
        self.ns, self.n = len(scatter), len(scatter) + len(gather)
        self.out_shape = ([_sds((N_DEV, cnt, arr.shape[2]), arr.dtype) for arr, _, cnt in scatter]
                          + [_sds((N_DEV,) + a.shape, a.dtype) for a in gather])

    def _peer(self, rel):
        return tuple(1 - p if r else p for p, r in zip(_my_place(), rel))

    def _src(self, srcs, a, place):
        if a >= self.ns:
            return srcs[a]
        lo, cnt = self.ranges[a]
        return srcs[a].at[_flat(*place), pl.ds(lo, cnt)]

    def _send(self, srcs, outs, sems, a, k, rel):
        peer = self._peer(rel)
        return pltpu.make_async_remote_copy(
            src_ref=self._src(srcs, a, peer), dst_ref=outs[a].at[_flat(*_my_place())],
            send_sem=sems[0].at[a, k], recv_sem=sems[1].at[a, k],
            device_id=peer, device_id_type=MESH)

    def _landed(self, outs, sems, a, k, rel):
        peer = self._peer(rel)
        slot = outs[a].at[_flat(*peer)]
        return pltpu.make_async_remote_copy(
            src_ref=slot, dst_ref=slot, send_sem=sems[0].at[a, k], recv_sem=sems[1].at[a, k],
            device_id=peer, device_id_type=MESH)

    def _own(self, srcs, outs, sems, a):
        me = _my_place()
        return pltpu.make_async_copy(self._src(srcs, a, me), outs[a].at[_flat(*me)], sems[2].at[a])

    def start(self, srcs, outs, sems):
        for a in range(self.n):
            self._own(srcs, outs, sems, a).start()
            for k, rel in enumerate(RELATIONS):
                self._send(srcs, outs, sems, a, k, rel).start()

    def mid(self, srcs, outs, sems):
        pass

    def finish(self, srcs, outs, sems):
        for a in range(self.n):
            for k, rel in enumerate(RELATIONS):
                self._landed(outs, sems, a, k, rel).wait_recv()
            for k, rel in enumerate(RELATIONS):
                self._send(srcs, outs, sems, a, k, rel).wait_send()
            self._own(srcs, outs, sems, a).wait()


def _job_sems(job):
    return [pltpu.SemaphoreType.DMA((job.n, 7)), pltpu.SemaphoreType.DMA((job.n, 7)),
            pltpu.SemaphoreType.DMA((job.n,))]


def _run_job(job, name):
    n = job.n

    def body(*refs):
        srcs, outs, sems = refs[:n], refs[n:2 * n], refs[2 * n:]
        job.start(srcs, outs, sems)
        job.mid(srcs, outs, sems)
        job.finish(srcs, outs, sems)

    return pl.pallas_call(
        body, name=name, out_shape=job.out_shape,
        in_specs=[HBM_SPEC] * n, out_specs=[HBM_SPEC] * n, scratch_shapes=_job_sems(job),
    )(*job.operands)


def _gather_first(job, meta_at, x, target, t, name):
    n = job.n
    seq, d = x.shape
    n_valid = N_META + seq
    pad = t - n_valid
    wide = d // N_DEV
    zeros = jnp.zeros((max(pad, N_META), d), F32)

    def body(*refs):
        srcs, (x_ref, tgt_ref, z_ref) = refs[:n], refs[n:n + 3]
        outs, (h0_ref, tp_ref) = refs[n + 3:2 * n + 3], refs[2 * n + 3:2 * n + 5]
        sems, local = refs[2 * n + 5:2 * n + 8], refs[2 * n + 8]
        moves = [(x_ref, h0_ref.at[pl.ds(N_META, seq)]), (tgt_ref, tp_ref.at[pl.ds(N_META, seq)]),
                 (z_ref.at[pl.ds(0, N_META)], tp_ref.at[pl.ds(0, N_META)])]
        if pad:
            moves += [(z_ref.at[pl.ds(0, pad)], h0_ref.at[pl.ds(n_valid, pad)]),
                      (z_ref.at[pl.ds(0, pad)], tp_ref.at[pl.ds(n_valid, pad)])]
        copies = [pltpu.make_async_copy(src, dst, local.at[i]) for i, (src, dst) in enumerate(moves)]
        for cp in copies:
            cp.start()
        job.start(srcs, outs, sems)
        job.mid(srcs, outs, sems)
        job.finish(srcs, outs, sems)
        for j in range(N_DEV):
            cp = pltpu.make_async_copy(outs[meta_at].at[j], h0_ref.at[pl.ds(0, N_META), pl.ds(j * wide, wide)],
                                       local.at[len(moves) + j])
            cp.start()
            copies.append(cp)
        for cp in copies:
            cp.wait()

    res = pl.pallas_call(
        body, name=name, out_shape=job.out_shape + [_sds((t, d), F32)] * 2,
        in_specs=[HBM_SPEC] * (n + 3), out_specs=[HBM_SPEC] * (n + 2),
        scratch_shapes=_job_sems(job) + [pltpu.SemaphoreType.DMA((5 + N_DEV,))],
    )(*job.operands, x, target, zeros)
    return res[:n], res[n], res[n + 1]


def _call(body, *, name, grid, in_specs, out_specs, out_shape, scratch_shapes, semantics,
          operands, job=None, mid_at=0.5):
    if job is None:
        res = pl.pallas_call(
            body, name=name, grid=grid, in_specs=in_specs, out_specs=out_specs, out_shape=out_shape,
            scratch_shapes=scratch_shapes, compiler_params=_params(semantics))(*operands)
        return res, []
    n_in, n_out, n_scr = len(in_specs), len(out_specs), len(scratch_shapes)
    total = 1
    for g in grid:
        total *= g
    mid_step = min(int(total * mid_at), total - 1)

    def carried(*refs):
        c_in, j_in = refs[:n_in], refs[n_in:n_in + job.n]
        o0 = n_in + job.n
        c_out, j_out = refs[o0:o0 + n_out], refs[o0 + n_out:o0 + n_out + job.n]
        s0 = o0 + n_out + job.n
        c_scr, sems = refs[s0:s0 + n_scr], refs[s0 + n_scr:]
        step = pl.program_id(0)
        for ax in range(1, len(grid)):
            step = step * grid[ax] + pl.program_id(ax)

        @pl.when(step == 0)
        def _():
            job.start(j_in, j_out, sems)

        body(*c_in, *c_out, *c_scr)

        @pl.when(step == mid_step)
        def _():
            job.mid(j_in, j_out, sems)

        @pl.when(step == total - 1)
        def _():
            job.finish(j_in, j_out, sems)

    res = pl.pallas_call(
        carried, name=name, grid=grid,
        in_specs=list(in_specs) + [HBM_SPEC] * job.n,
        out_specs=list(out_specs) + [HBM_SPEC] * job.n,
        out_shape=list(out_shape) + job.out_shape,
        scratch_shapes=list(scratch_shapes) + _job_sems(job),
        compiler_params=_params(("arbitrary",) * len(grid)),
    )(*operands, *job.operands)
    return list(res[:n_out]), list(res[n_out:])


def _matmul(a, b, *, name, trans_b=False, extras=(), epilogue=None, out_dtypes=(F32,),
            tm=None, tn=None, tk=None, rows=None, cols=None, trans_a=False, slots=False, job=None,
            mid_at=0.5):
    k, m = a.shape if trans_a else a.shape[::-1]
    n = b.shape[0] if trans_b else b.shape[1]
    tm = tm or _tile(m, (1408, 1024, 512, 256, 128))
    tn = tn or _tile(n // N_DEV if slots else n, (512, 256, 128))
    tk = tk or _tile(k, (2048, 1408, 1024, 512, 256, 128))
    nk = k // tk
    row0, n_rows = rows or (0, m // tm)
    m = n_rows * tm
    col0, n_cols = cols or (0, n // tn)
    n = n_cols * tn
    n_ex, n_out = len(extras), len(out_dtypes)
    dims = (((0,) if trans_a else (1,), (1,) if trans_b else (0,)), ((), ()))

    def body(*refs):
        a_ref, b_ref = refs[:2]
        ex_refs = refs[2:2 + n_ex]
        out_refs = refs[2 + n_ex:2 + n_ex + n_out]
        part = lax.dot_general(a_ref[...].astype(BF16), b_ref[...].astype(BF16), dims,
                               preferred_element_type=F32)

        def finish(acc):
            if epilogue is None:
                res = (acc,)
            else:
                res = epilogue(acc, pl.program_id(0), pl.program_id(1), *[e[...] for e in ex_refs])
            for o_ref, r in zip(out_refs, res):
                o_ref[...] = r.astype(o_ref.dtype)

        if nk == 1:
            finish(part)
        else:
            acc_ref = refs[-1]
            kk = pl.program_id(2)

            @pl.when(kk == 0)
            def _():
                acc_ref[...] = part

            @pl.when(kk > 0)
            def _():
                acc_ref[...] += part

            @pl.when(kk == nk - 1)
            def _():
                finish(acc_ref[...])

    in_specs = [pl.BlockSpec((tk, tm), lambda i, j, kk: (kk, row0 + i)) if trans_a
                else pl.BlockSpec((tm, tk), lambda i, j, kk: (row0 + i, kk)),
                pl.BlockSpec((tn, tk), lambda i, j, kk: (col0 + j, kk)) if trans_b
                else pl.BlockSpec((tk, tn), lambda i, j, kk: (kk, col0 + j))]
    for e in extras:
        if e.shape[0] == 1:
            in_specs.append(pl.BlockSpec((1, tn), lambda i, j, kk: (0, j)))
        else:
            in_specs.append(pl.BlockSpec((tm, tn), lambda i, j, kk: (i, j)))
    if slots:
        per_slot = n // N_DEV // tn
        out_spec = pl.BlockSpec((None, tm, tn), lambda i, j, kk: (j // per_slot, i, j % per_slot))
        out_shape = [_sds((N_DEV, m, n // N_DEV), d) for d in out_dtypes]
    else:
        out_spec = pl.BlockSpec((tm, tn), lambda i, j, kk: (i, j))
        out_shape = [_sds((m, n), d) for d in out_dtypes]
    res, moved = _call(
        body, name=name, grid=(n_rows, n // tn, nk),
        in_specs=in_specs,
        out_specs=[out_spec] * n_out,
        out_shape=out_shape,
        scratch_shapes=[pltpu.VMEM((tm, tn), F32)] if nk > 1 else [],
        semantics=("parallel", "parallel", "arbitrary"),
        operands=(a, b, *extras), job=job, mid_at=mid_at)
    res = res[0] if n_out == 1 else tuple(res)
    return res if job is None else (res, moved)


def _rstd(x):
    return lax.rsqrt(jnp.mean(x * x, axis=-1, keepdims=True) + EPS)


def _norm_bwd(x, dy, g):
    r = _rstd(x)
    u = dy * g
    dx = r * u - x * (r * r * r) * jnp.mean(u * x, axis=-1, keepdims=True)
    return dx, dy * (x * r)


def _rmsnorm_fwd(h, g, name):
    t, d = h.shape
    tr = _tile(t, (384, 256, 128))

    def body(h_ref, g_ref, o_ref):
        x = h_ref[...]
        o_ref[...] = ((x * _rstd(x)) * g_ref[...]).astype(o_ref.dtype)

    row = pl.BlockSpec((tr, d), lambda i: (i, 0))
    return pl.pallas_call(
        body, name=name, grid=(t // tr,),
        in_specs=[row, pl.BlockSpec((1, d), lambda i: (0, 0))], out_specs=row,
        out_shape=_sds((t, d), BF16), compiler_params=_params(("parallel",)),
    )(h, g)


def _rmsnorm_bwd(h, dy, g, res, name):
    t, d = h.shape
    tr = _tile(t, (384, 256, 128))

    def body(h_ref, dy_ref, g_ref, res_ref, dh_ref, dhb_ref, dg_ref):
        dx, dg_rows = _norm_bwd(h_ref[...], dy_ref[...], g_ref[...])
        dh = res_ref[...] + dx
        dh_ref[...] = dh
        dhb_ref[...] = dh.astype(BF16)

        @pl.when(pl.program_id(0) == 0)
        def _():
            dg_ref[...] = jnp.zeros_like(dg_ref)

        dg_ref[...] += jnp.sum(dg_rows, axis=0, keepdims=True)

    row = pl.BlockSpec((tr, d), lambda i: (i, 0))
    vec = pl.BlockSpec((1, d), lambda i: (0, 0))
    return pl.pallas_call(
        body, name=name, grid=(t // tr,),
        in_specs=[row, row, vec, row], out_specs=[row, row, vec],
        out_shape=[_sds((t, d), F32), _sds((t, d), BF16), _sds((1, d), F32)],
        compiler_params=_params(("arbitrary",)),
    )(h, dy, g, res)


def _qk_prep(proj, gq, gk, aw, name):
    t = proj.shape[0]
    heads = aw // HEAD_DIM
    tr = _tile(t, (384, 256, 128))

    def body(q_ref, k_ref, v_ref, gq_ref, gk_ref, qo_ref, ko_ref, vo_ref):
        for h in range(heads):
            sl = slice(h * HEAD_DIM, (h + 1) * HEAD_DIM)
            xq, xk = q_ref[:, sl].astype(F32), k_ref[:, sl].astype(F32)
            qo_ref[:, sl] = ((xq * _rstd(xq)) * gq_ref[...]).astype(BF16)
            ko_ref[:, sl] = ((xk * _rstd(xk)) * gk_ref[...]).astype(BF16)
        vo_ref[...] = v_ref[...].astype(BF16)

    vec = pl.BlockSpec((1, HEAD_DIM), lambda i: (0, 0))
    out = pl.BlockSpec((tr, aw), lambda i: (i, 0))
    return pl.pallas_call(
        body, name=name, grid=(t // tr,),
        in_specs=[pl.BlockSpec((tr, aw), lambda i: (i, 0)), pl.BlockSpec((tr, aw), lambda i: (i, 1)),
                  pl.BlockSpec((tr, aw), lambda i: (i, 2)), vec, vec],
        out_specs=[out, out, out], out_shape=[_sds((t, aw), BF16)] * 3,
        compiler_params=_params(("parallel",)),
    )(proj, proj, proj, gq, gk)


def _qk_bwd(dqn, dkn, proj, gq, gk, aw, name):
    t = proj.shape[0]
    heads = aw // HEAD_DIM
    tr = _tile(t, (384, 256, 128))

    def body(dq_ref, dk_ref, q_ref, k_ref, gq_ref, gk_ref, dqo_ref, dko_ref, dgq_ref, dgk_ref):
        @pl.when(pl.program_id(0) == 0)
        def _():
            dgq_ref[...] = jnp.zeros_like(dgq_ref)
            dgk_ref[...] = jnp.zeros_like(dgk_ref)

        for h in range(heads):
            sl = slice(h * HEAD_DIM, (h + 1) * HEAD_DIM)
            dx, dg_rows = _norm_bwd(q_ref[:, sl].astype(F32), dq_ref[:, sl], gq_ref[...])
            dqo_ref[:, sl] = dx.astype(BF16)
            dgq_ref[...] += jnp.sum(dg_rows, axis=0, keepdims=True)
            dx, dg_rows = _norm_bwd(k_ref[:, sl].astype(F32), dk_ref[:, sl], gk_ref[...])
            dko_ref[:, sl] = dx.astype(BF16)
            dgk_ref[...] += jnp.sum(dg_rows, axis=0, keepdims=True)

    vec = pl.BlockSpec((1, HEAD_DIM), lambda i: (0, 0))
    row = pl.BlockSpec((tr, aw), lambda i: (i, 0))
    return pl.pallas_call(
        body, name=name, grid=(t // tr,),
        in_specs=[row, row, row, pl.BlockSpec((tr, aw), lambda i: (i, 1)), vec, vec],
        out_specs=[row, row, vec, vec],
        out_shape=[_sds((t, aw), BF16), _sds((t, aw), BF16), _sds((1, HEAD_DIM), F32), _sds((1, HEAD_DIM), F32)],
        compiler_params=_params(("arbitrary",)),
    )(dqn, dkn, proj, proj, gq, gk)


def _triangle(lower):
    r = lax.broadcasted_iota(jnp.int32, (LANES, LANES), 0)
    c = lax.broadcasted_iota(jnp.int32, (LANES, LANES), 1)
    return ((c <= r) if lower else (c >= r)).astype(F32)


def _forget_fwd(fg, b, name):
    t = fg.shape[0]

    def body(fg_ref, b_ref, cum_ref, carry):
        @pl.when(pl.program_id(0) == 0)
        def _():
            carry[...] = jnp.zeros_like(carry)

        z = fg_ref[...] + b_ref[...]
        log_f = jnp.minimum(z, 0.0) - jnp.log1p(jnp.exp(-jnp.abs(z)))
        cs = jnp.dot(_triangle(True), log_f, precision=lax.Precision.HIGHEST,
                     preferred_element_type=F32) + carry[0:1, :]
        cum_ref[...] = cs
        carry[...] = jnp.broadcast_to(cs[LANES - 1:LANES, :], carry.shape)

    row = pl.BlockSpec((LANES, LANES), lambda i: (i, 0))
    return pl.pallas_call(
        body, name=name, grid=(t // LANES,),
        in_specs=[row, pl.BlockSpec((1, LANES), lambda i: (0, 0))], out_specs=row,
        out_shape=_sds((t, LANES), F32), scratch_shapes=[pltpu.VMEM((SUBLANES, LANES), F32)],
        compiler_params=_params(("arbitrary",)),
    )(fg, b)


def _forget_bwd(dcum, fg, b, name):
    t = fg.shape[0]
    nt = t // LANES

    def body(dc_ref, fg_ref, b_ref, dfg_ref, db_ref, carry):
        @pl.when(pl.program_id(0) == 0)
        def _():
            carry[...] = jnp.zeros_like(carry)
            db_ref[...] = jnp.zeros_like(db_ref)

        d_log_f = jnp.dot(_triangle(False), dc_ref[...], precision=lax.Precision.HIGHEST,
                          preferred_element_type=F32) + carry[0:1, :]
        carry[...] = jnp.broadcast_to(d_log_f[0:1, :], carry.shape)
        dz = d_log_f * jax.nn.sigmoid(-(fg_ref[...] + b_ref[...]))
        dfg_ref[...] = dz.astype(BF16)
        db_ref[...] += jnp.sum(dz, axis=0, keepdims=True)

    row = pl.BlockSpec((LANES, LANES), lambda i: (nt - 1 - i, 0))
    vec = pl.BlockSpec((1, LANES), lambda i: (0, 0))
    return pl.pallas_call(
        body, name=name, grid=(nt,),
        in_specs=[row, row, vec], out_specs=[row, vec],
        out_shape=[_sds((t, LANES), BF16), _sds((1, LANES), F32)],
        scratch_shapes=[pltpu.VMEM((SUBLANES, LANES), F32)],
        compiler_params=_params(("arbitrary",)),
    )(dcum, fg, b)


def _causal(qi, kj, tq):
    rows = qi * tq + lax.broadcasted_iota(jnp.int32, (tq, tq), 0)
    cols = kj * tq + lax.broadcasted_iota(jnp.int32, (tq, tq), 1)
    return cols <= rows


NT_DIMS = (((1,), (1,)), ((), ()))
TN_DIMS = (((0,), (0,)), ((), ()))


def _attn_tile():
    return (384, 256, 128)


def _attn_fwd(q, k, v, cum_row, name, job=None, mid_at=0.5):
    t, aw = q.shape
    heads = aw // HEAD_DIM
    tq = _tile(t, _attn_tile())
    nq = t // tq
    scale = HEAD_DIM ** -0.5

    def body(q_ref, k_ref, v_ref, ck_ref, o_ref, of_ref, lse_ref):
        qi = pl.program_id(1)
        qv = q_ref[...]

        def tile(kj, carry, masked):
            m_prev, l_prev, acc, res = carry
            ks = pl.ds(pl.multiple_of(kj * tq, tq), tq)
            s = lax.dot_general(qv, k_ref[ks, :], NT_DIMS, preferred_element_type=F32) * scale - ck_ref[kj]
            if masked:
                s = jnp.where(_causal(0, 0, tq), s, -jnp.inf)
            m_new = jnp.maximum(m_prev, jnp.max(s, axis=-1, keepdims=True))
            alpha = jnp.exp(m_prev - m_new)
            p = jnp.exp(s - m_new)
            p_hi = p.astype(BF16)
            p_lo = (p - p_hi.astype(F32)).astype(BF16)
            vv = v_ref[ks, :]
            return (m_new, alpha * l_prev + jnp.sum(p, axis=-1, keepdims=True),
                    alpha * acc + jnp.dot(p_hi, vv, preferred_element_type=F32),
                    alpha * res + jnp.dot(p_lo, vv, preferred_element_type=F32))

        init = (jnp.full((tq, 1), -jnp.inf, F32), jnp.zeros((tq, 1), F32),
                jnp.zeros((tq, HEAD_DIM), F32), jnp.zeros((tq, HEAD_DIM), F32))
        carry = lax.fori_loop(0, qi, lambda kj, c: tile(kj, c, False), init)
        m_fin, l_fin, acc, res = tile(qi, carry, True)
        o_ref[...] = (acc / l_fin).astype(o_ref.dtype)
        of_ref[...] = (acc + res) / l_fin
        lse_ref[...] = m_fin + jnp.log(l_fin)

    q_spec = pl.BlockSpec((tq, HEAD_DIM), lambda h, i: (i, h))
    head = pl.BlockSpec((t, HEAD_DIM), lambda h, i: (0, h))
    return _call(
        body, name=name, grid=(heads, nq),
        in_specs=[q_spec, head, head, pl.BlockSpec((None, nq, 1, tq), lambda h, i: (h, 0, 0, 0))],
        out_specs=[q_spec, q_spec, pl.BlockSpec((None, tq, 1), lambda h, i: (h, i, 0))],
        out_shape=[_sds((t, aw), BF16), _sds((t, aw), F32), _sds((heads, t, 1), F32)],
        scratch_shapes=[], semantics=("parallel", "arbitrary"),
        operands=(q, k, v, cum_row), job=job, mid_at=mid_at)


def _attn_stats(do, o, name):
    t, aw = o.shape
    heads = aw // HEAD_DIM
    tr = _tile(t, (384, 256, 128))

    def body(do_ref, o_ref, delta_ref):
        for h in range(heads):
            sl = slice(h * HEAD_DIM, (h + 1) * HEAD_DIM)
            do_seen = do_ref[:, sl].astype(BF16).astype(F32)
            delta_ref[h] = jnp.sum(do_seen * o_ref[:, sl], axis=-1, keepdims=True)

    row = pl.BlockSpec((tr, aw), lambda i: (i, 0))
    return pl.pallas_call(
        body, name=name, grid=(t // tr,),
        in_specs=[row, row], out_specs=pl.BlockSpec((heads, tr, 1), lambda i: (0, i, 0)),
        out_shape=_sds((heads, t, 1), F32), compiler_params=_params(("parallel",)),
    )(do, o)


def _attn_bwd(q, k, v, do, lse, delta, cum_row, name, job=None):
    t, aw = q.shape
    heads = aw // HEAD_DIM
    tq = _tile(t, _attn_tile())
    nq = t // tq
    scale = HEAD_DIM ** -0.5

    def body(q_ref, k_ref, v_ref, do_ref, lse_ref, delta_ref, ck_ref, dq_ref, dk_ref, dv_ref, dck_ref):
        kj = pl.program_id(1)

        @pl.when(kj == 0)
        def _():
            dq_ref[...] = jnp.zeros_like(dq_ref)

        kv, vv, ck = k_ref[...], v_ref[...], ck_ref[...]

        def tile(qi, carry, masked):
            dk_acc, dv_acc, dck_acc = carry
            rows = pl.ds(pl.multiple_of(qi * tq, tq), tq)
            qv, dov = q_ref[rows, :], do_ref[rows, :].astype(BF16)
            s = lax.dot_general(qv, kv, NT_DIMS, preferred_element_type=F32) * scale - ck - lse_ref[rows, :]
            p = jnp.exp(s)
            if masked:
                p = jnp.where(_causal(0, 0, tq), p, 0.0)
            dp = lax.dot_general(dov, vv, NT_DIMS, preferred_element_type=F32)
            ds = p * (dp - delta_ref[rows, :])
            dsb = ds.astype(BF16)
            dq_ref[rows, :] += jnp.dot(dsb, kv, preferred_element_type=F32) * scale
            return (dk_acc + lax.dot_general(dsb, qv, TN_DIMS, preferred_element_type=F32),
                    dv_acc + lax.dot_general(p.astype(BF16), dov, TN_DIMS, preferred_element_type=F32),
                    dck_acc + jnp.sum(ds, axis=0, keepdims=True))

        init = (jnp.zeros((tq, HEAD_DIM), F32), jnp.zeros((tq, HEAD_DIM), F32), jnp.zeros((1, tq), F32))
        carry = tile(kj, init, True)
        dk_acc, dv_acc, dck_acc = lax.fori_loop(kj + 1, nq, lambda qi, c: tile(qi, c, False), carry)
        dk_ref[...] = dk_acc * scale
        dv_ref[...] = dv_acc.astype(dv_ref.dtype)
        dck_ref[...] = -dck_acc

    head = pl.BlockSpec((t, HEAD_DIM), lambda h, j: (0, h))
    k_spec = pl.BlockSpec((tq, HEAD_DIM), lambda h, j: (j, h))
    col = pl.BlockSpec((None, t, 1), lambda h, j: (h, 0, 0))
    row = pl.BlockSpec((None, 1, tq), lambda h, j: (h, 0, j))
    return _call(
        body, name=name, grid=(heads, nq),
        in_specs=[head, k_spec, k_spec, head, col, col, row],
        out_specs=[head, k_spec, k_spec, row],
        out_shape=[_sds((t, aw), F32), _sds((t, aw), F32), _sds((t, aw), BF16), _sds((heads, 1, t), F32)],
        scratch_shapes=[], semantics=("parallel", "arbitrary"),
        operands=(q, k, v, do, lse, delta, cum_row), job=job)


def _shift_down(u, by):
    rows = lax.broadcasted_iota(jnp.int32, u.shape, 0)
    return jnp.where(rows >= by, pltpu.roll(u, by, 0), 0.0)


def _shift_up(u, by):
    t = u.shape[0]
    rows = lax.broadcasted_iota(jnp.int32, u.shape, 0)
    return jnp.where(rows < t - by, pltpu.roll(u, t - by, 0), 0.0)


def _conv_specs(t, off_b, cw_width):
    nb = cw_width // LANES
    base = off_b // LANES
    return [pl.BlockSpec((t, LANES), lambda j, s=s: (0, base + s * nb + j)) for s in range(3)]


def _conv_fwd(proj, cw, off_b, name):
    t = proj.shape[0]
    width = cw.shape[1]

    def body(cb_ref, cc_ref, cx_ref, w_ref, o_ref):
        u = cc_ref[...].astype(F32) * cx_ref[...]
        y = w_ref[0:1, :] * _shift_down(u, 2) + w_ref[1:2, :] * _shift_down(u, 1) + w_ref[2:3, :] * u
        o_ref[...] = (cb_ref[...] * y).astype(BF16)

    return pl.pallas_call(
        body, name=name, grid=(width // LANES,),
        in_specs=_conv_specs(t, off_b, width) + [pl.BlockSpec((SUBLANES, LANES), lambda j: (0, j))],
        out_specs=pl.BlockSpec((t, LANES), lambda j: (0, j)),
        out_shape=_sds((t, width), BF16), compiler_params=_params(("parallel",)),
    )(proj, proj, proj, cw)


def _conv_bwd(dcp, proj, cw, off_b, name):
    t = proj.shape[0]
    width = cw.shape[1]

    def body(d_ref, cb_ref, cc_ref, cx_ref, w_ref, dcb_ref, dcc_ref, dcx_ref, dw_ref):
        cc, cx = cc_ref[...].astype(F32), cx_ref[...].astype(F32)
        u = cc * cx
        u1, u2 = _shift_down(u, 1), _shift_down(u, 2)
        w0, w1, w2 = w_ref[0:1, :], w_ref[1:2, :], w_ref[2:3, :]
        d = d_ref[...]
        dcb_ref[...] = (d * (w0 * u2 + w1 * u1 + w2 * u)).astype(BF16)
        dy = d * cb_ref[...]
        du = w2 * dy + w1 * _shift_up(dy, 1) + w0 * _shift_up(dy, 2)
        dcc_ref[...] = (du * cx).astype(BF16)
        dcx_ref[...] = (du * cc).astype(BF16)
        dw = [jnp.sum(dy * s, axis=0, keepdims=True) for s in (u2, u1, u)]
        dw_ref[...] = jnp.concatenate(dw + [jnp.zeros((SUBLANES - 3, LANES), F32)], axis=0)

    col = pl.BlockSpec((t, LANES), lambda j: (0, j))
    wspec = pl.BlockSpec((SUBLANES, LANES), lambda j: (0, j))
    return pl.pallas_call(
        body, name=name, grid=(width // LANES,),
        in_specs=[col] + _conv_specs(t, off_b, width) + [wspec],
        out_specs=[col, col, col, wspec],
        out_shape=[_sds((t, width), BF16)] * 3 + [_sds((SUBLANES, width), F32)],
        compiler_params=_params(("parallel",)),
    )(dcp, proj, proj, proj, cw)


def _gate_specs(t, d, off_g, tr, tc, rows_first):
    nb = d // tc
    base = off_g // tc
    if rows_first:
        tile = lambda s: pl.BlockSpec((tr, tc), lambda i, j: (i, base + s * nb + j))
        vec = lambda s: pl.BlockSpec((1, tc), lambda i, j: (0, s * nb + j))
        plain = pl.BlockSpec((tr, tc), lambda i, j: (i, j))
    else:
        tile = lambda s: pl.BlockSpec((tr, tc), lambda j, i: (i, base + s * nb + j))
        vec = lambda s: pl.BlockSpec((1, tc), lambda j, i: (0, s * nb + j))
        plain = pl.BlockSpec((tr, tc), lambda j, i: (i, j))
    return tile, vec, plain


def _gate_fwd(a, c, proj, bg, off_g, name):
    t, d = a.shape
    tr, tc = _tile(t, (384, 256, 128)), _tile(d, (512, 256, 128))
    tile, vec, plain = _gate_specs(t, d, off_g, tr, tc, True)

    def body(a_ref, c_ref, g0_ref, g1_ref, b0_ref, b1_ref, o_ref):
        g0 = jax.nn.sigmoid(g0_ref[...] + b0_ref[...])
        g1 = jax.nn.sigmoid(g1_ref[...] + b1_ref[...])
        o_ref[...] = (g0 * a_ref[...] + g1 * c_ref[...]).astype(BF16)

    return pl.pallas_call(
        body, name=name, grid=(t // tr, d // tc),
        in_specs=[plain, plain, tile(0), tile(1), vec(0), vec(1)], out_specs=plain,
        out_shape=_sds((t, d), BF16), compiler_params=_params(("parallel", "parallel")),
    )(a, c, proj, proj, bg, bg)


def _gate_bwd(dm, a, c, proj, bg, off_g, name):
    t, d = a.shape
    tr, tc = _tile(t, (384, 256, 128)), _tile(d, (512, 256, 128))
    tile, vec, plain = _gate_specs(t, d, off_g, tr, tc, False)

    def body(dm_ref, a_ref, c_ref, g0_ref, g1_ref, b0_ref, b1_ref,
             da_ref, dc_ref, dg0_ref, dg1_ref, db0_ref, db1_ref):
        @pl.when(pl.program_id(1) == 0)
        def _():
            db0_ref[...] = jnp.zeros_like(db0_ref)
            db1_ref[...] = jnp.zeros_like(db1_ref)

        dm = dm_ref[...]
        g0 = jax.nn.sigmoid(g0_ref[...] + b0_ref[...])
        g1 = jax.nn.sigmoid(g1_ref[...] + b1_ref[...])
        da_ref[...] = (dm * g0).astype(BF16)
        dc_ref[...] = (dm * g1).astype(BF16)
        dz0 = dm * a_ref[...] * (g0 * (1.0 - g0))
        dz1 = dm * c_ref[...] * (g1 * (1.0 - g1))
        dg0_ref[...] = dz0.astype(BF16)
        dg1_ref[...] = dz1.astype(BF16)
        db0_ref[...] += jnp.sum(dz0, axis=0, keepdims=True)
        db1_ref[...] += jnp.sum(dz1, axis=0, keepdims=True)

    bvec = pl.BlockSpec((1, tc), lambda j, i: (0, j))
    return pl.pallas_call(
        body, name=name, grid=(d // tc, t // tr),
        in_specs=[plain, plain, plain, tile(0), tile(1), vec(0), vec(1)],
        out_specs=[plain] * 4 + [bvec, bvec],
        out_shape=[_sds((t, d), BF16)] * 4 + [_sds((1, d), F32)] * 2,
        compiler_params=_params(("parallel", "arbitrary")),
    )(dm, a, c, proj, proj, bg, bg)


def _sum_squares(x, name):
    t, d = x.shape
    tr = _tile(t, (384, 256, 128))

    def body(x_ref, o_ref):
        @pl.when(pl.program_id(0) == 0)
        def _():
            o_ref[...] = jnp.zeros_like(o_ref)

        v = x_ref[...]
        o_ref[...] += jnp.sum(jnp.sum(v * v, axis=0, keepdims=True), axis=1, keepdims=True)

    return pl.pallas_call(
        body, name=name, grid=(t // tr,),
        in_specs=[pl.BlockSpec((tr, d), lambda i: (i, 0))],
        out_specs=pl.BlockSpec((1, LANES), lambda i: (0, 0)),
        out_shape=_sds((1, LANES), F32), compiler_params=_params(("arbitrary",)),
    )(x)


def _row_tile(r, c):
    return r if r * c <= 128 * 1024 else _tile(r, (128, 64, 32, 16))


def _sum_parts(parts, name):
    n, r, c = parts.shape
    tr = _row_tile(r, c)

    def body(p_ref, o_ref):
        acc = p_ref[0].astype(F32)
        for i in range(1, n):
            acc = acc + p_ref[i].astype(F32)
        o_ref[...] = acc

    return pl.pallas_call(
        body, name=name, grid=(r // tr,),
        in_specs=[pl.BlockSpec((n, tr, c), lambda i: (0, i, 0))],
        out_specs=pl.BlockSpec((tr, c), lambda i: (i, 0)),
        out_shape=_sds((r, c), F32), compiler_params=_params(("parallel",)),
    )(parts)


def _adamw(chunks, w, m, v, name):
    n, rc, c = chunks[0].shape
    r = rc * len(chunks)
    tr = _row_tile(rc, c)
    per = rc // tr

    def body(*refs):
        p_refs = refs[:len(chunks)]
        w_ref, m_ref, v_ref, g_ref, d_ref, nm_ref, nv_ref = refs[len(chunks):]
        i = pl.program_id(0)

        def update(p_ref):
            g = p_ref[0].astype(F32)
            for s in range(1, n):
                g = g + p_ref[s].astype(F32)
            nm = ADAM_B1 * m_ref[...] + (1.0 - ADAM_B1) * g
            nv = ADAM_B2 * v_ref[...] + (1.0 - ADAM_B2) * (g * g)
            m_hat = nm / (1.0 - ADAM_B1 ** ADAM_STEP)
            v_hat = nv / (1.0 - ADAM_B2 ** ADAM_STEP)
            g_ref[...] = g
            d_ref[...] = -ADAM_LR * (m_hat / (jnp.sqrt(v_hat) + ADAM_EPS) + ADAM_WD * w_ref[...])
            nm_ref[...] = nm
            nv_ref[...] = nv

        if len(chunks) == 1:
            update(p_refs[0])
        else:
            for ci, p_ref in enumerate(p_refs):
                pl.when((i >= ci * per) & (i < (ci + 1) * per))(functools.partial(update, p_ref))

    row = pl.BlockSpec((tr, c), lambda i: (i, 0))
    part_specs = [pl.BlockSpec((n, tr, c), lambda i, ci=ci: (0, jnp.clip(i - ci * per, 0, per - 1), 0))
                  for ci in range(len(chunks))]
    return pl.pallas_call(
        body, name=name, grid=(r // tr,),
        in_specs=part_specs + [row, row, row],
        out_specs=[row] * 4, out_shape=[_sds((r, c), F32)] * 4,
        compiler_params=_params(("parallel",)),
    )(*chunks, w, m, v)


def _adamw_cols(chunks, w, m, v, name):
    n, r, _ = chunks[0].shape
    widths = [ch.shape[2] for ch in chunks]
    tc = min([LANES] + widths)
    firsts = [sum(widths[:ci]) // tc for ci in range(len(chunks) + 1)]

    def body(*refs):
        p_refs = refs[:len(chunks)]
        w_ref, m_ref, v_ref, g_ref, d_ref, nm_ref, nv_ref = refs[len(chunks):]
        j = pl.program_id(0)

        def update(p_ref):
            g = p_ref[0].astype(F32)
            for s in range(1, n):
                g = g + p_ref[s].astype(F32)
            nm = ADAM_B1 * m_ref[...] + (1.0 - ADAM_B1) * g
            nv = ADAM_B2 * v_ref[...] + (1.0 - ADAM_B2) * (g * g)
            m_hat = nm / (1.0 - ADAM_B1 ** ADAM_STEP)
            v_hat = nv / (1.0 - ADAM_B2 ** ADAM_STEP)
            g_ref[...] = g
            d_ref[...] = -ADAM_LR * (m_hat / (jnp.sqrt(v_hat) + ADAM_EPS) + ADAM_WD * w_ref[...])
            nm_ref[...] = nm
            nv_ref[...] = nv

        for ci, p_ref in enumerate(p_refs):
            pl.when((j >= firsts[ci]) & (j < firsts[ci + 1]))(functools.partial(update, p_ref))

    col = pl.BlockSpec((r, tc), lambda j: (0, j))
    part_specs = [pl.BlockSpec((n, r, tc),
                               lambda j, lo=firsts[ci], hi=firsts[ci + 1]: (0, 0, jnp.clip(j - lo, 0, hi - lo - 1)))
                  for ci in range(len(chunks))]
    return pl.pallas_call(
        body, name=name, grid=(firsts[-1],),
        in_specs=part_specs + [col, col, col],
        out_specs=[col] * 4, out_shape=[_sds((r, firsts[-1] * tc), F32)] * 4,
        compiler_params=_params(("parallel",)),
    )(*chunks, w, m, v)


def _pad_lanes(a, width=LANES):
    return jnp.pad(a, ((0, 0), (0, width - a.shape[1])))


def _rows_of(a):
    flat = a.reshape(-1)
    n = -(-flat.shape[0] // LANES) * LANES
    return jnp.pad(flat, (0, n - flat.shape[0])).reshape(-1, LANES)


def _columns_to_slots(full, n_rows):
    return full.reshape(n_rows, N_DEV, -1).transpose(1, 0, 2)


def _slots_to_columns(slots):
    return slots.transpose(1, 0, 2).reshape(slots.shape[1], -1)


def kernel(x, meta_tokens, norm_mix, w_in, b_fgate, b_gate, q_norm, k_norm, conv_w, w_attn_out, w_conv_out, w_o, norm_mlp, w_up, w_down, loss_target, m_meta_tokens, m_norm_mix, m_w_in, m_b_fgate, m_b_gate, m_q_norm, m_k_norm, m_conv_w, m_w_attn_out, m_w_conv_out, m_w_o, m_norm_mlp, m_w_up, m_w_down, v_meta_tokens, v_norm_mix, v_w_in, v_b_fgate, v_b_gate, v_q_norm, v_k_norm, v_conv_w, v_w_attn_out, v_w_conv_out, v_w_o, v_norm_mlp, v_w_up, v_w_down):
    seq, d = x.shape[1], x.shape[2]
    heads = b_fgate.shape[1]
    aw = heads * HEAD_DIM
    cwid = conv_w.shape[2] * N_DEV
    dff = w_up.shape[2] * N_DEV
    n_valid = N_META + seq
    t = -(-n_valid // LANES) * LANES
    me = _flat(*_my_place())
    off_cb, off_gl = 3 * aw, 3 * aw + 3 * cwid

    conv_shard = jnp.pad(conv_w[0], ((0, SUBLANES - conv_w.shape[1]), (0, 0)))
    w_in_t, m_in_t, v_in_t = (jnp.swapaxes(p, 1, 2)[0] for p in (w_in, m_w_in, v_w_in))
    (g_in, _, g_cw), h0, target = _gather_first(
        _Gather([w_in_t.astype(BF16), meta_tokens, conv_shard]), 1, x[0], loss_target[0], t, "gather_first")
    n_in = N_DEV * g_in.shape[1]
    w_all_t = g_in.reshape(n_in, d)
    w_main_t = jnp.concatenate([w_all_t[:3 * aw], w_all_t[3 * aw + heads:]], axis=0)
    w_fg_t = jnp.pad(w_all_t[3 * aw:3 * aw + heads], ((0, LANES - heads), (0, 0)))
    cw_full = _slots_to_columns(g_cw)

    b_f = _pad_lanes(b_fgate)

    xn = _rmsnorm_fwd(h0, norm_mix, "norm_mix_fwd")
    proj, (g_ao, g_co, g_o) = _matmul(
        xn, w_main_t, name="in_proj", trans_b=True, mid_at=0.6, out_dtypes=(BF16,),
        job=_Gather([w_attn_out[0].astype(BF16), w_conv_out[0].astype(BF16), w_o[0].astype(BF16)]))
    w_ao, w_co, w_o_f = _slots_to_columns(g_ao), _slots_to_columns(g_co), g_o.reshape(d, d)
    fg = _matmul(xn, w_fg_t, name="in_proj_fgate", trans_b=True)
    qn, kn, vb = _qk_prep(proj, q_norm, k_norm, aw, "qk_norm_fwd")
    cum = _forget_fwd(fg, b_f, "forget_cumsum")
    cum_heads = cum[:, :heads].T
    cum_row = cum_heads[:, None, :]
    t_attn = _tile(t, _attn_tile())
    (o, o_fine, lse), (g_up, g_down) = _attn_fwd(
        qn, kn, vb, cum_heads.reshape(heads, t // t_attn, 1, t_attn), "attention_fwd", mid_at=0.88,
        job=_Gather([w_up[0].astype(BF16), w_down[0].astype(BF16)]))
    w_up_f, w_down_f = _slots_to_columns(g_up), g_down.reshape(dff, d)
    a = _matmul(o, w_ao, name="attn_out_proj", out_dtypes=(BF16,))
    cpre = _conv_fwd(proj, cw_full, off_cb, "short_conv_fwd")
    c = _matmul(cpre, w_co, name="conv_out_proj", out_dtypes=(BF16,))
    merged = _gate_fwd(a, c, proj, b_gate, off_gl, "gate_merge_fwd")
    h1 = _matmul(merged, w_o_f, name="out_proj", extras=(h0,), epilogue=lambda acc, i, j, r: (r + acc,))
    hn = _rmsnorm_fwd(h1, norm_mlp, "norm_mlp_fwd")
    z, u = _matmul(hn, w_up_f, name="mlp_up", out_dtypes=(F32, BF16),
                   epilogue=lambda acc, i, j: (acc, jnp.square(jnp.maximum(acc, 0.0))))

    tm_down = _tile(t, (1408, 1024, 512, 256, 128))

    def loss_grad(acc, i, j, h1_tile, tgt_tile):
        rows = i * tm_down + lax.broadcasted_iota(jnp.int32, acc.shape, 0)
        valid = (rows >= N_META) & (rows < n_valid)
        dy = jnp.where(valid, ((h1_tile + acc) - tgt_tile) / d, 0.0)
        return dy, dy

    dh2, dh2b = _matmul(u, w_down_f, name="mlp_down_loss", extras=(h1, target), epilogue=loss_grad,
                        out_dtypes=(F32, BF16), tm=tm_down)
    loss_part = _sum_squares(dh2, "loss_sum") * (0.5 * d)

    wide = lambda n_cols: _tile(n_cols, (1024, 512, 256, 128))
    dw_down = _matmul(u, dh2b, name="mlp_down_wgrad", trans_a=True, tn=wide(d), out_dtypes=(BF16,))
    s_down = dw_down.reshape(N_DEV, dff // N_DEV, d)
    half_down = dff // N_DEV // 2
    dz, l_down0 = _matmul(dh2b, w_down_f, name="mlp_down_bwd", trans_b=True, extras=(z,), out_dtypes=(BF16,),
                          epilogue=lambda acc, i, j, zt: (acc * (2.0 * jnp.maximum(zt, 0.0)),),
                          job=_Scatter([(s_down, 0, half_down)]))
    s_up, l_down1 = _matmul(hn, dz, name="mlp_up_wgrad", trans_a=True, slots=True, out_dtypes=(BF16,),
                            tn=wide(dff // N_DEV), job=_Scatter([(s_down, half_down, half_down)]))
    dhn, l_up0 = _matmul(dz, w_up_f, name="mlp_up_bwd", trans_b=True, tn=wide(d),
                         job=_Scatter([(s_up, 0, d // 2)]))
    dh1, dh1b, dg_mlp = _rmsnorm_bwd(h1, dhn, norm_mlp, dh2, "norm_mlp_bwd")
    dmerged = _matmul(dh1b, w_o_f, name="out_proj_bwd", trans_b=True)
    dw_o = _matmul(merged, dh1b, name="out_proj_wgrad", trans_a=True, tn=wide(d), out_dtypes=(BF16,))
    da, dc, dgl0, dgl1, dbg0, dbg1 = _gate_bwd(dmerged, a, c, proj, b_gate, off_gl, "gate_merge_bwd")
    do = _matmul(da, w_ao, name="attn_out_bwd", trans_b=True)
    s_ao = _matmul(o, da, name="attn_out_wgrad", trans_a=True, slots=True, out_dtypes=(BF16,))
    dcp = _matmul(dc, w_co, name="conv_out_bwd", trans_b=True)
    s_co = _matmul(cpre, dc, name="conv_out_wgrad", trans_a=True, slots=True, out_dtypes=(BF16,))
    dcb, dcc, dcx, dcw = _conv_bwd(dcp, proj, cw_full, off_cb, "short_conv_bwd")
    delta = _attn_stats(do, o_fine, "attention_stats")
    (dqn, dkn, dv, dck), (l_up1, l_o, l_ao, l_co) = _attn_bwd(
        qn, kn, vb, do, lse, delta, cum_row, "attention_bwd",
        job=_Scatter([(s_up, d // 2, d // 2), dw_o.reshape(N_DEV, d // N_DEV, d), s_ao, s_co]))
    dq_raw, dk_raw, dg_q, dg_k = _qk_bwd(dqn, dkn, proj, q_norm, k_norm, aw, "qk_norm_bwd")
    dcum = _pad_lanes(dck.reshape(heads, t).T)
    dfg, db_f = _forget_bwd(dcum, fg, b_f, "forget_bwd")
    dproj = jnp.concatenate([dq_raw, dk_raw, dv, dcb, dcc, dcx, dgl0, dgl1], axis=1)
    dwt_fg = _matmul(dfg, xn, name="in_proj_fgate_wgrad", trans_a=True, out_dtypes=(BF16,))
    quarter = d // 4

    def in_slots(dwt, first):
        width = dwt.shape[1]
        parts = ((0, 3 * aw, dwt, 0), (3 * aw, 3 * aw + heads, dwt_fg[:heads, first:first + width], 3 * aw),
                 (3 * aw + heads, n_in, dwt, heads))
        slots = []
        for j in range(N_DEV):
            lo, hi = j * n_in // N_DEV, (j + 1) * n_in // N_DEV
            rows = [src[max(lo, a) - shift:min(hi, b) - shift] for a, b, src, shift in parts
                    if max(lo, a) < min(hi, b)]
            slots.append(rows[0] if len(rows) == 1 else jnp.concatenate(rows, axis=0))
        return jnp.stack(slots)

    def in_wgrad(idx, first_block, n_blocks, job):
        return _matmul(dproj, xn, name="in_proj_wgrad_%d" % idx, trans_a=True, tn=quarter,
                       cols=(first_block, n_blocks), out_dtypes=(BF16,), job=job)

    dwt0 = in_wgrad(0, 0, 1, None)
    dwt1, l_in0 = in_wgrad(1, 1, 1, _Scatter([in_slots(dwt0, 0)]))
    dwt2, l_in1 = in_wgrad(2, 2, 2, _Scatter([in_slots(dwt1, quarter)]))
    dxn_fg = _matmul(dfg, w_fg_t, name="in_proj_fgate_bwd")
    dxn, l_in2 = _matmul(dproj, w_main_t, name="in_proj_bwd", extras=(dxn_fg,),
                         epilogue=lambda acc, i, j, r: (r + acc,), job=_Scatter([in_slots(dwt2, 2 * quarter)]))
    dh0, _, dg_mix = _rmsnorm_bwd(h0, dxn, norm_mix, dh1, "norm_mix_bwd")

    small = [dg_mix, dbg0, dbg1, dg_mlp, dg_q, dg_k, db_f, loss_part, dcw, dh0[:N_META]]
    small_rows = [_rows_of(s) for s in small]
    pack = jnp.concatenate(small_rows, axis=0)
    pack = jnp.pad(pack, ((0, -pack.shape[0] % SUBLANES), (0, 0)))
    (pack_all,) = _run_job(_Scatter([], [pack]), "gather_small")

    landed = {"w_attn_out": [l_ao], "w_conv_out": [l_co], "w_o": [l_o],
              "w_up": l_up0 + [l_up1], "w_down": l_down0 + l_down1}
    shards = {"w_attn_out": (w_attn_out, m_w_attn_out, v_w_attn_out),
              "w_conv_out": (w_conv_out, m_w_conv_out, v_w_conv_out), "w_o": (w_o, m_w_o, v_w_o),
              "w_up": (w_up, m_w_up, v_w_up), "w_down": (w_down, m_w_down, v_w_down)}
    out = {}
    for nm, chunks in landed.items():
        w_, m_, v_ = shards[nm]
        res = _adamw(list(chunks), w_[0], m_[0], v_[0], "adamw_" + nm)
        out[nm] = [r[None] for r in res]
    res = _adamw_cols(l_in0 + l_in1 + l_in2, w_in_t, m_in_t, v_in_t, "adamw_w_in")
    out["w_in"] = [r.T[None] for r in res]

    total = _sum_parts(pack_all, "sum_small")
    pieces, at = [], 0
    for s, rows in zip(small, small_rows):
        n_el = 1
        for dim in s.shape:
            n_el *= dim
        pieces.append(total[at:at + rows.shape[0]].reshape(-1)[:n_el].reshape(s.shape))
        at += rows.shape[0]
    g_mix, g_bg0, g_bg1, g_mlp, g_q, g_k, g_bf, loss_row, g_cw_full, g_meta_full = pieces
    loss = loss_row[0, 0]
    cshard = conv_w.shape[2]
    g_small = {
        "norm_mix": g_mix, "b_gate": jnp.concatenate([g_bg0, g_bg1], axis=1), "norm_mlp": g_mlp,
        "q_norm": g_q, "k_norm": g_k, "b_fgate": g_bf[:, :heads],
        "conv_w": lax.dynamic_slice_in_dim(g_cw_full[:conv_w.shape[1]], me * cshard, cshard, axis=1)[None],
        "meta_tokens": lax.dynamic_slice_in_dim(g_meta_full, me * (d // N_DEV), d // N_DEV, axis=1),
    }
    small_w = {"norm_mix": (norm_mix, m_norm_mix, v_norm_mix), "b_gate": (b_gate, m_b_gate, v_b_gate),
               "norm_mlp": (norm_mlp, m_norm_mlp, v_norm_mlp), "q_norm": (q_norm, m_q_norm, v_q_norm),
               "k_norm": (k_norm, m_k_norm, v_k_norm), "b_fgate": (b_fgate, m_b_fgate, v_b_fgate),
               "conv_w": (conv_w, m_conv_w, v_conv_w), "meta_tokens": (meta_tokens, m_meta_tokens, v_meta_tokens)}
    order = list(small_w)
    packed = []
    for idx in range(4):
        cols = [g_small[nm] if idx == 0 else small_w[nm][idx - 1] for nm in order]
        rows = jnp.concatenate([_rows_of(c_) for c_ in cols], axis=0)
        packed.append(jnp.pad(rows, ((0, -rows.shape[0] % SUBLANES), (0, 0))))
    res = _adamw([packed[0][None]], packed[1], packed[2], packed[3], "adamw_small")
    at = 0
    for nm in order:
        shape = small_w[nm][0].shape
        n_el = 1
        for dim in shape:
            n_el *= dim
        n_rows = -(-n_el // LANES)
        out[nm] = [r[at:at + n_rows].reshape(-1)[:n_el].reshape(shape) for r in res]
        at += n_rows

    weights = ["meta_tokens", "norm_mix", "w_in", "b_fgate", "b_gate", "q_norm", "k_norm", "conv_w",
               "w_attn_out", "w_conv_out", "w_o", "norm_mlp", "w_up", "w_down"]
    grad_x = dh0[N_META:n_valid][None]
    return (loss, grad_x, *[out[nm][0] for nm in weights], *[out[nm][1] for nm in weights],
            *[out[nm][2] for nm in weights], *[out[nm][3] for nm in weights])
```

```python
import functools

import jax
import jax.numpy as jnp
from jax import lax
from jax.experimental import pallas as pl
from jax.experimental.pallas import tpu as pltpu

F32 = jnp.float32
BF16 = jnp.bfloat16

N_DEV = 8
N_META = 16
HEAD_DIM = 128
LANES = 128
SUBLANES = 8
EPS = 1e-6
VMEM_LIMIT = 56 * 1024 * 1024

ADAM_LR = 0.001
ADAM_B1 = 0.9
ADAM_B2 = 0.999
ADAM_EPS = 1e-08
ADAM_WD = 0.01
ADAM_STEP = 10

MESH = pl.DeviceIdType.MESH
HBM_SPEC = pl.BlockSpec(memory_space=pltpu.HBM)
RELATIONS = tuple((r >> 2 & 1, r >> 1 & 1, r & 1) for r in range(1, N_DEV))


def _params(semantics=None):
    return pltpu.CompilerParams(dimension_semantics=semantics, vmem_limit_bytes=VMEM_LIMIT)


def _tile(n, prefs):
    for p in prefs:
        if n % p == 0:
            return p
    return n


def _sds(shape, dtype):
    return jax.ShapeDtypeStruct(shape, dtype)


def _my_place():
    return lax.axis_index("x"), lax.axis_index("y"), lax.axis_index("c")


def _flat(px, py, pc):
    return 4 * px + 2 * py + pc


class _Gather:
    def __init__(self, arrays):
        self.operands = list(arrays)
        self.n = len(arrays)
        self.out_shape = [_sds((N_DEV,) + a.shape, a.dtype) for a in arrays]

    def _copy(self, srcs, outs, sems, a, k, block, to, from_src=False):
        slot = outs[a].at[_flat(*block)]
        return pltpu.make_async_remote_copy(
            src_ref=srcs[a] if from_src else slot, dst_ref=slot,
            send_sem=sems[0].at[a, k], recv_sem=sems[1].at[a, k],
            device_id=to, device_id_type=MESH)

    def _places(self):
        x, y, c = _my_place()
        return (x, y, c), (x, y, 1 - c), [(1 - x, y), (x, 1 - y), (1 - x, 1 - y)], c

    def start(self, srcs, outs, sems):
        me, sibling, chips, c = self._places()
        for a in range(self.n):
            pltpu.make_async_copy(srcs[a], outs[a].at[_flat(*me)], sems[2].at[a]).start()
            for j, chip in enumerate(chips):
                self._copy(srcs, outs, sems, a, 1 + j, me, (*chip, c), from_src=True).start()
            self._copy(srcs, outs, sems, a, 0, me, sibling, from_src=True).start()

    def mid(self, srcs, outs, sems):
        me, sibling, chips, c = self._places()
        for a in range(self.n):
            for j, chip in enumerate(chips):
                self._copy(srcs, outs, sems, a, 1 + j, (*chip, c), me).wait_recv()
                self._copy(srcs, outs, sems, a, 4 + j, (*chip, c), sibling).start()

    def finish(self, srcs, outs, sems):
        me, sibling, chips, c = self._places()
        for a in range(self.n):
            self._copy(srcs, outs, sems, a, 0, sibling, me).wait_recv()
            for j, chip in enumerate(chips):
                self._copy(srcs, outs, sems, a, 4 + j, (*chip, 1 - c), me).wait_recv()
            for k in range(7):
                self._copy(srcs, outs, sems, a, k, me, sibling).wait_send()
            pltpu.make_async_copy(srcs[a], outs[a].at[_flat(*me)], sems[2].at[a]).wait()


class _Scatter:
    def __init__(self, scatter, gather=()):
        scatter = [s if isinstance(s, tuple) else (s, 0, s.shape[1]) for s in scatter]
        self.ranges = [(lo, cnt) for _, lo, cnt in scatter]
        self.operands = [s[0] for s in scatter] + list(gather)
        self.ns, self.n = len(scatter), len(scatter) + len(gather)
        self.out_shape = ([_sds((N_DEV, cnt, arr.shape[2]), arr.dtype) for arr, _, cnt in scatter]
                          + [_sds((N_DEV,) + a.shape, a.dtype) for a in gather])

    def _peer(self, rel):
        return tuple(1 - p if r else p for p, r in zip(_my_place(), rel))

    def _src(self, srcs, a, place):
        if a >= self.ns:
            return srcs[a]
        lo, cnt = self.ranges[a]
        return srcs[a].at[_flat(*place), pl.ds(lo, cnt)]

    def _send(self, srcs, outs, sems, a, k, rel):
        peer = self._peer(rel)
        return pltpu.make_async_remote_copy(
            src_ref=self._src(srcs, a, peer), dst_ref=outs[a].at[_flat(*_my_place())],
            send_sem=sems[0].at[a, k], recv_sem=sems[1].at[a, k],
            device_id=peer, device_id_type=MESH)

    def _landed(self, outs, sems, a, k, rel):
        peer = self._peer(rel)
        slot = outs[a].at[_flat(*peer)]
        return pltpu.make_async_remote_copy(
            src_ref=slot, dst_ref=slot, send_sem=sems[0].at[a, k], recv_sem=sems[1].at[a, k],
            device_id=peer, device_id_type=MESH)

    def _own(self, srcs, outs, sems, a):
        me = _my_place()
        return pltpu.make_async_copy(self._src(srcs, a, me), outs[a].at[_flat(*me)], sems[2].at[a])

    def start(self, srcs, outs, sems):
        for a in range(self.n):
            self._own(srcs, outs, sems, a).start()
            for k, rel in enumerate(RELATIONS):
                self._send(srcs, outs, sems, a, k, rel).start()

    def mid(self, srcs, outs, sems):
        pass

    def finish(self, srcs, outs, sems):
        for a in range(self.n):
            for k, rel in enumerate(RELATIONS):
                self._landed(outs, sems, a, k, rel).wait_recv()
            for k, rel in enumerate(RELATIONS):
                self._send(srcs, outs, sems, a, k, rel).wait_send()
            self._own(srcs, outs, sems, a).wait()


def _job_sems(job):
    return [pltpu.SemaphoreType.DMA((job.n, 7)), pltpu.SemaphoreType.DMA((job.n, 7)),
            pltpu.SemaphoreType.DMA((job.n,))]


def _run_job(job, name):
    n = job.n

    def body(*refs):
        srcs, outs, sems = refs[:n], refs[n:2 * n], refs[2 * n:]
        job.start(srcs, outs, sems)
        job.mid(srcs, outs, sems)
        job.finish(srcs, outs, sems)

    return pl.pallas_call(
        body, name=name, out_shape=job.out_shape,
        in_specs=[HBM_SPEC] * n, out_specs=[HBM_SPEC] * n, scratch_shapes=_job_sems(job),
    )(*job.operands)


def _call(body, *, name, grid, in_specs, out_specs, out_shape, scratch_shapes, semantics,
          operands, job=None, mid_at=0.5):
    if job is None:
        res = pl.pallas_call(
            body, name=name, grid=grid, in_specs=in_specs, out_specs=out_specs, out_shape=out_shape,
            scratch_shapes=scratch_shapes, compiler_params=_params(semantics))(*operands)
        return res, []
    n_in, n_out, n_scr = len(in_specs), len(out_specs), len(scratch_shapes)
    total = 1
    for g in grid:
        total *= g
    mid_step = min(int(total * mid_at), total - 1)

    def carried(*refs):
        c_in, j_in = refs[:n_in], refs[n_in:n_in + job.n]
        o0 = n_in + job.n
        c_out, j_out = refs[o0:o0 + n_out], refs[o0 + n_out:o0 + n_out + job.n]
        s0 = o0 + n_out + job.n
        c_scr, sems = refs[s0:s0 + n_scr], refs[s0 + n_scr:]
        step = pl.program_id(0)
        for ax in range(1, len(grid)):
            step = step * grid[ax] + pl.program_id(ax)

        @pl.when(step == 0)
        def _():
            job.start(j_in, j_out, sems)

        body(*c_in, *c_out, *c_scr)

        @pl.when(step == mid_step)
        def _():
            job.mid(j_in, j_out, sems)

        @pl.when(step == total - 1)
        def _():
            job.finish(j_in, j_out, sems)

    res = pl.pallas_call(
        carried, name=name, grid=grid,
        in_specs=list(in_specs) + [HBM_SPEC] * job.n,
        out_specs=list(out_specs) + [HBM_SPEC] * job.n,
        out_shape=list(out_shape) + job.out_shape,
        scratch_shapes=list(scratch_shapes) + _job_sems(job),
        compiler_params=_params(("arbitrary",) * len(grid)),
    )(*operands, *job.operands)
    return list(res[:n_out]), list(res[n_out:])


def _matmul(a, b, *, name, trans_b=False, extras=(), epilogue=None, out_dtypes=(F32,),
            tm=None, tn=None, tk=None, rows=None, cols=None, trans_a=False, slots=False, job=None,
            mid_at=0.5):
    k, m = a.shape if trans_a else a.shape[::-1]
    n = b.shape[0] if trans_b else b.shape[1]
    tm = tm or _tile(m, (1408, 1024, 512, 256, 128))
    tn = tn or _tile(n // N_DEV if slots else n, (512, 256, 128))
    tk = tk or _tile(k, (2048, 1408, 1024, 512, 256, 128))
    nk = k // tk
    row0, n_rows = rows or (0, m // tm)
    m = n_rows * tm
    col0, n_cols = cols or (0, n // tn)
    n = n_cols * tn
    n_ex, n_out = len(extras), len(out_dtypes)
    dims = (((0,) if trans_a else (1,), (1,) if trans_b else (0,)), ((), ()))

    def body(*refs):
        a_ref, b_ref = refs[:2]
        ex_refs = refs[2:2 + n_ex]
        out_refs = refs[2 + n_ex:2 + n_ex + n_out]
        part = lax.dot_general(a_ref[...].astype(BF16), b_ref[...].astype(BF16), dims,
                               preferred_element_type=F32)

        def finish(acc):
            if epilogue is None:
                res = (acc,)
            else:
                res = epilogue(acc, pl.program_id(0), pl.program_id(1), *[e[...] for e in ex_refs])
            for o_ref, r in zip(out_refs, res):
                o_ref[...] = r.astype(o_ref.dtype)

        if nk == 1:
            finish(part)
        else:
            acc_ref = refs[-1]
            kk = pl.program_id(2)

            @pl.when(kk == 0)
            def _():
                acc_ref[...] = part

            @pl.when(kk > 0)
            def _():
                acc_ref[...] += part

            @pl.when(kk == nk - 1)
            def _():
                finish(acc_ref[...])

    in_specs = [pl.BlockSpec((tk, tm), lambda i, j, kk: (kk, row0 + i)) if trans_a
                else pl.BlockSpec((tm, tk), lambda i, j, kk: (row0 + i, kk)),
                pl.BlockSpec((tn, tk), lambda i, j, kk: (col0 + j, kk)) if trans_b
                else pl.BlockSpec((tk, tn), lambda i, j, kk: (kk, col0 + j))]
    for e in extras:
        if e.shape[0] == 1:
            in_specs.append(pl.BlockSpec((1, tn), lambda i, j, kk: (0, j)))
        else:
            in_specs.append(pl.BlockSpec((tm, tn), lambda i, j, kk: (i, j)))
    if slots:
        per_slot = n // N_DEV // tn
        out_spec = pl.BlockSpec((None, tm, tn), lambda i, j, kk: (j // per_slot, i, j % per_slot))
        out_shape = [_sds((N_DEV, m, n // N_DEV), d) for d in out_dtypes]
    else:
        out_spec = pl.BlockSpec((tm, tn), lambda i, j, kk: (i, j))
        out_shape = [_sds((m, n), d) for d in out_dtypes]
    res, moved = _call(
        body, name=name, grid=(n_rows, n // tn, nk),
        in_specs=in_specs,
        out_specs=[out_spec] * n_out,
        out_shape=out_shape,
        scratch_shapes=[pltpu.VMEM((tm, tn), F32)] if nk > 1 else [],
        semantics=("parallel", "parallel", "arbitrary"),
        operands=(a, b, *extras), job=job, mid_at=mid_at)
    res = res[0] if n_out == 1 else tuple(res)
    return res if job is None else (res, moved)


def _rstd(x):
    return lax.rsqrt(jnp.mean(x * x, axis=-1, keepdims=True) + EPS)


def _norm_bwd(x, dy, g):
    r = _rstd(x)
    u = dy * g
    dx = r * u - x * (r * r * r) * jnp.mean(u * x, axis=-1, keepdims=True)
    return dx, dy * (x * r)


def _rmsnorm_fwd(h, g, name):
    t, d = h.shape
    tr = _tile(t, (384, 256, 128))

    def body(h_ref, g_ref, o_ref):
        x = h_ref[...]
        o_ref[...] = ((x * _rstd(x)) * g_ref[...]).astype(o_ref.dtype)

    row = pl.BlockSpec((tr, d), lambda i: (i, 0))
    return pl.pallas_call(
        body, name=name, grid=(t // tr,),
        in_specs=[row, pl.BlockSpec((1, d), lambda i: (0, 0))], out_specs=row,
        out_shape=_sds((t, d), BF16), compiler_params=_params(("parallel",)),
    )(h, g)


def _rmsnorm_bwd(h, dy, g, res, name):
    t, d = h.shape
    tr = _tile(t, (384, 256, 128))

    def body(h_ref, dy_ref, g_ref, res_ref, dh_ref, dhb_ref, dg_ref):
        dx, dg_rows = _norm_bwd(h_ref[...], dy_ref[...], g_ref[...])
        dh = res_ref[...] + dx
        dh_ref[...] = dh
        dhb_ref[...] = dh.astype(BF16)

        @pl.when(pl.program_id(0) == 0)
        def _():
            dg_ref[...] = jnp.zeros_like(dg_ref)

        dg_ref[...] += jnp.sum(dg_rows, axis=0, keepdims=True)

    row = pl.BlockSpec((tr, d), lambda i: (i, 0))
    vec = pl.BlockSpec((1, d), lambda i: (0, 0))
    return pl.pallas_call(
        body, name=name, grid=(t // tr,),
        in_specs=[row, row, vec, row], out_specs=[row, row, vec],
        out_shape=[_sds((t, d), F32), _sds((t, d), BF16), _sds((1, d), F32)],
        compiler_params=_params(("arbitrary",)),
    )(h, dy, g, res)


def _qk_prep(proj, gq, gk, aw, name):
    t = proj.shape[0]
    heads = aw // HEAD_DIM
    tr = _tile(t, (384, 256, 128))

    def body(q_ref, k_ref, v_ref, gq_ref, gk_ref, qo_ref, ko_ref, vo_ref):
        for h in range(heads):
            sl = slice(h * HEAD_DIM, (h + 1) * HEAD_DIM)
            xq, xk = q_ref[:, sl].astype(F32), k_ref[:, sl].astype(F32)
            qo_ref[:, sl] = ((xq * _rstd(xq)) * gq_ref[...]).astype(BF16)
            ko_ref[:, sl] = ((xk * _rstd(xk)) * gk_ref[...]).astype(BF16)
        vo_ref[...] = v_ref[...].astype(BF16)

    vec = pl.BlockSpec((1, HEAD_DIM), lambda i: (0, 0))
    out = pl.BlockSpec((tr, aw), lambda i: (i, 0))
    return pl.pallas_call(
        body, name=name, grid=(t // tr,),
        in_specs=[pl.BlockSpec((tr, aw), lambda i: (i, 0)), pl.BlockSpec((tr, aw), lambda i: (i, 1)),
                  pl.BlockSpec((tr, aw), lambda i: (i, 2)), vec, vec],
        out_specs=[out, out, out], out_shape=[_sds((t, aw), BF16)] * 3,
        compiler_params=_params(("parallel",)),
    )(proj, proj, proj, gq, gk)


def _qk_bwd(dqn, dkn, proj, gq, gk, aw, name):
    t = proj.shape[0]
    heads = aw // HEAD_DIM
    tr = _tile(t, (384, 256, 128))

    def body(dq_ref, dk_ref, q_ref, k_ref, gq_ref, gk_ref, dqo_ref, dko_ref, dgq_ref, dgk_ref):
        @pl.when(pl.program_id(0) == 0)
        def _():
            dgq_ref[...] = jnp.zeros_like(dgq_ref)
            dgk_ref[...] = jnp.zeros_like(dgk_ref)

        for h in range(heads):
            sl = slice(h * HEAD_DIM, (h + 1) * HEAD_DIM)
            dx, dg_rows = _norm_bwd(q_ref[:, sl].astype(F32), dq_ref[:, sl], gq_ref[...])
            dqo_ref[:, sl] = dx.astype(BF16)
            dgq_ref[...] += jnp.sum(dg_rows, axis=0, keepdims=True)
            dx, dg_rows = _norm_bwd(k_ref[:, sl].astype(F32), dk_ref[:, sl], gk_ref[...])
            dko_ref[:, sl] = dx.astype(BF16)
            dgk_ref[...] += jnp.sum(dg_rows, axis=0, keepdims=True)

    vec = pl.BlockSpec((1, HEAD_DIM), lambda i: (0, 0))
    row = pl.BlockSpec((tr, aw), lambda i: (i, 0))
    return pl.pallas_call(
        body, name=name, grid=(t // tr,),
        in_specs=[row, row, row, pl.BlockSpec((tr, aw), lambda i: (i, 1)), vec, vec],
        out_specs=[row, row, vec, vec],
        out_shape=[_sds((t, aw), BF16), _sds((t, aw), BF16), _sds((1, HEAD_DIM), F32), _sds((1, HEAD_DIM), F32)],
        compiler_params=_params(("arbitrary",)),
    )(dqn, dkn, proj, proj, gq, gk)


def _triangle(lower):
    r = lax.broadcasted_iota(jnp.int32, (LANES, LANES), 0)
    c = lax.broadcasted_iota(jnp.int32, (LANES, LANES), 1)
    return ((c <= r) if lower else (c >= r)).astype(F32)


def _forget_fwd(fg, b, name):
    t = fg.shape[0]

    def body(fg_ref, b_ref, cum_ref, carry):
        @pl.when(pl.program_id(0) == 0)
        def _():
            carry[...] = jnp.zeros_like(carry)

        z = fg_ref[...] + b_ref[...]
        log_f = jnp.minimum(z, 0.0) - jnp.log1p(jnp.exp(-jnp.abs(z)))
        cs = jnp.dot(_triangle(True), log_f, precision=lax.Precision.HIGHEST,
                     preferred_element_type=F32) + carry[0:1, :]
        cum_ref[...] = cs
        carry[...] = jnp.broadcast_to(cs[LANES - 1:LANES, :], carry.shape)

    row = pl.BlockSpec((LANES, LANES), lambda i: (i, 0))
    return pl.pallas_call(
        body, name=name, grid=(t // LANES,),
        in_specs=[row, pl.BlockSpec((1, LANES), lambda i: (0, 0))], out_specs=row,
        out_shape=_sds((t, LANES), F32), scratch_shapes=[pltpu.VMEM((SUBLANES, LANES), F32)],
        compiler_params=_params(("arbitrary",)),
    )(fg, b)


def _forget_bwd(dcum, fg, b, name):
    t = fg.shape[0]
    nt = t // LANES

    def body(dc_ref, fg_ref, b_ref, dfg_ref, db_ref, carry):
        @pl.when(pl.program_id(0) == 0)
        def _():
            carry[...] = jnp.zeros_like(carry)
            db_ref[...] = jnp.zeros_like(db_ref)

        d_log_f = jnp.dot(_triangle(False), dc_ref[...], precision=lax.Precision.HIGHEST,
                          preferred_element_type=F32) + carry[0:1, :]
        carry[...] = jnp.broadcast_to(d_log_f[0:1, :], carry.shape)
        dz = d_log_f * jax.nn.sigmoid(-(fg_ref[...] + b_ref[...]))
        dfg_ref[...] = dz.astype(BF16)
        db_ref[...] += jnp.sum(dz, axis=0, keepdims=True)

    row = pl.BlockSpec((LANES, LANES), lambda i: (nt - 1 - i, 0))
    vec = pl.BlockSpec((1, LANES), lambda i: (0, 0))
    return pl.pallas_call(
        body, name=name, grid=(nt,),
        in_specs=[row, row, vec], out_specs=[row, vec],
        out_shape=[_sds((t, LANES), BF16), _sds((1, LANES), F32)],
        scratch_shapes=[pltpu.VMEM((SUBLANES, LANES), F32)],
        compiler_params=_params(("arbitrary",)),
    )(dcum, fg, b)


def _causal(qi, kj, tq):
    rows = qi * tq + lax.broadcasted_iota(jnp.int32, (tq, tq), 0)
    cols = kj * tq + lax.broadcasted_iota(jnp.int32, (tq, tq), 1)
    return cols <= rows


NT_DIMS = (((1,), (1,)), ((), ()))
TN_DIMS = (((0,), (0,)), ((), ()))


def _attn_tile():
    return (384, 256, 128)


def _attn_fwd(q, k, v, cum_row, name, job=None, mid_at=0.5):
    t, aw = q.shape
    heads = aw // HEAD_DIM
    tq = _tile(t, _attn_tile())
    nq = t // tq
    scale = HEAD_DIM ** -0.5

    def body(q_ref, k_ref, v_ref, ck_ref, o_ref, of_ref, lse_ref):
        qi = pl.program_id(1)
        qv = q_ref[...]

        def tile(kj, carry, masked):
            m_prev, l_prev, acc, res = carry
            ks = pl.ds(pl.multiple_of(kj * tq, tq), tq)
            s = lax.dot_general(qv, k_ref[ks, :], NT_DIMS, preferred_element_type=F32) * scale - ck_ref[kj]
            if masked:
                s = jnp.where(_causal(0, 0, tq), s, -jnp.inf)
            m_new = jnp.maximum(m_prev, jnp.max(s, axis=-1, keepdims=True))
            alpha = jnp.exp(m_prev - m_new)
            p = jnp.exp(s - m_new)
            p_hi = p.astype(BF16)
            p_lo = (p - p_hi.astype(F32)).astype(BF16)
            vv = v_ref[ks, :]
            return (m_new, alpha * l_prev + jnp.sum(p, axis=-1, keepdims=True),
                    alpha * acc + jnp.dot(p_hi, vv, preferred_element_type=F32),
                    alpha * res + jnp.dot(p_lo, vv, preferred_element_type=F32))

        init = (jnp.full((tq, 1), -jnp.inf, F32), jnp.zeros((tq, 1), F32),
                jnp.zeros((tq, HEAD_DIM), F32), jnp.zeros((tq, HEAD_DIM), F32))
        carry = lax.fori_loop(0, qi, lambda kj, c: tile(kj, c, False), init)
        m_fin, l_fin, acc, res = tile(qi, carry, True)
        o_ref[...] = (acc / l_fin).astype(o_ref.dtype)
        of_ref[...] = (acc + res) / l_fin
        lse_ref[...] = m_fin + jnp.log(l_fin)

    q_spec = pl.BlockSpec((tq, HEAD_DIM), lambda h, i: (i, h))
    head = pl.BlockSpec((t, HEAD_DIM), lambda h, i: (0, h))
    return _call(
        body, name=name, grid=(heads, nq),
        in_specs=[q_spec, head, head, pl.BlockSpec((None, nq, 1, tq), lambda h, i: (h, 0, 0, 0))],
        out_specs=[q_spec, q_spec, pl.BlockSpec((None, tq, 1), lambda h, i: (h, i, 0))],
        out_shape=[_sds((t, aw), BF16), _sds((t, aw), F32), _sds((heads, t, 1), F32)],
        scratch_shapes=[], semantics=("parallel", "arbitrary"),
        operands=(q, k, v, cum_row), job=job, mid_at=mid_at)


def _attn_stats(do, o, name):
    t, aw = o.shape
    heads = aw // HEAD_DIM
    tr = _tile(t, (384, 256, 128))

    def body(do_ref, o_ref, delta_ref):
        for h in range(heads):
            sl = slice(h * HEAD_DIM, (h + 1) * HEAD_DIM)
            do_seen = do_ref[:, sl].astype(BF16).astype(F32)
            delta_ref[h] = jnp.sum(do_seen * o_ref[:, sl], axis=-1, keepdims=True)

    row = pl.BlockSpec((tr, aw), lambda i: (i, 0))
    return pl.pallas_call(
        body, name=name, grid=(t // tr,),
        in_specs=[row, row], out_specs=pl.BlockSpec((heads, tr, 1), lambda i: (0, i, 0)),
        out_shape=_sds((heads, t, 1), F32), compiler_params=_params(("parallel",)),
    )(do, o)


def _attn_bwd(q, k, v, do, lse, delta, cum_row, name, job=None):
    t, aw = q.shape
    heads = aw // HEAD_DIM
    tq = _tile(t, _attn_tile())
    nq = t // tq
    scale = HEAD_DIM ** -0.5

    def body(q_ref, k_ref, v_ref, do_ref, lse_ref, delta_ref, ck_ref, dq_ref, dk_ref, dv_ref, dck_ref):
        kj = pl.program_id(1)

        @pl.when(kj == 0)
        def _():
            dq_ref[...] = jnp.zeros_like(dq_ref)

        kv, vv, ck = k_ref[...], v_ref[...], ck_ref[...]

        def tile(qi, carry, masked):
            dk_acc, dv_acc, dck_acc = carry
            rows = pl.ds(pl.multiple_of(qi * tq, tq), tq)
            qv, dov = q_ref[rows, :], do_ref[rows, :].astype(BF16)
            s = lax.dot_general(qv, kv, NT_DIMS, preferred_element_type=F32) * scale - ck - lse_ref[rows, :]
            p = jnp.exp(s)
            if masked:
                p = jnp.where(_causal(0, 0, tq), p, 0.0)
            dp = lax.dot_general(dov, vv, NT_DIMS, preferred_element_type=F32)
            ds = p * (dp - delta_ref[rows, :])
            dsb = ds.astype(BF16)
            dq_ref[rows, :] += jnp.dot(dsb, kv, preferred_element_type=F32) * scale
            return (dk_acc + lax.dot_general(dsb, qv, TN_DIMS, preferred_element_type=F32),
                    dv_acc + lax.dot_general(p.astype(BF16), dov, TN_DIMS, preferred_element_type=F32),
                    dck_acc + jnp.sum(ds, axis=0, keepdims=True))

        init = (jnp.zeros((tq, HEAD_DIM), F32), jnp.zeros((tq, HEAD_DIM), F32), jnp.zeros((1, tq), F32))
        carry = tile(kj, init, True)
        dk_acc, dv_acc, dck_acc = lax.fori_loop(kj + 1, nq, lambda qi, c: tile(qi, c, False), carry)
        dk_ref[...] = dk_acc * scale
        dv_ref[...] = dv_acc.astype(dv_ref.dtype)
        dck_ref[...] = -dck_acc

    head = pl.BlockSpec((t, HEAD_DIM), lambda h, j: (0, h))
    k_spec = pl.BlockSpec((tq, HEAD_DIM), lambda h, j: (j, h))
    col = pl.BlockSpec((None, t, 1), lambda h, j: (h, 0, 0))
    row = pl.BlockSpec((None, 1, tq), lambda h, j: (h, 0, j))
    return _call(
        body, name=name, grid=(heads, nq),
        in_specs=[head, k_spec, k_spec, head, col, col, row],
        out_specs=[head, k_spec, k_spec, row],
        out_shape=[_sds((t, aw), F32), _sds((t, aw), F32), _sds((t, aw), BF16), _sds((heads, 1, t), F32)],
        scratch_shapes=[], semantics=("parallel", "arbitrary"),
        operands=(q, k, v, do, lse, delta, cum_row), job=job)


def _shift_down(u, by):
    rows = lax.broadcasted_iota(jnp.int32, u.shape, 0)
    return jnp.where(rows >= by, pltpu.roll(u, by, 0), 0.0)


def _shift_up(u, by):
    t = u.shape[0]
    rows = lax.broadcasted_iota(jnp.int32, u.shape, 0)
    return jnp.where(rows < t - by, pltpu.roll(u, t - by, 0), 0.0)


def _conv_specs(t, off_b, cw_width):
    nb = cw_width // LANES
    base = off_b // LANES
    return [pl.BlockSpec((t, LANES), lambda j, s=s: (0, base + s * nb + j)) for s in range(3)]


def _conv_fwd(proj, cw, off_b, name):
    t = proj.shape[0]
    width = cw.shape[1]

    def body(cb_ref, cc_ref, cx_ref, w_ref, o_ref):
        u = cc_ref[...].astype(F32) * cx_ref[...]
        y = w_ref[0:1, :] * _shift_down(u, 2) + w_ref[1:2, :] * _shift_down(u, 1) + w_ref[2:3, :] * u
        o_ref[...] = (cb_ref[...] * y).astype(BF16)

    return pl.pallas_call(
        body, name=name, grid=(width // LANES,),
        in_specs=_conv_specs(t, off_b, width) + [pl.BlockSpec((SUBLANES, LANES), lambda j: (0, j))],
        out_specs=pl.BlockSpec((t, LANES), lambda j: (0, j)),
        out_shape=_sds((t, width), BF16), compiler_params=_params(("parallel",)),
    )(proj, proj, proj, cw)


def _conv_bwd(dcp, proj, cw, off_b, name):
    t = proj.shape[0]
    width = cw.shape[1]

    def body(d_ref, cb_ref, cc_ref, cx_ref, w_ref, dcb_ref, dcc_ref, dcx_ref, dw_ref):
        cc, cx = cc_ref[...].astype(F32), cx_ref[...].astype(F32)
        u = cc * cx
        u1, u2 = _shift_down(u, 1), _shift_down(u, 2)
        w0, w1, w2 = w_ref[0:1, :], w_ref[1:2, :], w_ref[2:3, :]
        d = d_ref[...]
        dcb_ref[...] = (d * (w0 * u2 + w1 * u1 + w2 * u)).astype(BF16)
        dy = d * cb_ref[...]
        du = w2 * dy + w1 * _shift_up(dy, 1) + w0 * _shift_up(dy, 2)
        dcc_ref[...] = (du * cx).astype(BF16)
        dcx_ref[...] = (du * cc).astype(BF16)
        dw = [jnp.sum(dy * s, axis=0, keepdims=True) for s in (u2, u1, u)]
        dw_ref[...] = jnp.concatenate(dw + [jnp.zeros((SUBLANES - 3, LANES), F32)], axis=0)

    col = pl.BlockSpec((t, LANES), lambda j: (0, j))
    wspec = pl.BlockSpec((SUBLANES, LANES), lambda j: (0, j))
    return pl.pallas_call(
        body, name=name, grid=(width // LANES,),
        in_specs=[col] + _conv_specs(t, off_b, width) + [wspec],
        out_specs=[col, col, col, wspec],
        out_shape=[_sds((t, width), BF16)] * 3 + [_sds((SUBLANES, width), F32)],
        compiler_params=_params(("parallel",)),
    )(dcp, proj, proj, proj, cw)


def _gate_specs(t, d, off_g, tr, tc, rows_first):
    nb = d // tc
    base = off_g // tc
    if rows_first:
        tile = lambda s: pl.BlockSpec((tr, tc), lambda i, j: (i, base + s * nb + j))
        vec = lambda s: pl.BlockSpec((1, tc), lambda i, j: (0, s * nb + j))
        plain = pl.BlockSpec((tr, tc), lambda i, j: (i, j))
    else:
        tile = lambda s: pl.BlockSpec((tr, tc), lambda j, i: (i, base + s * nb + j))
        vec = lambda s: pl.BlockSpec((1, tc), lambda j, i: (0, s * nb + j))
        plain = pl.BlockSpec((tr, tc), lambda j, i: (i, j))
    return tile, vec, plain


def _gate_fwd(a, c, proj, bg, off_g, name):
    t, d = a.shape
    tr, tc = _tile(t, (384, 256, 128)), _tile(d, (512, 256, 128))
    tile, vec, plain = _gate_specs(t, d, off_g, tr, tc, True)

    def body(a_ref, c_ref, g0_ref, g1_ref, b0_ref, b1_ref, o_ref):
        g0 = jax.nn.sigmoid(g0_ref[...] + b0_ref[...])
        g1 = jax.nn.sigmoid(g1_ref[...] + b1_ref[...])
        o_ref[...] = (g0 * a_ref[...] + g1 * c_ref[...]).astype(BF16)

    return pl.pallas_call(
        body, name=name, grid=(t // tr, d // tc),
        in_specs=[plain, plain, tile(0), tile(1), vec(0), vec(1)], out_specs=plain,
        out_shape=_sds((t, d), BF16), compiler_params=_params(("parallel", "parallel")),
    )(a, c, proj, proj, bg, bg)


def _gate_bwd(dm, a, c, proj, bg, off_g, name):
    t, d = a.shape
    tr, tc = _tile(t, (384, 256, 128)), _tile(d, (512, 256, 128))
    tile, vec, plain = _gate_specs(t, d, off_g, tr, tc, False)

    def body(dm_ref, a_ref, c_ref, g0_ref, g1_ref, b0_ref, b1_ref,
             da_ref, dc_ref, dg0_ref, dg1_ref, db0_ref, db1_ref):
        @pl.when(pl.program_id(1) == 0)
        def _():
            db0_ref[...] = jnp.zeros_like(db0_ref)
            db1_ref[...] = jnp.zeros_like(db1_ref)

        dm = dm_ref[...]
        g0 = jax.nn.sigmoid(g0_ref[...] + b0_ref[...])
        g1 = jax.nn.sigmoid(g1_ref[...] + b1_ref[...])
        da_ref[...] = (dm * g0).astype(BF16)
        dc_ref[...] = (dm * g1).astype(BF16)
        dz0 = dm * a_ref[...] * (g0 * (1.0 - g0))
        dz1 = dm * c_ref[...] * (g1 * (1.0 - g1))
        dg0_ref[...] = dz0.astype(BF16)
        dg1_ref[...] = dz1.astype(BF16)
        db0_ref[...] += jnp.sum(dz0, axis=0, keepdims=True)
        db1_ref[...] += jnp.sum(dz1, axis=0, keepdims=True)

    bvec = pl.BlockSpec((1, tc), lambda j, i: (0, j))
    return pl.pallas_call(
        body, name=name, grid=(d // tc, t // tr),
        in_specs=[plain, plain, plain, tile(0), tile(1), vec(0), vec(1)],
        out_specs=[plain] * 4 + [bvec, bvec],
        out_shape=[_sds((t, d), BF16)] * 4 + [_sds((1, d), F32)] * 2,
        compiler_params=_params(("parallel", "arbitrary")),
    )(dm, a, c, proj, proj, bg, bg)


def _sum_squares(x, name):
    t, d = x.shape
    tr = _tile(t, (384, 256, 128))

    def body(x_ref, o_ref):
        @pl.when(pl.program_id(0) == 0)
        def _():
            o_ref[...] = jnp.zeros_like(o_ref)

        v = x_ref[...]
        o_ref[...] += jnp.sum(jnp.sum(v * v, axis=0, keepdims=True), axis=1, keepdims=True)

    return pl.pallas_call(
        body, name=name, grid=(t // tr,),
        in_specs=[pl.BlockSpec((tr, d), lambda i: (i, 0))],
        out_specs=pl.BlockSpec((1, LANES), lambda i: (0, 0)),
        out_shape=_sds((1, LANES), F32), compiler_params=_params(("arbitrary",)),
    )(x)


def _row_tile(r, c):
    return r if r * c <= 128 * 1024 else _tile(r, (128, 64, 32, 16))


def _sum_parts(parts, name):
    n, r, c = parts.shape
    tr = _row_tile(r, c)

    def body(p_ref, o_ref):
        acc = p_ref[0].astype(F32)
        for i in range(1, n):
            acc = acc + p_ref[i].astype(F32)
        o_ref[...] = acc

    return pl.pallas_call(
        body, name=name, grid=(r // tr,),
        in_specs=[pl.BlockSpec((n, tr, c), lambda i: (0, i, 0))],
        out_specs=pl.BlockSpec((tr, c), lambda i: (i, 0)),
        out_shape=_sds((r, c), F32), compiler_params=_params(("parallel",)),
    )(parts)


def _adamw(chunks, w, m, v, name):
    n, rc, c = chunks[0].shape
    r = rc * len(chunks)
    tr = _row_tile(rc, c)
    per = rc // tr

    def body(*refs):
        p_refs = refs[:len(chunks)]
        w_ref, m_ref, v_ref, g_ref, d_ref, nm_ref, nv_ref = refs[len(chunks):]
        i = pl.program_id(0)

        def update(p_ref):
            g = p_ref[0].astype(F32)
            for s in range(1, n):
                g = g + p_ref[s].astype(F32)
            nm = ADAM_B1 * m_ref[...] + (1.0 - ADAM_B1) * g
            nv = ADAM_B2 * v_ref[...] + (1.0 - ADAM_B2) * (g * g)
            m_hat = nm / (1.0 - ADAM_B1 ** ADAM_STEP)
            v_hat = nv / (1.0 - ADAM_B2 ** ADAM_STEP)
            g_ref[...] = g
            d_ref[...] = -ADAM_LR * (m_hat / (jnp.sqrt(v_hat) + ADAM_EPS) + ADAM_WD * w_ref[...])
            nm_ref[...] = nm
            nv_ref[...] = nv

        if len(chunks) == 1:
            update(p_refs[0])
        else:
            for ci, p_ref in enumerate(p_refs):
                pl.when((i >= ci * per) & (i < (ci + 1) * per))(functools.partial(update, p_ref))

    row = pl.BlockSpec((tr, c), lambda i: (i, 0))
    part_specs = [pl.BlockSpec((n, tr, c), lambda i, ci=ci: (0, jnp.clip(i - ci * per, 0, per - 1), 0))
                  for ci in range(len(chunks))]
    return pl.pallas_call(
        body, name=name, grid=(r // tr,),
        in_specs=part_specs + [row, row, row],
        out_specs=[row] * 4, out_shape=[_sds((r, c), F32)] * 4,
        compiler_params=_params(("parallel",)),
    )(*chunks, w, m, v)


def _adamw_cols(chunks, w, m, v, name):
    n, r, _ = chunks[0].shape
    widths = [ch.shape[2] for ch in chunks]
    tc = min([LANES] + widths)
    firsts = [sum(widths[:ci]) // tc for ci in range(len(chunks) + 1)]

    def body(*refs):
        p_refs = refs[:len(chunks)]
        w_ref, m_ref, v_ref, g_ref, d_ref, nm_ref, nv_ref = refs[len(chunks):]
        j = pl.program_id(0)

        def update(p_ref):
            g = p_ref[0].astype(F32)
            for s in range(1, n):
                g = g + p_ref[s].astype(F32)
            nm = ADAM_B1 * m_ref[...] + (1.0 - ADAM_B1) * g
            nv = ADAM_B2 * v_ref[...] + (1.0 - ADAM_B2) * (g * g)
            m_hat = nm / (1.0 - ADAM_B1 ** ADAM_STEP)
            v_hat = nv / (1.0 - ADAM_B2 ** ADAM_STEP)
            g_ref[...] = g
            d_ref[...] = -ADAM_LR * (m_hat / (jnp.sqrt(v_hat) + ADAM_EPS) + ADAM_WD * w_ref[...])
            nm_ref[...] = nm
            nv_ref[...] = nv

        for ci, p_ref in enumerate(p_refs):
            pl.when((j >= firsts[ci]) & (j < firsts[ci + 1]))(functools.partial(update, p_ref))

    col = pl.BlockSpec((r, tc), lambda j: (0, j))
    part_specs = [pl.BlockSpec((n, r, tc),
                               lambda j, lo=firsts[ci], hi=firsts[ci + 1]: (0, 0, jnp.clip(j - lo, 0, hi - lo - 1)))
                  for ci in range(len(chunks))]
    return pl.pallas_call(
        body, name=name, grid=(firsts[-1],),
        in_specs=part_specs + [col, col, col],
        out_specs=[col] * 4, out_shape=[_sds((r, firsts[-1] * tc), F32)] * 4,
        compiler_params=_params(("parallel",)),
    )(*chunks, w, m, v)


def _pad_lanes(a, width=LANES):
    return jnp.pad(a, ((0, 0), (0, width - a.shape[1])))


def _rows_of(a):
    flat = a.reshape(-1)
    n = -(-flat.shape[0] // LANES) * LANES
    return jnp.pad(flat, (0, n - flat.shape[0])).reshape(-1, LANES)


def _columns_to_slots(full, n_rows):
    return full.reshape(n_rows, N_DEV, -1).transpose(1, 0, 2)


def _slots_to_columns(slots):
    return slots.transpose(1, 0, 2).reshape(slots.shape[1], -1)


def kernel(x, meta_tokens, norm_mix, w_in, b_fgate, b_gate, q_norm, k_norm, conv_w, w_attn_out, w_conv_out, w_o, norm_mlp, w_up, w_down, loss_target, m_meta_tokens, m_norm_mix, m_w_in, m_b_fgate, m_b_gate, m_q_norm, m_k_norm, m_conv_w, m_w_attn_out, m_w_conv_out, m_w_o, m_norm_mlp, m_w_up, m_w_down, v_meta_tokens, v_norm_mix, v_w_in, v_b_fgate, v_b_gate, v_q_norm, v_k_norm, v_conv_w, v_w_attn_out, v_w_conv_out, v_w_o, v_norm_mlp, v_w_up, v_w_down):
    seq, d = x.shape[1], x.shape[2]
    heads = b_fgate.shape[1]
    aw = heads * HEAD_DIM
    cwid = conv_w.shape[2] * N_DEV
    dff = w_up.shape[2] * N_DEV
    n_valid = N_META + seq
    t = -(-n_valid // LANES) * LANES
    me = _flat(*_my_place())
    off_cb, off_gl = 3 * aw, 3 * aw + 3 * cwid

    conv_shard = jnp.pad(conv_w[0], ((0, SUBLANES - conv_w.shape[1]), (0, 0)))
    w_in_t, m_in_t, v_in_t = (jnp.swapaxes(p, 1, 2)[0] for p in (w_in, m_w_in, v_w_in))
    g_in, g_meta, g_cw = _run_job(_Gather([w_in_t.astype(BF16), meta_tokens, conv_shard]), "gather_first")
    n_in = N_DEV * g_in.shape[1]
    w_all_t = g_in.reshape(n_in, d)
    w_main_t = jnp.concatenate([w_all_t[:3 * aw], w_all_t[3 * aw + heads:]], axis=0)
    w_fg_t = jnp.pad(w_all_t[3 * aw:3 * aw + heads], ((0, LANES - heads), (0, 0)))
    meta_full, cw_full = _slots_to_columns(g_meta), _slots_to_columns(g_cw)

    pad_rows = t - n_valid
    h0 = jnp.concatenate([meta_full, x[0], jnp.zeros((pad_rows, d), F32)], axis=0)
    target = jnp.concatenate([jnp.zeros((N_META, d), F32), loss_target[0], jnp.zeros((pad_rows, d), F32)], axis=0)
    b_f = _pad_lanes(b_fgate)

    xn = _rmsnorm_fwd(h0, norm_mix, "norm_mix_fwd")
    proj, (g_ao, g_co, g_o) = _matmul(
        xn, w_main_t, name="in_proj", trans_b=True, mid_at=0.6, out_dtypes=(BF16,),
        job=_Gather([w_attn_out[0].astype(BF16), w_conv_out[0].astype(BF16), w_o[0].astype(BF16)]))
    w_ao, w_co, w_o_f = _slots_to_columns(g_ao), _slots_to_columns(g_co), g_o.reshape(d, d)
    fg = _matmul(xn, w_fg_t, name="in_proj_fgate", trans_b=True)
    qn, kn, vb = _qk_prep(proj, q_norm, k_norm, aw, "qk_norm_fwd")
    cum = _forget_fwd(fg, b_f, "forget_cumsum")
    cum_heads = cum[:, :heads].T
    cum_row = cum_heads[:, None, :]
    t_attn = _tile(t, _attn_tile())
    (o, o_fine, lse), (g_up, g_down) = _attn_fwd(
        qn, kn, vb, cum_heads.reshape(heads, t // t_attn, 1, t_attn), "attention_fwd", mid_at=0.88,
        job=_Gather([w_up[0].astype(BF16), w_down[0].astype(BF16)]))
    w_up_f, w_down_f = _slots_to_columns(g_up), g_down.reshape(dff, d)
    a = _matmul(o, w_ao, name="attn_out_proj", out_dtypes=(BF16,))
    cpre = _conv_fwd(proj, cw_full, off_cb, "short_conv_fwd")
    c = _matmul(cpre, w_co, name="conv_out_proj", out_dtypes=(BF16,))
    merged = _gate_fwd(a, c, proj, b_gate, off_gl, "gate_merge_fwd")
    h1 = _matmul(merged, w_o_f, name="out_proj", extras=(h0,), epilogue=lambda acc, i, j, r: (r + acc,))
    hn = _rmsnorm_fwd(h1, norm_mlp, "norm_mlp_fwd")
    z, u = _matmul(hn, w_up_f, name="mlp_up", out_dtypes=(F32, BF16),
                   epilogue=lambda acc, i, j: (acc, jnp.square(jnp.maximum(acc, 0.0))))

    tm_down = _tile(t, (1408, 1024, 512, 256, 128))

    def loss_grad(acc, i, j, h1_tile, tgt_tile):
        rows = i * tm_down + lax.broadcasted_iota(jnp.int32, acc.shape, 0)
        valid = (rows >= N_META) & (rows < n_valid)
        dy = jnp.where(valid, ((h1_tile + acc) - tgt_tile) / d, 0.0)
        return dy, dy

    dh2, dh2b = _matmul(u, w_down_f, name="mlp_down_loss", extras=(h1, target), epilogue=loss_grad,
                        out_dtypes=(F32, BF16), tm=tm_down)
    loss_part = _sum_squares(dh2, "loss_sum") * (0.5 * d)

    wide = lambda n_cols: _tile(n_cols, (1024, 512, 256, 128))
    dw_down = _matmul(u, dh2b, name="mlp_down_wgrad", trans_a=True, tn=wide(d), out_dtypes=(BF16,))
    s_down = dw_down.reshape(N_DEV, dff // N_DEV, d)
    half_down = dff // N_DEV // 2
    dz, l_down0 = _matmul(dh2b, w_down_f, name="mlp_down_bwd", trans_b=True, extras=(z,), out_dtypes=(BF16,),
                          epilogue=lambda acc, i, j, zt: (acc * (2.0 * jnp.maximum(zt, 0.0)),),
                          job=_Scatter([(s_down, 0, half_down)]))
    s_up, l_down1 = _matmul(hn, dz, name="mlp_up_wgrad", trans_a=True, slots=True, out_dtypes=(BF16,),
                            tn=wide(dff // N_DEV), job=_Scatter([(s_down, half_down, half_down)]))
    dhn, l_up0 = _matmul(dz, w_up_f, name="mlp_up_bwd", trans_b=True, tn=wide(d),
                         job=_Scatter([(s_up, 0, d // 2)]))
    dh1, dh1b, dg_mlp = _rmsnorm_bwd(h1, dhn, norm_mlp, dh2, "norm_mlp_bwd")
    dmerged = _matmul(dh1b, w_o_f, name="out_proj_bwd", trans_b=True)
    dw_o = _matmul(merged, dh1b, name="out_proj_wgrad", trans_a=True, tn=wide(d), out_dtypes=(BF16,))
    da, dc, dgl0, dgl1, dbg0, dbg1 = _gate_bwd(dmerged, a, c, proj, b_gate, off_gl, "gate_merge_bwd")
    do = _matmul(da, w_ao, name="attn_out_bwd", trans_b=True)
    s_ao = _matmul(o, da, name="attn_out_wgrad", trans_a=True, slots=True, out_dtypes=(BF16,), tk=t)
    dcp = _matmul(dc, w_co, name="conv_out_bwd", trans_b=True)
    s_co = _matmul(cpre, dc, name="conv_out_wgrad", trans_a=True, slots=True, out_dtypes=(BF16,), tk=t)
    dcb, dcc, dcx, dcw = _conv_bwd(dcp, proj, cw_full, off_cb, "short_conv_bwd")
    delta = _attn_stats(do, o_fine, "attention_stats")
    (dqn, dkn, dv, dck), (l_up1, l_o, l_ao, l_co) = _attn_bwd(
        qn, kn, vb, do, lse, delta, cum_row, "attention_bwd",
        job=_Scatter([(s_up, d // 2, d // 2), dw_o.reshape(N_DEV, d // N_DEV, d), s_ao, s_co]))
    dq_raw, dk_raw, dg_q, dg_k = _qk_bwd(dqn, dkn, proj, q_norm, k_norm, aw, "qk_norm_bwd")
    dcum = _pad_lanes(dck.reshape(heads, t).T)
    dfg, db_f = _forget_bwd(dcum, fg, b_f, "forget_bwd")
    dproj = jnp.concatenate([dq_raw, dk_raw, dv, dcb, dcc, dcx, dgl0, dgl1], axis=1)
    dwt_fg = _matmul(dfg, xn, name="in_proj_fgate_wgrad", trans_a=True, out_dtypes=(BF16,))
    quarter = d // 4

    def in_slots(dwt, first):
        width = dwt.shape[1]
        parts = ((0, 3 * aw, dwt, 0), (3 * aw, 3 * aw + heads, dwt_fg[:heads, first:first + width], 3 * aw),
                 (3 * aw + heads, n_in, dwt, heads))
        slots = []
        for j in range(N_DEV):
            lo, hi = j * n_in // N_DEV, (j + 1) * n_in // N_DEV
            rows = [src[max(lo, a) - shift:min(hi, b) - shift] for a, b, src, shift in parts
                    if max(lo, a) < min(hi, b)]
            slots.append(rows[0] if len(rows) == 1 else jnp.concatenate(rows, axis=0))
        return jnp.stack(slots)

    def in_wgrad(idx, first_block, n_blocks, job):
        return _matmul(dproj, xn, name="in_proj_wgrad_%d" % idx, trans_a=True, tn=quarter,
                       cols=(first_block, n_blocks), out_dtypes=(BF16,), job=job)

    dwt0 = in_wgrad(0, 0, 1, None)
    dwt1, l_in0 = in_wgrad(1, 1, 1, _Scatter([in_slots(dwt0, 0)]))
    dwt2, l_in1 = in_wgrad(2, 2, 2, _Scatter([in_slots(dwt1, quarter)]))
    dxn_fg = _matmul(dfg, w_fg_t, name="in_proj_fgate_bwd")
    dxn, l_in2 = _matmul(dproj, w_main_t, name="in_proj_bwd", extras=(dxn_fg,),
                         epilogue=lambda acc, i, j, r: (r + acc,), job=_Scatter([in_slots(dwt2, 2 * quarter)]))
    dh0, _, dg_mix = _rmsnorm_bwd(h0, dxn, norm_mix, dh1, "norm_mix_bwd")

    small = [dg_mix, dbg0, dbg1, dg_mlp, dg_q, dg_k, db_f, loss_part, dcw, dh0[:N_META]]
    small_rows = [_rows_of(s) for s in small]
    pack = jnp.concatenate(small_rows, axis=0)
    pack = jnp.pad(pack, ((0, -pack.shape[0] % SUBLANES), (0, 0)))
    (pack_all,) = _run_job(_Scatter([], [pack]), "gather_small")

    landed = {"w_attn_out": [l_ao], "w_conv_out": [l_co], "w_o": [l_o],
              "w_up": l_up0 + [l_up1], "w_down": l_down0 + l_down1}
    shards = {"w_attn_out": (w_attn_out, m_w_attn_out, v_w_attn_out),
              "w_conv_out": (w_conv_out, m_w_conv_out, v_w_conv_out), "w_o": (w_o, m_w_o, v_w_o),
              "w_up": (w_up, m_w_up, v_w_up), "w_down": (w_down, m_w_down, v_w_down)}
    out = {}
    for nm, chunks in landed.items():
        w_, m_, v_ = shards[nm]
        res = _adamw(list(chunks), w_[0], m_[0], v_[0], "adamw_" + nm)
        out[nm] = [r[None] for r in res]
    res = _adamw_cols(l_in0 + l_in1 + l_in2, w_in_t, m_in_t, v_in_t, "adamw_w_in")
    out["w_in"] = [r.T[None] for r in res]

    total = _sum_parts(pack_all, "sum_small")
    pieces, at = [], 0
    for s, rows in zip(small, small_rows):
        n_el = 1
        for dim in s.shape:
            n_el *= dim
        pieces.append(total[at:at + rows.shape[0]].reshape(-1)[:n_el].reshape(s.shape))
        at += rows.shape[0]
    g_mix, g_bg0, g_bg1, g_mlp, g_q, g_k, g_bf, loss_row, g_cw_full, g_meta_full = pieces
    loss = loss_row[0, 0]
    cshard = conv_w.shape[2]
    g_small = {
        "norm_mix": g_mix, "b_gate": jnp.concatenate([g_bg0, g_bg1], axis=1), "norm_mlp": g_mlp,
        "q_norm": g_q, "k_norm": g_k, "b_fgate": g_bf[:, :heads],
        "conv_w": lax.dynamic_slice_in_dim(g_cw_full[:conv_w.shape[1]], me * cshard, cshard, axis=1)[None],
        "meta_tokens": lax.dynamic_slice_in_dim(g_meta_full, me * (d // N_DEV), d // N_DEV, axis=1),
    }
    small_w = {"norm_mix": (norm_mix, m_norm_mix, v_norm_mix), "b_gate": (b_gate, m_b_gate, v_b_gate),
               "norm_mlp": (norm_mlp, m_norm_mlp, v_norm_mlp), "q_norm": (q_norm, m_q_norm, v_q_norm),
               "k_norm": (k_norm, m_k_norm, v_k_norm), "b_fgate": (b_fgate, m_b_fgate, v_b_fgate),
               "conv_w": (conv_w, m_conv_w, v_conv_w), "meta_tokens": (meta_tokens, m_meta_tokens, v_meta_tokens)}
    order = list(small_w)
    packed = []
    for idx in range(4):
        cols = [g_small[nm] if idx == 0 else small_w[nm][idx - 1] for nm in order]
        rows = jnp.concatenate([_rows_of(c_) for c_ in cols], axis=0)
        packed.append(jnp.pad(rows, ((0, -rows.shape[0] % SUBLANES), (0, 0))))
    res = _adamw([packed[0][None]], packed[1], packed[2], packed[3], "adamw_small")
    at = 0
    for nm in order:
        shape = small_w[nm][0].shape
        n_el = 1
        for dim in shape:
            n_el *= dim
        n_rows = -(-n_el // LANES)
        out[nm] = [r[at:at + n_rows].reshape(-1)[:n_el].reshape(shape) for r in res]
        at += n_rows

    weights = ["meta_tokens", "norm_mix", "w_in", "b_fgate", "b_gate", "q_norm", "k_norm", "conv_w",
               "w_attn_out", "w_conv_out", "w_o", "norm_mlp", "w_up", "w_down"]
    grad_x = dh0[N_META:n_valid][None]
    return (loss, grad_x, *[out[nm][0] for nm in weights], *[out[nm][1] for nm in weights],
            *[out[nm][2] for nm in weights], *[out[nm][3] for nm in weights])
```

```python
import functools
import math

import jax
import jax.numpy as jnp
from jax import lax
from jax.experimental import pallas as pl
from jax.experimental.pallas import tpu as pltpu

F32 = jnp.float32
BF16 = jnp.bfloat16

N_DEV = 8
N_META = 16
HEAD_DIM = 128
LANES = 128
SUBLANES = 8
EPS = 1e-6
VMEM_LIMIT = 56 * 1024 * 1024

ADAM_LR = 0.001
ADAM_B1 = 0.9
ADAM_B2 = 0.999
ADAM_EPS = 1e-08
ADAM_WD = 0.01
ADAM_STEP = 10

MESH = pl.DeviceIdType.MESH
HBM_SPEC = pl.BlockSpec(memory_space=pltpu.HBM)
RELATIONS = tuple((r >> 2 & 1, r >> 1 & 1, r & 1) for r in range(1, N_DEV))


def _params(semantics=None):
    return pltpu.CompilerParams(dimension_semantics=semantics, vmem_limit_bytes=VMEM_LIMIT)


def _tile(n, prefs):
    for p in prefs:
        if n % p == 0:
            return p
    return n


def _sds(shape, dtype):
    return jax.ShapeDtypeStruct(shape, dtype)


def _my_place():
    return lax.axis_index("x"), lax.axis_index("y"), lax.axis_index("c")


def _flat(px, py, pc):
    return 4 * px + 2 * py + pc


class _Gather:
    def __init__(self, arrays):
        self.operands = list(arrays)
        self.n = len(arrays)
        self.out_shape = [_sds((N_DEV,) + a.shape, a.dtype) for a in arrays]
        self.split = [(a.shape[0] // 2 // 16 * 16) or a.shape[0] for a in arrays]

    def _copy(self, srcs, outs, sems, a, k, block, to, from_src=False, rows=None):
        slot = outs[a].at[_flat(*block)]
        if rows is not None:
            slot = slot.at[pl.ds(*rows)]
        return pltpu.make_async_remote_copy(
            src_ref=srcs[a] if from_src else slot, dst_ref=slot,
            send_sem=sems[0].at[a, k], recv_sem=sems[1].at[a, k],
            device_id=to, device_id_type=MESH)

    def _places(self):
        x, y, c = _my_place()
        return {"me": (x, y, c), "sib": (x, y, 1 - c), "x": (1 - x, y, c), "y": (x, 1 - y, c),
                "diag": (1 - x, 1 - y, c)}

    def _parts(self, a):
        n_rows, first = self.operands[a].shape[0], self.split[a]
        return (0, first), ((first, n_rows - first) if first < n_rows else None)

    def start(self, srcs, outs, sems):
        at = self._places()
        for a in range(self.n):
            pltpu.make_async_copy(srcs[a], outs[a].at[_flat(*at["me"])], sems[2].at[a]).start()
            self._copy(srcs, outs, sems, a, 1, at["me"], at["x"], from_src=True).start()
            self._copy(srcs, outs, sems, a, 2, at["me"], at["y"], from_src=True).start()
            self._copy(srcs, outs, sems, a, 0, at["me"], at["sib"], from_src=True).start()

    def mid(self, srcs, outs, sems):
        at = self._places()
        for a in range(self.n):
            first, rest = self._parts(a)
            self._copy(srcs, outs, sems, a, 1, at["x"], at["me"]).wait_recv()
            self._copy(srcs, outs, sems, a, 3, at["x"], at["y"], rows=first).start()
            self._copy(srcs, outs, sems, a, 5, at["x"], at["sib"]).start()
            self._copy(srcs, outs, sems, a, 2, at["y"], at["me"]).wait_recv()
            if rest:
                self._copy(srcs, outs, sems, a, 4, at["y"], at["x"], rows=rest).start()
            self._copy(srcs, outs, sems, a, 6, at["y"], at["sib"]).start()

    def finish(self, srcs, outs, sems):
        at = self._places()
        x, y, c = at["me"]
        for a in range(self.n):
            first, rest = self._parts(a)
            self._copy(srcs, outs, sems, a, 3, at["diag"], at["me"], rows=first).wait_recv()
            if rest:
                self._copy(srcs, outs, sems, a, 4, at["diag"], at["me"], rows=rest).wait_recv()
            self._copy(srcs, outs, sems, a, 7, at["diag"], at["sib"]).start()
        for a in range(self.n):
            first, rest = self._parts(a)
            self._copy(srcs, outs, sems, a, 0, at["sib"], at["me"]).wait_recv()
            for k, chip in ((5, (1 - x, y)), (6, (x, 1 - y)), (7, (1 - x, 1 - y))):
                self._copy(srcs, outs, sems, a, k, (*chip, 1 - c), at["me"]).wait_recv()
            for k in (0, 1, 2, 5, 6, 7):
                self._copy(srcs, outs, sems, a, k, at["me"], at["sib"]).wait_send()
            self._copy(srcs, outs, sems, a, 3, at["me"], at["sib"], rows=first).wait_send()
            if rest:
                self._copy(srcs, outs, sems, a, 4, at["me"], at["sib"], rows=rest).wait_send()
            pltpu.make_async_copy(srcs[a], outs[a].at[_flat(*at["me"])], sems[2].at[a]).wait()


class _Scatter:
    def __init__(self, scatter, gather=()):
        scatter = [s if isinstance(s, tuple) else (s, 0, s.shape[1]) for s in scatter]
        self.ranges = [(lo, cnt) for _, lo, cnt in scatter]
        self.operands = [s[0] for s in scatter] + list(gather)
        self.ns, self.n = len(scatter), len(scatter) + len(gather)
        self.out_shape = ([_sds((N_DEV, cnt, arr.shape[2]), arr.dtype) for arr, _, cnt in scatter]
                          + [_sds((N_DEV,) + a.shape, a.dtype) for a in gather])

    def _peer(self, rel):
        return tuple(1 - p if r else p for p, r in zip(_my_place(), rel))

    def _src(self, srcs, a, place):
        if a >= self.ns:
            return srcs[a]
        lo, cnt = self.ranges[a]
        return srcs[a].at[_flat(*place), pl.ds(lo, cnt)]

    def _send(self, srcs, outs, sems, a, k, rel):
        peer = self._peer(rel)
        return pltpu.make_async_remote_copy(
            src_ref=self._src(srcs, a, peer), dst_ref=outs[a].at[_flat(*_my_place())],
            send_sem=sems[0].at[a, k], recv_sem=sems[1].at[a, k],
            device_id=peer, device_id_type=MESH)

    def _landed(self, outs, sems, a, k, rel):
        peer = self._peer(rel)
        slot = outs[a].at[_flat(*peer)]
        return pltpu.make_async_remote_copy(
            src_ref=slot, dst_ref=slot, send_sem=sems[0].at[a, k], recv_sem=sems[1].at[a, k],
            device_id=peer, device_id_type=MESH)

    def _own(self, srcs, outs, sems, a):
        me = _my_place()
        return pltpu.make_async_copy(self._src(srcs, a, me), outs[a].at[_flat(*me)], sems[2].at[a])

    def start(self, srcs, outs, sems):
        for a in range(self.n):
            self._own(srcs, outs, sems, a).start()
            for k, rel in enumerate(RELATIONS):
                self._send(srcs, outs, sems, a, k, rel).start()

    def mid(self, srcs, outs, sems):
        pass

    def finish(self, srcs, outs, sems):
        for a in range(self.n):
            for k, rel in enumerate(RELATIONS):
                self._landed(outs, sems, a, k, rel).wait_recv()
            for k, rel in enumerate(RELATIONS):
                self._send(srcs, outs, sems, a, k, rel).wait_send()
            self._own(srcs, outs, sems, a).wait()


def _job_sems(job):
    return [pltpu.SemaphoreType.DMA((job.n, 8)), pltpu.SemaphoreType.DMA((job.n, 8)),
            pltpu.SemaphoreType.DMA((job.n,))]


def _run_job(job, name):
    n = job.n

    def body(*refs):
        srcs, outs, sems = refs[:n], refs[n:2 * n], refs[2 * n:]
        job.start(srcs, outs, sems)
        job.mid(srcs, outs, sems)
        job.finish(srcs, outs, sems)

    return pl.pallas_call(
        body, name=name, out_shape=job.out_shape,
        in_specs=[HBM_SPEC] * n, out_specs=[HBM_SPEC] * n, scratch_shapes=_job_sems(job),
    )(*job.operands)


def _call(body, *, name, grid, in_specs, out_specs, out_shape, scratch_shapes, semantics,
          operands, job=None, mid_at=0.5):
    if job is None:
        res = pl.pallas_call(
            body, name=name, grid=grid, in_specs=in_specs, out_specs=out_specs, out_shape=out_shape,
            scratch_shapes=scratch_shapes, compiler_params=_params(semantics))(*operands)
        return res, []
    n_in, n_out, n_scr = len(in_specs), len(out_specs), len(scratch_shapes)
    total = 1
    for g in grid:
        total *= g
    mid_step = min(int(total * mid_at), total - 1)

    def carried(*refs):
        c_in, j_in = refs[:n_in], refs[n_in:n_in + job.n]
        o0 = n_in + job.n
        c_out, j_out = refs[o0:o0 + n_out], refs[o0 + n_out:o0 + n_out + job.n]
        s0 = o0 + n_out + job.n
        c_scr, sems = refs[s0:s0 + n_scr], refs[s0 + n_scr:]
        step = pl.program_id(0)
        for ax in range(1, len(grid)):
            step = step * grid[ax] + pl.program_id(ax)

        @pl.when(step == 0)
        def _():
            job.start(j_in, j_out, sems)

        body(*c_in, *c_out, *c_scr)

        @pl.when(step == mid_step)
        def _():
            job.mid(j_in, j_out, sems)

        @pl.when(step == total - 1)
        def _():
            job.finish(j_in, j_out, sems)

    res = pl.pallas_call(
        carried, name=name, grid=grid,
        in_specs=list(in_specs) + [HBM_SPEC] * job.n,
        out_specs=list(out_specs) + [HBM_SPEC] * job.n,
        out_shape=list(out_shape) + job.out_shape,
        scratch_shapes=list(scratch_shapes) + _job_sems(job),
        compiler_params=_params(("arbitrary",) * len(grid)),
    )(*operands, *job.operands)
    return list(res[:n_out]), list(res[n_out:])


def _matmul(a, b, *, name, trans_b=False, extras=(), epilogue=None, out_dtypes=(F32,),
            tm=None, tn=None, tk=None, rows=None, cols=None, trans_a=False, slots=False, job=None,
            mid_at=0.5):
    k, m = a.shape if trans_a else a.shape[::-1]
    n = b.shape[0] if trans_b else b.shape[1]
    tm = tm or _tile(m, (1408, 1024, 512, 256, 128))
    tn = tn or _tile(n // N_DEV if slots else n, (512, 256, 128))
    tk = tk or _tile(k, (2048, 1408, 1024, 512, 256, 128))
    nk = k // tk
    row0, n_rows = rows or (0, m // tm)
    m = n_rows * tm
    col0, n_cols = cols or (0, n // tn)
    n = n_cols * tn
    n_ex, n_out = len(extras), len(out_dtypes)
    dims = (((0,) if trans_a else (1,), (1,) if trans_b else (0,)), ((), ()))

    def body(*refs):
        a_ref, b_ref = refs[:2]
        ex_refs = refs[2:2 + n_ex]
        out_refs = refs[2 + n_ex:2 + n_ex + n_out]
        part = lax.dot_general(a_ref[...].astype(BF16), b_ref[...].astype(BF16), dims,
                               preferred_element_type=F32)

        def finish(acc):
            if epilogue is None:
                res = (acc,)
            else:
                res = epilogue(acc, pl.program_id(0), pl.program_id(1), *[e[...] for e in ex_refs])
            for o_ref, r in zip(out_refs, res):
                o_ref[...] = r.astype(o_ref.dtype)

        if nk == 1:
            finish(part)
        else:
            acc_ref = refs[-1]
            kk = pl.program_id(2)

            @pl.when(kk == 0)
            def _():
                acc_ref[...] = part

            @pl.when(kk > 0)
            def _():
                acc_ref[...] += part

            @pl.when(kk == nk - 1)
            def _():
                finish(acc_ref[...])

    in_specs = [pl.BlockSpec((tk, tm), lambda i, j, kk: (kk, row0 + i)) if trans_a
                else pl.BlockSpec((tm, tk), lambda i, j, kk: (row0 + i, kk)),
                pl.BlockSpec((tn, tk), lambda i, j, kk: (col0 + j, kk)) if trans_b
                else pl.BlockSpec((tk, tn), lambda i, j, kk: (kk, col0 + j))]
    for e in extras:
        if e.shape[0] == 1:
            in_specs.append(pl.BlockSpec((1, tn), lambda i, j, kk: (0, j)))
        else:
            in_specs.append(pl.BlockSpec((tm, tn), lambda i, j, kk: (i, j)))
    if slots:
        per_slot = n // N_DEV // tn
        out_spec = pl.BlockSpec((None, tm, tn), lambda i, j, kk: (j // per_slot, i, j % per_slot))
        out_shape = [_sds((N_DEV, m, n // N_DEV), d) for d in out_dtypes]
    else:
        out_spec = pl.BlockSpec((tm, tn), lambda i, j, kk: (i, j))
        out_shape = [_sds((m, n), d) for d in out_dtypes]
    res, moved = _call(
        body, name=name, grid=(n_rows, n // tn, nk),
        in_specs=in_specs,
        out_specs=[out_spec] * n_out,
        out_shape=out_shape,
        scratch_shapes=[pltpu.VMEM((tm, tn), F32)] if nk > 1 else [],
        semantics=("parallel", "parallel", "arbitrary"),
        operands=(a, b, *extras), job=job, mid_at=mid_at)
    res = res[0] if n_out == 1 else tuple(res)
    return res if job is None else (res, moved)


def _rstd(x):
    return lax.rsqrt(jnp.mean(x * x, axis=-1, keepdims=True) + EPS)


def _norm_bwd(x, dy, g):
    r = _rstd(x)
    u = dy * g
    dx = r * u - x * (r * r * r) * jnp.mean(u * x, axis=-1, keepdims=True)
    return dx, dy * (x * r)


def _rmsnorm_fwd(h, g, name):
    t, d = h.shape
    tr = _tile(t, (384, 256, 128))

    def body(h_ref, g_ref, o_ref):
        x = h_ref[...]
        o_ref[...] = ((x * _rstd(x)) * g_ref[...]).astype(o_ref.dtype)

    row = pl.BlockSpec((tr, d), lambda i: (i, 0))
    return pl.pallas_call(
        body, name=name, grid=(t // tr,),
        in_specs=[row, pl.BlockSpec((1, d), lambda i: (0, 0))], out_specs=row,
        out_shape=_sds((t, d), BF16), compiler_params=_params(("parallel",)),
    )(h, g)


def _rmsnorm_bwd(h, dy, g, res, name):
    t, d = h.shape
    tr = _tile(t, (384, 256, 128))

    def body(h_ref, dy_ref, g_ref, res_ref, dh_ref, dhb_ref, dg_ref):
        dx, dg_rows = _norm_bwd(h_ref[...], dy_ref[...], g_ref[...])
        dh = res_ref[...] + dx
        dh_ref[...] = dh
        dhb_ref[...] = dh.astype(BF16)

        @pl.when(pl.program_id(0) == 0)
        def _():
            dg_ref[...] = jnp.zeros_like(dg_ref)

        dg_ref[...] += jnp.sum(dg_rows, axis=0, keepdims=True)

    row = pl.BlockSpec((tr, d), lambda i: (i, 0))
    vec = pl.BlockSpec((1, d), lambda i: (0, 0))
    return pl.pallas_call(
        body, name=name, grid=(t // tr,),
        in_specs=[row, row, vec, row], out_specs=[row, row, vec],
        out_shape=[_sds((t, d), F32), _sds((t, d), BF16), _sds((1, d), F32)],
        compiler_params=_params(("arbitrary",)),
    )(h, dy, g, res)


def _qk_prep(proj, gq, gk, aw, name):
    t = proj.shape[0]
    heads = aw // HEAD_DIM
    tr = _tile(t, (384, 256, 128))

    def body(q_ref, k_ref, v_ref, gq_ref, gk_ref, qo_ref, ko_ref, vo_ref):
        for h in range(heads):
            sl = slice(h * HEAD_DIM, (h + 1) * HEAD_DIM)
            xq, xk = q_ref[:, sl].astype(F32), k_ref[:, sl].astype(F32)
            qo_ref[:, sl] = ((xq * _rstd(xq)) * gq_ref[...]).astype(BF16)
            ko_ref[:, sl] = ((xk * _rstd(xk)) * gk_ref[...]).astype(BF16)
        vo_ref[...] = v_ref[...].astype(BF16)

    vec = pl.BlockSpec((1, HEAD_DIM), lambda i: (0, 0))
    out = pl.BlockSpec((tr, aw), lambda i: (i, 0))
    return pl.pallas_call(
        body, name=name, grid=(t // tr,),
        in_specs=[pl.BlockSpec((tr, aw), lambda i: (i, 0)), pl.BlockSpec((tr, aw), lambda i: (i, 1)),
                  pl.BlockSpec((tr, aw), lambda i: (i, 2)), vec, vec],
        out_specs=[out, out, out], out_shape=[_sds((t, aw), BF16)] * 3,
        compiler_params=_params(("parallel",)),
    )(proj, proj, proj, gq, gk)


def _qk_bwd(dqn, dkn, proj, gq, gk, aw, name):
    t = proj.shape[0]
    heads = aw // HEAD_DIM
    tr = _tile(t, (384, 256, 128))

    def body(dq_ref, dk_ref, q_ref, k_ref, gq_ref, gk_ref, dqo_ref, dko_ref, dgq_ref, dgk_ref):
        @pl.when(pl.program_id(0) == 0)
        def _():
            dgq_ref[...] = jnp.zeros_like(dgq_ref)
            dgk_ref[...] = jnp.zeros_like(dgk_ref)

        for h in range(heads):
            sl = slice(h * HEAD_DIM, (h + 1) * HEAD_DIM)
            dx, dg_rows = _norm_bwd(q_ref[:, sl].astype(F32), dq_ref[:, sl], gq_ref[...])
            dqo_ref[:, sl] = dx.astype(BF16)
            dgq_ref[...] += jnp.sum(dg_rows, axis=0, keepdims=True)
            dx, dg_rows = _norm_bwd(k_ref[:, sl].astype(F32), dk_ref[:, sl], gk_ref[...])
            dko_ref[:, sl] = dx.astype(BF16)
            dgk_ref[...] += jnp.sum(dg_rows, axis=0, keepdims=True)

    vec = pl.BlockSpec((1, HEAD_DIM), lambda i: (0, 0))
    row = pl.BlockSpec((tr, aw), lambda i: (i, 0))
    return pl.pallas_call(
        body, name=name, grid=(t // tr,),
        in_specs=[row, row, row, pl.BlockSpec((tr, aw), lambda i: (i, 1)), vec, vec],
        out_specs=[row, row, vec, vec],
        out_shape=[_sds((t, aw), BF16), _sds((t, aw), BF16), _sds((1, HEAD_DIM), F32), _sds((1, HEAD_DIM), F32)],
        compiler_params=_params(("arbitrary",)),
    )(dqn, dkn, proj, proj, gq, gk)


def _triangle(lower):
    r = lax.broadcasted_iota(jnp.int32, (LANES, LANES), 0)
    c = lax.broadcasted_iota(jnp.int32, (LANES, LANES), 1)
    return ((c <= r) if lower else (c >= r)).astype(F32)


def _forget_fwd(fg, b, name):
    t = fg.shape[0]

    def body(fg_ref, b_ref, cum_ref, carry):
        @pl.when(pl.program_id(0) == 0)
        def _():
            carry[...] = jnp.zeros_like(carry)

        z = fg_ref[...] + b_ref[...]
        log_f = jnp.minimum(z, 0.0) - jnp.log1p(jnp.exp(-jnp.abs(z)))
        cs = jnp.dot(_triangle(True), log_f, precision=lax.Precision.HIGHEST,
                     preferred_element_type=F32) + carry[0:1, :]
        cum_ref[...] = cs
        carry[...] = jnp.broadcast_to(cs[LANES - 1:LANES, :], carry.shape)

    row = pl.BlockSpec((LANES, LANES), lambda i: (i, 0))
    return pl.pallas_call(
        body, name=name, grid=(t // LANES,),
        in_specs=[row, pl.BlockSpec((1, LANES), lambda i: (0, 0))], out_specs=row,
        out_shape=_sds((t, LANES), F32), scratch_shapes=[pltpu.VMEM((SUBLANES, LANES), F32)],
        compiler_params=_params(("arbitrary",)),
    )(fg, b)


def _forget_bwd(dcum, fg, b, name):
    t = fg.shape[0]
    nt = t // LANES

    def body(dc_ref, fg_ref, b_ref, dfg_ref, db_ref, carry):
        @pl.when(pl.program_id(0) == 0)
        def _():
            carry[...] = jnp.zeros_like(carry)
            db_ref[...] = jnp.zeros_like(db_ref)

        d_log_f = jnp.dot(_triangle(False), dc_ref[...], precision=lax.Precision.HIGHEST,
                          preferred_element_type=F32) + carry[0:1, :]
        carry[...] = jnp.broadcast_to(d_log_f[0:1, :], carry.shape)
        dz = d_log_f * jax.nn.sigmoid(-(fg_ref[...] + b_ref[...]))
        dfg_ref[...] = dz.astype(BF16)
        db_ref[...] += jnp.sum(dz, axis=0, keepdims=True)

    row = pl.BlockSpec((LANES, LANES), lambda i: (nt - 1 - i, 0))
    vec = pl.BlockSpec((1, LANES), lambda i: (0, 0))
    return pl.pallas_call(
        body, name=name, grid=(nt,),
        in_specs=[row, row, vec], out_specs=[row, vec],
        out_shape=[_sds((t, LANES), BF16), _sds((1, LANES), F32)],
        scratch_shapes=[pltpu.VMEM((SUBLANES, LANES), F32)],
        compiler_params=_params(("arbitrary",)),
    )(dcum, fg, b)


def _causal(qi, kj, tq):
    rows = qi * tq + lax.broadcasted_iota(jnp.int32, (tq, tq), 0)
    cols = kj * tq + lax.broadcasted_iota(jnp.int32, (tq, tq), 1)
    return cols <= rows


NT_DIMS = (((1,), (1,)), ((), ()))
TN_DIMS = (((0,), (0,)), ((), ()))


def _attn_tile():
    return (384, 256, 128)


def _attn_fwd(q, k, v, cum_row, name, job=None, mid_at=0.5):
    t, aw = q.shape
    heads = aw // HEAD_DIM
    tq = _tile(t, _attn_tile())
    nq = t // tq
    rq = tq
    scale = HEAD_DIM ** -0.5

    def body(q_ref, k_ref, v_ref, ck_ref, o_ref, of_ref, lse_ref):
        qi = pl.program_id(1)
        qv = q_ref[...]
        n_full = (qi * rq) // tq

        def tile(kj, carry, masked):
            m_prev, l_prev, acc, res = carry
            ks = pl.ds(pl.multiple_of(kj * tq, tq), tq)
            s = lax.dot_general(qv, k_ref[ks, :], NT_DIMS, preferred_element_type=F32) * scale - ck_ref[kj]
            if masked:
                rows = qi * rq + lax.broadcasted_iota(jnp.int32, (rq, tq), 0)
                cols = kj * tq + lax.broadcasted_iota(jnp.int32, (rq, tq), 1)
                s = jnp.where(cols <= rows, s, -jnp.inf)
            m_new = jnp.maximum(m_prev, jnp.max(s, axis=-1, keepdims=True))
            alpha = jnp.exp(m_prev - m_new)
            p = jnp.exp(s - m_new)
            p_hi = p.astype(BF16)
            p_lo = (p - p_hi.astype(F32)).astype(BF16)
            vv = v_ref[ks, :]
            return (m_new, alpha * l_prev + jnp.sum(p, axis=-1, keepdims=True),
                    alpha * acc + jnp.dot(p_hi, vv, preferred_element_type=F32),
                    alpha * res + jnp.dot(p_lo, vv, preferred_element_type=F32))

        init = (jnp.full((rq, 1), -jnp.inf, F32), jnp.zeros((rq, 1), F32),
                jnp.zeros((rq, HEAD_DIM), F32), jnp.zeros((rq, HEAD_DIM), F32))
        carry = lax.fori_loop(0, n_full, lambda kj, c: tile(kj, c, False), init)
        m_fin, l_fin, acc, res = tile(n_full, carry, True)
        o_ref[...] = (acc / l_fin).astype(o_ref.dtype)
        of_ref[...] = (acc + res) / l_fin
        lse_ref[...] = m_fin + jnp.log(l_fin)

    q_spec = pl.BlockSpec((rq, HEAD_DIM), lambda h, i: (i, h))
    head = pl.BlockSpec((t, HEAD_DIM), lambda h, i: (0, h))
    return _call(
        body, name=name, grid=(heads, t // rq),
        in_specs=[q_spec, head, head, pl.BlockSpec((None, nq, 1, tq), lambda h, i: (h, 0, 0, 0))],
        out_specs=[q_spec, q_spec, pl.BlockSpec((None, rq, 1), lambda h, i: (h, i, 0))],
        out_shape=[_sds((t, aw), BF16), _sds((t, aw), F32), _sds((heads, t, 1), F32)],
        scratch_shapes=[], semantics=("parallel", "arbitrary"),
        operands=(q, k, v, cum_row), job=job, mid_at=mid_at)


def _attn_stats(do, o, name):
    t, aw = o.shape
    heads = aw // HEAD_DIM
    tr = _tile(t, (384, 256, 128))

    def body(do_ref, o_ref, delta_ref):
        for h in range(heads):
            sl = slice(h * HEAD_DIM, (h + 1) * HEAD_DIM)
            do_seen = do_ref[:, sl].astype(BF16).astype(F32)
            delta_ref[h] = jnp.sum(do_seen * o_ref[:, sl], axis=-1, keepdims=True)

    row = pl.BlockSpec((tr, aw), lambda i: (i, 0))
    return pl.pallas_call(
        body, name=name, grid=(t // tr,),
        in_specs=[row, row], out_specs=pl.BlockSpec((heads, tr, 1), lambda i: (0, i, 0)),
        out_shape=_sds((heads, t, 1), F32), compiler_params=_params(("parallel",)),
    )(do, o)


def _attn_bwd(q, k, v, do, lse, delta, cum_row, name, job=None):
    t, aw = q.shape
    heads = aw // HEAD_DIM
    tq = _tile(t, _attn_tile())
    nq = t // tq
    scale = HEAD_DIM ** -0.5

    def body(q_ref, k_ref, v_ref, do_ref, lse_ref, delta_ref, ck_ref, dq_ref, dk_ref, dv_ref, dck_ref):
        kj = pl.program_id(1)

        @pl.when(kj == 0)
        def _():
            dq_ref[...] = jnp.zeros_like(dq_ref)

        kv, vv, ck = k_ref[...], v_ref[...], ck_ref[...]

        def tile(qi, carry, masked):
            dk_acc, dv_acc, dck_acc = carry
            rows = pl.ds(pl.multiple_of(qi * tq, tq), tq)
            qv, dov = q_ref[rows, :], do_ref[rows, :].astype(BF16)
            s = lax.dot_general(qv, kv, NT_DIMS, preferred_element_type=F32) * scale - ck - lse_ref[rows, :]
            p = jnp.exp(s)
            if masked:
                p = jnp.where(_causal(0, 0, tq), p, 0.0)
            dp = lax.dot_general(dov, vv, NT_DIMS, preferred_element_type=F32)
            ds = p * (dp - delta_ref[rows, :])
            dsb = ds.astype(BF16)
            dq_ref[rows, :] += jnp.dot(dsb, kv, preferred_element_type=F32) * scale
            return (dk_acc + lax.dot_general(dsb, qv, TN_DIMS, preferred_element_type=F32),
                    dv_acc + lax.dot_general(p.astype(BF16), dov, TN_DIMS, preferred_element_type=F32),
                    dck_acc + jnp.sum(ds, axis=0, keepdims=True))

        init = (jnp.zeros((tq, HEAD_DIM), F32), jnp.zeros((tq, HEAD_DIM), F32), jnp.zeros((1, tq), F32))
        carry = tile(kj, init, True)
        dk_acc, dv_acc, dck_acc = lax.fori_loop(kj + 1, nq, lambda qi, c: tile(qi, c, False), carry)
        dk_ref[...] = dk_acc * scale
        dv_ref[...] = dv_acc.astype(dv_ref.dtype)
        dck_ref[...] = -dck_acc

    head = pl.BlockSpec((t, HEAD_DIM), lambda h, j: (0, h))
    k_spec = pl.BlockSpec((tq, HEAD_DIM), lambda h, j: (j, h))
    col = pl.BlockSpec((None, t, 1), lambda h, j: (h, 0, 0))
    row = pl.BlockSpec((None, 1, tq), lambda h, j: (h, 0, j))
    return _call(
        body, name=name, grid=(heads, nq),
        in_specs=[head, k_spec, k_spec, head, col, col, row],
        out_specs=[head, k_spec, k_spec, row],
        out_shape=[_sds((t, aw), F32), _sds((t, aw), F32), _sds((t, aw), BF16), _sds((heads, 1, t), F32)],
        scratch_shapes=[], semantics=("parallel", "arbitrary"),
        operands=(q, k, v, do, lse, delta, cum_row), job=job)


def _shift_down(u, by):
    rows = lax.broadcasted_iota(jnp.int32, u.shape, 0)
    return jnp.where(rows >= by, pltpu.roll(u, by, 0), 0.0)


def _shift_up(u, by):
    t = u.shape[0]
    rows = lax.broadcasted_iota(jnp.int32, u.shape, 0)
    return jnp.where(rows < t - by, pltpu.roll(u, t - by, 0), 0.0)


def _conv_specs(t, off_b, cw_width):
    nb = cw_width // LANES
    base = off_b // LANES
    return [pl.BlockSpec((t, LANES), lambda j, s=s: (0, base + s * nb + j)) for s in range(3)]


def _conv_fwd(proj, cw, off_b, name):
    t = proj.shape[0]
    width = cw.shape[1]

    def body(cb_ref, cc_ref, cx_ref, w_ref, o_ref):
        u = cc_ref[...].astype(F32) * cx_ref[...]
        y = w_ref[0:1, :] * _shift_down(u, 2) + w_ref[1:2, :] * _shift_down(u, 1) + w_ref[2:3, :] * u
        o_ref[...] = (cb_ref[...] * y).astype(BF16)

    return pl.pallas_call(
        body, name=name, grid=(width // LANES,),
        in_specs=_conv_specs(t, off_b, width) + [pl.BlockSpec((SUBLANES, LANES), lambda j: (0, j))],
        out_specs=pl.BlockSpec((t, LANES), lambda j: (0, j)),
        out_shape=_sds((t, width), BF16), compiler_params=_params(("parallel",)),
    )(proj, proj, proj, cw)


def _conv_bwd(dcp, proj, cw, off_b, name):
    t = proj.shape[0]
    width = cw.shape[1]

    def body(d_ref, cb_ref, cc_ref, cx_ref, w_ref, dcb_ref, dcc_ref, dcx_ref, dw_ref):
        cc, cx = cc_ref[...].astype(F32), cx_ref[...].astype(F32)
        u = cc * cx
        u1, u2 = _shift_down(u, 1), _shift_down(u, 2)
        w0, w1, w2 = w_ref[0:1, :], w_ref[1:2, :], w_ref[2:3, :]
        d = d_ref[...]
        dcb_ref[...] = (d * (w0 * u2 + w1 * u1 + w2 * u)).astype(BF16)
        dy = d * cb_ref[...]
        du = w2 * dy + w1 * _shift_up(dy, 1) + w0 * _shift_up(dy, 2)
        dcc_ref[...] = (du * cx).astype(BF16)
        dcx_ref[...] = (du * cc).astype(BF16)
        dw = [jnp.sum(dy * s, axis=0, keepdims=True) for s in (u2, u1, u)]
        dw_ref[...] = jnp.concatenate(dw + [jnp.zeros((SUBLANES - 3, LANES), F32)], axis=0)

    col = pl.BlockSpec((t, LANES), lambda j: (0, j))
    wspec = pl.BlockSpec((SUBLANES, LANES), lambda j: (0, j))
    return pl.pallas_call(
        body, name=name, grid=(width // LANES,),
        in_specs=[col] + _conv_specs(t, off_b, width) + [wspec],
        out_specs=[col, col, col, wspec],
        out_shape=[_sds((t, width), BF16)] * 3 + [_sds((SUBLANES, width), F32)],
        compiler_params=_params(("parallel",)),
    )(dcp, proj, proj, proj, cw)


def _gate_specs(t, d, off_g, tr, tc, rows_first):
    nb = d // tc
    base = off_g // tc
    if rows_first:
        tile = lambda s: pl.BlockSpec((tr, tc), lambda i, j: (i, base + s * nb + j))
        vec = lambda s: pl.BlockSpec((1, tc), lambda i, j: (0, s * nb + j))
        plain = pl.BlockSpec((tr, tc), lambda i, j: (i, j))
    else:
        tile = lambda s: pl.BlockSpec((tr, tc), lambda j, i: (i, base + s * nb + j))
        vec = lambda s: pl.BlockSpec((1, tc), lambda j, i: (0, s * nb + j))
        plain = pl.BlockSpec((tr, tc), lambda j, i: (i, j))
    return tile, vec, plain


def _gate_fwd(a, c, proj, bg, off_g, name):
    t, d = a.shape
    tr, tc = _tile(t, (384, 256, 128)), _tile(d, (512, 256, 128))
    tile, vec, plain = _gate_specs(t, d, off_g, tr, tc, True)

    def body(a_ref, c_ref, g0_ref, g1_ref, b0_ref, b1_ref, o_ref):
        g0 = jax.nn.sigmoid(g0_ref[...] + b0_ref[...])
        g1 = jax.nn.sigmoid(g1_ref[...] + b1_ref[...])
        o_ref[...] = (g0 * a_ref[...] + g1 * c_ref[...]).astype(BF16)

    return pl.pallas_call(
        body, name=name, grid=(t // tr, d // tc),
        in_specs=[plain, plain, tile(0), tile(1), vec(0), vec(1)], out_specs=plain,
        out_shape=_sds((t, d), BF16), compiler_params=_params(("parallel", "parallel")),
    )(a, c, proj, proj, bg, bg)


def _gate_bwd(dm, a, c, proj, bg, off_g, name):
    t, d = a.shape
    tr, tc = _tile(t, (384, 256, 128)), _tile(d, (512, 256, 128))
    tile, vec, plain = _gate_specs(t, d, off_g, tr, tc, False)

    def body(dm_ref, a_ref, c_ref, g0_ref, g1_ref, b0_ref, b1_ref,
             da_ref, dc_ref, dg0_ref, dg1_ref, db0_ref, db1_ref):
        @pl.when(pl.program_id(1) == 0)
        def _():
            db0_ref[...] = jnp.zeros_like(db0_ref)
            db1_ref[...] = jnp.zeros_like(db1_ref)

        dm = dm_ref[...]
        g0 = jax.nn.sigmoid(g0_ref[...] + b0_ref[...])
        g1 = jax.nn.sigmoid(g1_ref[...] + b1_ref[...])
        da_ref[...] = (dm * g0).astype(BF16)
        dc_ref[...] = (dm * g1).astype(BF16)
        dz0 = dm * a_ref[...] * (g0 * (1.0 - g0))
        dz1 = dm * c_ref[...] * (g1 * (1.0 - g1))
        dg0_ref[...] = dz0.astype(BF16)
        dg1_ref[...] = dz1.astype(BF16)
        db0_ref[...] += jnp.sum(dz0, axis=0, keepdims=True)
        db1_ref[...] += jnp.sum(dz1, axis=0, keepdims=True)

    bvec = pl.BlockSpec((1, tc), lambda j, i: (0, j))
    return pl.pallas_call(
        body, name=name, grid=(d // tc, t // tr),
        in_specs=[plain, plain, plain, tile(0), tile(1), vec(0), vec(1)],
        out_specs=[plain] * 4 + [bvec, bvec],
        out_shape=[_sds((t, d), BF16)] * 4 + [_sds((1, d), F32)] * 2,
        compiler_params=_params(("parallel", "arbitrary")),
    )(dm, a, c, proj, proj, bg, bg)


def _sum_squares(x, name):
    t, d = x.shape
    tr = _tile(t, (384, 256, 128))

    def body(x_ref, o_ref):
        @pl.when(pl.program_id(0) == 0)
        def _():
            o_ref[...] = jnp.zeros_like(o_ref)

        v = x_ref[...]
        o_ref[...] += jnp.sum(jnp.sum(v * v, axis=0, keepdims=True), axis=1, keepdims=True)

    return pl.pallas_call(
        body, name=name, grid=(t // tr,),
        in_specs=[pl.BlockSpec((tr, d), lambda i: (i, 0))],
        out_specs=pl.BlockSpec((1, LANES), lambda i: (0, 0)),
        out_shape=_sds((1, LANES), F32), compiler_params=_params(("arbitrary",)),
    )(x)


def _row_tile(r, c):
    return r if r * c <= 128 * 1024 else _tile(r, (128, 64, 32, 16))


def _sum_parts(parts, name):
    n, r, c = parts.shape
    tr = _row_tile(r, c)

    def body(p_ref, o_ref):
        acc = p_ref[0].astype(F32)
        for i in range(1, n):
            acc = acc + p_ref[i].astype(F32)
        o_ref[...] = acc

    return pl.pallas_call(
        body, name=name, grid=(r // tr,),
        in_specs=[pl.BlockSpec((n, tr, c), lambda i: (0, i, 0))],
        out_specs=pl.BlockSpec((tr, c), lambda i: (i, 0)),
        out_shape=_sds((r, c), F32), compiler_params=_params(("parallel",)),
    )(parts)


def _adamw(chunks, w, m, v, name):
    n, rc, c = chunks[0].shape
    r = rc * len(chunks)
    tr = _row_tile(rc, c)
    per = rc // tr

    def body(*refs):
        p_refs = refs[:len(chunks)]
        w_ref, m_ref, v_ref, g_ref, d_ref, nm_ref, nv_ref = refs[len(chunks):]
        i = pl.program_id(0)

        def update(p_ref):
            g = p_ref[0].astype(F32)
            for s in range(1, n):
                g = g + p_ref[s].astype(F32)
            nm = ADAM_B1 * m_ref[...] + (1.0 - ADAM_B1) * g
            nv = ADAM_B2 * v_ref[...] + (1.0 - ADAM_B2) * (g * g)
            m_hat = nm / (1.0 - ADAM_B1 ** ADAM_STEP)
            v_hat = nv / (1.0 - ADAM_B2 ** ADAM_STEP)
            g_ref[...] = g
            d_ref[...] = -ADAM_LR * (m_hat / (jnp.sqrt(v_hat) + ADAM_EPS) + ADAM_WD * w_ref[...])
            nm_ref[...] = nm
            nv_ref[...] = nv

        if len(chunks) == 1:
            update(p_refs[0])
        else:
            for ci, p_ref in enumerate(p_refs):
                pl.when((i >= ci * per) & (i < (ci + 1) * per))(functools.partial(update, p_ref))

    row = pl.BlockSpec((tr, c), lambda i: (i, 0))
    part_specs = [pl.BlockSpec((n, tr, c), lambda i, ci=ci: (0, jnp.clip(i - ci * per, 0, per - 1), 0))
                  for ci in range(len(chunks))]
    return pl.pallas_call(
        body, name=name, grid=(r // tr,),
        in_specs=part_specs + [row, row, row],
        out_specs=[row] * 4, out_shape=[_sds((r, c), F32)] * 4,
        compiler_params=_params(("parallel",)),
    )(*chunks, w, m, v)


def _adamw_cols(chunks, w, m, v, name):
    n, r, _ = chunks[0].shape
    widths = [ch.shape[2] for ch in chunks]
    tc = functools.reduce(math.gcd, widths, LANES)
    firsts = [sum(widths[:ci]) // tc for ci in range(len(chunks) + 1)]

    def body(*refs):
        p_refs = refs[:len(chunks)]
        w_ref, m_ref, v_ref, g_ref, d_ref, nm_ref, nv_ref = refs[len(chunks):]
        j = pl.program_id(0)

        def update(p_ref):
            g = p_ref[0].astype(F32)
            for s in range(1, n):
                g = g + p_ref[s].astype(F32)
            nm = ADAM_B1 * m_ref[...] + (1.0 - ADAM_B1) * g
            nv = ADAM_B2 * v_ref[...] + (1.0 - ADAM_B2) * (g * g)
            m_hat = nm / (1.0 - ADAM_B1 ** ADAM_STEP)
            v_hat = nv / (1.0 - ADAM_B2 ** ADAM_STEP)
            g_ref[...] = g
            d_ref[...] = -ADAM_LR * (m_hat / (jnp.sqrt(v_hat) + ADAM_EPS) + ADAM_WD * w_ref[...])
            nm_ref[...] = nm
            nv_ref[...] = nv

        for ci, p_ref in enumerate(p_refs):
            pl.when((j >= firsts[ci]) & (j < firsts[ci + 1]))(functools.partial(update, p_ref))

    col = pl.BlockSpec((r, tc), lambda j: (0, j))
    part_specs = [pl.BlockSpec((n, r, tc),
                               lambda j, lo=firsts[ci], hi=firsts[ci + 1]: (0, 0, jnp.clip(j - lo, 0, hi - lo - 1)))
                  for ci in range(len(chunks))]
    return pl.pallas_call(
        body, name=name, grid=(firsts[-1],),
        in_specs=part_specs + [col, col, col],
        out_specs=[col] * 4, out_shape=[_sds((r, firsts[-1] * tc), F32)] * 4,
        compiler_params=_params(("parallel",)),
    )(*chunks, w, m, v)


def _pad_lanes(a, width=LANES):
    return jnp.pad(a, ((0, 0), (0, width - a.shape[1])))


def _rows_of(a):
    flat = a.reshape(-1)
    n = -(-flat.shape[0] // LANES) * LANES
    return jnp.pad(flat, (0, n - flat.shape[0])).reshape(-1, LANES)


def _columns_to_slots(full, n_rows):
    return full.reshape(n_rows, N_DEV, -1).transpose(1, 0, 2)


def _slots_to_columns(slots):
    return slots.transpose(1, 0, 2).reshape(slots.shape[1], -1)


def kernel(x, meta_tokens, norm_mix, w_in, b_fgate, b_gate, q_norm, k_norm, conv_w, w_attn_out, w_conv_out, w_o, norm_mlp, w_up, w_down, loss_target, m_meta_tokens, m_norm_mix, m_w_in, m_b_fgate, m_b_gate, m_q_norm, m_k_norm, m_conv_w, m_w_attn_out, m_w_conv_out, m_w_o, m_norm_mlp, m_w_up, m_w_down, v_meta_tokens, v_norm_mix, v_w_in, v_b_fgate, v_b_gate, v_q_norm, v_k_norm, v_conv_w, v_w_attn_out, v_w_conv_out, v_w_o, v_norm_mlp, v_w_up, v_w_down):
    seq, d = x.shape[1], x.shape[2]
    heads = b_fgate.shape[1]
    aw = heads * HEAD_DIM
    cwid = conv_w.shape[2] * N_DEV
    dff = w_up.shape[2] * N_DEV
    n_valid = N_META + seq
    t = -(-n_valid // LANES) * LANES
    me = _flat(*_my_place())
    off_cb, off_gl = 3 * aw, 3 * aw + 3 * cwid

    conv_shard = jnp.pad(conv_w[0], ((0, SUBLANES - conv_w.shape[1]), (0, 0)))
    w_in_t, m_in_t, v_in_t = (jnp.swapaxes(p, 1, 2)[0] for p in (w_in, m_w_in, v_w_in))
    g_in, g_meta, g_cw = _run_job(_Gather([w_in_t.astype(BF16), meta_tokens, conv_shard]), "gather_first")
    n_in = N_DEV * g_in.shape[1]
    w_all_t = g_in.reshape(n_in, d)
    w_main_t = jnp.concatenate([w_all_t[:3 * aw], w_all_t[3 * aw + heads:]], axis=0)
    w_fg_t = jnp.pad(w_all_t[3 * aw:3 * aw + heads], ((0, LANES - heads), (0, 0)))
    meta_full, cw_full = _slots_to_columns(g_meta), _slots_to_columns(g_cw)

    pad_rows = t - n_valid
    h0 = jnp.concatenate([meta_full, x[0], jnp.zeros((pad_rows, d), F32)], axis=0)
    target = jnp.concatenate([jnp.zeros((N_META, d), F32), loss_target[0], jnp.zeros((pad_rows, d), F32)], axis=0)
    b_f = _pad_lanes(b_fgate)

    xn = _rmsnorm_fwd(h0, norm_mix, "norm_mix_fwd")
    proj, (g_ao, g_co, g_o) = _matmul(
        xn, w_main_t, name="in_proj", trans_b=True, mid_at=0.5, out_dtypes=(BF16,),
        job=_Gather([w_attn_out[0].astype(BF16), w_conv_out[0].astype(BF16), w_o[0].astype(BF16)]))
    w_ao, w_co, w_o_f = _slots_to_columns(g_ao), _slots_to_columns(g_co), g_o.reshape(d, d)
    fg = _matmul(xn, w_fg_t, name="in_proj_fgate", trans_b=True)
    qn, kn, vb = _qk_prep(proj, q_norm, k_norm, aw, "qk_norm_fwd")
    cum = _forget_fwd(fg, b_f, "forget_cumsum")
    cum_heads = cum[:, :heads].T
    cum_row = cum_heads[:, None, :]
    t_attn = _tile(t, _attn_tile())
    (o, o_fine, lse), (g_up, g_down) = _attn_fwd(
        qn, kn, vb, cum_heads.reshape(heads, t // t_attn, 1, t_attn), "attention_fwd", mid_at=0.55,
        job=_Gather([w_up[0].astype(BF16), w_down[0].astype(BF16)]))
    w_up_f, w_down_f = _slots_to_columns(g_up), g_down.reshape(dff, d)
    a = _matmul(o, w_ao, name="attn_out_proj", out_dtypes=(BF16,))
    cpre = _conv_fwd(proj, cw_full, off_cb, "short_conv_fwd")
    c = _matmul(cpre, w_co, name="conv_out_proj", out_dtypes=(BF16,))
    merged = _gate_fwd(a, c, proj, b_gate, off_gl, "gate_merge_fwd")
    h1 = _matmul(merged, w_o_f, name="out_proj", extras=(h0,), epilogue=lambda acc, i, j, r: (r + acc,))
    hn = _rmsnorm_fwd(h1, norm_mlp, "norm_mlp_fwd")
    z, u = _matmul(hn, w_up_f, name="mlp_up", out_dtypes=(F32, BF16),
                   epilogue=lambda acc, i, j: (acc, jnp.square(jnp.maximum(acc, 0.0))))

    tm_down = _tile(t, (1408, 1024, 512, 256, 128))

    def loss_grad(acc, i, j, h1_tile, tgt_tile):
        rows = i * tm_down + lax.broadcasted_iota(jnp.int32, acc.shape, 0)
        valid = (rows >= N_META) & (rows < n_valid)
        dy = jnp.where(valid, ((h1_tile + acc) - tgt_tile) / d, 0.0)
        return dy, dy

    dh2, dh2b = _matmul(u, w_down_f, name="mlp_down_loss", extras=(h1, target), epilogue=loss_grad,
                        out_dtypes=(F32, BF16), tm=tm_down)
    loss_part = _sum_squares(dh2, "loss_sum") * (0.5 * d)

    wide = lambda n_cols: _tile(n_cols, (1024, 512, 256, 128))
    dw_down = _matmul(u, dh2b, name="mlp_down_wgrad", trans_a=True, tn=wide(d), out_dtypes=(BF16,))
    s_down = dw_down.reshape(N_DEV, dff // N_DEV, d)
    half_down = dff // N_DEV // 2
    dz, l_down0 = _matmul(dh2b, w_down_f, name="mlp_down_bwd", trans_b=True, extras=(z,), out_dtypes=(BF16,),
                          epilogue=lambda acc, i, j, zt: (acc * (2.0 * jnp.maximum(zt, 0.0)),),
                          job=_Scatter([(s_down, 0, half_down)]))
    s_up, l_down1 = _matmul(hn, dz, name="mlp_up_wgrad", trans_a=True, slots=True, out_dtypes=(BF16,),
                            tn=wide(dff // N_DEV), job=_Scatter([(s_down, half_down, half_down)]))
    dhn, l_up0 = _matmul(dz, w_up_f, name="mlp_up_bwd", trans_b=True, tn=wide(d),
                         job=_Scatter([(s_up, 0, d // 2)]))
    dh1, dh1b, dg_mlp = _rmsnorm_bwd(h1, dhn, norm_mlp, dh2, "norm_mlp_bwd")
    dmerged = _matmul(dh1b, w_o_f, name="out_proj_bwd", trans_b=True)
    dw_o = _matmul(merged, dh1b, name="out_proj_wgrad", trans_a=True, tn=wide(d), out_dtypes=(BF16,))
    da, dc, dgl0, dgl1, dbg0, dbg1 = _gate_bwd(dmerged, a, c, proj, b_gate, off_gl, "gate_merge_bwd")
    do = _matmul(da, w_ao, name="attn_out_bwd", trans_b=True)
    s_ao = _matmul(o, da, name="attn_out_wgrad", trans_a=True, slots=True, out_dtypes=(BF16,), tk=t)
    dcp = _matmul(dc, w_co, name="conv_out_bwd", trans_b=True)
    s_co = _matmul(cpre, dc, name="conv_out_wgrad", trans_a=True, slots=True, out_dtypes=(BF16,), tk=t)
    dcb, dcc, dcx, dcw = _conv_bwd(dcp, proj, cw_full, off_cb, "short_conv_bwd")
    delta = _attn_stats(do, o_fine, "attention_stats")
    (dqn, dkn, dv, dck), (l_up1, l_o, l_ao, l_co) = _attn_bwd(
        qn, kn, vb, do, lse, delta, cum_row, "attention_bwd",
        job=_Scatter([(s_up, d // 2, d // 2), dw_o.reshape(N_DEV, d // N_DEV, d), s_ao, s_co]))
    dq_raw, dk_raw, dg_q, dg_k = _qk_bwd(dqn, dkn, proj, q_norm, k_norm, aw, "qk_norm_bwd")
    dcum = _pad_lanes(dck.reshape(heads, t).T)
    dfg, db_f = _forget_bwd(dcum, fg, b_f, "forget_bwd")
    dproj = jnp.concatenate([dq_raw, dk_raw, dv, dcb, dcc, dcx, dgl0, dgl1], axis=1)
    dwt_fg = _matmul(dfg, xn, name="in_proj_fgate_wgrad", trans_a=True, out_dtypes=(BF16,))
    range_ends = [3 * d // 16, d // 2, d]

    def in_slots(dwt, first):
        width = dwt.shape[1]
        parts = ((0, 3 * aw, dwt, 0), (3 * aw, 3 * aw + heads, dwt_fg[:heads, first:first + width], 3 * aw),
                 (3 * aw + heads, n_in, dwt, heads))
        slots = []
        for j in range(N_DEV):
            lo, hi = j * n_in // N_DEV, (j + 1) * n_in // N_DEV
            rows = [src[max(lo, a) - shift:min(hi, b) - shift] for a, b, src, shift in parts
                    if max(lo, a) < min(hi, b)]
            slots.append(rows[0] if len(rows) == 1 else jnp.concatenate(rows, axis=0))
        return jnp.stack(slots)

    def in_wgrad(idx, job):
        lo, hi = ([0] + range_ends)[idx], range_ends[idx]
        return _matmul(dproj, xn[:, lo:hi], name="in_proj_wgrad_%d" % idx, trans_a=True, tn=hi - lo,
                       out_dtypes=(BF16,), job=job)

    dwt0 = in_wgrad(0, None)
    dwt1, l_in0 = in_wgrad(1, _Scatter([in_slots(dwt0, 0)]))
    dwt2, l_in1 = in_wgrad(2, _Scatter([in_slots(dwt1, range_ends[0])]))
    dxn_fg = _matmul(dfg, w_fg_t, name="in_proj_fgate_bwd")
    dxn, l_in2 = _matmul(dproj, w_main_t, name="in_proj_bwd", extras=(dxn_fg,),
                         epilogue=lambda acc, i, j, r: (r + acc,), job=_Scatter([in_slots(dwt2, range_ends[1])]))
    dh0, _, dg_mix = _rmsnorm_bwd(h0, dxn, norm_mix, dh1, "norm_mix_bwd")

    small = [dg_mix, dbg0, dbg1, dg_mlp, dg_q, dg_k, db_f, loss_part, dcw, dh0[:N_META]]
    small_rows = [_rows_of(s) for s in small]
    pack = jnp.concatenate(small_rows, axis=0)
    pack = jnp.pad(pack, ((0, -pack.shape[0] % SUBLANES), (0, 0)))
    (pack_all,) = _run_job(_Scatter([], [pack]), "gather_small")

    landed = {"w_attn_out": [l_ao], "w_conv_out": [l_co], "w_o": [l_o],
              "w_up": l_up0 + [l_up1], "w_down": l_down0 + l_down1}
    shards = {"w_attn_out": (w_attn_out, m_w_attn_out, v_w_attn_out),
              "w_conv_out": (w_conv_out, m_w_conv_out, v_w_conv_out), "w_o": (w_o, m_w_o, v_w_o),
              "w_up": (w_up, m_w_up, v_w_up), "w_down": (w_down, m_w_down, v_w_down)}
    out = {}
    for nm, chunks in landed.items():
        w_, m_, v_ = shards[nm]
        res = _adamw(list(chunks), w_[0], m_[0], v_[0], "adamw_" + nm)
        out[nm] = [r[None] for r in res]
    res = _adamw_cols(l_in0 + l_in1 + l_in2, w_in_t, m_in_t, v_in_t, "adamw_w_in")
    out["w_in"] = [r.T[None] for r in res]

    total = _sum_parts(pack_all, "sum_small")
    pieces, at = [], 0
    for s, rows in zip(small, small_rows):
        n_el = 1
        for dim in s.shape:
            n_el *= dim
        pieces.append(total[at:at + rows.shape[0]].reshape(-1)[:n_el].reshape(s.shape))
        at += rows.shape[0]
    g_mix, g_bg0, g_bg1, g_mlp, g_q, g_k, g_bf, loss_row, g_cw_full, g_meta_full = pieces
    loss = loss_row[0, 0]
    cshard = conv_w.shape[2]
    g_small = {
        "norm_mix": g_mix, "b_gate": jnp.concatenate([g_bg0, g_bg1], axis=1), "norm_mlp": g_mlp,
        "q_norm": g_q, "k_norm": g_k, "b_fgate": g_bf[:, :heads],
        "conv_w": lax.dynamic_slice_in_dim(g_cw_full[:conv_w.shape[1]], me * cshard, cshard, axis=1)[None],
        "meta_tokens": lax.dynamic_slice_in_dim(g_meta_full, me * (d // N_DEV), d // N_DEV, axis=1),
    }
    small_w = {"norm_mix": (norm_mix, m_norm_mix, v_norm_mix), "b_gate": (b_gate, m_b_gate, v_b_gate),
               "norm_mlp": (norm_mlp, m_norm_mlp, v_norm_mlp), "q_norm": (q_norm, m_q_norm, v_q_norm),
               "k_norm": (k_norm, m_k_norm, v_k_norm), "b_fgate": (b_fgate, m_b_fgate, v_b_fgate),
               "conv_w": (conv_w, m_conv_w, v_conv_w), "meta_tokens": (meta_tokens, m_meta_tokens, v_meta_tokens)}
    order = list(small_w)
    packed = []
    for idx in range(4):
        cols = [g_small[nm] if idx == 0 else small_w[nm][idx - 1] for nm in order]
        rows = jnp.concatenate([_rows_of(c_) for c_ in cols], axis=0)
        packed.append(jnp.pad(rows, ((0, -rows.shape[0] % SUBLANES), (0, 0))))
    res = _adamw([packed[0][None]], packed[1], packed[2], packed[3], "adamw_small")
    at = 0
    for nm in order:
        shape = small_w[nm][0].shape
        n_el = 1
        for dim in shape:
            n_el *= dim
        n_rows = -(-n_el // LANES)
        out[nm] = [r[at:at + n_rows].reshape(-1)[:n_el].reshape(shape) for r in res]
        at += n_rows

    weights = ["meta_tokens", "norm_mix", "w_in", "b_fgate", "b_gate", "q_norm", "k_norm", "conv_w",
               "w_attn_out", "w_conv_out", "w_o", "norm_mlp", "w_up", "w_down"]
    grad_x = dh0[N_META:n_valid][None]
    return (loss, grad_x, *[out[nm][0] for nm in weights], *[out[nm][1] for nm in weights],
            *[out[nm][2] for nm in weights], *[out[nm][3] for nm in weights])
```

```python
import functools
import math

import jax
import jax.numpy as jnp
from jax import lax
from jax.experimental import pallas as pl
from jax.experimental.pallas import tpu as pltpu

F32 = jnp.float32
BF16 = jnp.bfloat16

N_DEV = 8
N_META = 16
HEAD_DIM = 128
LANES = 128
SUBLANES = 8
EPS = 1e-6
VMEM_LIMIT = 56 * 1024 * 1024

ADAM_LR = 0.001
ADAM_B1 = 0.9
ADAM_B2 = 0.999
ADAM_EPS = 1e-08
ADAM_WD = 0.01
ADAM_STEP = 10

MESH = pl.DeviceIdType.MESH
HBM_SPEC = pl.BlockSpec(memory_space=pltpu.HBM)
RELATIONS = tuple((r >> 2 & 1, r >> 1 & 1, r & 1) for r in range(1, N_DEV))


def _params(semantics=None):
    return pltpu.CompilerParams(dimension_semantics=semantics, vmem_limit_bytes=VMEM_LIMIT)


def _tile(n, prefs):
    for p in prefs:
        if n % p == 0:
            return p
    return n


def _sds(shape, dtype):
    return jax.ShapeDtypeStruct(shape, dtype)


def _my_place():
    return lax.axis_index("x"), lax.axis_index("y"), lax.axis_index("c")


def _flat(px, py, pc):
    return 4 * px + 2 * py + pc


class _Gather:
    def __init__(self, arrays):
        self.operands = list(arrays)
        self.n = len(arrays)
        self.out_shape = [_sds((N_DEV,) + a.shape, a.dtype) for a in arrays]
        self.split = [(a.shape[0] // 2 // 16 * 16) or a.shape[0] for a in arrays]

    def _copy(self, srcs, outs, sems, a, k, block, to, from_src=False, rows=None):
        slot = outs[a].at[_flat(*block)]
        if rows is not None:
            slot = slot.at[pl.ds(*rows)]
        return pltpu.make_async_remote_copy(
            src_ref=srcs[a] if from_src else slot, dst_ref=slot,
            send_sem=sems[0].at[a, k], recv_sem=sems[1].at[a, k],
            device_id=to, device_id_type=MESH)

    def _places(self):
        x, y, c = _my_place()
        return {"me": (x, y, c), "sib": (x, y, 1 - c), "x": (1 - x, y, c), "y": (x, 1 - y, c),
                "diag": (1 - x, 1 - y, c)}

    def _parts(self, a):
        n_rows, first = self.operands[a].shape[0], self.split[a]
        return (0, first), ((first, n_rows - first) if first < n_rows else None)

    def start(self, srcs, outs, sems):
        at = self._places()
        for a in range(self.n):
            pltpu.make_async_copy(srcs[a], outs[a].at[_flat(*at["me"])], sems[2].at[a]).start()
            self._copy(srcs, outs, sems, a, 1, at["me"], at["x"], from_src=True).start()
            self._copy(srcs, outs, sems, a, 2, at["me"], at["y"], from_src=True).start()
            self._copy(srcs, outs, sems, a, 0, at["me"], at["sib"], from_src=True).start()

    def mid(self, srcs, outs, sems):
        at = self._places()
        for a in range(self.n):
            first, rest = self._parts(a)
            self._copy(srcs, outs, sems, a, 1, at["x"], at["me"]).wait_recv()
            self._copy(srcs, outs, sems, a, 3, at["x"], at["y"], rows=first).start()
            self._copy(srcs, outs, sems, a, 5, at["x"], at["sib"]).start()
            self._copy(srcs, outs, sems, a, 2, at["y"], at["me"]).wait_recv()
            if rest:
                self._copy(srcs, outs, sems, a, 4, at["y"], at["x"], rows=rest).start()
            self._copy(srcs, outs, sems, a, 6, at["y"], at["sib"]).start()

    def finish(self, srcs, outs, sems):
        at = self._places()
        x, y, c = at["me"]
        for a in range(self.n):
            first, rest = self._parts(a)
            self._copy(srcs, outs, sems, a, 3, at["diag"], at["me"], rows=first).wait_recv()
            if rest:
                self._copy(srcs, outs, sems, a, 4, at["diag"], at["me"], rows=rest).wait_recv()
            self._copy(srcs, outs, sems, a, 7, at["diag"], at["sib"]).start()
        for a in range(self.n):
            first, rest = self._parts(a)
            self._copy(srcs, outs, sems, a, 0, at["sib"], at["me"]).wait_recv()
            for k, chip in ((5, (1 - x, y)), (6, (x, 1 - y)), (7, (1 - x, 1 - y))):
                self._copy(srcs, outs, sems, a, k, (*chip, 1 - c), at["me"]).wait_recv()
            for k in (0, 1, 2, 5, 6, 7):
                self._copy(srcs, outs, sems, a, k, at["me"], at["sib"]).wait_send()
            self._copy(srcs, outs, sems, a, 3, at["me"], at["sib"], rows=first).wait_send()
            if rest:
                self._copy(srcs, outs, sems, a, 4, at["me"], at["sib"], rows=rest).wait_send()
            pltpu.make_async_copy(srcs[a], outs[a].at[_flat(*at["me"])], sems[2].at[a]).wait()


class _Scatter:
    def __init__(self, scatter, gather=()):
        scatter = [s if isinstance(s, tuple) else (s, 0, s.shape[1]) for s in scatter]
        self.ranges = [(lo, cnt) for _, lo, cnt in scatter]
        self.operands = [s[0] for s in scatter] + list(gather)
        self.ns, self.n = len(scatter), len(scatter) + len(gather)
        self.out_shape = ([_sds((N_DEV, cnt, arr.shape[2]), arr.dtype) for arr, _, cnt in scatter]
                          + [_sds((N_DEV,) + a.shape, a.dtype) for a in gather])

    def _peer(self, rel):
        return tuple(1 - p if r else p for p, r in zip(_my_place(), rel))

    def _src(self, srcs, a, place):
        if a >= self.ns:
            return srcs[a]
        lo, cnt = self.ranges[a]
        return srcs[a].at[_flat(*place), pl.ds(lo, cnt)]

    def _send(self, srcs, outs, sems, a, k, rel):
        peer = self._peer(rel)
        return pltpu.make_async_remote_copy(
            src_ref=self._src(srcs, a, peer), dst_ref=outs[a].at[_flat(*_my_place())],
            send_sem=sems[0].at[a, k], recv_sem=sems[1].at[a, k],
            device_id=peer, device_id_type=MESH)

    def _landed(self, outs, sems, a, k, rel):
        peer = self._peer(rel)
        slot = outs[a].at[_flat(*peer)]
        return pltpu.make_async_remote_copy(
            src_ref=slot, dst_ref=slot, send_sem=sems[0].at[a, k], recv_sem=sems[1].at[a, k],
            device_id=peer, device_id_type=MESH)

    def _own(self, srcs, outs, sems, a):
        me = _my_place()
        return pltpu.make_async_copy(self._src(srcs, a, me), outs[a].at[_flat(*me)], sems[2].at[a])

    def start(self, srcs, outs, sems):
        for a in range(self.n):
            self._own(srcs, outs, sems, a).start()
            for k, rel in enumerate(RELATIONS):
                self._send(srcs, outs, sems, a, k, rel).start()

    def mid(self, srcs, outs, sems):
        pass

    def finish(self, srcs, outs, sems):
        for a in range(self.n):
            for k, rel in enumerate(RELATIONS):
                self._landed(outs, sems, a, k, rel).wait_recv()
            for k, rel in enumerate(RELATIONS):
                self._send(srcs, outs, sems, a, k, rel).wait_send()
            self._own(srcs, outs, sems, a).wait()


def _job_sems(job):
    return [pltpu.SemaphoreType.DMA((job.n, 8)), pltpu.SemaphoreType.DMA((job.n, 8)),
            pltpu.SemaphoreType.DMA((job.n,))]


def _run_job(job, name):
    n = job.n

    def body(*refs):
        srcs, outs, sems = refs[:n], refs[n:2 * n], refs[2 * n:]
        job.start(srcs, outs, sems)
        job.mid(srcs, outs, sems)
        job.finish(srcs, outs, sems)

    return pl.pallas_call(
        body, name=name, out_shape=job.out_shape,
        in_specs=[HBM_SPEC] * n, out_specs=[HBM_SPEC] * n, scratch_shapes=_job_sems(job),
    )(*job.operands)


def _call(body, *, name, grid, in_specs, out_specs, out_shape, scratch_shapes, semantics,
          operands, job=None, mid_at=0.5):
    if job is None:
        res = pl.pallas_call(
            body, name=name, grid=grid, in_specs=in_specs, out_specs=out_specs, out_shape=out_shape,
            scratch_shapes=scratch_shapes, compiler_params=_params(semantics))(*operands)
        return res, []
    n_in, n_out, n_scr = len(in_specs), len(out_specs), len(scratch_shapes)
    total = 1
    for g in grid:
        total *= g
    mid_step = min(int(total * mid_at), total - 1)

    def carried(*refs):
        c_in, j_in = refs[:n_in], refs[n_in:n_in + job.n]
        o0 = n_in + job.n
        c_out, j_out = refs[o0:o0 + n_out], refs[o0 + n_out:o0 + n_out + job.n]
        s0 = o0 + n_out + job.n
        c_scr, sems = refs[s0:s0 + n_scr], refs[s0 + n_scr:]
        step = pl.program_id(0)
        for ax in range(1, len(grid)):
            step = step * grid[ax] + pl.program_id(ax)

        @pl.when(step == 0)
        def _():
            job.start(j_in, j_out, sems)

        body(*c_in, *c_out, *c_scr)

        @pl.when(step == mid_step)
        def _():
            job.mid(j_in, j_out, sems)

        @pl.when(step == total - 1)
        def _():
            job.finish(j_in, j_out, sems)

    res = pl.pallas_call(
        carried, name=name, grid=grid,
        in_specs=list(in_specs) + [HBM_SPEC] * job.n,
        out_specs=list(out_specs) + [HBM_SPEC] * job.n,
        out_shape=list(out_shape) + job.out_shape,
        scratch_shapes=list(scratch_shapes) + _job_sems(job),
        compiler_params=_params(("arbitrary",) * len(grid)),
    )(*operands, *job.operands)
    return list(res[:n_out]), list(res[n_out:])


def _matmul(a, b, *, name, trans_b=False, extras=(), epilogue=None, out_dtypes=(F32,),
            tm=None, tn=None, tk=None, rows=None, cols=None, trans_a=False, slots=False, job=None,
            mid_at=0.5):
    a_list = list(a) if isinstance(a, (list, tuple)) else [a]
    n_a = len(a_list)
    split = sum(x.shape[1] for x in a_list)
    k, m = (a_list[0].shape[0], split) if trans_a else (split, a_list[0].shape[0])
    n = b.shape[0] if trans_b else b.shape[1]
    tm = tm or _tile(m, (1408, 1024, 512, 256, 128))
    tn = tn or _tile(n // N_DEV if slots else n, (512, 256, 128))
    tk = tk or _tile(k, (2048, 1408, 1024, 512, 256, 128))
    nk = k // tk
    row0, n_rows = rows or (0, m // tm)
    m = n_rows * tm
    col0, n_cols = cols or (0, n // tn)
    n = n_cols * tn
    n_ex, n_out = len(extras), len(out_dtypes)
    dims = (((0,) if trans_a else (1,), (1,) if trans_b else (0,)), ((), ()))
    t_split = tm if trans_a else tk
    firsts = [sum(x.shape[1] for x in a_list[:s]) // t_split for s in range(n_a + 1)]

    def body(*refs):
        a_refs, b_ref = refs[:n_a], refs[n_a]
        ex_refs = refs[n_a + 1:n_a + 1 + n_ex]
        out_refs = refs[n_a + 1 + n_ex:n_a + 1 + n_ex + n_out]

        def finish(acc):
            if epilogue is None:
                res = (acc,)
            else:
                res = epilogue(acc, pl.program_id(0), pl.program_id(1), *[e[...] for e in ex_refs])
            for o_ref, r in zip(out_refs, res):
                o_ref[...] = r.astype(o_ref.dtype)

        def step(a_ref):
            part = lax.dot_general(a_ref[...].astype(BF16), b_ref[...].astype(BF16), dims,
                                   preferred_element_type=F32)
            if nk == 1:
                finish(part)
                return
            acc_ref = refs[-1]
            kk = pl.program_id(2)

            @pl.when(kk == 0)
            def _():
                acc_ref[...] = part

            @pl.when(kk > 0)
            def _():
                acc_ref[...] += part

            @pl.when(kk == nk - 1)
            def _():
                finish(acc_ref[...])

        if n_a == 1:
            step(a_refs[0])
        else:
            at = row0 + pl.program_id(0) if trans_a else pl.program_id(2)
            for s, a_ref in enumerate(a_refs):
                pl.when((at >= firsts[s]) & (at < firsts[s + 1]))(functools.partial(step, a_ref))

    def a_spec(s):
        lo, cnt = firsts[s], firsts[s + 1] - firsts[s]
        if trans_a:
            return pl.BlockSpec((tk, tm), lambda i, j, kk: (kk, jnp.clip(row0 + i - lo, 0, cnt - 1)))
        return pl.BlockSpec((tm, tk), lambda i, j, kk: (row0 + i, jnp.clip(kk - lo, 0, cnt - 1)))

    in_specs = [a_spec(s) for s in range(n_a)] + [
                pl.BlockSpec((tn, tk), lambda i, j, kk: (col0 + j, kk)) if trans_b
                else pl.BlockSpec((tk, tn), lambda i, j, kk: (kk, col0 + j))]
    for e in extras:
        if e.shape[0] == 1:
            in_specs.append(pl.BlockSpec((1, tn), lambda i, j, kk: (0, j)))
        else:
            in_specs.append(pl.BlockSpec((tm, tn), lambda i, j, kk: (i, j)))
    if slots:
        per_slot = n // N_DEV // tn
        out_spec = pl.BlockSpec((None, tm, tn), lambda i, j, kk: (j // per_slot, i, j % per_slot))
        out_shape = [_sds((N_DEV, m, n // N_DEV), d) for d in out_dtypes]
    else:
        out_spec = pl.BlockSpec((tm, tn), lambda i, j, kk: (i, j))
        out_shape = [_sds((m, n), d) for d in out_dtypes]
    res, moved = _call(
        body, name=name, grid=(n_rows, n // tn, nk),
        in_specs=in_specs,
        out_specs=[out_spec] * n_out,
        out_shape=out_shape,
        scratch_shapes=[pltpu.VMEM((tm, tn), F32)] if nk > 1 else [],
        semantics=("parallel", "parallel", "arbitrary"),
        operands=(*a_list, b, *extras), job=job, mid_at=mid_at)
    res = res[0] if n_out == 1 else tuple(res)
    return res if job is None else (res, moved)


def _rstd(x):
    return lax.rsqrt(jnp.mean(x * x, axis=-1, keepdims=True) + EPS)


def _norm_bwd(x, dy, g):
    r = _rstd(x)
    u = dy * g
    dx = r * u - x * (r * r * r) * jnp.mean(u * x, axis=-1, keepdims=True)
    return dx, dy * (x * r)


def _rmsnorm_fwd(h, g, name):
    t, d = h.shape
    tr = _tile(t, (384, 256, 128))

    def body(h_ref, g_ref, o_ref):
        x = h_ref[...]
        o_ref[...] = ((x * _rstd(x)) * g_ref[...]).astype(o_ref.dtype)

    row = pl.BlockSpec((tr, d), lambda i: (i, 0))
    return pl.pallas_call(
        body, name=name, grid=(t // tr,),
        in_specs=[row, pl.BlockSpec((1, d), lambda i: (0, 0))], out_specs=row,
        out_shape=_sds((t, d), BF16), compiler_params=_params(("parallel",)),
    )(h, g)


def _rmsnorm_bwd(h, dy, g, res, name):
    t, d = h.shape
    tr = _tile(t, (384, 256, 128))

    def body(h_ref, dy_ref, g_ref, res_ref, dh_ref, dhb_ref, dg_ref):
        dx, dg_rows = _norm_bwd(h_ref[...], dy_ref[...], g_ref[...])
        dh = res_ref[...] + dx
        dh_ref[...] = dh
        dhb_ref[...] = dh.astype(BF16)

        @pl.when(pl.program_id(0) == 0)
        def _():
            dg_ref[...] = jnp.zeros_like(dg_ref)

        dg_ref[...] += jnp.sum(dg_rows, axis=0, keepdims=True)

    row = pl.BlockSpec((tr, d), lambda i: (i, 0))
    vec = pl.BlockSpec((1, d), lambda i: (0, 0))
    return pl.pallas_call(
        body, name=name, grid=(t // tr,),
        in_specs=[row, row, vec, row], out_specs=[row, row, vec],
        out_shape=[_sds((t, d), F32), _sds((t, d), BF16), _sds((1, d), F32)],
        compiler_params=_params(("arbitrary",)),
    )(h, dy, g, res)


def _qk_prep(proj, gq, gk, aw, name):
    t = proj.shape[0]
    heads = aw // HEAD_DIM
    tr = _tile(t, (384, 256, 128))

    def body(q_ref, k_ref, v_ref, gq_ref, gk_ref, qo_ref, ko_ref, vo_ref):
        for h in range(heads):
            sl = slice(h * HEAD_DIM, (h + 1) * HEAD_DIM)
            xq, xk = q_ref[:, sl].astype(F32), k_ref[:, sl].astype(F32)
            qo_ref[:, sl] = ((xq * _rstd(xq)) * gq_ref[...]).astype(BF16)
            ko_ref[:, sl] = ((xk * _rstd(xk)) * gk_ref[...]).astype(BF16)
        vo_ref[...] = v_ref[...].astype(BF16)

    vec = pl.BlockSpec((1, HEAD_DIM), lambda i: (0, 0))
    out = pl.BlockSpec((tr, aw), lambda i: (i, 0))
    return pl.pallas_call(
        body, name=name, grid=(t // tr,),
        in_specs=[pl.BlockSpec((tr, aw), lambda i: (i, 0)), pl.BlockSpec((tr, aw), lambda i: (i, 1)),
                  pl.BlockSpec((tr, aw), lambda i: (i, 2)), vec, vec],
        out_specs=[out, out, out], out_shape=[_sds((t, aw), BF16)] * 3,
        compiler_params=_params(("parallel",)),
    )(proj, proj, proj, gq, gk)


def _qk_bwd(dqn, dkn, proj, gq, gk, aw, name):
    t = proj.shape[0]
    heads = aw // HEAD_DIM
    tr = _tile(t, (384, 256, 128))

    def body(dq_ref, dk_ref, q_ref, k_ref, gq_ref, gk_ref, dqo_ref, dko_ref, dgq_ref, dgk_ref):
        @pl.when(pl.program_id(0) == 0)
        def _():
            dgq_ref[...] = jnp.zeros_like(dgq_ref)
            dgk_ref[...] = jnp.zeros_like(dgk_ref)

        for h in range(heads):
            sl = slice(h * HEAD_DIM, (h + 1) * HEAD_DIM)
            dx, dg_rows = _norm_bwd(q_ref[:, sl].astype(F32), dq_ref[:, sl], gq_ref[...])
            dqo_ref[:, sl] = dx.astype(BF16)
            dgq_ref[...] += jnp.sum(dg_rows, axis=0, keepdims=True)
            dx, dg_rows = _norm_bwd(k_ref[:, sl].astype(F32), dk_ref[:, sl], gk_ref[...])
            dko_ref[:, sl] = dx.astype(BF16)
            dgk_ref[...] += jnp.sum(dg_rows, axis=0, keepdims=True)

    vec = pl.BlockSpec((1, HEAD_DIM), lambda i: (0, 0))
    row = pl.BlockSpec((tr, aw), lambda i: (i, 0))
    return pl.pallas_call(
        body, name=name, grid=(t // tr,),
        in_specs=[row, row, row, pl.BlockSpec((tr, aw), lambda i: (i, 1)), vec, vec],
        out_specs=[row, row, vec, vec],
        out_shape=[_sds((t, aw), BF16), _sds((t, aw), BF16), _sds((1, HEAD_DIM), F32), _sds((1, HEAD_DIM), F32)],
        compiler_params=_params(("arbitrary",)),
    )(dqn, dkn, proj, proj, gq, gk)


def _triangle(lower):
    r = lax.broadcasted_iota(jnp.int32, (LANES, LANES), 0)
    c = lax.broadcasted_iota(jnp.int32, (LANES, LANES), 1)
    return ((c <= r) if lower else (c >= r)).astype(F32)


def _forget_fwd(fg, b, name):
    t = fg.shape[0]

    def body(fg_ref, b_ref, cum_ref, carry):
        @pl.when(pl.program_id(0) == 0)
        def _():
            carry[...] = jnp.zeros_like(carry)

        z = fg_ref[...] + b_ref[...]
        log_f = jnp.minimum(z, 0.0) - jnp.log1p(jnp.exp(-jnp.abs(z)))
        cs = jnp.dot(_triangle(True), log_f, precision=lax.Precision.HIGHEST,
                     preferred_element_type=F32) + carry[0:1, :]
        cum_ref[...] = cs
        carry[...] = jnp.broadcast_to(cs[LANES - 1:LANES, :], carry.shape)

    row = pl.BlockSpec((LANES, LANES), lambda i: (i, 0))
    return pl.pallas_call(
        body, name=name, grid=(t // LANES,),
        in_specs=[row, pl.BlockSpec((1, LANES), lambda i: (0, 0))], out_specs=row,
        out_shape=_sds((t, LANES), F32), scratch_shapes=[pltpu.VMEM((SUBLANES, LANES), F32)],
        compiler_params=_params(("arbitrary",)),
    )(fg, b)


def _forget_bwd(dcum, fg, b, name):
    t = fg.shape[0]
    nt = t // LANES

    def body(dc_ref, fg_ref, b_ref, dfg_ref, db_ref, carry):
        @pl.when(pl.program_id(0) == 0)
        def _():
            carry[...] = jnp.zeros_like(carry)
            db_ref[...] = jnp.zeros_like(db_ref)

        d_log_f = jnp.dot(_triangle(False), dc_ref[...], precision=lax.Precision.HIGHEST,
                          preferred_element_type=F32) + carry[0:1, :]
        carry[...] = jnp.broadcast_to(d_log_f[0:1, :], carry.shape)
        dz = d_log_f * jax.nn.sigmoid(-(fg_ref[...] + b_ref[...]))
        dfg_ref[...] = dz.astype(BF16)
        db_ref[...] += jnp.sum(dz, axis=0, keepdims=True)

    row = pl.BlockSpec((LANES, LANES), lambda i: (nt - 1 - i, 0))
    vec = pl.BlockSpec((1, LANES), lambda i: (0, 0))
    return pl.pallas_call(
        body, name=name, grid=(nt,),
        in_specs=[row, row, vec], out_specs=[row, vec],
        out_shape=[_sds((t, LANES), BF16), _sds((1, LANES), F32)],
        scratch_shapes=[pltpu.VMEM((SUBLANES, LANES), F32)],
        compiler_params=_params(("arbitrary",)),
    )(dcum, fg, b)


def _causal(qi, kj, tq):
    rows = qi * tq + lax.broadcasted_iota(jnp.int32, (tq, tq), 0)
    cols = kj * tq + lax.broadcasted_iota(jnp.int32, (tq, tq), 1)
    return cols <= rows


NT_DIMS = (((1,), (1,)), ((), ()))
TN_DIMS = (((0,), (0,)), ((), ()))


def _attn_tile():
    return (384, 256, 128)


def _attn_fwd(q, k, v, cum_row, name, job=None, mid_at=0.5):
    t, aw = q.shape
    heads = aw // HEAD_DIM
    tq = _tile(t, _attn_tile())
    nq = t // tq
    rq = tq
    scale = HEAD_DIM ** -0.5

    def body(q_ref, k_ref, v_ref, ck_ref, o_ref, of_ref, lse_ref):
        qi = pl.program_id(1)
        qv = q_ref[...]
        n_full = (qi * rq) // tq

        def tile(kj, carry, masked):
            m_prev, l_prev, acc, res = carry
            ks = pl.ds(pl.multiple_of(kj * tq, tq), tq)
            s = lax.dot_general(qv, k_ref[ks, :], NT_DIMS, preferred_element_type=F32) * scale - ck_ref[kj]
            if masked:
                rows = qi * rq + lax.broadcasted_iota(jnp.int32, (rq, tq), 0)
                cols = kj * tq + lax.broadcasted_iota(jnp.int32, (rq, tq), 1)
                s = jnp.where(cols <= rows, s, -jnp.inf)
            m_new = jnp.maximum(m_prev, jnp.max(s, axis=-1, keepdims=True))
            alpha = jnp.exp(m_prev - m_new)
            p = jnp.exp(s - m_new)
            p_hi = p.astype(BF16)
            p_lo = (p - p_hi.astype(F32)).astype(BF16)
            vv = v_ref[ks, :]
            return (m_new, alpha * l_prev + jnp.sum(p, axis=-1, keepdims=True),
                    alpha * acc + jnp.dot(p_hi, vv, preferred_element_type=F32),
                    alpha * res + jnp.dot(p_lo, vv, preferred_element_type=F32))

        init = (jnp.full((rq, 1), -jnp.inf, F32), jnp.zeros((rq, 1), F32),
                jnp.zeros((rq, HEAD_DIM), F32), jnp.zeros((rq, HEAD_DIM), F32))
        carry = lax.fori_loop(0, n_full, lambda kj, c: tile(kj, c, False), init)
        m_fin, l_fin, acc, res = tile(n_full, carry, True)
        o_ref[...] = (acc / l_fin).astype(o_ref.dtype)
        of_ref[...] = (acc + res) / l_fin
        lse_ref[...] = m_fin + jnp.log(l_fin)

    q_spec = pl.BlockSpec((rq, HEAD_DIM), lambda h, i: (i, h))
    head = pl.BlockSpec((t, HEAD_DIM), lambda h, i: (0, h))
    return _call(
        body, name=name, grid=(heads, t // rq),
        in_specs=[q_spec, head, head, pl.BlockSpec((None, nq, 1, tq), lambda h, i: (h, 0, 0, 0))],
        out_specs=[q_spec, q_spec, pl.BlockSpec((None, rq, 1), lambda h, i: (h, i, 0))],
        out_shape=[_sds((t, aw), BF16), _sds((t, aw), F32), _sds((heads, t, 1), F32)],
        scratch_shapes=[], semantics=("parallel", "arbitrary"),
        operands=(q, k, v, cum_row), job=job, mid_at=mid_at)


def _attn_stats(do, o, name):
    t, aw = o.shape
    heads = aw // HEAD_DIM
    tr = _tile(t, (384, 256, 128))

    def body(do_ref, o_ref, delta_ref):
        for h in range(heads):
            sl = slice(h * HEAD_DIM, (h + 1) * HEAD_DIM)
            do_seen = do_ref[:, sl].astype(BF16).astype(F32)
            delta_ref[h] = jnp.sum(do_seen * o_ref[:, sl], axis=-1, keepdims=True)

    row = pl.BlockSpec((tr, aw), lambda i: (i, 0))
    return pl.pallas_call(
        body, name=name, grid=(t // tr,),
        in_specs=[row, row], out_specs=pl.BlockSpec((heads, tr, 1), lambda i: (0, i, 0)),
        out_shape=_sds((heads, t, 1), F32), compiler_params=_params(("parallel",)),
    )(do, o)


def _attn_bwd(q, k, v, do, lse, delta, cum_row, name, job=None):
    t, aw = q.shape
    heads = aw // HEAD_DIM
    tq = _tile(t, _attn_tile())
    nq = t // tq
    scale = HEAD_DIM ** -0.5

    def body(q_ref, k_ref, v_ref, do_ref, lse_ref, delta_ref, ck_ref, dq_ref, dk_ref, dv_ref, dck_ref):
        kj = pl.program_id(1)

        @pl.when(kj == 0)
        def _():
            dq_ref[...] = jnp.zeros_like(dq_ref)

        kv, vv, ck = k_ref[...], v_ref[...], ck_ref[...]

        def tile(qi, carry, masked):
            dk_acc, dv_acc, dck_acc = carry
            rows = pl.ds(pl.multiple_of(qi * tq, tq), tq)
            qv, dov = q_ref[rows, :], do_ref[rows, :].astype(BF16)
            s = lax.dot_general(qv, kv, NT_DIMS, preferred_element_type=F32) * scale - ck - lse_ref[rows, :]
            p = jnp.exp(s)
            if masked:
                p = jnp.where(_causal(0, 0, tq), p, 0.0)
            dp = lax.dot_general(dov, vv, NT_DIMS, preferred_element_type=F32)
            ds = p * (dp - delta_ref[rows, :])
            dsb = ds.astype(BF16)
            dq_ref[rows, :] += jnp.dot(dsb, kv, preferred_element_type=F32) * scale
            return (dk_acc + lax.dot_general(dsb, qv, TN_DIMS, preferred_element_type=F32),
                    dv_acc + lax.dot_general(p.astype(BF16), dov, TN_DIMS, preferred_element_type=F32),
                    dck_acc + jnp.sum(ds, axis=0, keepdims=True))

        init = (jnp.zeros((tq, HEAD_DIM), F32), jnp.zeros((tq, HEAD_DIM), F32), jnp.zeros((1, tq), F32))
        carry = tile(kj, init, True)
        dk_acc, dv_acc, dck_acc = lax.fori_loop(kj + 1, nq, lambda qi, c: tile(qi, c, False), carry)
        dk_ref[...] = dk_acc * scale
        dv_ref[...] = dv_acc.astype(dv_ref.dtype)
        dck_ref[...] = -dck_acc

    head = pl.BlockSpec((t, HEAD_DIM), lambda h, j: (0, h))
    k_spec = pl.BlockSpec((tq, HEAD_DIM), lambda h, j: (j, h))
    col = pl.BlockSpec((None, t, 1), lambda h, j: (h, 0, 0))
    row = pl.BlockSpec((None, 1, tq), lambda h, j: (h, 0, j))
    return _call(
        body, name=name, grid=(heads, nq),
        in_specs=[head, k_spec, k_spec, head, col, col, row],
        out_specs=[head, k_spec, k_spec, row],
        out_shape=[_sds((t, aw), F32), _sds((t, aw), F32), _sds((t, aw), BF16), _sds((heads, 1, t), F32)],
        scratch_shapes=[], semantics=("parallel", "arbitrary"),
        operands=(q, k, v, do, lse, delta, cum_row), job=job)


def _shift_down(u, by):
    rows = lax.broadcasted_iota(jnp.int32, u.shape, 0)
    return jnp.where(rows >= by, pltpu.roll(u, by, 0), 0.0)


def _shift_up(u, by):
    t = u.shape[0]
    rows = lax.broadcasted_iota(jnp.int32, u.shape, 0)
    return jnp.where(rows < t - by, pltpu.roll(u, t - by, 0), 0.0)


def _conv_specs(t, off_b, cw_width):
    nb = cw_width // LANES
    base = off_b // LANES
    return [pl.BlockSpec((t, LANES), lambda j, s=s: (0, base + s * nb + j)) for s in range(3)]


def _conv_fwd(proj, cw, off_b, name):
    t = proj.shape[0]
    width = cw.shape[1]

    def body(cb_ref, cc_ref, cx_ref, w_ref, o_ref):
        u = cc_ref[...].astype(F32) * cx_ref[...]
        y = w_ref[0:1, :] * _shift_down(u, 2) + w_ref[1:2, :] * _shift_down(u, 1) + w_ref[2:3, :] * u
        o_ref[...] = (cb_ref[...] * y).astype(BF16)

    return pl.pallas_call(
        body, name=name, grid=(width // LANES,),
        in_specs=_conv_specs(t, off_b, width) + [pl.BlockSpec((SUBLANES, LANES), lambda j: (0, j))],
        out_specs=pl.BlockSpec((t, LANES), lambda j: (0, j)),
        out_shape=_sds((t, width), BF16), compiler_params=_params(("parallel",)),
    )(proj, proj, proj, cw)


def _conv_bwd(dcp, proj, cw, off_b, name):
    t = proj.shape[0]
    width = cw.shape[1]

    def body(d_ref, cb_ref, cc_ref, cx_ref, w_ref, dcb_ref, dcc_ref, dcx_ref, dw_ref):
        cc, cx = cc_ref[...].astype(F32), cx_ref[...].astype(F32)
        u = cc * cx
        u1, u2 = _shift_down(u, 1), _shift_down(u, 2)
        w0, w1, w2 = w_ref[0:1, :], w_ref[1:2, :], w_ref[2:3, :]
        d = d_ref[...]
        dcb_ref[...] = (d * (w0 * u2 + w1 * u1 + w2 * u)).astype(BF16)
        dy = d * cb_ref[...]
        du = w2 * dy + w1 * _shift_up(dy, 1) + w0 * _shift_up(dy, 2)
        dcc_ref[...] = (du * cx).astype(BF16)
        dcx_ref[...] = (du * cc).astype(BF16)
        dw = [jnp.sum(dy * s, axis=0, keepdims=True) for s in (u2, u1, u)]
        dw_ref[...] = jnp.concatenate(dw + [jnp.zeros((SUBLANES - 3, LANES), F32)], axis=0)

    col = pl.BlockSpec((t, LANES), lambda j: (0, j))
    wspec = pl.BlockSpec((SUBLANES, LANES), lambda j: (0, j))
    return pl.pallas_call(
        body, name=name, grid=(width // LANES,),
        in_specs=[col] + _conv_specs(t, off_b, width) + [wspec],
        out_specs=[col, col, col, wspec],
        out_shape=[_sds((t, width), BF16)] * 3 + [_sds((SUBLANES, width), F32)],
        compiler_params=_params(("parallel",)),
    )(dcp, proj, proj, proj, cw)


def _gate_specs(t, d, off_g, tr, tc, rows_first):
    nb = d // tc
    base = off_g // tc
    if rows_first:
        tile = lambda s: pl.BlockSpec((tr, tc), lambda i, j: (i, base + s * nb + j))
        vec = lambda s: pl.BlockSpec((1, tc), lambda i, j: (0, s * nb + j))
        plain = pl.BlockSpec((tr, tc), lambda i, j: (i, j))
    else:
        tile = lambda s: pl.BlockSpec((tr, tc), lambda j, i: (i, base + s * nb + j))
        vec = lambda s: pl.BlockSpec((1, tc), lambda j, i: (0, s * nb + j))
        plain = pl.BlockSpec((tr, tc), lambda j, i: (i, j))
    return tile, vec, plain


def _gate_fwd(a, c, proj, bg, off_g, name):
    t, d = a.shape
    tr, tc = _tile(t, (384, 256, 128)), _tile(d, (512, 256, 128))
    tile, vec, plain = _gate_specs(t, d, off_g, tr, tc, True)

    def body(a_ref, c_ref, g0_ref, g1_ref, b0_ref, b1_ref, o_ref):
        g0 = jax.nn.sigmoid(g0_ref[...] + b0_ref[...])
        g1 = jax.nn.sigmoid(g1_ref[...] + b1_ref[...])
        o_ref[...] = (g0 * a_ref[...] + g1 * c_ref[...]).astype(BF16)

    return pl.pallas_call(
        body, name=name, grid=(t // tr, d // tc),
        in_specs=[plain, plain, tile(0), tile(1), vec(0), vec(1)], out_specs=plain,
        out_shape=_sds((t, d), BF16), compiler_params=_params(("parallel", "parallel")),
    )(a, c, proj, proj, bg, bg)


def _gate_bwd(dm, a, c, proj, bg, off_g, name):
    t, d = a.shape
    tr, tc = _tile(t, (384, 256, 128)), _tile(d, (512, 256, 128))
    tile, vec, plain = _gate_specs(t, d, off_g, tr, tc, False)

    def body(dm_ref, a_ref, c_ref, g0_ref, g1_ref, b0_ref, b1_ref,
             da_ref, dc_ref, dg0_ref, dg1_ref, db0_ref, db1_ref):
        @pl.when(pl.program_id(1) == 0)
        def _():
            db0_ref[...] = jnp.zeros_like(db0_ref)
            db1_ref[...] = jnp.zeros_like(db1_ref)

        dm = dm_ref[...]
        g0 = jax.nn.sigmoid(g0_ref[...] + b0_ref[...])
        g1 = jax.nn.sigmoid(g1_ref[...] + b1_ref[...])
        da_ref[...] = (dm * g0).astype(BF16)
        dc_ref[...] = (dm * g1).astype(BF16)
        dz0 = dm * a_ref[...] * (g0 * (1.0 - g0))
        dz1 = dm * c_ref[...] * (g1 * (1.0 - g1))
        dg0_ref[...] = dz0.astype(BF16)
        dg1_ref[...] = dz1.astype(BF16)
        db0_ref[...] += jnp.sum(dz0, axis=0, keepdims=True)
        db1_ref[...] += jnp.sum(dz1, axis=0, keepdims=True)

    bvec = pl.BlockSpec((1, tc), lambda j, i: (0, j))
    return pl.pallas_call(
        body, name=name, grid=(d // tc, t // tr),
        in_specs=[plain, plain, plain, tile(0), tile(1), vec(0), vec(1)],
        out_specs=[plain] * 4 + [bvec, bvec],
        out_shape=[_sds((t, d), BF16)] * 4 + [_sds((1, d), F32)] * 2,
        compiler_params=_params(("parallel", "arbitrary")),
    )(dm, a, c, proj, proj, bg, bg)


def _sum_squares(x, name):
    t, d = x.shape
    tr = _tile(t, (384, 256, 128))

    def body(x_ref, o_ref):
        @pl.when(pl.program_id(0) == 0)
        def _():
            o_ref[...] = jnp.zeros_like(o_ref)

        v = x_ref[...]
        o_ref[...] += jnp.sum(jnp.sum(v * v, axis=0, keepdims=True), axis=1, keepdims=True)

    return pl.pallas_call(
        body, name=name, grid=(t // tr,),
        in_specs=[pl.BlockSpec((tr, d), lambda i: (i, 0))],
        out_specs=pl.BlockSpec((1, LANES), lambda i: (0, 0)),
        out_shape=_sds((1, LANES), F32), compiler_params=_params(("arbitrary",)),
    )(x)


def _row_tile(r, c):
    return r if r * c <= 128 * 1024 else _tile(r, (128, 64, 32, 16))


def _sum_parts(parts, name):
    n, r, c = parts.shape
    tr = _row_tile(r, c)

    def body(p_ref, o_ref):
        acc = p_ref[0].astype(F32)
        for i in range(1, n):
            acc = acc + p_ref[i].astype(F32)
        o_ref[...] = acc

    return pl.pallas_call(
        body, name=name, grid=(r // tr,),
        in_specs=[pl.BlockSpec((n, tr, c), lambda i: (0, i, 0))],
        out_specs=pl.BlockSpec((tr, c), lambda i: (i, 0)),
        out_shape=_sds((r, c), F32), compiler_params=_params(("parallel",)),
    )(parts)


def _adamw(chunks, w, m, v, name):
    n, rc, c = chunks[0].shape
    r = rc * len(chunks)
    tr = _row_tile(rc, c)
    per = rc // tr

    def body(*refs):
        p_refs = refs[:len(chunks)]
        w_ref, m_ref, v_ref, g_ref, d_ref, nm_ref, nv_ref = refs[len(chunks):]
        i = pl.program_id(0)

        def update(p_ref):
            g = p_ref[0].astype(F32)
            for s in range(1, n):
                g = g + p_ref[s].astype(F32)
            nm = ADAM_B1 * m_ref[...] + (1.0 - ADAM_B1) * g
            nv = ADAM_B2 * v_ref[...] + (1.0 - ADAM_B2) * (g * g)
            m_hat = nm / (1.0 - ADAM_B1 ** ADAM_STEP)
            v_hat = nv / (1.0 - ADAM_B2 ** ADAM_STEP)
            g_ref[...] = g
            d_ref[...] = -ADAM_LR * (m_hat / (jnp.sqrt(v_hat) + ADAM_EPS) + ADAM_WD * w_ref[...])
            nm_ref[...] = nm
            nv_ref[...] = nv

        if len(chunks) == 1:
            update(p_refs[0])
        else:
            for ci, p_ref in enumerate(p_refs):
                pl.when((i >= ci * per) & (i < (ci + 1) * per))(functools.partial(update, p_ref))

    row = pl.BlockSpec((tr, c), lambda i: (i, 0))
    part_specs = [pl.BlockSpec((n, tr, c), lambda i, ci=ci: (0, jnp.clip(i - ci * per, 0, per - 1), 0))
                  for ci in range(len(chunks))]
    return pl.pallas_call(
        body, name=name, grid=(r // tr,),
        in_specs=part_specs + [row, row, row],
        out_specs=[row] * 4, out_shape=[_sds((r, c), F32)] * 4,
        compiler_params=_params(("parallel",)),
    )(*chunks, w, m, v)


def _adamw_cols(chunks, w, m, v, name):
    n, r, _ = chunks[0].shape
    widths = [ch.shape[2] for ch in chunks]
    tc = functools.reduce(math.gcd, widths, LANES)
    firsts = [sum(widths[:ci]) // tc for ci in range(len(chunks) + 1)]

    def body(*refs):
        p_refs = refs[:len(chunks)]
        w_ref, m_ref, v_ref, g_ref, d_ref, nm_ref, nv_ref = refs[len(chunks):]
        j = pl.program_id(0)

        def update(p_ref):
            g = p_ref[0].astype(F32)
            for s in range(1, n):
                g = g + p_ref[s].astype(F32)
            nm = ADAM_B1 * m_ref[...] + (1.0 - ADAM_B1) * g
            nv = ADAM_B2 * v_ref[...] + (1.0 - ADAM_B2) * (g * g)
            m_hat = nm / (1.0 - ADAM_B1 ** ADAM_STEP)
            v_hat = nv / (1.0 - ADAM_B2 ** ADAM_STEP)
            g_ref[...] = g
            d_ref[...] = -ADAM_LR * (m_hat / (jnp.sqrt(v_hat) + ADAM_EPS) + ADAM_WD * w_ref[...])
            nm_ref[...] = nm
            nv_ref[...] = nv

        for ci, p_ref in enumerate(p_refs):
            pl.when((j >= firsts[ci]) & (j < firsts[ci + 1]))(functools.partial(update, p_ref))

    col = pl.BlockSpec((r, tc), lambda j: (0, j))
    part_specs = [pl.BlockSpec((n, r, tc),
                               lambda j, lo=firsts[ci], hi=firsts[ci + 1]: (0, 0, jnp.clip(j - lo, 0, hi - lo - 1)))
                  for ci in range(len(chunks))]
    return pl.pallas_call(
        body, name=name, grid=(firsts[-1],),
        in_specs=part_specs + [col, col, col],
        out_specs=[col] * 4, out_shape=[_sds((r, firsts[-1] * tc), F32)] * 4,
        compiler_params=_params(("parallel",)),
    )(*chunks, w, m, v)


def _pad_lanes(a, width=LANES):
    return jnp.pad(a, ((0, 0), (0, width - a.shape[1])))


def _rows_of(a):
    flat = a.reshape(-1)
    n = -(-flat.shape[0] // LANES) * LANES
    return jnp.pad(flat, (0, n - flat.shape[0])).reshape(-1, LANES)


def _columns_to_slots(full, n_rows):
    return full.reshape(n_rows, N_DEV, -1).transpose(1, 0, 2)


def _slots_to_columns(slots):
    return slots.transpose(1, 0, 2).reshape(slots.shape[1], -1)


def kernel(x, meta_tokens, norm_mix, w_in, b_fgate, b_gate, q_norm, k_norm, conv_w, w_attn_out, w_conv_out, w_o, norm_mlp, w_up, w_down, loss_target, m_meta_tokens, m_norm_mix, m_w_in, m_b_fgate, m_b_gate, m_q_norm, m_k_norm, m_conv_w, m_w_attn_out, m_w_conv_out, m_w_o, m_norm_mlp, m_w_up, m_w_down, v_meta_tokens, v_norm_mix, v_w_in, v_b_fgate, v_b_gate, v_q_norm, v_k_norm, v_conv_w, v_w_attn_out, v_w_conv_out, v_w_o, v_norm_mlp, v_w_up, v_w_down):
    seq, d = x.shape[1], x.shape[2]
    heads = b_fgate.shape[1]
    aw = heads * HEAD_DIM
    cwid = conv_w.shape[2] * N_DEV
    dff = w_up.shape[2] * N_DEV
    n_valid = N_META + seq
    t = -(-n_valid // LANES) * LANES
    me = _flat(*_my_place())
    off_cb, off_gl = 3 * aw, 3 * aw + 3 * cwid

    conv_shard = jnp.pad(conv_w[0], ((0, SUBLANES - conv_w.shape[1]), (0, 0)))
    w_in_t, m_in_t, v_in_t = (jnp.swapaxes(p, 1, 2)[0] for p in (w_in, m_w_in, v_w_in))
    g_in, g_meta, g_cw = _run_job(_Gather([w_in_t.astype(BF16), meta_tokens, conv_shard]), "gather_first")
    n_in = N_DEV * g_in.shape[1]
    w_all_t = g_in.reshape(n_in, d)
    w_main_t = jnp.concatenate([w_all_t[:3 * aw], w_all_t[3 * aw + heads:]], axis=0)
    w_fg_t = jnp.pad(w_all_t[3 * aw:3 * aw + heads], ((0, LANES - heads), (0, 0)))
    meta_full, cw_full = _slots_to_columns(g_meta), _slots_to_columns(g_cw)

    pad_rows = t - n_valid
    h0 = jnp.concatenate([meta_full, x[0], jnp.zeros((pad_rows, d), F32)], axis=0)
    target = jnp.concatenate([jnp.zeros((N_META, d), F32), loss_target[0], jnp.zeros((pad_rows, d), F32)], axis=0)
    b_f = _pad_lanes(b_fgate)

    xn = _rmsnorm_fwd(h0, norm_mix, "norm_mix_fwd")
    proj, (g_ao, g_co, g_o) = _matmul(
        xn, w_main_t, name="in_proj", trans_b=True, mid_at=0.5, out_dtypes=(BF16,),
        job=_Gather([w_attn_out[0].astype(BF16), w_conv_out[0].astype(BF16), w_o[0].astype(BF16)]))
    w_ao, w_co, w_o_f = _slots_to_columns(g_ao), _slots_to_columns(g_co), g_o.reshape(d, d)
    fg = _matmul(xn, w_fg_t, name="in_proj_fgate", trans_b=True)
    qn, kn, vb = _qk_prep(proj, q_norm, k_norm, aw, "qk_norm_fwd")
    cum = _forget_fwd(fg, b_f, "forget_cumsum")
    cum_heads = cum[:, :heads].T
    cum_row = cum_heads[:, None, :]
    t_attn = _tile(t, _attn_tile())
    (o, o_fine, lse), (g_up, g_down) = _attn_fwd(
        qn, kn, vb, cum_heads.reshape(heads, t // t_attn, 1, t_attn), "attention_fwd", mid_at=0.55,
        job=_Gather([w_up[0].astype(BF16), w_down[0].astype(BF16)]))
    w_up_f, w_down_f = _slots_to_columns(g_up), g_down.reshape(dff, d)
    a = _matmul(o, w_ao, name="attn_out_proj", out_dtypes=(BF16,))
    cpre = _conv_fwd(proj, cw_full, off_cb, "short_conv_fwd")
    c = _matmul(cpre, w_co, name="conv_out_proj", out_dtypes=(BF16,))
    merged = _gate_fwd(a, c, proj, b_gate, off_gl, "gate_merge_fwd")
    h1 = _matmul(merged, w_o_f, name="out_proj", extras=(h0,), epilogue=lambda acc, i, j, r: (r + acc,))
    hn = _rmsnorm_fwd(h1, norm_mlp, "norm_mlp_fwd")
    z, u = _matmul(hn, w_up_f, name="mlp_up", out_dtypes=(F32, BF16),
                   epilogue=lambda acc, i, j: (acc, jnp.square(jnp.maximum(acc, 0.0))))

    tm_down = _tile(t, (1408, 1024, 512, 256, 128))

    def loss_grad(acc, i, j, h1_tile, tgt_tile):
        rows = i * tm_down + lax.broadcasted_iota(jnp.int32, acc.shape, 0)
        valid = (rows >= N_META) & (rows < n_valid)
        dy = jnp.where(valid, ((h1_tile + acc) - tgt_tile) / d, 0.0)
        return dy, dy

    dh2, dh2b = _matmul(u, w_down_f, name="mlp_down_loss", extras=(h1, target), epilogue=loss_grad,
                        out_dtypes=(F32, BF16), tm=tm_down)
    loss_part = _sum_squares(dh2, "loss_sum") * (0.5 * d)

    wide = lambda n_cols: _tile(n_cols, (1024, 512, 256, 128))
    dw_down = _matmul(u, dh2b, name="mlp_down_wgrad", trans_a=True, tn=wide(d), out_dtypes=(BF16,))
    s_down = dw_down.reshape(N_DEV, dff // N_DEV, d)
    half_down = dff // N_DEV // 2
    dz, l_down0 = _matmul(dh2b, w_down_f, name="mlp_down_bwd", trans_b=True, extras=(z,), out_dtypes=(BF16,),
                          epilogue=lambda acc, i, j, zt: (acc * (2.0 * jnp.maximum(zt, 0.0)),),
                          job=_Scatter([(s_down, 0, half_down)]))
    s_up, l_down1 = _matmul(hn, dz, name="mlp_up_wgrad", trans_a=True, slots=True, out_dtypes=(BF16,),
                            tn=wide(dff // N_DEV), job=_Scatter([(s_down, half_down, half_down)]))
    dhn, l_up0 = _matmul(dz, w_up_f, name="mlp_up_bwd", trans_b=True, tn=wide(d),
                         job=_Scatter([(s_up, 0, d // 2)]))
    dh1, dh1b, dg_mlp = _rmsnorm_bwd(h1, dhn, norm_mlp, dh2, "norm_mlp_bwd")
    dmerged = _matmul(dh1b, w_o_f, name="out_proj_bwd", trans_b=True)
    dw_o = _matmul(merged, dh1b, name="out_proj_wgrad", trans_a=True, tn=wide(d), out_dtypes=(BF16,))
    da, dc, dgl0, dgl1, dbg0, dbg1 = _gate_bwd(dmerged, a, c, proj, b_gate, off_gl, "gate_merge_bwd")
    do = _matmul(da, w_ao, name="attn_out_bwd", trans_b=True)
    s_ao = _matmul(o, da, name="attn_out_wgrad", trans_a=True, slots=True, out_dtypes=(BF16,), tk=t)
    dcp = _matmul(dc, w_co, name="conv_out_bwd", trans_b=True)
    s_co = _matmul(cpre, dc, name="conv_out_wgrad", trans_a=True, slots=True, out_dtypes=(BF16,), tk=t)
    dcb, dcc, dcx, dcw = _conv_bwd(dcp, proj, cw_full, off_cb, "short_conv_bwd")
    delta = _attn_stats(do, o_fine, "attention_stats")
    (dqn, dkn, dv, dck), (l_up1, l_o, l_ao, l_co) = _attn_bwd(
        qn, kn, vb, do, lse, delta, cum_row, "attention_bwd",
        job=_Scatter([(s_up, d // 2, d // 2), dw_o.reshape(N_DEV, d // N_DEV, d), s_ao, s_co]))
    dq_raw, dk_raw, dg_q, dg_k = _qk_bwd(dqn, dkn, proj, q_norm, k_norm, aw, "qk_norm_bwd")
    dcum = _pad_lanes(dck.reshape(heads, t).T)
    dfg, db_f = _forget_bwd(dcum, fg, b_f, "forget_bwd")
    dproj = [dq_raw, dk_raw, dv, dcb, dcc, dcx, dgl0, dgl1]
    t_sec = functools.reduce(math.gcd, [s.shape[1] for s in dproj], 1024)
    t_tok = _tile(t, (704, 512, 384, 256, 128))
    dwt_fg = _matmul(dfg, xn, name="in_proj_fgate_wgrad", trans_a=True, out_dtypes=(BF16,))
    range_ends = [3 * d // 16, d // 2, d]

    def in_slots(dwt, first):
        width = dwt.shape[1]
        parts = ((0, 3 * aw, dwt, 0), (3 * aw, 3 * aw + heads, dwt_fg[:heads, first:first + width], 3 * aw),
                 (3 * aw + heads, n_in, dwt, heads))
        slots = []
        for j in range(N_DEV):
            lo, hi = j * n_in // N_DEV, (j + 1) * n_in // N_DEV
            rows = [src[max(lo, a) - shift:min(hi, b) - shift] for a, b, src, shift in parts
                    if max(lo, a) < min(hi, b)]
            slots.append(rows[0] if len(rows) == 1 else jnp.concatenate(rows, axis=0))
        return jnp.stack(slots)

    def in_wgrad(idx, job):
        lo, hi = ([0] + range_ends)[idx], range_ends[idx]
        return _matmul(dproj, xn[:, lo:hi], name="in_proj_wgrad_%d" % idx, trans_a=True, tm=t_sec, tk=t_tok,
                       tn=hi - lo, out_dtypes=(BF16,), job=job)

    dwt0 = in_wgrad(0, None)
    dwt1, l_in0 = in_wgrad(1, _Scatter([in_slots(dwt0, 0)]))
    dwt2, l_in1 = in_wgrad(2, _Scatter([in_slots(dwt1, range_ends[0])]))
    dxn_fg = _matmul(dfg, w_fg_t, name="in_proj_fgate_bwd")
    dxn, l_in2 = _matmul(dproj, w_main_t, name="in_proj_bwd", extras=(dxn_fg,), tm=t_tok, tk=t_sec,
                         tn=_tile(d, (1024, 512, 256, 128)), epilogue=lambda acc, i, j, r: (r + acc,), job=_Scatter([in_slots(dwt2, range_ends[1])]))
    dh0, _, dg_mix = _rmsnorm_bwd(h0, dxn, norm_mix, dh1, "norm_mix_bwd")

    small = [dg_mix, dbg0, dbg1, dg_mlp, dg_q, dg_k, db_f, loss_part, dcw, dh0[:N_META]]
    small_rows = [_rows_of(s) for s in small]
    pack = jnp.concatenate(small_rows, axis=0)
    pack = jnp.pad(pack, ((0, -pack.shape[0] % SUBLANES), (0, 0)))
    (pack_all,) = _run_job(_Scatter([], [pack]), "gather_small")

    landed = {"w_attn_out": [l_ao], "w_conv_out": [l_co], "w_o": [l_o],
              "w_up": l_up0 + [l_up1], "w_down": l_down0 + l_down1}
    shards = {"w_attn_out": (w_attn_out, m_w_attn_out, v_w_attn_out),
              "w_conv_out": (w_conv_out, m_w_conv_out, v_w_conv_out), "w_o": (w_o, m_w_o, v_w_o),
              "w_up": (w_up, m_w_up, v_w_up), "w_down": (w_down, m_w_down, v_w_down)}
    out = {}
    for nm, chunks in landed.items():
        w_, m_, v_ = shards[nm]
        res = _adamw(list(chunks), w_[0], m_[0], v_[0], "adamw_" + nm)
        out[nm] = [r[None] for r in res]
    res = _adamw_cols(l_in0 + l_in1 + l_in2, w_in_t, m_in_t, v_in_t, "adamw_w_in")
    out["w_in"] = [r.T[None] for r in res]

    total = _sum_parts(pack_all, "sum_small")
    pieces, at = [], 0
    for s, rows in zip(small, small_rows):
        n_el = 1
        for dim in s.shape:
            n_el *= dim
        pieces.append(total[at:at + rows.shape[0]].reshape(-1)[:n_el].reshape(s.shape))
        at += rows.shape[0]
    g_mix, g_bg0, g_bg1, g_mlp, g_q, g_k, g_bf, loss_row, g_cw_full, g_meta_full = pieces
    loss = loss_row[0, 0]
    cshard = conv_w.shape[2]
    g_small = {
        "norm_mix": g_mix, "b_gate": jnp.concatenate([g_bg0, g_bg1], axis=1), "norm_mlp": g_mlp,
        "q_norm": g_q, "k_norm": g_k, "b_fgate": g_bf[:, :heads],
        "conv_w": lax.dynamic_slice_in_dim(g_cw_full[:conv_w.shape[1]], me * cshard, cshard, axis=1)[None],
        "meta_tokens": lax.dynamic_slice_in_dim(g_meta_full, me * (d // N_DEV), d // N_DEV, axis=1),
    }
    small_w = {"norm_mix": (norm_mix, m_norm_mix, v_norm_mix), "b_gate": (b_gate, m_b_gate, v_b_gate),
               "norm_mlp": (norm_mlp, m_norm_mlp, v_norm_mlp), "q_norm": (q_norm, m_q_norm, v_q_norm),
               "k_norm": (k_norm, m_k_norm, v_k_norm), "b_fgate": (b_fgate, m_b_fgate, v_b_fgate),
               "conv_w": (conv_w, m_conv_w, v_conv_w), "meta_tokens": (meta_tokens, m_meta_tokens, v_meta_tokens)}
    order = list(small_w)
    packed = []
    for idx in range(4):
        cols = [g_small[nm] if idx == 0 else small_w[nm][idx - 1] for nm in order]
        rows = jnp.concatenate([_rows_of(c_) for c_ in cols], axis=0)
        packed.append(jnp.pad(rows, ((0, -rows.shape[0] % SUBLANES), (0, 0))))
    res = _adamw([packed[0][None]], packed[1], packed[2], packed[3], "adamw_small")
    at = 0
    for nm in order:
        shape = small_w[nm][0].shape
        n_el = 1
        for dim in shape:
            n_el *= dim
        n_rows = -(-n_el // LANES)
        out[nm] = [r[at:at + n_rows].reshape(-1)[:n_el].reshape(shape) for r in res]
        at += n_rows

    weights = ["meta_tokens", "norm_mix", "w_in", "b_fgate", "b_gate", "q_norm", "k_norm", "conv_w",
               "w_attn_out", "w_conv_out", "w_o", "norm_mlp", "w_up", "w_down"]
    grad_x = dh0[N_META:n_valid][None]
    return (loss, grad_x, *[out[nm][0] for nm in weights], *[out[nm][1] for nm in weights],
            *[out[nm][2] for nm in weights], *[out[nm][3] for nm in weights])
```

```python
import functools
import math

import jax
import jax.numpy as jnp
from jax import lax
from jax.experimental import pallas as pl
from jax.experimental.pallas import tpu as pltpu

F32 = jnp.float32
BF16 = jnp.bfloat16

N_DEV = 8
N_META = 16
HEAD_DIM = 128
LANES = 128
SUBLANES = 8
EPS = 1e-6
VMEM_LIMIT = 56 * 1024 * 1024

ADAM_LR = 0.001
ADAM_B1 = 0.9
ADAM_B2 = 0.999
ADAM_EPS = 1e-08
ADAM_WD = 0.01
ADAM_STEP = 10

MESH = pl.DeviceIdType.MESH
HBM_SPEC = pl.BlockSpec(memory_space=pltpu.HBM)
RELATIONS = tuple((r >> 2 & 1, r >> 1 & 1, r & 1) for r in range(1, N_DEV))


def _params(semantics=None):
    return pltpu.CompilerParams(dimension_semantics=semantics, vmem_limit_bytes=VMEM_LIMIT)


def _tile(n, prefs):
    for p in prefs:
        if n % p == 0:
            return p
    return n


def _sds(shape, dtype):
    return jax.ShapeDtypeStruct(shape, dtype)


def _my_place():
    return lax.axis_index("x"), lax.axis_index("y"), lax.axis_index("c")


def _flat(px, py, pc):
    return 4 * px + 2 * py + pc


class _Gather:
    def __init__(self, arrays):
        self.operands = list(arrays)
        self.n = len(arrays)
        self.out_shape = [_sds((N_DEV,) + a.shape, a.dtype) for a in arrays]
        self.split = [(a.shape[0] // 2 // 16 * 16) or a.shape[0] for a in arrays]

    def _copy(self, srcs, outs, sems, a, k, block, to, from_src=False, rows=None):
        slot = outs[a].at[_flat(*block)]
        if rows is not None:
            slot = slot.at[pl.ds(*rows)]
        return pltpu.make_async_remote_copy(
            src_ref=srcs[a] if from_src else slot, dst_ref=slot,
            send_sem=sems[0].at[a, k], recv_sem=sems[1].at[a, k],
            device_id=to, device_id_type=MESH)

    def _places(self):
        x, y, c = _my_place()
        return {"me": (x, y, c), "sib": (x, y, 1 - c), "x": (1 - x, y, c), "y": (x, 1 - y, c),
                "diag": (1 - x, 1 - y, c)}

    def _parts(self, a):
        n_rows, first = self.operands[a].shape[0], self.split[a]
        return (0, first), ((first, n_rows - first) if first < n_rows else None)

    def start(self, srcs, outs, sems):
        at = self._places()
        for a in range(self.n):
            pltpu.make_async_copy(srcs[a], outs[a].at[_flat(*at["me"])], sems[2].at[a]).start()
            self._copy(srcs, outs, sems, a, 1, at["me"], at["x"], from_src=True).start()
            self._copy(srcs, outs, sems, a, 2, at["me"], at["y"], from_src=True).start()
            self._copy(srcs, outs, sems, a, 0, at["me"], at["sib"], from_src=True).start()

    def mid(self, srcs, outs, sems):
        at = self._places()
        for a in range(self.n):
            first, rest = self._parts(a)
            self._copy(srcs, outs, sems, a, 1, at["x"], at["me"]).wait_recv()
            self._copy(srcs, outs, sems, a, 3, at["x"], at["y"], rows=first).start()
            self._copy(srcs, outs, sems, a, 5, at["x"], at["sib"]).start()
            self._copy(srcs, outs, sems, a, 2, at["y"], at["me"]).wait_recv()
            if rest:
                self._copy(srcs, outs, sems, a, 4, at["y"], at["x"], rows=rest).start()
            self._copy(srcs, outs, sems, a, 6, at["y"], at["sib"]).start()

    def finish(self, srcs, outs, sems):
        at = self._places()
        x, y, c = at["me"]
        for a in range(self.n):
            first, rest = self._parts(a)
            self._copy(srcs, outs, sems, a, 3, at["diag"], at["me"], rows=first).wait_recv()
            if rest:
                self._copy(srcs, outs, sems, a, 4, at["diag"], at["me"], rows=rest).wait_recv()
            self._copy(srcs, outs, sems, a, 7, at["diag"], at["sib"]).start()
        for a in range(self.n):
            first, rest = self._parts(a)
            self._copy(srcs, outs, sems, a, 0, at["sib"], at["me"]).wait_recv()
            for k, chip in ((5, (1 - x, y)), (6, (x, 1 - y)), (7, (1 - x, 1 - y))):
                self._copy(srcs, outs, sems, a, k, (*chip, 1 - c), at["me"]).wait_recv()
            for k in (0, 1, 2, 5, 6, 7):
                self._copy(srcs, outs, sems, a, k, at["me"], at["sib"]).wait_send()
            self._copy(srcs, outs, sems, a, 3, at["me"], at["sib"], rows=first).wait_send()
            if rest:
                self._copy(srcs, outs, sems, a, 4, at["me"], at["sib"], rows=rest).wait_send()
            pltpu.make_async_copy(srcs[a], outs[a].at[_flat(*at["me"])], sems[2].at[a]).wait()


class _Scatter:
    def __init__(self, scatter, gather=()):
        scatter = [s if isinstance(s, tuple) else (s, 0, s.shape[1]) for s in scatter]
        self.ranges = [(lo, cnt) for _, lo, cnt in scatter]
        self.operands = [s[0] for s in scatter] + list(gather)
        self.ns, self.n = len(scatter), len(scatter) + len(gather)
        self.out_shape = ([_sds((N_DEV, cnt, arr.shape[2]), arr.dtype) for arr, _, cnt in scatter]
                          + [_sds((N_DEV,) + a.shape, a.dtype) for a in gather])

    def _peer(self, rel):
        return tuple(1 - p if r else p for p, r in zip(_my_place(), rel))

    def _src(self, srcs, a, place):
        if a >= self.ns:
            return srcs[a]
        lo, cnt = self.ranges[a]
        return srcs[a].at[_flat(*place), pl.ds(lo, cnt)]

    def _send(self, srcs, outs, sems, a, k, rel):
        peer = self._peer(rel)
        return pltpu.make_async_remote_copy(
            src_ref=self._src(srcs, a, peer), dst_ref=outs[a].at[_flat(*_my_place())],
            send_sem=sems[0].at[a, k], recv_sem=sems[1].at[a, k],
            device_id=peer, device_id_type=MESH)

    def _landed(self, outs, sems, a, k, rel):
        peer = self._peer(rel)
        slot = outs[a].at[_flat(*peer)]
        return pltpu.make_async_remote_copy(
            src_ref=slot, dst_ref=slot, send_sem=sems[0].at[a, k], recv_sem=sems[1].at[a, k],
            device_id=peer, device_id_type=MESH)

    def _own(self, srcs, outs, sems, a):
        me = _my_place()
        return pltpu.make_async_copy(self._src(srcs, a, me), outs[a].at[_flat(*me)], sems[2].at[a])

    def start(self, srcs, outs, sems):
        for a in range(self.n):
            self._own(srcs, outs, sems, a).start()
            for k, rel in enumerate(RELATIONS):
                self._send(srcs, outs, sems, a, k, rel).start()

    def mid(self, srcs, outs, sems):
        pass

    def finish(self, srcs, outs, sems):
        for a in range(self.n):
            for k, rel in enumerate(RELATIONS):
                self._landed(outs, sems, a, k, rel).wait_recv()
            for k, rel in enumerate(RELATIONS):
                self._send(srcs, outs, sems, a, k, rel).wait_send()
            self._own(srcs, outs, sems, a).wait()


def _job_sems(job):
    return [pltpu.SemaphoreType.DMA((job.n, 8)), pltpu.SemaphoreType.DMA((job.n, 8)),
            pltpu.SemaphoreType.DMA((job.n,))]


def _run_job(job, name):
    n = job.n

    def body(*refs):
        srcs, outs, sems = refs[:n], refs[n:2 * n], refs[2 * n:]
        job.start(srcs, outs, sems)
        job.mid(srcs, outs, sems)
        job.finish(srcs, outs, sems)

    return pl.pallas_call(
        body, name=name, out_shape=job.out_shape,
        in_specs=[HBM_SPEC] * n, out_specs=[HBM_SPEC] * n, scratch_shapes=_job_sems(job),
    )(*job.operands)


def _call(body, *, name, grid, in_specs, out_specs, out_shape, scratch_shapes, semantics,
          operands, job=None, mid_at=0.5):
    if job is None:
        res = pl.pallas_call(
            body, name=name, grid=grid, in_specs=in_specs, out_specs=out_specs, out_shape=out_shape,
            scratch_shapes=scratch_shapes, compiler_params=_params(semantics))(*operands)
        return res, []
    n_in, n_out, n_scr = len(in_specs), len(out_specs), len(scratch_shapes)
    total = 1
    for g in grid:
        total *= g
    mid_step = min(int(total * mid_at), total - 1)

    def carried(*refs):
        c_in, j_in = refs[:n_in], refs[n_in:n_in + job.n]
        o0 = n_in + job.n
        c_out, j_out = refs[o0:o0 + n_out], refs[o0 + n_out:o0 + n_out + job.n]
        s0 = o0 + n_out + job.n
        c_scr, sems = refs[s0:s0 + n_scr], refs[s0 + n_scr:]
        step = pl.program_id(0)
        for ax in range(1, len(grid)):
            step = step * grid[ax] + pl.program_id(ax)

        @pl.when(step == 0)
        def _():
            job.start(j_in, j_out, sems)

        body(*c_in, *c_out, *c_scr)

        @pl.when(step == mid_step)
        def _():
            job.mid(j_in, j_out, sems)

        @pl.when(step == total - 1)
        def _():
            job.finish(j_in, j_out, sems)

    res = pl.pallas_call(
        carried, name=name, grid=grid,
        in_specs=list(in_specs) + [HBM_SPEC] * job.n,
        out_specs=list(out_specs) + [HBM_SPEC] * job.n,
        out_shape=list(out_shape) + job.out_shape,
        scratch_shapes=list(scratch_shapes) + _job_sems(job),
        compiler_params=_params(("arbitrary",) * len(grid)),
    )(*operands, *job.operands)
    return list(res[:n_out]), list(res[n_out:])


def _matmul(a, b, *, name, trans_b=False, extras=(), epilogue=None, out_dtypes=(F32,),
            tm=None, tn=None, tk=None, rows=None, cols=None, trans_a=False, slots=False, job=None,
            mid_at=0.5):
    k, m = a.shape if trans_a else a.shape[::-1]
    n = b.shape[0] if trans_b else b.shape[1]
    tm = tm or _tile(m, (1408, 1024, 512, 256, 128))
    tn = tn or _tile(n // N_DEV if slots else n, (512, 256, 128))
    tk = tk or _tile(k, (2048, 1408, 1024, 512, 256, 128))
    nk = k // tk
    row0, n_rows = rows or (0, m // tm)
    m = n_rows * tm
    col0, n_cols = cols or (0, n // tn)
    n = n_cols * tn
    n_ex, n_out = len(extras), len(out_dtypes)
    dims = (((0,) if trans_a else (1,), (1,) if trans_b else (0,)), ((), ()))

    def body(*refs):
        a_ref, b_ref = refs[:2]
        ex_refs = refs[2:2 + n_ex]
        out_refs = refs[2 + n_ex:2 + n_ex + n_out]
        part = lax.dot_general(a_ref[...].astype(BF16), b_ref[...].astype(BF16), dims,
                               preferred_element_type=F32)

        def finish(acc):
            if epilogue is None:
                res = (acc,)
            else:
                res = epilogue(acc, pl.program_id(0), pl.program_id(1), *[e[...] for e in ex_refs])
            for o_ref, r in zip(out_refs, res):
                o_ref[...] = r.astype(o_ref.dtype)

        if nk == 1:
            finish(part)
        else:
            acc_ref = refs[-1]
            kk = pl.program_id(2)

            @pl.when(kk == 0)
            def _():
                acc_ref[...] = part

            @pl.when(kk > 0)
            def _():
                acc_ref[...] += part

            @pl.when(kk == nk - 1)
            def _():
                finish(acc_ref[...])

    in_specs = [pl.BlockSpec((tk, tm), lambda i, j, kk: (kk, row0 + i)) if trans_a
                else pl.BlockSpec((tm, tk), lambda i, j, kk: (row0 + i, kk)),
                pl.BlockSpec((tn, tk), lambda i, j, kk: (col0 + j, kk)) if trans_b
                else pl.BlockSpec((tk, tn), lambda i, j, kk: (kk, col0 + j))]
    for e in extras:
        if e.shape[0] == 1:
            in_specs.append(pl.BlockSpec((1, tn), lambda i, j, kk: (0, j)))
        else:
            in_specs.append(pl.BlockSpec((tm, tn), lambda i, j, kk: (i, j)))
    if slots:
        per_slot = n // N_DEV // tn
        out_spec = pl.BlockSpec((None, tm, tn), lambda i, j, kk: (j // per_slot, i, j % per_slot))
        out_shape = [_sds((N_DEV, m, n // N_DEV), d) for d in out_dtypes]
    else:
        out_spec = pl.BlockSpec((tm, tn), lambda i, j, kk: (i, j))
        out_shape = [_sds((m, n), d) for d in out_dtypes]
    res, moved = _call(
        body, name=name, grid=(n_rows, n // tn, nk),
        in_specs=in_specs,
        out_specs=[out_spec] * n_out,
        out_shape=out_shape,
        scratch_shapes=[pltpu.VMEM((tm, tn), F32)] if nk > 1 else [],
        semantics=("parallel", "parallel", "arbitrary"),
        operands=(a, b, *extras), job=job, mid_at=mid_at)
    res = res[0] if n_out == 1 else tuple(res)
    return res if job is None else (res, moved)


def _rstd(x):
    return lax.rsqrt(jnp.mean(x * x, axis=-1, keepdims=True) + EPS)


def _norm_bwd(x, dy, g):
    r = _rstd(x)
    u = dy * g
    dx = r * u - x * (r * r * r) * jnp.mean(u * x, axis=-1, keepdims=True)
    return dx, dy * (x * r)


def _rmsnorm_fwd(h, g, name):
    t, d = h.shape
    tr = _tile(t, (384, 256, 128))

    def body(h_ref, g_ref, o_ref):
        x = h_ref[...]
        o_ref[...] = ((x * _rstd(x)) * g_ref[...]).astype(o_ref.dtype)

    row = pl.BlockSpec((tr, d), lambda i: (i, 0))
    return pl.pallas_call(
        body, name=name, grid=(t // tr,),
        in_specs=[row, pl.BlockSpec((1, d), lambda i: (0, 0))], out_specs=row,
        out_shape=_sds((t, d), BF16), compiler_params=_params(("parallel",)),
    )(h, g)


def _rmsnorm_bwd(h, dy, g, res, name):
    t, d = h.shape
    tr = _tile(t, (384, 256, 128))

    def body(h_ref, dy_ref, g_ref, res_ref, dh_ref, dhb_ref, dg_ref):
        dx, dg_rows = _norm_bwd(h_ref[...], dy_ref[...], g_ref[...])
        dh = res_ref[...] + dx
        dh_ref[...] = dh
        dhb_ref[...] = dh.astype(BF16)

        @pl.when(pl.program_id(0) == 0)
        def _():
            dg_ref[...] = jnp.zeros_like(dg_ref)

        dg_ref[...] += jnp.sum(dg_rows, axis=0, keepdims=True)

    row = pl.BlockSpec((tr, d), lambda i: (i, 0))
    vec = pl.BlockSpec((1, d), lambda i: (0, 0))
    return pl.pallas_call(
        body, name=name, grid=(t // tr,),
        in_specs=[row, row, vec, row], out_specs=[row, row, vec],
        out_shape=[_sds((t, d), F32), _sds((t, d), BF16), _sds((1, d), F32)],
        compiler_params=_params(("arbitrary",)),
    )(h, dy, g, res)


def _qk_prep(proj, gq, gk, aw, name):
    t = proj.shape[0]
    heads = aw // HEAD_DIM
    tr = _tile(t, (384, 256, 128))

    def body(q_ref, k_ref, v_ref, gq_ref, gk_ref, qo_ref, ko_ref, vo_ref):
        for h in range(heads):
            sl = slice(h * HEAD_DIM, (h + 1) * HEAD_DIM)
            xq, xk = q_ref[:, sl].astype(F32), k_ref[:, sl].astype(F32)
            qo_ref[:, sl] = ((xq * _rstd(xq)) * gq_ref[...]).astype(BF16)
            ko_ref[:, sl] = ((xk * _rstd(xk)) * gk_ref[...]).astype(BF16)
        vo_ref[...] = v_ref[...].astype(BF16)

    vec = pl.BlockSpec((1, HEAD_DIM), lambda i: (0, 0))
    out = pl.BlockSpec((tr, aw), lambda i: (i, 0))
    return pl.pallas_call(
        body, name=name, grid=(t // tr,),
        in_specs=[pl.BlockSpec((tr, aw), lambda i: (i, 0)), pl.BlockSpec((tr, aw), lambda i: (i, 1)),
                  pl.BlockSpec((tr, aw), lambda i: (i, 2)), vec, vec],
        out_specs=[out, out, out], out_shape=[_sds((t, aw), BF16)] * 3,
        compiler_params=_params(("parallel",)),
    )(proj, proj, proj, gq, gk)


def _qk_bwd(dqn, dkn, proj, gq, gk, aw, name):
    t = proj.shape[0]
    heads = aw // HEAD_DIM
    tr = _tile(t, (384, 256, 128))

    def body(dq_ref, dk_ref, q_ref, k_ref, gq_ref, gk_ref, dqo_ref, dko_ref, dgq_ref, dgk_ref):
        @pl.when(pl.program_id(0) == 0)
        def _():
            dgq_ref[...] = jnp.zeros_like(dgq_ref)
            dgk_ref[...] = jnp.zeros_like(dgk_ref)

        for h in range(heads):
            sl = slice(h * HEAD_DIM, (h + 1) * HEAD_DIM)
            dx, dg_rows = _norm_bwd(q_ref[:, sl].astype(F32), dq_ref[:, sl], gq_ref[...])
            dqo_ref[:, sl] = dx.astype(BF16)
            dgq_ref[...] += jnp.sum(dg_rows, axis=0, keepdims=True)
            dx, dg_rows = _norm_bwd(k_ref[:, sl].astype(F32), dk_ref[:, sl], gk_ref[...])
            dko_ref[:, sl] = dx.astype(BF16)
            dgk_ref[...] += jnp.sum(dg_rows, axis=0, keepdims=True)

    vec = pl.BlockSpec((1, HEAD_DIM), lambda i: (0, 0))
    row = pl.BlockSpec((tr, aw), lambda i: (i, 0))
    return pl.pallas_call(
        body, name=name, grid=(t // tr,),
        in_specs=[row, row, row, pl.BlockSpec((tr, aw), lambda i: (i, 1)), vec, vec],
        out_specs=[row, row, vec, vec],
        out_shape=[_sds((t, aw), BF16), _sds((t, aw), BF16), _sds((1, HEAD_DIM), F32), _sds((1, HEAD_DIM), F32)],
        compiler_params=_params(("arbitrary",)),
    )(dqn, dkn, proj, proj, gq, gk)


def _triangle(lower):
    r = lax.broadcasted_iota(jnp.int32, (LANES, LANES), 0)
    c = lax.broadcasted_iota(jnp.int32, (LANES, LANES), 1)
    return ((c <= r) if lower else (c >= r)).astype(F32)


def _forget_fwd(fg, b, name):
    t = fg.shape[0]

    def body(fg_ref, b_ref, cum_ref, carry):
        @pl.when(pl.program_id(0) == 0)
        def _():
            carry[...] = jnp.zeros_like(carry)

        z = fg_ref[...] + b_ref[...]
        log_f = jnp.minimum(z, 0.0) - jnp.log1p(jnp.exp(-jnp.abs(z)))
        cs = jnp.dot(_triangle(True), log_f, precision=lax.Precision.HIGHEST,
                     preferred_element_type=F32) + carry[0:1, :]
        cum_ref[...] = cs
        carry[...] = jnp.broadcast_to(cs[LANES - 1:LANES, :], carry.shape)

    row = pl.BlockSpec((LANES, LANES), lambda i: (i, 0))
    return pl.pallas_call(
        body, name=name, grid=(t // LANES,),
        in_specs=[row, pl.BlockSpec((1, LANES), lambda i: (0, 0))], out_specs=row,
        out_shape=_sds((t, LANES), F32), scratch_shapes=[pltpu.VMEM((SUBLANES, LANES), F32)],
        compiler_params=_params(("arbitrary",)),
    )(fg, b)


def _forget_bwd(dcum, fg, b, name):
    t = fg.shape[0]
    nt = t // LANES

    def body(dc_ref, fg_ref, b_ref, dfg_ref, db_ref, carry):
        @pl.when(pl.program_id(0) == 0)
        def _():
            carry[...] = jnp.zeros_like(carry)
            db_ref[...] = jnp.zeros_like(db_ref)

        d_log_f = jnp.dot(_triangle(False), dc_ref[...], precision=lax.Precision.HIGHEST,
                          preferred_element_type=F32) + carry[0:1, :]
        carry[...] = jnp.broadcast_to(d_log_f[0:1, :], carry.shape)
        dz = d_log_f * jax.nn.sigmoid(-(fg_ref[...] + b_ref[...]))
        dfg_ref[...] = dz.astype(BF16)
        db_ref[...] += jnp.sum(dz, axis=0, keepdims=True)

    row = pl.BlockSpec((LANES, LANES), lambda i: (nt - 1 - i, 0))
    vec = pl.BlockSpec((1, LANES), lambda i: (0, 0))
    return pl.pallas_call(
        body, name=name, grid=(nt,),
        in_specs=[row, row, vec], out_specs=[row, vec],
        out_shape=[_sds((t, LANES), BF16), _sds((1, LANES), F32)],
        scratch_shapes=[pltpu.VMEM((SUBLANES, LANES), F32)],
        compiler_params=_params(("arbitrary",)),
    )(dcum, fg, b)


def _causal(qi, kj, tq):
    rows = qi * tq + lax.broadcasted_iota(jnp.int32, (tq, tq), 0)
    cols = kj * tq + lax.broadcasted_iota(jnp.int32, (tq, tq), 1)
    return cols <= rows


NT_DIMS = (((1,), (1,)), ((), ()))
TN_DIMS = (((0,), (0,)), ((), ()))


def _attn_tile():
    return (384, 256, 128)


def _attn_fwd(q, k, v, cum_row, name, job=None, mid_at=0.5):
    t, aw = q.shape
    heads = aw // HEAD_DIM
    tq = _tile(t, _attn_tile())
    nq = t // tq
    rq = tq
    scale = HEAD_DIM ** -0.5

    def body(q_ref, k_ref, v_ref, ck_ref, o_ref, of_ref, lse_ref):
        qi = pl.program_id(1)
        qv = q_ref[...]
        n_full = (qi * rq) // tq

        def tile(kj, carry, masked):
            m_prev, l_prev, acc, res = carry
            ks = pl.ds(pl.multiple_of(kj * tq, tq), tq)
            s = lax.dot_general(qv, k_ref[ks, :], NT_DIMS, preferred_element_type=F32) * scale - ck_ref[kj]
            if masked:
                rows = qi * rq + lax.broadcasted_iota(jnp.int32, (rq, tq), 0)
                cols = kj * tq + lax.broadcasted_iota(jnp.int32, (rq, tq), 1)
                s = jnp.where(cols <= rows, s, -jnp.inf)
            m_new = jnp.maximum(m_prev, jnp.max(s, axis=-1, keepdims=True))
            alpha = jnp.exp(m_prev - m_new)
            p = jnp.exp(s - m_new)
            p_hi = p.astype(BF16)
            p_lo = (p - p_hi.astype(F32)).astype(BF16)
            vv = v_ref[ks, :]
            return (m_new, alpha * l_prev + jnp.sum(p, axis=-1, keepdims=True),
                    alpha * acc + jnp.dot(p_hi, vv, preferred_element_type=F32),
                    alpha * res + jnp.dot(p_lo, vv, preferred_element_type=F32))

        init = (jnp.full((rq, 1), -jnp.inf, F32), jnp.zeros((rq, 1), F32),
                jnp.zeros((rq, HEAD_DIM), F32), jnp.zeros((rq, HEAD_DIM), F32))
        carry = lax.fori_loop(0, n_full, lambda kj, c: tile(kj, c, False), init)
        m_fin, l_fin, acc, res = tile(n_full, carry, True)
        o_ref[...] = (acc / l_fin).astype(o_ref.dtype)
        of_ref[...] = (acc + res) / l_fin
        lse_ref[...] = m_fin + jnp.log(l_fin)

    q_spec = pl.BlockSpec((rq, HEAD_DIM), lambda h, i: (i, h))
    head = pl.BlockSpec((t, HEAD_DIM), lambda h, i: (0, h))
    return _call(
        body, name=name, grid=(heads, t // rq),
        in_specs=[q_spec, head, head, pl.BlockSpec((None, nq, 1, tq), lambda h, i: (h, 0, 0, 0))],
        out_specs=[q_spec, q_spec, pl.BlockSpec((None, rq, 1), lambda h, i: (h, i, 0))],
        out_shape=[_sds((t, aw), BF16), _sds((t, aw), F32), _sds((heads, t, 1), F32)],
        scratch_shapes=[], semantics=("parallel", "arbitrary"),
        operands=(q, k, v, cum_row), job=job, mid_at=mid_at)


def _attn_stats(do, o, name):
    t, aw = o.shape
    heads = aw // HEAD_DIM
    tr = _tile(t, (384, 256, 128))

    def body(do_ref, o_ref, delta_ref):
        for h in range(heads):
            sl = slice(h * HEAD_DIM, (h + 1) * HEAD_DIM)
            do_seen = do_ref[:, sl].astype(BF16).astype(F32)
            delta_ref[h] = jnp.sum(do_seen * o_ref[:, sl], axis=-1, keepdims=True)

    row = pl.BlockSpec((tr, aw), lambda i: (i, 0))
    return pl.pallas_call(
        body, name=name, grid=(t // tr,),
        in_specs=[row, row], out_specs=pl.BlockSpec((heads, tr, 1), lambda i: (0, i, 0)),
        out_shape=_sds((heads, t, 1), F32), compiler_params=_params(("parallel",)),
    )(do, o)


def _attn_bwd(q, k, v, do, lse, delta, cum_row, name, job=None):
    t, aw = q.shape
    heads = aw // HEAD_DIM
    tq = _tile(t, _attn_tile())
    nq = t // tq
    scale = HEAD_DIM ** -0.5

    def body(q_ref, k_ref, v_ref, do_ref, lse_ref, delta_ref, ck_ref, dq_ref, dk_ref, dv_ref, dck_ref):
        kj = pl.program_id(1)

        @pl.when(kj == 0)
        def _():
            dq_ref[...] = jnp.zeros_like(dq_ref)

        kv, vv, ck = k_ref[...], v_ref[...], ck_ref[...]

        def tile(qi, carry, masked):
            dk_acc, dv_acc, dck_acc = carry
            rows = pl.ds(pl.multiple_of(qi * tq, tq), tq)
            qv, dov = q_ref[rows, :], do_ref[rows, :].astype(BF16)
            s = lax.dot_general(qv, kv, NT_DIMS, preferred_element_type=F32) * scale - ck - lse_ref[rows, :]
            p = jnp.exp(s)
            if masked:
                p = jnp.where(_causal(0, 0, tq), p, 0.0)
            dp = lax.dot_general(dov, vv, NT_DIMS, preferred_element_type=F32)
            ds = p * (dp - delta_ref[rows, :])
            dsb = ds.astype(BF16)
            dq_ref[rows, :] += jnp.dot(dsb, kv, preferred_element_type=F32) * scale
            return (dk_acc + lax.dot_general(dsb, qv, TN_DIMS, preferred_element_type=F32),
                    dv_acc + lax.dot_general(p.astype(BF16), dov, TN_DIMS, preferred_element_type=F32),
                    dck_acc + jnp.sum(ds, axis=0, keepdims=True))

        init = (jnp.zeros((tq, HEAD_DIM), F32), jnp.zeros((tq, HEAD_DIM), F32), jnp.zeros((1, tq), F32))
        carry = tile(kj, init, True)
        dk_acc, dv_acc, dck_acc = lax.fori_loop(kj + 1, nq, lambda qi, c: tile(qi, c, False), carry)
        dk_ref[...] = dk_acc * scale
        dv_ref[...] = dv_acc.astype(dv_ref.dtype)
        dck_ref[...] = -dck_acc

    head = pl.BlockSpec((t, HEAD_DIM), lambda h, j: (0, h))
    k_spec = pl.BlockSpec((tq, HEAD_DIM), lambda h, j: (j, h))
    col = pl.BlockSpec((None, t, 1), lambda h, j: (h, 0, 0))
    row = pl.BlockSpec((None, 1, tq), lambda h, j: (h, 0, j))
    return _call(
        body, name=name, grid=(heads, nq),
        in_specs=[head, k_spec, k_spec, head, col, col, row],
        out_specs=[head, k_spec, k_spec, row],
        out_shape=[_sds((t, aw), F32), _sds((t, aw), F32), _sds((t, aw), BF16), _sds((heads, 1, t), F32)],
        scratch_shapes=[], semantics=("parallel", "arbitrary"),
        operands=(q, k, v, do, lse, delta, cum_row), job=job)


def _shift_down(u, by):
    rows = lax.broadcasted_iota(jnp.int32, u.shape, 0)
    return jnp.where(rows >= by, pltpu.roll(u, by, 0), 0.0)


def _shift_up(u, by):
    t = u.shape[0]
    rows = lax.broadcasted_iota(jnp.int32, u.shape, 0)
    return jnp.where(rows < t - by, pltpu.roll(u, t - by, 0), 0.0)


def _conv_specs(t, off_b, cw_width):
    nb = cw_width // LANES
    base = off_b // LANES
    return [pl.BlockSpec((t, LANES), lambda j, s=s: (0, base + s * nb + j)) for s in range(3)]


def _conv_fwd(proj, cw, off_b, name):
    t = proj.shape[0]
    width = cw.shape[1]

    def body(cb_ref, cc_ref, cx_ref, w_ref, o_ref):
        u = cc_ref[...].astype(F32) * cx_ref[...]
        y = w_ref[0:1, :] * _shift_down(u, 2) + w_ref[1:2, :] * _shift_down(u, 1) + w_ref[2:3, :] * u
        o_ref[...] = (cb_ref[...] * y).astype(BF16)

    return pl.pallas_call(
        body, name=name, grid=(width // LANES,),
        in_specs=_conv_specs(t, off_b, width) + [pl.BlockSpec((SUBLANES, LANES), lambda j: (0, j))],
        out_specs=pl.BlockSpec((t, LANES), lambda j: (0, j)),
        out_shape=_sds((t, width), BF16), compiler_params=_params(("parallel",)),
    )(proj, proj, proj, cw)


def _conv_bwd(dcp, proj, cw, off_b, name):
    t = proj.shape[0]
    width = cw.shape[1]

    def body(d_ref, cb_ref, cc_ref, cx_ref, w_ref, dcb_ref, dcc_ref, dcx_ref, dw_ref):
        cc, cx = cc_ref[...].astype(F32), cx_ref[...].astype(F32)
        u = cc * cx
        u1, u2 = _shift_down(u, 1), _shift_down(u, 2)
        w0, w1, w2 = w_ref[0:1, :], w_ref[1:2, :], w_ref[2:3, :]
        d = d_ref[...]
        dcb_ref[...] = (d * (w0 * u2 + w1 * u1 + w2 * u)).astype(BF16)
        dy = d * cb_ref[...]
        du = w2 * dy + w1 * _shift_up(dy, 1) + w0 * _shift_up(dy, 2)
        dcc_ref[...] = (du * cx).astype(BF16)
        dcx_ref[...] = (du * cc).astype(BF16)
        dw = [jnp.sum(dy * s, axis=0, keepdims=True) for s in (u2, u1, u)]
        dw_ref[...] = jnp.concatenate(dw + [jnp.zeros((SUBLANES - 3, LANES), F32)], axis=0)

    col = pl.BlockSpec((t, LANES), lambda j: (0, j))
    wspec = pl.BlockSpec((SUBLANES, LANES), lambda j: (0, j))
    return pl.pallas_call(
        body, name=name, grid=(width // LANES,),
        in_specs=[col] + _conv_specs(t, off_b, width) + [wspec],
        out_specs=[col, col, col, wspec],
        out_shape=[_sds((t, width), BF16)] * 3 + [_sds((SUBLANES, width), F32)],
        compiler_params=_params(("parallel",)),
    )(dcp, proj, proj, proj, cw)


def _gate_specs(t, d, off_g, tr, tc, rows_first):
    nb = d // tc
    base = off_g // tc
    if rows_first:
        tile = lambda s: pl.BlockSpec((tr, tc), lambda i, j: (i, base + s * nb + j))
        vec = lambda s: pl.BlockSpec((1, tc), lambda i, j: (0, s * nb + j))
        plain = pl.BlockSpec((tr, tc), lambda i, j: (i, j))
    else:
        tile = lambda s: pl.BlockSpec((tr, tc), lambda j, i: (i, base + s * nb + j))
        vec = lambda s: pl.BlockSpec((1, tc), lambda j, i: (0, s * nb + j))
        plain = pl.BlockSpec((tr, tc), lambda j, i: (i, j))
    return tile, vec, plain


def _gate_fwd(a, c, proj, bg, off_g, name):
    t, d = a.shape
    tr, tc = _tile(t, (384, 256, 128)), _tile(d, (512, 256, 128))
    tile, vec, plain = _gate_specs(t, d, off_g, tr, tc, True)

    def body(a_ref, c_ref, g0_ref, g1_ref, b0_ref, b1_ref, o_ref):
        g0 = jax.nn.sigmoid(g0_ref[...] + b0_ref[...])
        g1 = jax.nn.sigmoid(g1_ref[...] + b1_ref[...])
        o_ref[...] = (g0 * a_ref[...] + g1 * c_ref[...]).astype(BF16)

    return pl.pallas_call(
        body, name=name, grid=(t // tr, d // tc),
        in_specs=[plain, plain, tile(0), tile(1), vec(0), vec(1)], out_specs=plain,
        out_shape=_sds((t, d), BF16), compiler_params=_params(("parallel", "parallel")),
    )(a, c, proj, proj, bg, bg)


def _gate_bwd(dm, a, c, proj, bg, off_g, name):
    t, d = a.shape
    tr, tc = _tile(t, (384, 256, 128)), _tile(d, (512, 256, 128))
    tile, vec, plain = _gate_specs(t, d, off_g, tr, tc, False)

    def body(dm_ref, a_ref, c_ref, g0_ref, g1_ref, b0_ref, b1_ref,
             da_ref, dc_ref, dg0_ref, dg1_ref, db0_ref, db1_ref):
        @pl.when(pl.program_id(1) == 0)
        def _():
            db0_ref[...] = jnp.zeros_like(db0_ref)
            db1_ref[...] = jnp.zeros_like(db1_ref)

        dm = dm_ref[...]
        g0 = jax.nn.sigmoid(g0_ref[...] + b0_ref[...])
        g1 = jax.nn.sigmoid(g1_ref[...] + b1_ref[...])
        da_ref[...] = (dm * g0).astype(BF16)
        dc_ref[...] = (dm * g1).astype(BF16)
        dz0 = dm * a_ref[...] * (g0 * (1.0 - g0))
        dz1 = dm * c_ref[...] * (g1 * (1.0 - g1))
        dg0_ref[...] = dz0.astype(BF16)
        dg1_ref[...] = dz1.astype(BF16)
        db0_ref[...] += jnp.sum(dz0, axis=0, keepdims=True)
        db1_ref[...] += jnp.sum(dz1, axis=0, keepdims=True)

    bvec = pl.BlockSpec((1, tc), lambda j, i: (0, j))
    return pl.pallas_call(
        body, name=name, grid=(d // tc, t // tr),
        in_specs=[plain, plain, plain, tile(0), tile(1), vec(0), vec(1)],
        out_specs=[plain] * 4 + [bvec, bvec],
        out_shape=[_sds((t, d), BF16)] * 4 + [_sds((1, d), F32)] * 2,
        compiler_params=_params(("parallel", "arbitrary")),
    )(dm, a, c, proj, proj, bg, bg)


def _sum_squares(x, name):
    t, d = x.shape
    tr = _tile(t, (384, 256, 128))

    def body(x_ref, o_ref):
        @pl.when(pl.program_id(0) == 0)
        def _():
            o_ref[...] = jnp.zeros_like(o_ref)

        v = x_ref[...]
        o_ref[...] += jnp.sum(jnp.sum(v * v, axis=0, keepdims=True), axis=1, keepdims=True)

    return pl.pallas_call(
        body, name=name, grid=(t // tr,),
        in_specs=[pl.BlockSpec((tr, d), lambda i: (i, 0))],
        out_specs=pl.BlockSpec((1, LANES), lambda i: (0, 0)),
        out_shape=_sds((1, LANES), F32), compiler_params=_params(("arbitrary",)),
    )(x)


def _row_tile(r, c):
    return r if r * c <= 128 * 1024 else _tile(r, (128, 64, 32, 16))


def _sum_parts(parts, name):
    n, r, c = parts.shape
    tr = _row_tile(r, c)

    def body(p_ref, o_ref):
        acc = p_ref[0].astype(F32)
        for i in range(1, n):
            acc = acc + p_ref[i].astype(F32)
        o_ref[...] = acc

    return pl.pallas_call(
        body, name=name, grid=(r // tr,),
        in_specs=[pl.BlockSpec((n, tr, c), lambda i: (0, i, 0))],
        out_specs=pl.BlockSpec((tr, c), lambda i: (i, 0)),
        out_shape=_sds((r, c), F32), compiler_params=_params(("parallel",)),
    )(parts)


def _adamw(chunks, w, m, v, name):
    n, rc, c = chunks[0].shape
    r = rc * len(chunks)
    tr = _row_tile(rc, c)
    per = rc // tr

    def body(*refs):
        p_refs = refs[:len(chunks)]
        w_ref, m_ref, v_ref, g_ref, d_ref, nm_ref, nv_ref = refs[len(chunks):]
        i = pl.program_id(0)

        def update(p_ref):
            g = p_ref[0].astype(F32)
            for s in range(1, n):
                g = g + p_ref[s].astype(F32)
            nm = ADAM_B1 * m_ref[...] + (1.0 - ADAM_B1) * g
            nv = ADAM_B2 * v_ref[...] + (1.0 - ADAM_B2) * (g * g)
            m_hat = nm / (1.0 - ADAM_B1 ** ADAM_STEP)
            v_hat = nv / (1.0 - ADAM_B2 ** ADAM_STEP)
            g_ref[...] = g
            d_ref[...] = -ADAM_LR * (m_hat / (jnp.sqrt(v_hat) + ADAM_EPS) + ADAM_WD * w_ref[...])
            nm_ref[...] = nm
            nv_ref[...] = nv

        if len(chunks) == 1:
            update(p_refs[0])
        else:
            for ci, p_ref in enumerate(p_refs):
                pl.when((i >= ci * per) & (i < (ci + 1) * per))(functools.partial(update, p_ref))

    row = pl.BlockSpec((tr, c), lambda i: (i, 0))
    part_specs = [pl.BlockSpec((n, tr, c), lambda i, ci=ci: (0, jnp.clip(i - ci * per, 0, per - 1), 0))
                  for ci in range(len(chunks))]
    return pl.pallas_call(
        body, name=name, grid=(r // tr,),
        in_specs=part_specs + [row, row, row],
        out_specs=[row] * 4, out_shape=[_sds((r, c), F32)] * 4,
        compiler_params=_params(("parallel",)),
    )(*chunks, w, m, v)


def _adamw_cols(chunks, w, m, v, name):
    n, r, _ = chunks[0].shape
    widths = [ch.shape[2] for ch in chunks]
    tc = functools.reduce(math.gcd, widths, LANES)
    firsts = [sum(widths[:ci]) // tc for ci in range(len(chunks) + 1)]

    def body(*refs):
        p_refs = refs[:len(chunks)]
        w_ref, m_ref, v_ref, g_ref, d_ref, nm_ref, nv_ref = refs[len(chunks):]
        j = pl.program_id(0)

        def update(p_ref):
            g = p_ref[0].astype(F32)
            for s in range(1, n):
                g = g + p_ref[s].astype(F32)
            nm = ADAM_B1 * m_ref[...] + (1.0 - ADAM_B1) * g
            nv = ADAM_B2 * v_ref[...] + (1.0 - ADAM_B2) * (g * g)
            m_hat = nm / (1.0 - ADAM_B1 ** ADAM_STEP)
            v_hat = nv / (1.0 - ADAM_B2 ** ADAM_STEP)
            g_ref[...] = g
            d_ref[...] = -ADAM_LR * (m_hat / (jnp.sqrt(v_hat) + ADAM_EPS) + ADAM_WD * w_ref[...])
            nm_ref[...] = nm
            nv_ref[...] = nv

        for ci, p_ref in enumerate(p_refs):
            pl.when((j >= firsts[ci]) & (j < firsts[ci + 1]))(functools.partial(update, p_ref))

    col = pl.BlockSpec((r, tc), lambda j: (0, j))
    part_specs = [pl.BlockSpec((n, r, tc),
                               lambda j, lo=firsts[ci], hi=firsts[ci + 1]: (0, 0, jnp.clip(j - lo, 0, hi - lo - 1)))
                  for ci in range(len(chunks))]
    return pl.pallas_call(
        body, name=name, grid=(firsts[-1],),
        in_specs=part_specs + [col, col, col],
        out_specs=[col] * 4, out_shape=[_sds((r, firsts[-1] * tc), F32)] * 4,
        compiler_params=_params(("parallel",)),
    )(*chunks, w, m, v)


def _pad_lanes(a, width=LANES):
    return jnp.pad(a, ((0, 0), (0, width - a.shape[1])))


def _rows_of(a):
    flat = a.reshape(-1)
    n = -(-flat.shape[0] // LANES) * LANES
    return jnp.pad(flat, (0, n - flat.shape[0])).reshape(-1, LANES)


def _columns_to_slots(full, n_rows):
    return full.reshape(n_rows, N_DEV, -1).transpose(1, 0, 2)


def _slots_to_columns(slots):
    return slots.transpose(1, 0, 2).reshape(slots.shape[1], -1)


def kernel(x, meta_tokens, norm_mix, w_in, b_fgate, b_gate, q_norm, k_norm, conv_w, w_attn_out, w_conv_out, w_o, norm_mlp, w_up, w_down, loss_target, m_meta_tokens, m_norm_mix, m_w_in, m_b_fgate, m_b_gate, m_q_norm, m_k_norm, m_conv_w, m_w_attn_out, m_w_conv_out, m_w_o, m_norm_mlp, m_w_up, m_w_down, v_meta_tokens, v_norm_mix, v_w_in, v_b_fgate, v_b_gate, v_q_norm, v_k_norm, v_conv_w, v_w_attn_out, v_w_conv_out, v_w_o, v_norm_mlp, v_w_up, v_w_down):
    seq, d = x.shape[1], x.shape[2]
    heads = b_fgate.shape[1]
    aw = heads * HEAD_DIM
    cwid = conv_w.shape[2] * N_DEV
    dff = w_up.shape[2] * N_DEV
    n_valid = N_META + seq
    t = -(-n_valid // LANES) * LANES
    me = _flat(*_my_place())
    off_cb, off_gl = 3 * aw, 3 * aw + 3 * cwid

    conv_shard = jnp.pad(conv_w[0], ((0, SUBLANES - conv_w.shape[1]), (0, 0)))
    w_in_t, m_in_t, v_in_t = (jnp.swapaxes(p, 1, 2)[0] for p in (w_in, m_w_in, v_w_in))
    g_in, g_meta, g_cw = _run_job(_Gather([w_in_t.astype(BF16), meta_tokens, conv_shard]), "gather_first")
    n_in = N_DEV * g_in.shape[1]
    w_all_t = g_in.reshape(n_in, d)
    w_main_t = jnp.concatenate([w_all_t[:3 * aw], w_all_t[3 * aw + heads:]], axis=0)
    w_fg_t = jnp.pad(w_all_t[3 * aw:3 * aw + heads], ((0, LANES - heads), (0, 0)))
    meta_full, cw_full = _slots_to_columns(g_meta), _slots_to_columns(g_cw)

    pad_rows = t - n_valid
    h0 = jnp.concatenate([meta_full, x[0], jnp.zeros((pad_rows, d), F32)], axis=0)
    target = jnp.concatenate([jnp.zeros((N_META, d), F32), loss_target[0], jnp.zeros((pad_rows, d), F32)], axis=0)
    b_f = _pad_lanes(b_fgate)

    xn = _rmsnorm_fwd(h0, norm_mix, "norm_mix_fwd")
    proj, (g_ao, g_co, g_o) = _matmul(
        xn, w_main_t, name="in_proj", trans_b=True, mid_at=0.5, out_dtypes=(BF16,),
        job=_Gather([w_attn_out[0].astype(BF16), w_conv_out[0].astype(BF16), w_o[0].astype(BF16)]))
    w_ao, w_co, w_o_f = _slots_to_columns(g_ao), _slots_to_columns(g_co), g_o.reshape(d, d)
    fg = _matmul(xn, w_fg_t, name="in_proj_fgate", trans_b=True)
    qn, kn, vb = _qk_prep(proj, q_norm, k_norm, aw, "qk_norm_fwd")
    cum = _forget_fwd(fg, b_f, "forget_cumsum")
    cum_heads = cum[:, :heads].T
    cum_row = cum_heads[:, None, :]
    t_attn = _tile(t, _attn_tile())
    (o, o_fine, lse), (g_up, g_down) = _attn_fwd(
        qn, kn, vb, cum_heads.reshape(heads, t // t_attn, 1, t_attn), "attention_fwd", mid_at=0.7,
        job=_Gather([w_up[0].astype(BF16), w_down[0].astype(BF16)]))
    w_up_f, w_down_f = _slots_to_columns(g_up), g_down.reshape(dff, d)
    a = _matmul(o, w_ao, name="attn_out_proj", out_dtypes=(BF16,))
    cpre = _conv_fwd(proj, cw_full, off_cb, "short_conv_fwd")
    c = _matmul(cpre, w_co, name="conv_out_proj", out_dtypes=(BF16,))
    merged = _gate_fwd(a, c, proj, b_gate, off_gl, "gate_merge_fwd")
    h1 = _matmul(merged, w_o_f, name="out_proj", extras=(h0,), epilogue=lambda acc, i, j, r: (r + acc,))
    hn = _rmsnorm_fwd(h1, norm_mlp, "norm_mlp_fwd")
    z, u = _matmul(hn, w_up_f, name="mlp_up", out_dtypes=(F32, BF16),
                   epilogue=lambda acc, i, j: (acc, jnp.square(jnp.maximum(acc, 0.0))))

    tm_down = _tile(t, (1056, 1024, 512, 256, 128))

    def loss_grad(acc, i, j, h1_tile, tgt_tile):
        rows = i * tm_down + lax.broadcasted_iota(jnp.int32, acc.shape, 0)
        valid = (rows >= N_META) & (rows < n_valid)
        dy = jnp.where(valid, ((h1_tile + acc) - tgt_tile) / d, 0.0)
        return dy, dy

    dh2, dh2b = _matmul(u, w_down_f, name="mlp_down_loss", extras=(h1, target), epilogue=loss_grad,
                        out_dtypes=(F32, BF16), tm=tm_down, tn=_tile(d, (1024, 512, 256, 128)),
                        tk=_tile(dff, (1024, 512, 256, 128)))
    loss_part = _sum_squares(dh2, "loss_sum") * (0.5 * d)

    wide = lambda n_cols: _tile(n_cols, (1024, 512, 256, 128))
    dw_down = _matmul(u, dh2b, name="mlp_down_wgrad", trans_a=True, tn=wide(d), out_dtypes=(BF16,))
    s_down = dw_down.reshape(N_DEV, dff // N_DEV, d)
    half_down = dff // N_DEV // 2
    dz, l_down0 = _matmul(dh2b, w_down_f, name="mlp_down_bwd", trans_b=True, extras=(z,), out_dtypes=(BF16,),
                          epilogue=lambda acc, i, j, zt: (acc * (2.0 * jnp.maximum(zt, 0.0)),),
                          job=_Scatter([(s_down, 0, half_down)]))
    s_up, l_down1 = _matmul(hn, dz, name="mlp_up_wgrad", trans_a=True, slots=True, out_dtypes=(BF16,),
                            tn=wide(dff // N_DEV), job=_Scatter([(s_down, half_down, half_down)]))
    dhn, l_up0 = _matmul(dz, w_up_f, name="mlp_up_bwd", trans_b=True, tn=wide(d),
                         job=_Scatter([(s_up, 0, d // 2)]))
    dh1, dh1b, dg_mlp = _rmsnorm_bwd(h1, dhn, norm_mlp, dh2, "norm_mlp_bwd")
    dmerged = _matmul(dh1b, w_o_f, name="out_proj_bwd", trans_b=True)
    dw_o = _matmul(merged, dh1b, name="out_proj_wgrad", trans_a=True, tn=wide(d), out_dtypes=(BF16,))
    da, dc, dgl0, dgl1, dbg0, dbg1 = _gate_bwd(dmerged, a, c, proj, b_gate, off_gl, "gate_merge_bwd")
    do = _matmul(da, w_ao, name="attn_out_bwd", trans_b=True)
    s_ao = _matmul(o, da, name="attn_out_wgrad", trans_a=True, slots=True, out_dtypes=(BF16,), tk=t)
    dcp = _matmul(dc, w_co, name="conv_out_bwd", trans_b=True)
    s_co = _matmul(cpre, dc, name="conv_out_wgrad", trans_a=True, slots=True, out_dtypes=(BF16,), tk=t)
    dcb, dcc, dcx, dcw = _conv_bwd(dcp, proj, cw_full, off_cb, "short_conv_bwd")
    delta = _attn_stats(do, o_fine, "attention_stats")
    (dqn, dkn, dv, dck), (l_up1, l_o, l_ao, l_co) = _attn_bwd(
        qn, kn, vb, do, lse, delta, cum_row, "attention_bwd",
        job=_Scatter([(s_up, d // 2, d // 2), dw_o.reshape(N_DEV, d // N_DEV, d), s_ao, s_co]))
    dq_raw, dk_raw, dg_q, dg_k = _qk_bwd(dqn, dkn, proj, q_norm, k_norm, aw, "qk_norm_bwd")
    dcum = _pad_lanes(dck.reshape(heads, t).T)
    dfg, db_f = _forget_bwd(dcum, fg, b_f, "forget_bwd")
    dproj = jnp.concatenate([dq_raw, dk_raw, dv, dcb, dcc, dcx, dgl0, dgl1], axis=1)
    dwt_fg = _matmul(dfg, xn, name="in_proj_fgate_wgrad", trans_a=True, out_dtypes=(BF16,))
    range_ends = [3 * d // 16, d // 2, d]

    def in_slots(dwt, first):
        width = dwt.shape[1]
        parts = ((0, 3 * aw, dwt, 0), (3 * aw, 3 * aw + heads, dwt_fg[:heads, first:first + width], 3 * aw),
                 (3 * aw + heads, n_in, dwt, heads))
        slots = []
        for j in range(N_DEV):
            lo, hi = j * n_in // N_DEV, (j + 1) * n_in // N_DEV
            rows = [src[max(lo, a) - shift:min(hi, b) - shift] for a, b, src, shift in parts
                    if max(lo, a) < min(hi, b)]
            slots.append(rows[0] if len(rows) == 1 else jnp.concatenate(rows, axis=0))
        return jnp.stack(slots)

    def in_wgrad(idx, job):
        lo, hi = ([0] + range_ends)[idx], range_ends[idx]
        return _matmul(dproj, xn[:, lo:hi], name="in_proj_wgrad_%d" % idx, trans_a=True, tn=hi - lo,
                       out_dtypes=(BF16,), job=job)

    dwt0 = in_wgrad(0, None)
    dwt1, l_in0 = in_wgrad(1, _Scatter([in_slots(dwt0, 0)]))
    dwt2, l_in1 = in_wgrad(2, _Scatter([in_slots(dwt1, range_ends[0])]))
    dxn_fg = _matmul(dfg, w_fg_t, name="in_proj_fgate_bwd")
    dxn, l_in2 = _matmul(dproj, w_main_t, name="in_proj_bwd", extras=(dxn_fg,),
                         epilogue=lambda acc, i, j, r: (r + acc,), job=_Scatter([in_slots(dwt2, range_ends[1])]))
    dh0, _, dg_mix = _rmsnorm_bwd(h0, dxn, norm_mix, dh1, "norm_mix_bwd")

    small = [dg_mix, dbg0, dbg1, dg_mlp, dg_q, dg_k, db_f, loss_part, dcw, dh0[:N_META]]
    small_rows = [_rows_of(s) for s in small]
    pack = jnp.concatenate(small_rows, axis=0)
    pack = jnp.pad(pack, ((0, -pack.shape[0] % SUBLANES), (0, 0)))
    (pack_all,) = _run_job(_Scatter([], [pack]), "gather_small")

    landed = {"w_attn_out": [l_ao], "w_conv_out": [l_co], "w_o": [l_o],
              "w_up": l_up0 + [l_up1], "w_down": l_down0 + l_down1}
    shards = {"w_attn_out": (w_attn_out, m_w_attn_out, v_w_attn_out),
              "w_conv_out": (w_conv_out, m_w_conv_out, v_w_conv_out), "w_o": (w_o, m_w_o, v_w_o),
              "w_up": (w_up, m_w_up, v_w_up), "w_down": (w_down, m_w_down, v_w_down)}
    out = {}
    for nm, chunks in landed.items():
        w_, m_, v_ = shards[nm]
        res = _adamw(list(chunks), w_[0], m_[0], v_[0], "adamw_" + nm)
        out[nm] = [r[None] for r in res]
    res = _adamw_cols(l_in0 + l_in1 + l_in2, w_in_t, m_in_t, v_in_t, "adamw_w_in")
    out["w_in"] = [r.T[None] for r in res]

    total = _sum_parts(pack_all, "sum_small")
    pieces, at = [], 0
    for s, rows in zip(small, small_rows):
        n_el = 1
        for dim in s.shape:
            n_el *= dim
        pieces.append(total[at:at + rows.shape[0]].reshape(-1)[:n_el].reshape(s.shape))
        at += rows.shape[0]
    g_mix, g_bg0, g_bg1, g_mlp, g_q, g_k, g_bf, loss_row, g_cw_full, g_meta_full = pieces
    loss = loss_row[0, 0]
    cshard = conv_w.shape[2]
    g_small = {
        "norm_mix": g_mix, "b_gate": jnp.concatenate([g_bg0, g_bg1], axis=1), "norm_mlp": g_mlp,
        "q_norm": g_q, "k_norm": g_k, "b_fgate": g_bf[:, :heads],
        "conv_w": lax.dynamic_slice_in_dim(g_cw_full[:conv_w.shape[1]], me * cshard, cshard, axis=1)[None],
        "meta_tokens": lax.dynamic_slice_in_dim(g_meta_full, me * (d // N_DEV), d // N_DEV, axis=1),
    }
    small_w = {"norm_mix": (norm_mix, m_norm_mix, v_norm_mix), "b_gate": (b_gate, m_b_gate, v_b_gate),
               "norm_mlp": (norm_mlp, m_norm_mlp, v_norm_mlp), "q_norm": (q_norm, m_q_norm, v_q_norm),
               "k_norm": (k_norm, m_k_norm, v_k_norm), "b_fgate": (b_fgate, m_b_fgate, v_b_fgate),
               "conv_w": (conv_w, m_conv_w, v_conv_w), "meta_tokens": (meta_tokens, m_meta_tokens, v_meta_tokens)}
    order = list(small_w)
    packed = []
    for idx in range(4):
        cols = [g_small[nm] if idx == 0 else small_w[nm][idx - 1] for nm in order]
        rows = jnp.concatenate([_rows_of(c_) for c_ in cols], axis=0)
        packed.append(jnp.pad(rows, ((0, -rows.shape[0] % SUBLANES), (0, 0))))
    res = _adamw([packed[0][None]], packed[1], packed[2], packed[3], "adamw_small")
    at = 0
    for nm in order:
        shape = small_w[nm][0].shape
        n_el = 1
        for dim in shape:
            n_el *= dim
        n_rows = -(-n_el // LANES)
        out[nm] = [r[at:at + n_rows].reshape(-1)[:n_el].reshape(shape) for r in res]
        at += n_rows

    weights = ["meta_tokens", "norm_mix", "w_in", "b_fgate", "b_gate", "q_norm", "k_norm", "conv_w",
               "w_attn_out", "w_conv_out", "w_o", "norm_mlp", "w_up", "w_down"]
    grad_x = dh0[N_META:n_valid][None]
    return (loss, grad_x, *[out[nm][0] for nm in weights], *[out[nm][1] for nm in weights],
            *[out[nm][2] for nm in weights], *[out[nm][3] for nm in weights])
```

```python
import functools
import math

import jax
import jax.numpy as jnp
from jax import lax
from jax.experimental import pallas as pl
from jax.experimental.pallas import tpu as pltpu

F32 = jnp.float32
BF16 = jnp.bfloat16

N_DEV = 8
N_META = 16
HEAD_DIM = 128
LANES = 128
SUBLANES = 8
EPS = 1e-6
VMEM_LIMIT = 56 * 1024 * 1024

ADAM_LR = 0.001
ADAM_B1 = 0.9
ADAM_B2 = 0.999
ADAM_EPS = 1e-08
ADAM_WD = 0.01
ADAM_STEP = 10

MESH = pl.DeviceIdType.MESH
HBM_SPEC = pl.BlockSpec(memory_space=pltpu.HBM)
RELATIONS = tuple((r >> 2 & 1, r >> 1 & 1, r & 1) for r in range(1, N_DEV))


def _params(semantics=None):
    return pltpu.CompilerParams(dimension_semantics=semantics, vmem_limit_bytes=VMEM_LIMIT)


def _tile(n, prefs):
    for p in prefs:
        if n % p == 0:
            return p
    return n


def _sds(shape, dtype):
    return jax.ShapeDtypeStruct(shape, dtype)


def _my_place():
    return lax.axis_index("x"), lax.axis_index("y"), lax.axis_index("c")


def _flat(px, py, pc):
    return 4 * px + 2 * py + pc


class _Gather:
    def __init__(self, arrays):
        self.operands = list(arrays)
        self.n = len(arrays)
        self.out_shape = [_sds((N_DEV,) + a.shape, a.dtype) for a in arrays]
        self.split = [(a.shape[0] // 2 // 16 * 16) or a.shape[0] for a in arrays]

    def _copy(self, srcs, outs, sems, a, k, block, to, from_src=False, rows=None):
        slot = outs[a].at[_flat(*block)]
        if rows is not None:
            slot = slot.at[pl.ds(*rows)]
        return pltpu.make_async_remote_copy(
            src_ref=srcs[a] if from_src else slot, dst_ref=slot,
            send_sem=sems[0].at[a, k], recv_sem=sems[1].at[a, k],
            device_id=to, device_id_type=MESH)

    def _places(self):
        x, y, c = _my_place()
        return {"me": (x, y, c), "sib": (x, y, 1 - c), "x": (1 - x, y, c), "y": (x, 1 - y, c),
                "diag": (1 - x, 1 - y, c)}

    def _parts(self, a):
        n_rows, first = self.operands[a].shape[0], self.split[a]
        return (0, first), ((first, n_rows - first) if first < n_rows else None)

    def start(self, srcs, outs, sems):
        at = self._places()
        for a in range(self.n):
            pltpu.make_async_copy(srcs[a], outs[a].at[_flat(*at["me"])], sems[2].at[a]).start()
            self._copy(srcs, outs, sems, a, 1, at["me"], at["x"], from_src=True).start()
            self._copy(srcs, outs, sems, a, 2, at["me"], at["y"], from_src=True).start()
            self._copy(srcs, outs, sems, a, 0, at["me"], at["sib"], from_src=True).start()

    def mid(self, srcs, outs, sems):
        at = self._places()
        for a in range(self.n):
            first, rest = self._parts(a)
            self._copy(srcs, outs, sems, a, 1, at["x"], at["me"]).wait_recv()
            self._copy(srcs, outs, sems, a, 3, at["x"], at["y"], rows=first).start()
            self._copy(srcs, outs, sems, a, 5, at["x"], at["sib"]).start()
            self._copy(srcs, outs, sems, a, 2, at["y"], at["me"]).wait_recv()
            if rest:
                self._copy(srcs, outs, sems, a, 4, at["y"], at["x"], rows=rest).start()
            self._copy(srcs, outs, sems, a, 6, at["y"], at["sib"]).start()

    def finish(self, srcs, outs, sems):
        at = self._places()
        x, y, c = at["me"]
        for a in range(self.n):
            first, rest = self._parts(a)
            self._copy(srcs, outs, sems, a, 3, at["diag"], at["me"], rows=first).wait_recv()
            if rest:
                self._copy(srcs, outs, sems, a, 4, at["diag"], at["me"], rows=rest).wait_recv()
            self._copy(srcs, outs, sems, a, 7, at["diag"], at["sib"]).start()
        for a in range(self.n):
            first, rest = self._parts(a)
            self._copy(srcs, outs, sems, a, 0, at["sib"], at["me"]).wait_recv()
            for k, chip in ((5, (1 - x, y)), (6, (x, 1 - y)), (7, (1 - x, 1 - y))):
                self._copy(srcs, outs, sems, a, k, (*chip, 1 - c), at["me"]).wait_recv()
            for k in (0, 1, 2, 5, 6, 7):
                self._copy(srcs, outs, sems, a, k, at["me"], at["sib"]).wait_send()
            self._copy(srcs, outs, sems, a, 3, at["me"], at["sib"], rows=first).wait_send()
            if rest:
                self._copy(srcs, outs, sems, a, 4, at["me"], at["sib"], rows=rest).wait_send()
            pltpu.make_async_copy(srcs[a], outs[a].at[_flat(*at["me"])], sems[2].at[a]).wait()


class _Scatter:
    def __init__(self, scatter, gather=()):
        scatter = [s if isinstance(s, tuple) else (s, 0, s.shape[1]) for s in scatter]
        self.ranges = [(lo, cnt) for _, lo, cnt in scatter]
        self.operands = [s[0] for s in scatter] + list(gather)
        self.ns, self.n = len(scatter), len(scatter) + len(gather)
        self.out_shape = ([_sds((N_DEV, cnt, arr.shape[2]), arr.dtype) for arr, _, cnt in scatter]
                          + [_sds((N_DEV,) + a.shape, a.dtype) for a in gather])

    def _peer(self, rel):
        return tuple(1 - p if r else p for p, r in zip(_my_place(), rel))

    def _src(self, srcs, a, place):
        if a >= self.ns:
            return srcs[a]
        lo, cnt = self.ranges[a]
        return srcs[a].at[_flat(*place), pl.ds(lo, cnt)]

    def _send(self, srcs, outs, sems, a, k, rel):
        peer = self._peer(rel)
        return pltpu.make_async_remote_copy(
            src_ref=self._src(srcs, a, peer), dst_ref=outs[a].at[_flat(*_my_place())],
            send_sem=sems[0].at[a, k], recv_sem=sems[1].at[a, k],
            device_id=peer, device_id_type=MESH)

    def _landed(self, outs, sems, a, k, rel):
        peer = self._peer(rel)
        slot = outs[a].at[_flat(*peer)]
        return pltpu.make_async_remote_copy(
            src_ref=slot, dst_ref=slot, send_sem=sems[0].at[a, k], recv_sem=sems[1].at[a, k],
            device_id=peer, device_id_type=MESH)

    def _own(self, srcs, outs, sems, a):
        me = _my_place()
        return pltpu.make_async_copy(self._src(srcs, a, me), outs[a].at[_flat(*me)], sems[2].at[a])

    def start(self, srcs, outs, sems):
        for a in range(self.n):
            self._own(srcs, outs, sems, a).start()
            for k, rel in enumerate(RELATIONS):
                self._send(srcs, outs, sems, a, k, rel).start()

    def mid(self, srcs, outs, sems):
        pass

    def finish(self, srcs, outs, sems):
        for a in range(self.n):
            for k, rel in enumerate(RELATIONS):
                self._landed(outs, sems, a, k, rel).wait_recv()
            for k, rel in enumerate(RELATIONS):
                self._send(srcs, outs, sems, a, k, rel).wait_send()
            self._own(srcs, outs, sems, a).wait()


def _job_sems(job):
    return [pltpu.SemaphoreType.DMA((job.n, 8)), pltpu.SemaphoreType.DMA((job.n, 8)),
            pltpu.SemaphoreType.DMA((job.n,))]


def _run_job(job, name):
    n = job.n

    def body(*refs):
        srcs, outs, sems = refs[:n], refs[n:2 * n], refs[2 * n:]
        job.start(srcs, outs, sems)
        job.mid(srcs, outs, sems)
        job.finish(srcs, outs, sems)

    return pl.pallas_call(
        body, name=name, out_shape=job.out_shape,
        in_specs=[HBM_SPEC] * n, out_specs=[HBM_SPEC] * n, scratch_shapes=_job_sems(job),
    )(*job.operands)


def _call(body, *, name, grid, in_specs, out_specs, out_shape, scratch_shapes, semantics,
          operands, job=None, mid_at=0.5):
    if job is None:
        res = pl.pallas_call(
            body, name=name, grid=grid, in_specs=in_specs, out_specs=out_specs, out_shape=out_shape,
            scratch_shapes=scratch_shapes, compiler_params=_params(semantics))(*operands)
        return res, []
    n_in, n_out, n_scr = len(in_specs), len(out_specs), len(scratch_shapes)
    total = 1
    for g in grid:
        total *= g
    mid_step = min(int(total * mid_at), total - 1)

    def carried(*refs):
        c_in, j_in = refs[:n_in], refs[n_in:n_in + job.n]
        o0 = n_in + job.n
        c_out, j_out = refs[o0:o0 + n_out], refs[o0 + n_out:o0 + n_out + job.n]
        s0 = o0 + n_out + job.n
        c_scr, sems = refs[s0:s0 + n_scr], refs[s0 + n_scr:]
        step = pl.program_id(0)
        for ax in range(1, len(grid)):
            step = step * grid[ax] + pl.program_id(ax)

        @pl.when(step == 0)
        def _():
            job.start(j_in, j_out, sems)

        body(*c_in, *c_out, *c_scr)

        @pl.when(step == mid_step)
        def _():
            job.mid(j_in, j_out, sems)

        @pl.when(step == total - 1)
        def _():
            job.finish(j_in, j_out, sems)

    res = pl.pallas_call(
        carried, name=name, grid=grid,
        in_specs=list(in_specs) + [HBM_SPEC] * job.n,
        out_specs=list(out_specs) + [HBM_SPEC] * job.n,
        out_shape=list(out_shape) + job.out_shape,
        scratch_shapes=list(scratch_shapes) + _job_sems(job),
        compiler_params=_params(("arbitrary",) * len(grid)),
    )(*operands, *job.operands)
    return list(res[:n_out]), list(res[n_out:])


def _matmul(a, b, *, name, trans_b=False, extras=(), epilogue=None, out_dtypes=(F32,),
            tm=None, tn=None, tk=None, rows=None, cols=None, trans_a=False, slots=False, job=None,
            mid_at=0.5):
    k, m = a.shape if trans_a else a.shape[::-1]
    n = b.shape[0] if trans_b else b.shape[1]
    tm = tm or _tile(m, (1408, 1024, 512, 256, 128))
    tn = tn or _tile(n // N_DEV if slots else n, (512, 256, 128))
    tk = tk or _tile(k, (2048, 1408, 1024, 512, 256, 128))
    nk = k // tk
    row0, n_rows = rows or (0, m // tm)
    m = n_rows * tm
    col0, n_cols = cols or (0, n // tn)
    n = n_cols * tn
    n_ex, n_out = len(extras), len(out_dtypes)
    dims = (((0,) if trans_a else (1,), (1,) if trans_b else (0,)), ((), ()))

    def body(*refs):
        a_ref, b_ref = refs[:2]
        ex_refs = refs[2:2 + n_ex]
        out_refs = refs[2 + n_ex:2 + n_ex + n_out]
        part = lax.dot_general(a_ref[...].astype(BF16), b_ref[...].astype(BF16), dims,
                               preferred_element_type=F32)

        def finish(acc):
            if epilogue is None:
                res = (acc,)
            else:
                res = epilogue(acc, pl.program_id(0), pl.program_id(1), *[e[...] for e in ex_refs])
            for o_ref, r in zip(out_refs, res):
                o_ref[...] = r.astype(o_ref.dtype)

        if nk == 1:
            finish(part)
        else:
            acc_ref = refs[-1]
            kk = pl.program_id(2)

            @pl.when(kk == 0)
            def _():
                acc_ref[...] = part

            @pl.when(kk > 0)
            def _():
                acc_ref[...] += part

            @pl.when(kk == nk - 1)
            def _():
                finish(acc_ref[...])

    in_specs = [pl.BlockSpec((tk, tm), lambda i, j, kk: (kk, row0 + i)) if trans_a
                else pl.BlockSpec((tm, tk), lambda i, j, kk: (row0 + i, kk)),
                pl.BlockSpec((tn, tk), lambda i, j, kk: (col0 + j, kk)) if trans_b
                else pl.BlockSpec((tk, tn), lambda i, j, kk: (kk, col0 + j))]
    for e in extras:
        if e.shape[0] == 1:
            in_specs.append(pl.BlockSpec((1, tn), lambda i, j, kk: (0, j)))
        else:
            in_specs.append(pl.BlockSpec((tm, tn), lambda i, j, kk: (i, j)))
    if slots:
        per_slot = n // N_DEV // tn
        out_spec = pl.BlockSpec((None, tm, tn), lambda i, j, kk: (j // per_slot, i, j % per_slot))
        out_shape = [_sds((N_DEV, m, n // N_DEV), d) for d in out_dtypes]
    else:
        out_spec = pl.BlockSpec((tm, tn), lambda i, j, kk: (i, j))
        out_shape = [_sds((m, n), d) for d in out_dtypes]
    res, moved = _call(
        body, name=name, grid=(n_rows, n // tn, nk),
        in_specs=in_specs,
        out_specs=[out_spec] * n_out,
        out_shape=out_shape,
        scratch_shapes=[pltpu.VMEM((tm, tn), F32)] if nk > 1 else [],
        semantics=("parallel", "parallel", "arbitrary"),
        operands=(a, b, *extras), job=job, mid_at=mid_at)
    res = res[0] if n_out == 1 else tuple(res)
    return res if job is None else (res, moved)


def _rstd(x):
    return lax.rsqrt(jnp.mean(x * x, axis=-1, keepdims=True) + EPS)


def _norm_bwd(x, dy, g):
    r = _rstd(x)
    u = dy * g
    dx = r * u - x * (r * r * r) * jnp.mean(u * x, axis=-1, keepdims=True)
    return dx, dy * (x * r)


def _rmsnorm_fwd(h, g, name):
    t, d = h.shape
    tr = _tile(t, (384, 256, 128))

    def body(h_ref, g_ref, o_ref):
        x = h_ref[...]
        o_ref[...] = ((x * _rstd(x)) * g_ref[...]).astype(o_ref.dtype)

    row = pl.BlockSpec((tr, d), lambda i: (i, 0))
    return pl.pallas_call(
        body, name=name, grid=(t // tr,),
        in_specs=[row, pl.BlockSpec((1, d), lambda i: (0, 0))], out_specs=row,
        out_shape=_sds((t, d), BF16), compiler_params=_params(("parallel",)),
    )(h, g)


def _rmsnorm_bwd(h, dy, g, res, name):
    t, d = h.shape
    tr = _tile(t, (384, 256, 128))

    def body(h_ref, dy_ref, g_ref, res_ref, dh_ref, dhb_ref, dg_ref):
        dx, dg_rows = _norm_bwd(h_ref[...], dy_ref[...], g_ref[...])
        dh = res_ref[...] + dx
        dh_ref[...] = dh
        dhb_ref[...] = dh.astype(BF16)

        @pl.when(pl.program_id(0) == 0)
        def _():
            dg_ref[...] = jnp.zeros_like(dg_ref)

        dg_ref[...] += jnp.sum(dg_rows, axis=0, keepdims=True)

    row = pl.BlockSpec((tr, d), lambda i: (i, 0))
    vec = pl.BlockSpec((1, d), lambda i: (0, 0))
    return pl.pallas_call(
        body, name=name, grid=(t // tr,),
        in_specs=[row, row, vec, row], out_specs=[row, row, vec],
        out_shape=[_sds((t, d), F32), _sds((t, d), BF16), _sds((1, d), F32)],
        compiler_params=_params(("arbitrary",)),
    )(h, dy, g, res)


def _qk_prep(proj, gq, gk, aw, name):
    t = proj.shape[0]
    heads = aw // HEAD_DIM
    tr = _tile(t, (384, 256, 128))

    def body(q_ref, k_ref, v_ref, gq_ref, gk_ref, qo_ref, ko_ref, vo_ref):
        for h in range(heads):
            sl = slice(h * HEAD_DIM, (h + 1) * HEAD_DIM)
            xq, xk = q_ref[:, sl].astype(F32), k_ref[:, sl].astype(F32)
            qo_ref[:, sl] = ((xq * _rstd(xq)) * gq_ref[...]).astype(BF16)
            ko_ref[:, sl] = ((xk * _rstd(xk)) * gk_ref[...]).astype(BF16)
        vo_ref[...] = v_ref[...].astype(BF16)

    vec = pl.BlockSpec((1, HEAD_DIM), lambda i: (0, 0))
    out = pl.BlockSpec((tr, aw), lambda i: (i, 0))
    return pl.pallas_call(
        body, name=name, grid=(t // tr,),
        in_specs=[pl.BlockSpec((tr, aw), lambda i: (i, 0)), pl.BlockSpec((tr, aw), lambda i: (i, 1)),
                  pl.BlockSpec((tr, aw), lambda i: (i, 2)), vec, vec],
        out_specs=[out, out, out], out_shape=[_sds((t, aw), BF16)] * 3,
        compiler_params=_params(("parallel",)),
    )(proj, proj, proj, gq, gk)


def _qk_bwd(dqn, dkn, proj, gq, gk, aw, name):
    t = proj.shape[0]
    heads = aw // HEAD_DIM
    tr = _tile(t, (384, 256, 128))

    def body(dq_ref, dk_ref, q_ref, k_ref, gq_ref, gk_ref, dqo_ref, dko_ref, dgq_ref, dgk_ref):
        @pl.when(pl.program_id(0) == 0)
        def _():
            dgq_ref[...] = jnp.zeros_like(dgq_ref)
            dgk_ref[...] = jnp.zeros_like(dgk_ref)

        for h in range(heads):
            sl = slice(h * HEAD_DIM, (h + 1) * HEAD_DIM)
            dx, dg_rows = _norm_bwd(q_ref[:, sl].astype(F32), dq_ref[:, sl], gq_ref[...])
            dqo_ref[:, sl] = dx.astype(BF16)
            dgq_ref[...] += jnp.sum(dg_rows, axis=0, keepdims=True)
            dx, dg_rows = _norm_bwd(k_ref[:, sl].astype(F32), dk_ref[:, sl], gk_ref[...])
            dko_ref[:, sl] = dx.astype(BF16)
            dgk_ref[...] += jnp.sum(dg_rows, axis=0, keepdims=True)

    vec = pl.BlockSpec((1, HEAD_DIM), lambda i: (0, 0))
    row = pl.BlockSpec((tr, aw), lambda i: (i, 0))
    return pl.pallas_call(
        body, name=name, grid=(t // tr,),
        in_specs=[row, row, row, pl.BlockSpec((tr, aw), lambda i: (i, 1)), vec, vec],
        out_specs=[row, row, vec, vec],
        out_shape=[_sds((t, aw), BF16), _sds((t, aw), BF16), _sds((1, HEAD_DIM), F32), _sds((1, HEAD_DIM), F32)],
        compiler_params=_params(("arbitrary",)),
    )(dqn, dkn, proj, proj, gq, gk)


def _triangle(lower):
    r = lax.broadcasted_iota(jnp.int32, (LANES, LANES), 0)
    c = lax.broadcasted_iota(jnp.int32, (LANES, LANES), 1)
    return ((c <= r) if lower else (c >= r)).astype(F32)


def _forget_fwd(fg, b, name):
    t = fg.shape[0]

    def body(fg_ref, b_ref, cum_ref, carry):
        @pl.when(pl.program_id(0) == 0)
        def _():
            carry[...] = jnp.zeros_like(carry)

        z = fg_ref[...] + b_ref[...]
        log_f = jnp.minimum(z, 0.0) - jnp.log1p(jnp.exp(-jnp.abs(z)))
        cs = jnp.dot(_triangle(True), log_f, precision=lax.Precision.HIGHEST,
                     preferred_element_type=F32) + carry[0:1, :]
        cum_ref[...] = cs
        carry[...] = jnp.broadcast_to(cs[LANES - 1:LANES, :], carry.shape)

    row = pl.BlockSpec((LANES, LANES), lambda i: (i, 0))
    return pl.pallas_call(
        body, name=name, grid=(t // LANES,),
        in_specs=[row, pl.BlockSpec((1, LANES), lambda i: (0, 0))], out_specs=row,
        out_shape=_sds((t, LANES), F32), scratch_shapes=[pltpu.VMEM((SUBLANES, LANES), F32)],
        compiler_params=_params(("arbitrary",)),
    )(fg, b)


def _forget_bwd(dcum, fg, b, name):
    t = fg.shape[0]
    nt = t // LANES

    def body(dc_ref, fg_ref, b_ref, dfg_ref, db_ref, carry):
        @pl.when(pl.program_id(0) == 0)
        def _():
            carry[...] = jnp.zeros_like(carry)
            db_ref[...] = jnp.zeros_like(db_ref)

        d_log_f = jnp.dot(_triangle(False), dc_ref[...], precision=lax.Precision.HIGHEST,
                          preferred_element_type=F32) + carry[0:1, :]
        carry[...] = jnp.broadcast_to(d_log_f[0:1, :], carry.shape)
        dz = d_log_f * jax.nn.sigmoid(-(fg_ref[...] + b_ref[...]))
        dfg_ref[...] = dz.astype(BF16)
        db_ref[...] += jnp.sum(dz, axis=0, keepdims=True)

    row = pl.BlockSpec((LANES, LANES), lambda i: (nt - 1 - i, 0))
    vec = pl.BlockSpec((1, LANES), lambda i: (0, 0))
    return pl.pallas_call(
        body, name=name, grid=(nt,),
        in_specs=[row, row, vec], out_specs=[row, vec],
        out_shape=[_sds((t, LANES), BF16), _sds((1, LANES), F32)],
        scratch_shapes=[pltpu.VMEM((SUBLANES, LANES), F32)],
        compiler_params=_params(("arbitrary",)),
    )(dcum, fg, b)


def _causal(qi, kj, tq):
    rows = qi * tq + lax.broadcasted_iota(jnp.int32, (tq, tq), 0)
    cols = kj * tq + lax.broadcasted_iota(jnp.int32, (tq, tq), 1)
    return cols <= rows


NT_DIMS = (((1,), (1,)), ((), ()))
TN_DIMS = (((0,), (0,)), ((), ()))


def _attn_tile():
    return (384, 256, 128)


def _attn_fwd(q, k, v, cum_row, name, job=None, mid_at=0.5):
    t, aw = q.shape
    heads = aw // HEAD_DIM
    tq = _tile(t, _attn_tile())
    nq = t // tq
    rq = tq
    scale = HEAD_DIM ** -0.5

    def body(q_ref, k_ref, v_ref, ck_ref, o_ref, of_ref, lse_ref):
        qi = pl.program_id(1)
        qv = q_ref[...]
        n_full = (qi * rq) // tq

        def tile(kj, carry, masked):
            m_prev, l_prev, acc = carry
            ks = pl.ds(pl.multiple_of(kj * tq, tq), tq)
            s = lax.dot_general(qv, k_ref[ks, :], NT_DIMS, preferred_element_type=F32) * scale - ck_ref[kj]
            if masked:
                rows = qi * rq + lax.broadcasted_iota(jnp.int32, (rq, tq), 0)
                cols = kj * tq + lax.broadcasted_iota(jnp.int32, (rq, tq), 1)
                s = jnp.where(cols <= rows, s, -jnp.inf)
            m_new = jnp.maximum(m_prev, jnp.max(s, axis=-1, keepdims=True))
            alpha = jnp.exp(m_prev - m_new)
            p = jnp.exp(s - m_new)
            return (m_new, alpha * l_prev + jnp.sum(p, axis=-1, keepdims=True),
                    alpha * acc + jnp.dot(p.astype(BF16), v_ref[ks, :], preferred_element_type=F32))

        init = (jnp.full((rq, 1), -jnp.inf, F32), jnp.zeros((rq, 1), F32), jnp.zeros((rq, HEAD_DIM), F32))
        carry = lax.fori_loop(0, n_full, lambda kj, c: tile(kj, c, False), init)
        m_fin, l_fin, acc = tile(n_full, carry, True)
        out = acc / l_fin
        o_ref[...] = out.astype(o_ref.dtype)
        of_ref[...] = out
        lse_ref[...] = m_fin + jnp.log(l_fin)

    q_spec = pl.BlockSpec((rq, HEAD_DIM), lambda h, i: (i, h))
    head = pl.BlockSpec((t, HEAD_DIM), lambda h, i: (0, h))
    return _call(
        body, name=name, grid=(heads, t // rq),
        in_specs=[q_spec, head, head, pl.BlockSpec((None, nq, 1, tq), lambda h, i: (h, 0, 0, 0))],
        out_specs=[q_spec, q_spec, pl.BlockSpec((None, rq, 1), lambda h, i: (h, i, 0))],
        out_shape=[_sds((t, aw), BF16), _sds((t, aw), F32), _sds((heads, t, 1), F32)],
        scratch_shapes=[], semantics=("parallel", "arbitrary"),
        operands=(q, k, v, cum_row), job=job, mid_at=mid_at)


def _attn_stats(do, o, name):
    t, aw = o.shape
    heads = aw // HEAD_DIM
    tr = _tile(t, (384, 256, 128))

    def body(do_ref, o_ref, delta_ref):
        for h in range(heads):
            sl = slice(h * HEAD_DIM, (h + 1) * HEAD_DIM)
            do_seen = do_ref[:, sl].astype(BF16).astype(F32)
            delta_ref[h] = jnp.sum(do_seen * o_ref[:, sl], axis=-1, keepdims=True)

    row = pl.BlockSpec((tr, aw), lambda i: (i, 0))
    return pl.pallas_call(
        body, name=name, grid=(t // tr,),
        in_specs=[row, row], out_specs=pl.BlockSpec((heads, tr, 1), lambda i: (0, i, 0)),
        out_shape=_sds((heads, t, 1), F32), compiler_params=_params(("parallel",)),
    )(do, o)


def _attn_bwd(q, k, v, do, lse, delta, cum_row, name, job=None):
    t, aw = q.shape
    heads = aw // HEAD_DIM
    tq = _tile(t, _attn_tile())
    nq = t // tq
    scale = HEAD_DIM ** -0.5

    def body(q_ref, k_ref, v_ref, do_ref, lse_ref, delta_ref, ck_ref, dq_ref, dk_ref, dv_ref, dck_ref,
             drow_ref):
        kj = pl.program_id(1)

        @pl.when(kj == 0)
        def _():
            dq_ref[...] = jnp.zeros_like(dq_ref)
            drow_ref[...] = jnp.zeros_like(drow_ref)

        kv, vv, ck = k_ref[...], v_ref[...], ck_ref[...]

        def tile(qi, carry, masked):
            dk_acc, dv_acc, dck_acc = carry
            rows = pl.ds(pl.multiple_of(qi * tq, tq), tq)
            qv, dov = q_ref[rows, :], do_ref[rows, :].astype(BF16)
            s = lax.dot_general(qv, kv, NT_DIMS, preferred_element_type=F32) * scale - ck - lse_ref[rows, :]
            p = jnp.exp(s)
            if masked:
                p = jnp.where(_causal(0, 0, tq), p, 0.0)
            dp = lax.dot_general(dov, vv, NT_DIMS, preferred_element_type=F32)
            ds = p * (dp - delta_ref[rows, :])
            dsb = ds.astype(BF16)
            dq_ref[rows, :] += jnp.dot(dsb, kv, preferred_element_type=F32) * scale
            drow_ref[rows, :] += jnp.sum(ds, axis=1, keepdims=True)
            return (dk_acc + lax.dot_general(dsb, qv, TN_DIMS, preferred_element_type=F32),
                    dv_acc + lax.dot_general(p.astype(BF16), dov, TN_DIMS, preferred_element_type=F32),
                    dck_acc + jnp.sum(ds, axis=0, keepdims=True))

        init = (jnp.zeros((tq, HEAD_DIM), F32), jnp.zeros((tq, HEAD_DIM), F32), jnp.zeros((1, tq), F32))
        carry = tile(kj, init, True)
        dk_acc, dv_acc, dck_acc = lax.fori_loop(kj + 1, nq, lambda qi, c: tile(qi, c, False), carry)
        dk_ref[...] = dk_acc * scale
        dv_ref[...] = dv_acc.astype(dv_ref.dtype)
        dck_ref[...] = -dck_acc

    head = pl.BlockSpec((t, HEAD_DIM), lambda h, j: (0, h))
    k_spec = pl.BlockSpec((tq, HEAD_DIM), lambda h, j: (j, h))
    col = pl.BlockSpec((None, t, 1), lambda h, j: (h, 0, 0))
    row = pl.BlockSpec((None, 1, tq), lambda h, j: (h, 0, j))
    return _call(
        body, name=name, grid=(heads, nq),
        in_specs=[head, k_spec, k_spec, head, col, col, row],
        out_specs=[head, k_spec, k_spec, row, col],
        out_shape=[_sds((t, aw), F32), _sds((t, aw), F32), _sds((t, aw), BF16), _sds((heads, 1, t), F32),
                   _sds((heads, t, 1), F32)],
        scratch_shapes=[], semantics=("parallel", "arbitrary"),
        operands=(q, k, v, do, lse, delta, cum_row), job=job)


def _shift_down(u, by):
    rows = lax.broadcasted_iota(jnp.int32, u.shape, 0)
    return jnp.where(rows >= by, pltpu.roll(u, by, 0), 0.0)


def _shift_up(u, by):
    t = u.shape[0]
    rows = lax.broadcasted_iota(jnp.int32, u.shape, 0)
    return jnp.where(rows < t - by, pltpu.roll(u, t - by, 0), 0.0)


def _conv_specs(t, off_b, cw_width):
    nb = cw_width // LANES
    base = off_b // LANES
    return [pl.BlockSpec((t, LANES), lambda j, s=s: (0, base + s * nb + j)) for s in range(3)]


def _conv_fwd(proj, cw, off_b, name):
    t = proj.shape[0]
    width = cw.shape[1]

    def body(cb_ref, cc_ref, cx_ref, w_ref, o_ref):
        u = cc_ref[...].astype(F32) * cx_ref[...]
        y = w_ref[0:1, :] * _shift_down(u, 2) + w_ref[1:2, :] * _shift_down(u, 1) + w_ref[2:3, :] * u
        o_ref[...] = (cb_ref[...] * y).astype(BF16)

    return pl.pallas_call(
        body, name=name, grid=(width // LANES,),
        in_specs=_conv_specs(t, off_b, width) + [pl.BlockSpec((SUBLANES, LANES), lambda j: (0, j))],
        out_specs=pl.BlockSpec((t, LANES), lambda j: (0, j)),
        out_shape=_sds((t, width), BF16), compiler_params=_params(("parallel",)),
    )(proj, proj, proj, cw)


def _conv_bwd(dcp, proj, cw, off_b, name):
    t = proj.shape[0]
    width = cw.shape[1]

    def body(d_ref, cb_ref, cc_ref, cx_ref, w_ref, dcb_ref, dcc_ref, dcx_ref, dw_ref):
        cc, cx = cc_ref[...].astype(F32), cx_ref[...].astype(F32)
        u = cc * cx
        u1, u2 = _shift_down(u, 1), _shift_down(u, 2)
        w0, w1, w2 = w_ref[0:1, :], w_ref[1:2, :], w_ref[2:3, :]
        d = d_ref[...]
        dcb_ref[...] = (d * (w0 * u2 + w1 * u1 + w2 * u)).astype(BF16)
        dy = d * cb_ref[...]
        du = w2 * dy + w1 * _shift_up(dy, 1) + w0 * _shift_up(dy, 2)
        dcc_ref[...] = (du * cx).astype(BF16)
        dcx_ref[...] = (du * cc).astype(BF16)
        dw = [jnp.sum(dy * s, axis=0, keepdims=True) for s in (u2, u1, u)]
        dw_ref[...] = jnp.concatenate(dw + [jnp.zeros((SUBLANES - 3, LANES), F32)], axis=0)

    col = pl.BlockSpec((t, LANES), lambda j: (0, j))
    wspec = pl.BlockSpec((SUBLANES, LANES), lambda j: (0, j))
    return pl.pallas_call(
        body, name=name, grid=(width // LANES,),
        in_specs=[col] + _conv_specs(t, off_b, width) + [wspec],
        out_specs=[col, col, col, wspec],
        out_shape=[_sds((t, width), BF16)] * 3 + [_sds((SUBLANES, width), F32)],
        compiler_params=_params(("parallel",)),
    )(dcp, proj, proj, proj, cw)


def _gate_specs(t, d, off_g, tr, tc, rows_first):
    nb = d // tc
    base = off_g // tc
    if rows_first:
        tile = lambda s: pl.BlockSpec((tr, tc), lambda i, j: (i, base + s * nb + j))
        vec = lambda s: pl.BlockSpec((1, tc), lambda i, j: (0, s * nb + j))
        plain = pl.BlockSpec((tr, tc), lambda i, j: (i, j))
    else:
        tile = lambda s: pl.BlockSpec((tr, tc), lambda j, i: (i, base + s * nb + j))
        vec = lambda s: pl.BlockSpec((1, tc), lambda j, i: (0, s * nb + j))
        plain = pl.BlockSpec((tr, tc), lambda j, i: (i, j))
    return tile, vec, plain


def _gate_fwd(a, c, proj, bg, off_g, name):
    t, d = a.shape
    tr, tc = _tile(t, (384, 256, 128)), _tile(d, (512, 256, 128))
    tile, vec, plain = _gate_specs(t, d, off_g, tr, tc, True)

    def body(a_ref, c_ref, g0_ref, g1_ref, b0_ref, b1_ref, o_ref):
        g0 = jax.nn.sigmoid(g0_ref[...] + b0_ref[...])
        g1 = jax.nn.sigmoid(g1_ref[...] + b1_ref[...])
        o_ref[...] = (g0 * a_ref[...] + g1 * c_ref[...]).astype(BF16)

    return pl.pallas_call(
        body, name=name, grid=(t // tr, d // tc),
        in_specs=[plain, plain, tile(0), tile(1), vec(0), vec(1)], out_specs=plain,
        out_shape=_sds((t, d), BF16), compiler_params=_params(("parallel", "parallel")),
    )(a, c, proj, proj, bg, bg)


def _gate_bwd(dm, a, c, proj, bg, off_g, name):
    t, d = a.shape
    tr, tc = _tile(t, (384, 256, 128)), _tile(d, (512, 256, 128))
    tile, vec, plain = _gate_specs(t, d, off_g, tr, tc, False)

    def body(dm_ref, a_ref, c_ref, g0_ref, g1_ref, b0_ref, b1_ref,
             da_ref, dc_ref, dg0_ref, dg1_ref, db0_ref, db1_ref):
        @pl.when(pl.program_id(1) == 0)
        def _():
            db0_ref[...] = jnp.zeros_like(db0_ref)
            db1_ref[...] = jnp.zeros_like(db1_ref)

        dm = dm_ref[...]
        g0 = jax.nn.sigmoid(g0_ref[...] + b0_ref[...])
        g1 = jax.nn.sigmoid(g1_ref[...] + b1_ref[...])
        da_ref[...] = (dm * g0).astype(BF16)
        dc_ref[...] = (dm * g1).astype(BF16)
        dz0 = dm * a_ref[...] * (g0 * (1.0 - g0))
        dz1 = dm * c_ref[...] * (g1 * (1.0 - g1))
        dg0_ref[...] = dz0.astype(BF16)
        dg1_ref[...] = dz1.astype(BF16)
        db0_ref[...] += jnp.sum(dz0, axis=0, keepdims=True)
        db1_ref[...] += jnp.sum(dz1, axis=0, keepdims=True)

    bvec = pl.BlockSpec((1, tc), lambda j, i: (0, j))
    return pl.pallas_call(
        body, name=name, grid=(d // tc, t // tr),
        in_specs=[plain, plain, plain, tile(0), tile(1), vec(0), vec(1)],
        out_specs=[plain] * 4 + [bvec, bvec],
        out_shape=[_sds((t, d), BF16)] * 4 + [_sds((1, d), F32)] * 2,
        compiler_params=_params(("parallel", "arbitrary")),
    )(dm, a, c, proj, proj, bg, bg)


def _sum_squares(x, name):
    t, d = x.shape
    tr = _tile(t, (384, 256, 128))

    def body(x_ref, o_ref):
        @pl.when(pl.program_id(0) == 0)
        def _():
            o_ref[...] = jnp.zeros_like(o_ref)

        v = x_ref[...]
        o_ref[...] += jnp.sum(jnp.sum(v * v, axis=0, keepdims=True), axis=1, keepdims=True)

    return pl.pallas_call(
        body, name=name, grid=(t // tr,),
        in_specs=[pl.BlockSpec((tr, d), lambda i: (i, 0))],
        out_specs=pl.BlockSpec((1, LANES), lambda i: (0, 0)),
        out_shape=_sds((1, LANES), F32), compiler_params=_params(("arbitrary",)),
    )(x)


def _row_tile(r, c):
    return r if r * c <= 128 * 1024 else _tile(r, (128, 64, 32, 16))


def _sum_parts(parts, name):
    n, r, c = parts.shape
    tr = _row_tile(r, c)

    def body(p_ref, o_ref):
        acc = p_ref[0].astype(F32)
        for i in range(1, n):
            acc = acc + p_ref[i].astype(F32)
        o_ref[...] = acc

    return pl.pallas_call(
        body, name=name, grid=(r // tr,),
        in_specs=[pl.BlockSpec((n, tr, c), lambda i: (0, i, 0))],
        out_specs=pl.BlockSpec((tr, c), lambda i: (i, 0)),
        out_shape=_sds((r, c), F32), compiler_params=_params(("parallel",)),
    )(parts)


def _adamw(chunks, w, m, v, name):
    n, rc, c = chunks[0].shape
    r = rc * len(chunks)
    tr = _row_tile(rc, c)
    per = rc // tr

    def body(*refs):
        p_refs = refs[:len(chunks)]
        w_ref, m_ref, v_ref, g_ref, d_ref, nm_ref, nv_ref = refs[len(chunks):]
        i = pl.program_id(0)

        def update(p_ref):
            g = p_ref[0].astype(F32)
            for s in range(1, n):
                g = g + p_ref[s].astype(F32)
            nm = ADAM_B1 * m_ref[...] + (1.0 - ADAM_B1) * g
            nv = ADAM_B2 * v_ref[...] + (1.0 - ADAM_B2) * (g * g)
            m_hat = nm / (1.0 - ADAM_B1 ** ADAM_STEP)
            v_hat = nv / (1.0 - ADAM_B2 ** ADAM_STEP)
            g_ref[...] = g
            d_ref[...] = -ADAM_LR * (m_hat / (jnp.sqrt(v_hat) + ADAM_EPS) + ADAM_WD * w_ref[...])
            nm_ref[...] = nm
            nv_ref[...] = nv

        if len(chunks) == 1:
            update(p_refs[0])
        else:
            for ci, p_ref in enumerate(p_refs):
                pl.when((i >= ci * per) & (i < (ci + 1) * per))(functools.partial(update, p_ref))

    row = pl.BlockSpec((tr, c), lambda i: (i, 0))
    part_specs = [pl.BlockSpec((n, tr, c), lambda i, ci=ci: (0, jnp.clip(i - ci * per, 0, per - 1), 0))
                  for ci in range(len(chunks))]
    return pl.pallas_call(
        body, name=name, grid=(r // tr,),
        in_specs=part_specs + [row, row, row],
        out_specs=[row] * 4, out_shape=[_sds((r, c), F32)] * 4,
        compiler_params=_params(("parallel",)),
    )(*chunks, w, m, v)


def _adamw_cols(chunks, w, m, v, name):
    n, r, _ = chunks[0].shape
    widths = [ch.shape[2] for ch in chunks]
    tc = functools.reduce(math.gcd, widths, LANES)
    firsts = [sum(widths[:ci]) // tc for ci in range(len(chunks) + 1)]

    def body(*refs):
        p_refs = refs[:len(chunks)]
        w_ref, m_ref, v_ref, g_ref, d_ref, nm_ref, nv_ref = refs[len(chunks):]
        j = pl.program_id(0)

        def update(p_ref):
            g = p_ref[0].astype(F32)
            for s in range(1, n):
                g = g + p_ref[s].astype(F32)
            nm = ADAM_B1 * m_ref[...] + (1.0 - ADAM_B1) * g
            nv = ADAM_B2 * v_ref[...] + (1.0 - ADAM_B2) * (g * g)
            m_hat = nm / (1.0 - ADAM_B1 ** ADAM_STEP)
            v_hat = nv / (1.0 - ADAM_B2 ** ADAM_STEP)
            g_ref[...] = g
            d_ref[...] = -ADAM_LR * (m_hat / (jnp.sqrt(v_hat) + ADAM_EPS) + ADAM_WD * w_ref[...])
            nm_ref[...] = nm
            nv_ref[...] = nv

        for ci, p_ref in enumerate(p_refs):
            pl.when((j >= firsts[ci]) & (j < firsts[ci + 1]))(functools.partial(update, p_ref))

    col = pl.BlockSpec((r, tc), lambda j: (0, j))
    part_specs = [pl.BlockSpec((n, r, tc),
                               lambda j, lo=firsts[ci], hi=firsts[ci + 1]: (0, 0, jnp.clip(j - lo, 0, hi - lo - 1)))
                  for ci in range(len(chunks))]
    return pl.pallas_call(
        body, name=name, grid=(firsts[-1],),
        in_specs=part_specs + [col, col, col],
        out_specs=[col] * 4, out_shape=[_sds((r, firsts[-1] * tc), F32)] * 4,
        compiler_params=_params(("parallel",)),
    )(*chunks, w, m, v)


def _pad_lanes(a, width=LANES):
    return jnp.pad(a, ((0, 0), (0, width - a.shape[1])))


def _rows_of(a):
    flat = a.reshape(-1)
    n = -(-flat.shape[0] // LANES) * LANES
    return jnp.pad(flat, (0, n - flat.shape[0])).reshape(-1, LANES)


def _columns_to_slots(full, n_rows):
    return full.reshape(n_rows, N_DEV, -1).transpose(1, 0, 2)


def _slots_to_columns(slots):
    return slots.transpose(1, 0, 2).reshape(slots.shape[1], -1)


def kernel(x, meta_tokens, norm_mix, w_in, b_fgate, b_gate, q_norm, k_norm, conv_w, w_attn_out, w_conv_out, w_o, norm_mlp, w_up, w_down, loss_target, m_meta_tokens, m_norm_mix, m_w_in, m_b_fgate, m_b_gate, m_q_norm, m_k_norm, m_conv_w, m_w_attn_out, m_w_conv_out, m_w_o, m_norm_mlp, m_w_up, m_w_down, v_meta_tokens, v_norm_mix, v_w_in, v_b_fgate, v_b_gate, v_q_norm, v_k_norm, v_conv_w, v_w_attn_out, v_w_conv_out, v_w_o, v_norm_mlp, v_w_up, v_w_down):
    seq, d = x.shape[1], x.shape[2]
    heads = b_fgate.shape[1]
    aw = heads * HEAD_DIM
    cwid = conv_w.shape[2] * N_DEV
    dff = w_up.shape[2] * N_DEV
    n_valid = N_META + seq
    t = -(-n_valid // LANES) * LANES
    me = _flat(*_my_place())
    off_cb, off_gl = 3 * aw, 3 * aw + 3 * cwid

    conv_shard = jnp.pad(conv_w[0], ((0, SUBLANES - conv_w.shape[1]), (0, 0)))
    w_in_t, m_in_t, v_in_t = (jnp.swapaxes(p, 1, 2)[0] for p in (w_in, m_w_in, v_w_in))
    g_in, g_meta, g_cw = _run_job(_Gather([w_in_t.astype(BF16), meta_tokens, conv_shard]), "gather_first")
    n_in = N_DEV * g_in.shape[1]
    w_all_t = g_in.reshape(n_in, d)
    w_main_t = jnp.concatenate([w_all_t[:3 * aw], w_all_t[3 * aw + heads:]], axis=0)
    w_fg_t = jnp.pad(w_all_t[3 * aw:3 * aw + heads], ((0, LANES - heads), (0, 0)))
    meta_full, cw_full = _slots_to_columns(g_meta), _slots_to_columns(g_cw)

    pad_rows = t - n_valid
    h0 = jnp.concatenate([meta_full, x[0], jnp.zeros((pad_rows, d), F32)], axis=0)
    target = jnp.concatenate([jnp.zeros((N_META, d), F32), loss_target[0], jnp.zeros((pad_rows, d), F32)], axis=0)
    b_f = _pad_lanes(b_fgate)

    xn = _rmsnorm_fwd(h0, norm_mix, "norm_mix_fwd")
    proj, (g_ao, g_co, g_o) = _matmul(
        xn, w_main_t, name="in_proj", trans_b=True, mid_at=0.5, out_dtypes=(BF16,),
        job=_Gather([w_attn_out[0].astype(BF16), w_conv_out[0].astype(BF16), w_o[0].astype(BF16)]))
    w_ao, w_co, w_o_f = _slots_to_columns(g_ao), _slots_to_columns(g_co), g_o.reshape(d, d)
    fg = _matmul(xn, w_fg_t, name="in_proj_fgate", trans_b=True)
    qn, kn, vb = _qk_prep(proj, q_norm, k_norm, aw, "qk_norm_fwd")
    cum = _forget_fwd(fg, b_f, "forget_cumsum")
    cum_heads = cum[:, :heads].T
    cum_row = cum_heads[:, None, :]
    t_attn = _tile(t, _attn_tile())
    (o, o_fine, lse), (g_up, g_down) = _attn_fwd(
        qn, kn, vb, cum_heads.reshape(heads, t // t_attn, 1, t_attn), "attention_fwd", mid_at=0.55,
        job=_Gather([w_up[0].astype(BF16), w_down[0].astype(BF16)]))
    w_up_f, w_down_f = _slots_to_columns(g_up), g_down.reshape(dff, d)
    a = _matmul(o, w_ao, name="attn_out_proj", out_dtypes=(BF16,))
    cpre = _conv_fwd(proj, cw_full, off_cb, "short_conv_fwd")
    c = _matmul(cpre, w_co, name="conv_out_proj", out_dtypes=(BF16,))
    merged = _gate_fwd(a, c, proj, b_gate, off_gl, "gate_merge_fwd")
    h1 = _matmul(merged, w_o_f, name="out_proj", extras=(h0,), epilogue=lambda acc, i, j, r: (r + acc,))
    hn = _rmsnorm_fwd(h1, norm_mlp, "norm_mlp_fwd")
    z, u = _matmul(hn, w_up_f, name="mlp_up", out_dtypes=(F32, BF16),
                   epilogue=lambda acc, i, j: (acc, jnp.square(jnp.maximum(acc, 0.0))))

    tm_down = _tile(t, (1408, 1024, 512, 256, 128))

    def loss_grad(acc, i, j, h1_tile, tgt_tile):
        rows = i * tm_down + lax.broadcasted_iota(jnp.int32, acc.shape, 0)
        valid = (rows >= N_META) & (rows < n_valid)
        dy = jnp.where(valid, ((h1_tile + acc) - tgt_tile) / d, 0.0)
        return dy, dy

    dh2, dh2b = _matmul(u, w_down_f, name="mlp_down_loss", extras=(h1, target), epilogue=loss_grad,
                        out_dtypes=(F32, BF16), tm=tm_down)
    loss_part = _sum_squares(dh2, "loss_sum") * (0.5 * d)

    wide = lambda n_cols: _tile(n_cols, (1024, 512, 256, 128))
    dw_down = _matmul(u, dh2b, name="mlp_down_wgrad", trans_a=True, tn=wide(d), out_dtypes=(BF16,))
    s_down = dw_down.reshape(N_DEV, dff // N_DEV, d)
    half_down = dff // N_DEV // 2
    dz, l_down0 = _matmul(dh2b, w_down_f, name="mlp_down_bwd", trans_b=True, extras=(z,), out_dtypes=(BF16,),
                          epilogue=lambda acc, i, j, zt: (acc * (2.0 * jnp.maximum(zt, 0.0)),),
                          job=_Scatter([(s_down, 0, half_down)]))
    s_up, l_down1 = _matmul(hn, dz, name="mlp_up_wgrad", trans_a=True, slots=True, out_dtypes=(BF16,),
                            tn=wide(dff // N_DEV), job=_Scatter([(s_down, half_down, half_down)]))
    dhn, l_up0 = _matmul(dz, w_up_f, name="mlp_up_bwd", trans_b=True, tn=wide(d),
                         job=_Scatter([(s_up, 0, d // 2)]))
    dh1, dh1b, dg_mlp = _rmsnorm_bwd(h1, dhn, norm_mlp, dh2, "norm_mlp_bwd")
    dmerged = _matmul(dh1b, w_o_f, name="out_proj_bwd", trans_b=True)
    dw_o = _matmul(merged, dh1b, name="out_proj_wgrad", trans_a=True, tn=wide(d), out_dtypes=(BF16,))
    da, dc, dgl0, dgl1, dbg0, dbg1 = _gate_bwd(dmerged, a, c, proj, b_gate, off_gl, "gate_merge_bwd")
    do = _matmul(da, w_ao, name="attn_out_bwd", trans_b=True)
    s_ao = _matmul(o, da, name="attn_out_wgrad", trans_a=True, slots=True, out_dtypes=(BF16,), tk=t)
    dcp = _matmul(dc, w_co, name="conv_out_bwd", trans_b=True)
    s_co = _matmul(cpre, dc, name="conv_out_wgrad", trans_a=True, slots=True, out_dtypes=(BF16,), tk=t)
    dcb, dcc, dcx, dcw = _conv_bwd(dcp, proj, cw_full, off_cb, "short_conv_bwd")
    delta = _attn_stats(do, o_fine, "attention_stats")
    (dqn, dkn, dv, dck, drow), (l_up1, l_o, l_ao, l_co) = _attn_bwd(
        qn, kn, vb, do, lse, delta, cum_row, "attention_bwd",
        job=_Scatter([(s_up, d // 2, d // 2), dw_o.reshape(N_DEV, d // N_DEV, d), s_ao, s_co]))
    dq_raw, dk_raw, dg_q, dg_k = _qk_bwd(dqn, dkn, proj, q_norm, k_norm, aw, "qk_norm_bwd")
    dcum = _pad_lanes((dck.reshape(heads, t) + drow.reshape(heads, t)).T)
    dfg, db_f = _forget_bwd(dcum, fg, b_f, "forget_bwd")
    dproj = jnp.concatenate([dq_raw, dk_raw, dv, dcb, dcc, dcx, dgl0, dgl1], axis=1)
    dwt_fg = _matmul(dfg, xn, name="in_proj_fgate_wgrad", trans_a=True, out_dtypes=(BF16,))
    range_ends = [3 * d // 16, d // 2, d]

    def in_slots(dwt, first):
        width = dwt.shape[1]
        parts = ((0, 3 * aw, dwt, 0), (3 * aw, 3 * aw + heads, dwt_fg[:heads, first:first + width], 3 * aw),
                 (3 * aw + heads, n_in, dwt, heads))
        slots = []
        for j in range(N_DEV):
            lo, hi = j * n_in // N_DEV, (j + 1) * n_in // N_DEV
            rows = [src[max(lo, a) - shift:min(hi, b) - shift] for a, b, src, shift in parts
                    if max(lo, a) < min(hi, b)]
            slots.append(rows[0] if len(rows) == 1 else jnp.concatenate(rows, axis=0))
        return jnp.stack(slots)

    def in_wgrad(idx, job):
        lo, hi = ([0] + range_ends)[idx], range_ends[idx]
        return _matmul(dproj, xn[:, lo:hi], name="in_proj_wgrad_%d" % idx, trans_a=True, tn=hi - lo,
                       out_dtypes=(BF16,), job=job)

    dwt0 = in_wgrad(0, None)
    dwt1, l_in0 = in_wgrad(1, _Scatter([in_slots(dwt0, 0)]))
    dwt2, l_in1 = in_wgrad(2, _Scatter([in_slots(dwt1, range_ends[0])]))
    dxn_fg = _matmul(dfg, w_fg_t, name="in_proj_fgate_bwd")
    dxn, l_in2 = _matmul(dproj, w_main_t, name="in_proj_bwd", extras=(dxn_fg,),
                         epilogue=lambda acc, i, j, r: (r + acc,), job=_Scatter([in_slots(dwt2, range_ends[1])]))
    dh0, _, dg_mix = _rmsnorm_bwd(h0, dxn, norm_mix, dh1, "norm_mix_bwd")

    small = [dg_mix, dbg0, dbg1, dg_mlp, dg_q, dg_k, db_f, loss_part, dcw, dh0[:N_META]]
    small_rows = [_rows_of(s) for s in small]
    pack = jnp.concatenate(small_rows, axis=0)
    pack = jnp.pad(pack, ((0, -pack.shape[0] % SUBLANES), (0, 0)))
    (pack_all,) = _run_job(_Scatter([], [pack]), "gather_small")

    landed = {"w_attn_out": [l_ao], "w_conv_out": [l_co], "w_o": [l_o],
              "w_up": l_up0 + [l_up1], "w_down": l_down0 + l_down1}
    shards = {"w_attn_out": (w_attn_out, m_w_attn_out, v_w_attn_out),
              "w_conv_out": (w_conv_out, m_w_conv_out, v_w_conv_out), "w_o": (w_o, m_w_o, v_w_o),
              "w_up": (w_up, m_w_up, v_w_up), "w_down": (w_down, m_w_down, v_w_down)}
    out = {}
    for nm, chunks in landed.items():
        w_, m_, v_ = shards[nm]
        res = _adamw(list(chunks), w_[0], m_[0], v_[0], "adamw_" + nm)
        out[nm] = [r[None] for r in res]
    res = _adamw_cols(l_in0 + l_in1 + l_in2, w_in_t, m_in_t, v_in_t, "adamw_w_in")
    out["w_in"] = [r.T[None] for r in res]

    total = _sum_parts(pack_all, "sum_small")
    pieces, at = [], 0
    for s, rows in zip(small, small_rows):
        n_el = 1
        for dim in s.shape:
            n_el *= dim
        pieces.append(total[at:at + rows.shape[0]].reshape(-1)[:n_el].reshape(s.shape))
        at += rows.shape[0]
    g_mix, g_bg0, g_bg1, g_mlp, g_q, g_k, g_bf, loss_row, g_cw_full, g_meta_full = pieces
    loss = loss_row[0, 0]
    cshard = conv_w.shape[2]
    g_small = {
        "norm_mix": g_mix, "b_gate": jnp.concatenate([g_bg0, g_bg1], axis=1), "norm_mlp": g_mlp,
        "q_norm": g_q, "k_norm": g_k, "b_fgate": g_bf[:, :heads],
        "conv_w": lax.dynamic_slice_in_dim(g_cw_full[:conv_w.shape[1]], me * cshard, cshard, axis=1)[None],
        "meta_tokens": lax.dynamic_slice_in_dim(g_meta_full, me * (d // N_DEV), d // N_DEV, axis=1),
    }
    small_w = {"norm_mix": (norm_mix, m_norm_mix, v_norm_mix), "b_gate": (b_gate, m_b_gate, v_b_gate),
               "norm_mlp": (norm_mlp, m_norm_mlp, v_norm_mlp), "q_norm": (q_norm, m_q_norm, v_q_norm),
               "k_norm": (k_norm, m_k_norm, v_k_norm), "b_fgate": (b_fgate, m_b_fgate, v_b_fgate),
               "conv_w": (conv_w, m_conv_w, v_conv_w), "meta_tokens": (meta_tokens, m_meta_tokens, v_meta_tokens)}
    order = list(small_w)
    packed = []
    for idx in range(4):
        cols = [g_small[nm] if idx == 0 else small_w[nm][idx - 1] for nm in order]
        rows = jnp.concatenate([_rows_of(c_) for c_ in cols], axis=0)
        packed.append(jnp.pad(rows, ((0, -rows.shape[0] % SUBLANES), (0, 0))))
    res = _adamw([packed[0][None]], packed[1], packed[2], packed[3], "adamw_small")
    at = 0
    for nm in order:
        shape = small_w[nm][0].shape
        n_el = 1
        for dim in shape:
            n_el *= dim
        n_rows = -(-n_el // LANES)
        out[nm] = [r[at:at + n_rows].reshape(-1)[:n_el].reshape(shape) for r in res]
        at += n_rows

    weights = ["meta_tokens", "norm_mix", "w_in", "b_fgate", "b_gate", "q_norm", "k_norm", "conv_w",
               "w_attn_out", "w_conv_out", "w_o", "norm_mlp", "w_up", "w_down"]
    grad_x = dh0[N_META:n_valid][None]
    return (loss, grad_x, *[out[nm][0] for nm in weights], *[out[nm][1] for nm in weights],
            *[out[nm][2] for nm in weights], *[out[nm][3] for nm in weights])
```

```python
import functools
import math

import jax
import jax.numpy as jnp
from jax import lax
from jax.experimental import pallas as pl
from jax.experimental.pallas import tpu as pltpu

F32 = jnp.float32
BF16 = jnp.bfloat16

N_DEV = 8
N_META = 16
HEAD_DIM = 128
LANES = 128
SUBLANES = 8
EPS = 1e-6
VMEM_LIMIT = 56 * 1024 * 1024

ADAM_LR = 0.001
ADAM_B1 = 0.9
ADAM_B2 = 0.999
ADAM_EPS = 1e-08
ADAM_WD = 0.01
ADAM_STEP = 10

MESH = pl.DeviceIdType.MESH
HBM_SPEC = pl.BlockSpec(memory_space=pltpu.HBM)
RELATIONS = tuple((r >> 2 & 1, r >> 1 & 1, r & 1) for r in range(1, N_DEV))


def _params(semantics=None):
    return pltpu.CompilerParams(dimension_semantics=semantics, vmem_limit_bytes=VMEM_LIMIT)


def _tile(n, prefs):
    for p in prefs:
        if n % p == 0:
            return p
    return n


def _sds(shape, dtype):
    return jax.ShapeDtypeStruct(shape, dtype)


def _my_place():
    return lax.axis_index("x"), lax.axis_index("y"), lax.axis_index("c")


def _flat(px, py, pc):
    return 4 * px + 2 * py + pc


class _Gather:
    def __init__(self, arrays):
        self.operands = list(arrays)
        self.n = len(arrays)
        self.out_shape = [_sds((N_DEV,) + a.shape, a.dtype) for a in arrays]
        self.split = [(a.shape[0] // 2 // 16 * 16) or a.shape[0] for a in arrays]

    def _copy(self, srcs, outs, sems, a, k, block, to, from_src=False, rows=None):
        slot = outs[a].at[_flat(*block)]
        if rows is not None:
            slot = slot.at[pl.ds(*rows)]
        return pltpu.make_async_remote_copy(
            src_ref=srcs[a] if from_src else slot, dst_ref=slot,
            send_sem=sems[0].at[a, k], recv_sem=sems[1].at[a, k],
            device_id=to, device_id_type=MESH)

    def _places(self):
        x, y, c = _my_place()
        return {"me": (x, y, c), "sib": (x, y, 1 - c), "x": (1 - x, y, c), "y": (x, 1 - y, c),
                "diag": (1 - x, 1 - y, c)}

    def _parts(self, a):
        n_rows, first = self.operands[a].shape[0], self.split[a]
        return (0, first), ((first, n_rows - first) if first < n_rows else None)

    def start(self, srcs, outs, sems):
        at = self._places()
        for a in range(self.n):
            pltpu.make_async_copy(srcs[a], outs[a].at[_flat(*at["me"])], sems[2].at[a]).start()
            self._copy(srcs, outs, sems, a, 1, at["me"], at["x"], from_src=True).start()
            self._copy(srcs, outs, sems, a, 2, at["me"], at["y"], from_src=True).start()
            self._copy(srcs, outs, sems, a, 0, at["me"], at["sib"], from_src=True).start()

    def mid(self, srcs, outs, sems):
        at = self._places()
        for a in range(self.n):
            first, rest = self._parts(a)
            self._copy(srcs, outs, sems, a, 1, at["x"], at["me"]).wait_recv()
            self._copy(srcs, outs, sems, a, 3, at["x"], at["y"], rows=first).start()
            self._copy(srcs, outs, sems, a, 5, at["x"], at["sib"]).start()
            self._copy(srcs, outs, sems, a, 2, at["y"], at["me"]).wait_recv()
            if rest:
                self._copy(srcs, outs, sems, a, 4, at["y"], at["x"], rows=rest).start()
            self._copy(srcs, outs, sems, a, 6, at["y"], at["sib"]).start()

    def finish(self, srcs, outs, sems):
        at = self._places()
        x, y, c = at["me"]
        for a in range(self.n):
            first, rest = self._parts(a)
            self._copy(srcs, outs, sems, a, 3, at["diag"], at["me"], rows=first).wait_recv()
            if rest:
                self._copy(srcs, outs, sems, a, 4, at["diag"], at["me"], rows=rest).wait_recv()
            self._copy(srcs, outs, sems, a, 7, at["diag"], at["sib"]).start()
        for a in range(self.n):
            first, rest = self._parts(a)
            self._copy(srcs, outs, sems, a, 0, at["sib"], at["me"]).wait_recv()
            for k, chip in ((5, (1 - x, y)), (6, (x, 1 - y)), (7, (1 - x, 1 - y))):
                self._copy(srcs, outs, sems, a, k, (*chip, 1 - c), at["me"]).wait_recv()
            for k in (0, 1, 2, 5, 6, 7):
                self._copy(srcs, outs, sems, a, k, at["me"], at["sib"]).wait_send()
            self._copy(srcs, outs, sems, a, 3, at["me"], at["sib"], rows=first).wait_send()
            if rest:
                self._copy(srcs, outs, sems, a, 4, at["me"], at["sib"], rows=rest).wait_send()
            pltpu.make_async_copy(srcs[a], outs[a].at[_flat(*at["me"])], sems[2].at[a]).wait()


class _Scatter:
    def __init__(self, scatter, gather=()):
        scatter = [s if isinstance(s, tuple) else (s, 0, s.shape[1]) for s in scatter]
        self.ranges = [(lo, cnt) for _, lo, cnt in scatter]
        self.operands = [s[0] for s in scatter] + list(gather)
        self.ns, self.n = len(scatter), len(scatter) + len(gather)
        self.out_shape = ([_sds((N_DEV, cnt, arr.shape[2]), arr.dtype) for arr, _, cnt in scatter]
                          + [_sds((N_DEV,) + a.shape, a.dtype) for a in gather])

    def _peer(self, rel):
        return tuple(1 - p if r else p for p, r in zip(_my_place(), rel))

    def _src(self, srcs, a, place):
        if a >= self.ns:
            return srcs[a]
        lo, cnt = self.ranges[a]
        return srcs[a].at[_flat(*place), pl.ds(lo, cnt)]

    def _send(self, srcs, outs, sems, a, k, rel):
        peer = self._peer(rel)
        return pltpu.make_async_remote_copy(
            src_ref=self._src(srcs, a, peer), dst_ref=outs[a].at[_flat(*_my_place())],
            send_sem=sems[0].at[a, k], recv_sem=sems[1].at[a, k],
            device_id=peer, device_id_type=MESH)

    def _landed(self, outs, sems, a, k, rel):
        peer = self._peer(rel)
        slot = outs[a].at[_flat(*peer)]
        return pltpu.make_async_remote_copy(
            src_ref=slot, dst_ref=slot, send_sem=sems[0].at[a, k], recv_sem=sems[1].at[a, k],
            device_id=peer, device_id_type=MESH)

    def _own(self, srcs, outs, sems, a):
        me = _my_place()
        return pltpu.make_async_copy(self._src(srcs, a, me), outs[a].at[_flat(*me)], sems[2].at[a])

    def start(self, srcs, outs, sems):
        for a in range(self.n):
            self._own(srcs, outs, sems, a).start()
            for k, rel in enumerate(RELATIONS):
                self._send(srcs, outs, sems, a, k, rel).start()

    def mid(self, srcs, outs, sems):
        pass

    def finish(self, srcs, outs, sems):
        for a in range(self.n):
            for k, rel in enumerate(RELATIONS):
                self._landed(outs, sems, a, k, rel).wait_recv()
            for k, rel in enumerate(RELATIONS):
                self._send(srcs, outs, sems, a, k, rel).wait_send()
            self._own(srcs, outs, sems, a).wait()


def _job_sems(job):
    return [pltpu.SemaphoreType.DMA((job.n, 8)), pltpu.SemaphoreType.DMA((job.n, 8)),
            pltpu.SemaphoreType.DMA((job.n,))]


def _run_job(job, name):
    n = job.n

    def body(*refs):
        srcs, outs, sems = refs[:n], refs[n:2 * n], refs[2 * n:]
        job.start(srcs, outs, sems)
        job.mid(srcs, outs, sems)
        job.finish(srcs, outs, sems)

    return pl.pallas_call(
        body, name=name, out_shape=job.out_shape,
        in_specs=[HBM_SPEC] * n, out_specs=[HBM_SPEC] * n, scratch_shapes=_job_sems(job),
    )(*job.operands)


def _call(body, *, name, grid, in_specs, out_specs, out_shape, scratch_shapes, semantics,
          operands, job=None, mid_at=0.5):
    if job is None:
        res = pl.pallas_call(
            body, name=name, grid=grid, in_specs=in_specs, out_specs=out_specs, out_shape=out_shape,
            scratch_shapes=scratch_shapes, compiler_params=_params(semantics))(*operands)
        return res, []
    n_in, n_out, n_scr = len(in_specs), len(out_specs), len(scratch_shapes)
    total = 1
    for g in grid:
        total *= g
    mid_step = min(int(total * mid_at), total - 1)

    def carried(*refs):
        c_in, j_in = refs[:n_in], refs[n_in:n_in + job.n]
        o0 = n_in + job.n
        c_out, j_out = refs[o0:o0 + n_out], refs[o0 + n_out:o0 + n_out + job.n]
        s0 = o0 + n_out + job.n
        c_scr, sems = refs[s0:s0 + n_scr], refs[s0 + n_scr:]
        step = pl.program_id(0)
        for ax in range(1, len(grid)):
            step = step * grid[ax] + pl.program_id(ax)

        @pl.when(step == 0)
        def _():
            job.start(j_in, j_out, sems)

        body(*c_in, *c_out, *c_scr)

        @pl.when(step == mid_step)
        def _():
            job.mid(j_in, j_out, sems)

        @pl.when(step == total - 1)
        def _():
            job.finish(j_in, j_out, sems)

    res = pl.pallas_call(
        carried, name=name, grid=grid,
        in_specs=list(in_specs) + [HBM_SPEC] * job.n,
        out_specs=list(out_specs) + [HBM_SPEC] * job.n,
        out_shape=list(out_shape) + job.out_shape,
        scratch_shapes=list(scratch_shapes) + _job_sems(job),
        compiler_params=_params(("arbitrary",) * len(grid)),
    )(*operands, *job.operands)
    return list(res[:n_out]), list(res[n_out:])


def _matmul(a, b, *, name, trans_b=False, extras=(), epilogue=None, out_dtypes=(F32,),
            tm=None, tn=None, tk=None, rows=None, cols=None, trans_a=False, slots=False, job=None,
            mid_at=0.5):
    k, m = a.shape if trans_a else a.shape[::-1]
    n = b.shape[0] if trans_b else b.shape[1]
    tm = tm or _tile(m, (1408, 1024, 512, 256, 128))
    tn = tn or _tile(n // N_DEV if slots else n, (512, 256, 128))
    tk = tk or _tile(k, (2048, 1408, 1024, 512, 256, 128))
    nk = k // tk
    row0, n_rows = rows or (0, m // tm)
    m = n_rows * tm
    col0, n_cols = cols or (0, n // tn)
    n = n_cols * tn
    n_ex, n_out = len(extras), len(out_dtypes)
    dims = (((0,) if trans_a else (1,), (1,) if trans_b else (0,)), ((), ()))

    def body(*refs):
        a_ref, b_ref = refs[:2]
        ex_refs = refs[2:2 + n_ex]
        out_refs = refs[2 + n_ex:2 + n_ex + n_out]
        part = lax.dot_general(a_ref[...].astype(BF16), b_ref[...].astype(BF16), dims,
                               preferred_element_type=F32)

        def finish(acc):
            if epilogue is None:
                res = (acc,)
            else:
                res = epilogue(acc, pl.program_id(0), pl.program_id(1), *[e[...] for e in ex_refs])
            for o_ref, r in zip(out_refs, res):
                o_ref[...] = r.astype(o_ref.dtype)

        if nk == 1:
            finish(part)
        else:
            acc_ref = refs[-1]
            kk = pl.program_id(2)

            @pl.when(kk == 0)
            def _():
                acc_ref[...] = part

            @pl.when(kk > 0)
            def _():
                acc_ref[...] += part

            @pl.when(kk == nk - 1)
            def _():
                finish(acc_ref[...])

    in_specs = [pl.BlockSpec((tk, tm), lambda i, j, kk: (kk, row0 + i)) if trans_a
                else pl.BlockSpec((tm, tk), lambda i, j, kk: (row0 + i, kk)),
                pl.BlockSpec((tn, tk), lambda i, j, kk: (col0 + j, kk)) if trans_b
                else pl.BlockSpec((tk, tn), lambda i, j, kk: (kk, col0 + j))]
    for e in extras:
        if e.shape[0] == 1:
            in_specs.append(pl.BlockSpec((1, tn), lambda i, j, kk: (0, j)))
        else:
            in_specs.append(pl.BlockSpec((tm, tn), lambda i, j, kk: (i, j)))
    if slots:
        per_slot = n // N_DEV // tn
        out_spec = pl.BlockSpec((None, tm, tn), lambda i, j, kk: (j // per_slot, i, j % per_slot))
        out_shape = [_sds((N_DEV, m, n // N_DEV), d) for d in out_dtypes]
    else:
        out_spec = pl.BlockSpec((tm, tn), lambda i, j, kk: (i, j))
        out_shape = [_sds((m, n), d) for d in out_dtypes]
    res, moved = _call(
        body, name=name, grid=(n_rows, n // tn, nk),
        in_specs=in_specs,
        out_specs=[out_spec] * n_out,
        out_shape=out_shape,
        scratch_shapes=[pltpu.VMEM((tm, tn), F32)] if nk > 1 else [],
        semantics=("parallel", "parallel", "arbitrary"),
        operands=(a, b, *extras), job=job, mid_at=mid_at)
    res = res[0] if n_out == 1 else tuple(res)
    return res if job is None else (res, moved)


def _rstd(x):
    return lax.rsqrt(jnp.mean(x * x, axis=-1, keepdims=True) + EPS)


def _norm_bwd(x, dy, g):
    r = _rstd(x)
    u = dy * g
    dx = r * u - x * (r * r * r) * jnp.mean(u * x, axis=-1, keepdims=True)
    return dx, dy * (x * r)


def _rmsnorm_fwd(h, g, name):
    t, d = h.shape
    tr = _tile(t, (384, 256, 128))

    def body(h_ref, g_ref, o_ref):
        x = h_ref[...]
        o_ref[...] = ((x * _rstd(x)) * g_ref[...]).astype(o_ref.dtype)

    row = pl.BlockSpec((tr, d), lambda i: (i, 0))
    return pl.pallas_call(
        body, name=name, grid=(t // tr,),
        in_specs=[row, pl.BlockSpec((1, d), lambda i: (0, 0))], out_specs=row,
        out_shape=_sds((t, d), BF16), compiler_params=_params(("parallel",)),
    )(h, g)


def _rmsnorm_bwd(h, dy, g, res, name):
    t, d = h.shape
    tr = _tile(t, (384, 256, 128))

    def body(h_ref, dy_ref, g_ref, res_ref, dh_ref, dhb_ref, dg_ref):
        dx, dg_rows = _norm_bwd(h_ref[...], dy_ref[...], g_ref[...])
        dh = res_ref[...] + dx
        dh_ref[...] = dh
        dhb_ref[...] = dh.astype(BF16)

        @pl.when(pl.program_id(0) == 0)
        def _():
            dg_ref[...] = jnp.zeros_like(dg_ref)

        dg_ref[...] += jnp.sum(dg_rows, axis=0, keepdims=True)

    row = pl.BlockSpec((tr, d), lambda i: (i, 0))
    vec = pl.BlockSpec((1, d), lambda i: (0, 0))
    return pl.pallas_call(
        body, name=name, grid=(t // tr,),
        in_specs=[row, row, vec, row], out_specs=[row, row, vec],
        out_shape=[_sds((t, d), F32), _sds((t, d), BF16), _sds((1, d), F32)],
        compiler_params=_params(("arbitrary",)),
    )(h, dy, g, res)


def _qk_prep(proj, gq, gk, aw, name):
    t = proj.shape[0]
    heads = aw // HEAD_DIM
    tr = _tile(t, (384, 256, 128))

    def body(q_ref, k_ref, v_ref, gq_ref, gk_ref, qo_ref, ko_ref, vo_ref):
        for h in range(heads):
            sl = slice(h * HEAD_DIM, (h + 1) * HEAD_DIM)
            xq, xk = q_ref[:, sl].astype(F32), k_ref[:, sl].astype(F32)
            qo_ref[:, sl] = ((xq * _rstd(xq)) * gq_ref[...]).astype(BF16)
            ko_ref[:, sl] = ((xk * _rstd(xk)) * gk_ref[...]).astype(BF16)
        vo_ref[...] = v_ref[...].astype(BF16)

    vec = pl.BlockSpec((1, HEAD_DIM), lambda i: (0, 0))
    out = pl.BlockSpec((tr, aw), lambda i: (i, 0))
    return pl.pallas_call(
        body, name=name, grid=(t // tr,),
        in_specs=[pl.BlockSpec((tr, aw), lambda i: (i, 0)), pl.BlockSpec((tr, aw), lambda i: (i, 1)),
                  pl.BlockSpec((tr, aw), lambda i: (i, 2)), vec, vec],
        out_specs=[out, out, out], out_shape=[_sds((t, aw), BF16)] * 3,
        compiler_params=_params(("parallel",)),
    )(proj, proj, proj, gq, gk)


def _qk_bwd(dqn, dkn, proj, gq, gk, aw, name):
    t = proj.shape[0]
    heads = aw // HEAD_DIM
    tr = _tile(t, (384, 256, 128))

    def body(dq_ref, dk_ref, q_ref, k_ref, gq_ref, gk_ref, dqo_ref, dko_ref, dgq_ref, dgk_ref):
        @pl.when(pl.program_id(0) == 0)
        def _():
            dgq_ref[...] = jnp.zeros_like(dgq_ref)
            dgk_ref[...] = jnp.zeros_like(dgk_ref)

        for h in range(heads):
            sl = slice(h * HEAD_DIM, (h + 1) * HEAD_DIM)
            dx, dg_rows = _norm_bwd(q_ref[:, sl].astype(F32), dq_ref[:, sl], gq_ref[...])
            dqo_ref[:, sl] = dx.astype(BF16)
            dgq_ref[...] += jnp.sum(dg_rows, axis=0, keepdims=True)
            dx, dg_rows = _norm_bwd(k_ref[:, sl].astype(F32), dk_ref[:, sl], gk_ref[...])
            dko_ref[:, sl] = dx.astype(BF16)
            dgk_ref[...] += jnp.sum(dg_rows, axis=0, keepdims=True)

    vec = pl.BlockSpec((1, HEAD_DIM), lambda i: (0, 0))
    row = pl.BlockSpec((tr, aw), lambda i: (i, 0))
    return pl.pallas_call(
        body, name=name, grid=(t // tr,),
        in_specs=[row, row, row, pl.BlockSpec((tr, aw), lambda i: (i, 1)), vec, vec],
        out_specs=[row, row, vec, vec],
        out_shape=[_sds((t, aw), BF16), _sds((t, aw), BF16), _sds((1, HEAD_DIM), F32), _sds((1, HEAD_DIM), F32)],
        compiler_params=_params(("arbitrary",)),
    )(dqn, dkn, proj, proj, gq, gk)


def _triangle(lower):
    r = lax.broadcasted_iota(jnp.int32, (LANES, LANES), 0)
    c = lax.broadcasted_iota(jnp.int32, (LANES, LANES), 1)
    return ((c <= r) if lower else (c >= r)).astype(F32)


def _forget_fwd(fg, b, name):
    t = fg.shape[0]

    def body(fg_ref, b_ref, cum_ref, carry):
        @pl.when(pl.program_id(0) == 0)
        def _():
            carry[...] = jnp.zeros_like(carry)

        z = fg_ref[...] + b_ref[...]
        log_f = jnp.minimum(z, 0.0) - jnp.log1p(jnp.exp(-jnp.abs(z)))
        cs = jnp.dot(_triangle(True), log_f, precision=lax.Precision.HIGHEST,
                     preferred_element_type=F32) + carry[0:1, :]
        cum_ref[...] = cs
        carry[...] = jnp.broadcast_to(cs[LANES - 1:LANES, :], carry.shape)

    row = pl.BlockSpec((LANES, LANES), lambda i: (i, 0))
    return pl.pallas_call(
        body, name=name, grid=(t // LANES,),
        in_specs=[row, pl.BlockSpec((1, LANES), lambda i: (0, 0))], out_specs=row,
        out_shape=_sds((t, LANES), F32), scratch_shapes=[pltpu.VMEM((SUBLANES, LANES), F32)],
        compiler_params=_params(("arbitrary",)),
    )(fg, b)


def _forget_bwd(dcum, fg, b, name):
    t = fg.shape[0]
    nt = t // LANES

    def body(dc_ref, fg_ref, b_ref, dfg_ref, db_ref, carry):
        @pl.when(pl.program_id(0) == 0)
        def _():
            carry[...] = jnp.zeros_like(carry)
            db_ref[...] = jnp.zeros_like(db_ref)

        d_log_f = jnp.dot(_triangle(False), dc_ref[...], precision=lax.Precision.HIGHEST,
                          preferred_element_type=F32) + carry[0:1, :]
        carry[...] = jnp.broadcast_to(d_log_f[0:1, :], carry.shape)
        dz = d_log_f * jax.nn.sigmoid(-(fg_ref[...] + b_ref[...]))
        dfg_ref[...] = dz.astype(BF16)
        db_ref[...] += jnp.sum(dz, axis=0, keepdims=True)

    row = pl.BlockSpec((LANES, LANES), lambda i: (nt - 1 - i, 0))
    vec = pl.BlockSpec((1, LANES), lambda i: (0, 0))
    return pl.pallas_call(
        body, name=name, grid=(nt,),
        in_specs=[row, row, vec], out_specs=[row, vec],
        out_shape=[_sds((t, LANES), BF16), _sds((1, LANES), F32)],
        scratch_shapes=[pltpu.VMEM((SUBLANES, LANES), F32)],
        compiler_params=_params(("arbitrary",)),
    )(dcum, fg, b)


def _causal(qi, kj, tq):
    rows = qi * tq + lax.broadcasted_iota(jnp.int32, (tq, tq), 0)
    cols = kj * tq + lax.broadcasted_iota(jnp.int32, (tq, tq), 1)
    return cols <= rows


NT_DIMS = (((1,), (1,)), ((), ()))
TN_DIMS = (((0,), (0,)), ((), ()))


def _attn_tile():
    return (384, 256, 128)


def _attn_fwd(q, k, v, cum_row, name, job=None, mid_at=0.5):
    t, aw = q.shape
    heads = aw // HEAD_DIM
    tq = _tile(t, _attn_tile())
    nq = t // tq
    rq = tq
    scale = HEAD_DIM ** -0.5

    def body(q_ref, k_ref, v_ref, ck_ref, o_ref, lse_ref):
        qi = pl.program_id(1)
        qv = q_ref[...]
        n_full = (qi * rq) // tq

        def tile(kj, carry, masked):
            m_prev, l_prev, acc = carry
            ks = pl.ds(pl.multiple_of(kj * tq, tq), tq)
            s = lax.dot_general(qv, k_ref[ks, :], NT_DIMS, preferred_element_type=F32) * scale - ck_ref[kj]
            if masked:
                rows = qi * rq + lax.broadcasted_iota(jnp.int32, (rq, tq), 0)
                cols = kj * tq + lax.broadcasted_iota(jnp.int32, (rq, tq), 1)
                s = jnp.where(cols <= rows, s, -jnp.inf)
            m_new = jnp.maximum(m_prev, jnp.max(s, axis=-1, keepdims=True))
            alpha = jnp.exp(m_prev - m_new)
            p = jnp.exp(s - m_new)
            return (m_new, alpha * l_prev + jnp.sum(p, axis=-1, keepdims=True),
                    alpha * acc + jnp.dot(p.astype(BF16), v_ref[ks, :], preferred_element_type=F32))

        init = (jnp.full((rq, 1), -jnp.inf, F32), jnp.zeros((rq, 1), F32), jnp.zeros((rq, HEAD_DIM), F32))
        carry = lax.fori_loop(0, n_full, lambda kj, c: tile(kj, c, False), init)
        m_fin, l_fin, acc = tile(n_full, carry, True)
        o_ref[...] = (acc / l_fin).astype(o_ref.dtype)
        lse_ref[...] = m_fin + jnp.log(l_fin)

    q_spec = pl.BlockSpec((rq, HEAD_DIM), lambda h, i: (i, h))
    head = pl.BlockSpec((t, HEAD_DIM), lambda h, i: (0, h))
    return _call(
        body, name=name, grid=(heads, t // rq),
        in_specs=[q_spec, head, head, pl.BlockSpec((None, nq, 1, tq), lambda h, i: (h, 0, 0, 0))],
        out_specs=[q_spec, pl.BlockSpec((None, rq, 1), lambda h, i: (h, i, 0))],
        out_shape=[_sds((t, aw), BF16), _sds((heads, t, 1), F32)],
        scratch_shapes=[], semantics=("parallel", "arbitrary"),
        operands=(q, k, v, cum_row), job=job, mid_at=mid_at)


def _attn_stats(do, o, name):
    t, aw = o.shape
    heads = aw // HEAD_DIM
    tr = _tile(t, (384, 256, 128))

    def body(do_ref, o_ref, delta_ref):
        for h in range(heads):
            sl = slice(h * HEAD_DIM, (h + 1) * HEAD_DIM)
            do_seen = do_ref[:, sl].astype(BF16).astype(F32)
            delta_ref[h] = jnp.sum(do_seen * o_ref[:, sl].astype(F32), axis=-1, keepdims=True)

    row = pl.BlockSpec((tr, aw), lambda i: (i, 0))
    return pl.pallas_call(
        body, name=name, grid=(t // tr,),
        in_specs=[row, row], out_specs=pl.BlockSpec((heads, tr, 1), lambda i: (0, i, 0)),
        out_shape=_sds((heads, t, 1), F32), compiler_params=_params(("parallel",)),
    )(do, o)


def _attn_bwd(q, k, v, do, lse, delta, cum_row, name, job=None):
    t, aw = q.shape
    heads = aw // HEAD_DIM
    tq = _tile(t, _attn_tile())
    nq = t // tq
    scale = HEAD_DIM ** -0.5

    def body(q_ref, k_ref, v_ref, do_ref, lse_ref, delta_ref, ck_ref, dq_ref, dk_ref, dv_ref, dck_ref,
             drow_ref):
        kj = pl.program_id(1)

        @pl.when(kj == 0)
        def _():
            dq_ref[...] = jnp.zeros_like(dq_ref)
            drow_ref[...] = jnp.zeros_like(drow_ref)

        kv, vv, ck = k_ref[...], v_ref[...], ck_ref[...]

        def tile(qi, carry, masked):
            dk_acc, dv_acc, dck_acc = carry
            rows = pl.ds(pl.multiple_of(qi * tq, tq), tq)
            qv, dov = q_ref[rows, :], do_ref[rows, :].astype(BF16)
            s = lax.dot_general(qv, kv, NT_DIMS, preferred_element_type=F32) * scale - ck - lse_ref[rows, :]
            p = jnp.exp(s)
            if masked:
                p = jnp.where(_causal(0, 0, tq), p, 0.0)
            dp = lax.dot_general(dov, vv, NT_DIMS, preferred_element_type=F32)
            ds = p * (dp - delta_ref[rows, :])
            dsb = ds.astype(BF16)
            dq_ref[rows, :] += jnp.dot(dsb, kv, preferred_element_type=F32) * scale
            drow_ref[rows, :] += jnp.sum(ds, axis=1, keepdims=True)
            return (dk_acc + lax.dot_general(dsb, qv, TN_DIMS, preferred_element_type=F32),
                    dv_acc + lax.dot_general(p.astype(BF16), dov, TN_DIMS, preferred_element_type=F32),
                    dck_acc + jnp.sum(ds, axis=0, keepdims=True))

        init = (jnp.zeros((tq, HEAD_DIM), F32), jnp.zeros((tq, HEAD_DIM), F32), jnp.zeros((1, tq), F32))
        carry = tile(kj, init, True)
        dk_acc, dv_acc, dck_acc = lax.fori_loop(kj + 1, nq, lambda qi, c: tile(qi, c, False), carry)
        dk_ref[...] = dk_acc * scale
        dv_ref[...] = dv_acc.astype(dv_ref.dtype)
        dck_ref[...] = -dck_acc

    head = pl.BlockSpec((t, HEAD_DIM), lambda h, j: (0, h))
    k_spec = pl.BlockSpec((tq, HEAD_DIM), lambda h, j: (j, h))
    col = pl.BlockSpec((None, t, 1), lambda h, j: (h, 0, 0))
    row = pl.BlockSpec((None, 1, tq), lambda h, j: (h, 0, j))
    return _call(
        body, name=name, grid=(heads, nq),
        in_specs=[head, k_spec, k_spec, head, col, col, row],
        out_specs=[head, k_spec, k_spec, row, col],
        out_shape=[_sds((t, aw), F32), _sds((t, aw), F32), _sds((t, aw), BF16), _sds((heads, 1, t), F32),
                   _sds((heads, t, 1), F32)],
        scratch_shapes=[], semantics=("parallel", "arbitrary"),
        operands=(q, k, v, do, lse, delta, cum_row), job=job)


def _shift_down(u, by):
    rows = lax.broadcasted_iota(jnp.int32, u.shape, 0)
    return jnp.where(rows >= by, pltpu.roll(u, by, 0), 0.0)


def _shift_up(u, by):
    t = u.shape[0]
    rows = lax.broadcasted_iota(jnp.int32, u.shape, 0)
    return jnp.where(rows < t - by, pltpu.roll(u, t - by, 0), 0.0)


def _conv_specs(t, off_b, cw_width):
    nb = cw_width // LANES
    base = off_b // LANES
    return [pl.BlockSpec((t, LANES), lambda j, s=s: (0, base + s * nb + j)) for s in range(3)]


def _conv_fwd(proj, cw, off_b, name):
    t = proj.shape[0]
    width = cw.shape[1]

    def body(cb_ref, cc_ref, cx_ref, w_ref, o_ref):
        u = cc_ref[...].astype(F32) * cx_ref[...]
        y = w_ref[0:1, :] * _shift_down(u, 2) + w_ref[1:2, :] * _shift_down(u, 1) + w_ref[2:3, :] * u
        o_ref[...] = (cb_ref[...] * y).astype(BF16)

    return pl.pallas_call(
        body, name=name, grid=(width // LANES,),
        in_specs=_conv_specs(t, off_b, width) + [pl.BlockSpec((SUBLANES, LANES), lambda j: (0, j))],
        out_specs=pl.BlockSpec((t, LANES), lambda j: (0, j)),
        out_shape=_sds((t, width), BF16), compiler_params=_params(("parallel",)),
    )(proj, proj, proj, cw)


def _conv_bwd(dcp, proj, cw, off_b, name):
    t = proj.shape[0]
    width = cw.shape[1]

    def body(d_ref, cb_ref, cc_ref, cx_ref, w_ref, dcb_ref, dcc_ref, dcx_ref, dw_ref):
        cc, cx = cc_ref[...].astype(F32), cx_ref[...].astype(F32)
        u = cc * cx
        u1, u2 = _shift_down(u, 1), _shift_down(u, 2)
        w0, w1, w2 = w_ref[0:1, :], w_ref[1:2, :], w_ref[2:3, :]
        d = d_ref[...]
        dcb_ref[...] = (d * (w0 * u2 + w1 * u1 + w2 * u)).astype(BF16)
        dy = d * cb_ref[...]
        du = w2 * dy + w1 * _shift_up(dy, 1) + w0 * _shift_up(dy, 2)
        dcc_ref[...] = (du * cx).astype(BF16)
        dcx_ref[...] = (du * cc).astype(BF16)
        dw = [jnp.sum(dy * s, axis=0, keepdims=True) for s in (u2, u1, u)]
        dw_ref[...] = jnp.concatenate(dw + [jnp.zeros((SUBLANES - 3, LANES), F32)], axis=0)

    col = pl.BlockSpec((t, LANES), lambda j: (0, j))
    wspec = pl.BlockSpec((SUBLANES, LANES), lambda j: (0, j))
    return pl.pallas_call(
        body, name=name, grid=(width // LANES,),
        in_specs=[col] + _conv_specs(t, off_b, width) + [wspec],
        out_specs=[col, col, col, wspec],
        out_shape=[_sds((t, width), BF16)] * 3 + [_sds((SUBLANES, width), F32)],
        compiler_params=_params(("parallel",)),
    )(dcp, proj, proj, proj, cw)


def _gate_specs(t, d, off_g, tr, tc, rows_first):
    nb = d // tc
    base = off_g // tc
    if rows_first:
        tile = lambda s: pl.BlockSpec((tr, tc), lambda i, j: (i, base + s * nb + j))
        vec = lambda s: pl.BlockSpec((1, tc), lambda i, j: (0, s * nb + j))
        plain = pl.BlockSpec((tr, tc), lambda i, j: (i, j))
    else:
        tile = lambda s: pl.BlockSpec((tr, tc), lambda j, i: (i, base + s * nb + j))
        vec = lambda s: pl.BlockSpec((1, tc), lambda j, i: (0, s * nb + j))
        plain = pl.BlockSpec((tr, tc), lambda j, i: (i, j))
    return tile, vec, plain


def _gate_fwd(a, c, proj, bg, off_g, name):
    t, d = a.shape
    tr, tc = _tile(t, (384, 256, 128)), _tile(d, (512, 256, 128))
    tile, vec, plain = _gate_specs(t, d, off_g, tr, tc, True)

    def body(a_ref, c_ref, g0_ref, g1_ref, b0_ref, b1_ref, o_ref):
        g0 = jax.nn.sigmoid(g0_ref[...] + b0_ref[...])
        g1 = jax.nn.sigmoid(g1_ref[...] + b1_ref[...])
        o_ref[...] = (g0 * a_ref[...] + g1 * c_ref[...]).astype(BF16)

    return pl.pallas_call(
        body, name=name, grid=(t // tr, d // tc),
        in_specs=[plain, plain, tile(0), tile(1), vec(0), vec(1)], out_specs=plain,
        out_shape=_sds((t, d), BF16), compiler_params=_params(("parallel", "parallel")),
    )(a, c, proj, proj, bg, bg)


def _gate_bwd(dm, a, c, proj, bg, off_g, name):
    t, d = a.shape
    tr, tc = _tile(t, (384, 256, 128)), _tile(d, (512, 256, 128))
    tile, vec, plain = _gate_specs(t, d, off_g, tr, tc, False)

    def body(dm_ref, a_ref, c_ref, g0_ref, g1_ref, b0_ref, b1_ref,
             da_ref, dc_ref, dg0_ref, dg1_ref, db0_ref, db1_ref):
        @pl.when(pl.program_id(1) == 0)
        def _():
            db0_ref[...] = jnp.zeros_like(db0_ref)
            db1_ref[...] = jnp.zeros_like(db1_ref)

        dm = dm_ref[...]
        g0 = jax.nn.sigmoid(g0_ref[...] + b0_ref[...])
        g1 = jax.nn.sigmoid(g1_ref[...] + b1_ref[...])
        da_ref[...] = (dm * g0).astype(BF16)
        dc_ref[...] = (dm * g1).astype(BF16)
        dz0 = dm * a_ref[...] * (g0 * (1.0 - g0))
        dz1 = dm * c_ref[...] * (g1 * (1.0 - g1))
        dg0_ref[...] = dz0.astype(BF16)
        dg1_ref[...] = dz1.astype(BF16)
        db0_ref[...] += jnp.sum(dz0, axis=0, keepdims=True)
        db1_ref[...] += jnp.sum(dz1, axis=0, keepdims=True)

    bvec = pl.BlockSpec((1, tc), lambda j, i: (0, j))
    return pl.pallas_call(
        body, name=name, grid=(d // tc, t // tr),
        in_specs=[plain, plain, plain, tile(0), tile(1), vec(0), vec(1)],
        out_specs=[plain] * 4 + [bvec, bvec],
        out_shape=[_sds((t, d), BF16)] * 4 + [_sds((1, d), F32)] * 2,
        compiler_params=_params(("parallel", "arbitrary")),
    )(dm, a, c, proj, proj, bg, bg)


def _sum_squares(x, name):
    t, d = x.shape
    tr = _tile(t, (384, 256, 128))

    def body(x_ref, o_ref):
        @pl.when(pl.program_id(0) == 0)
        def _():
            o_ref[...] = jnp.zeros_like(o_ref)

        v = x_ref[...]
        o_ref[...] += jnp.sum(jnp.sum(v * v, axis=0, keepdims=True), axis=1, keepdims=True)

    return pl.pallas_call(
        body, name=name, grid=(t // tr,),
        in_specs=[pl.BlockSpec((tr, d), lambda i: (i, 0))],
        out_specs=pl.BlockSpec((1, LANES), lambda i: (0, 0)),
        out_shape=_sds((1, LANES), F32), compiler_params=_params(("arbitrary",)),
    )(x)


def _row_tile(r, c):
    return r if r * c <= 128 * 1024 else _tile(r, (128, 64, 32, 16))


def _sum_parts(parts, name):
    n, r, c = parts.shape
    tr = _row_tile(r, c)

    def body(p_ref, o_ref):
        acc = p_ref[0].astype(F32)
        for i in range(1, n):
            acc = acc + p_ref[i].astype(F32)
        o_ref[...] = acc

    return pl.pallas_call(
        body, name=name, grid=(r // tr,),
        in_specs=[pl.BlockSpec((n, tr, c), lambda i: (0, i, 0))],
        out_specs=pl.BlockSpec((tr, c), lambda i: (i, 0)),
        out_shape=_sds((r, c), F32), compiler_params=_params(("parallel",)),
    )(parts)


def _adamw(chunks, w, m, v, name):
    n, rc, c = chunks[0].shape
    r = rc * len(chunks)
    tr = _row_tile(rc, c)
    per = rc // tr

    def body(*refs):
        p_refs = refs[:len(chunks)]
        w_ref, m_ref, v_ref, g_ref, d_ref, nm_ref, nv_ref = refs[len(chunks):]
        i = pl.program_id(0)

        def update(p_ref):
            g = p_ref[0].astype(F32)
            for s in range(1, n):
                g = g + p_ref[s].astype(F32)
            nm = ADAM_B1 * m_ref[...] + (1.0 - ADAM_B1) * g
            nv = ADAM_B2 * v_ref[...] + (1.0 - ADAM_B2) * (g * g)
            m_hat = nm / (1.0 - ADAM_B1 ** ADAM_STEP)
            v_hat = nv / (1.0 - ADAM_B2 ** ADAM_STEP)
            g_ref[...] = g
            d_ref[...] = -ADAM_LR * (m_hat / (jnp.sqrt(v_hat) + ADAM_EPS) + ADAM_WD * w_ref[...])
            nm_ref[...] = nm
            nv_ref[...] = nv

        if len(chunks) == 1:
            update(p_refs[0])
        else:
            for ci, p_ref in enumerate(p_refs):
                pl.when((i >= ci * per) & (i < (ci + 1) * per))(functools.partial(update, p_ref))

    row = pl.BlockSpec((tr, c), lambda i: (i, 0))
    part_specs = [pl.BlockSpec((n, tr, c), lambda i, ci=ci: (0, jnp.clip(i - ci * per, 0, per - 1), 0))
                  for ci in range(len(chunks))]
    return pl.pallas_call(
        body, name=name, grid=(r // tr,),
        in_specs=part_specs + [row, row, row],
        out_specs=[row] * 4, out_shape=[_sds((r, c), F32)] * 4,
        compiler_params=_params(("parallel",)),
    )(*chunks, w, m, v)


def _adamw_cols(chunks, w, m, v, name):
    n, r, _ = chunks[0].shape
    widths = [ch.shape[2] for ch in chunks]
    tc = functools.reduce(math.gcd, widths, LANES)
    firsts = [sum(widths[:ci]) // tc for ci in range(len(chunks) + 1)]

    def body(*refs):
        p_refs = refs[:len(chunks)]
        w_ref, m_ref, v_ref, g_ref, d_ref, nm_ref, nv_ref = refs[len(chunks):]
        j = pl.program_id(0)

        def update(p_ref):
            g = p_ref[0].astype(F32)
            for s in range(1, n):
                g = g + p_ref[s].astype(F32)
            nm = ADAM_B1 * m_ref[...] + (1.0 - ADAM_B1) * g
            nv = ADAM_B2 * v_ref[...] + (1.0 - ADAM_B2) * (g * g)
            m_hat = nm / (1.0 - ADAM_B1 ** ADAM_STEP)
            v_hat = nv / (1.0 - ADAM_B2 ** ADAM_STEP)
            g_ref[...] = g
            d_ref[...] = -ADAM_LR * (m_hat / (jnp.sqrt(v_hat) + ADAM_EPS) + ADAM_WD * w_ref[...])
            nm_ref[...] = nm
            nv_ref[...] = nv

        for ci, p_ref in enumerate(p_refs):
            pl.when((j >= firsts[ci]) & (j < firsts[ci + 1]))(functools.partial(update, p_ref))

    col = pl.BlockSpec((r, tc), lambda j: (0, j))
    part_specs = [pl.BlockSpec((n, r, tc),
                               lambda j, lo=firsts[ci], hi=firsts[ci + 1]: (0, 0, jnp.clip(j - lo, 0, hi - lo - 1)))
                  for ci in range(len(chunks))]
    return pl.pallas_call(
        body, name=name, grid=(firsts[-1],),
        in_specs=part_specs + [col, col, col],
        out_specs=[col] * 4, out_shape=[_sds((r, firsts[-1] * tc), F32)] * 4,
        compiler_params=_params(("parallel",)),
    )(*chunks, w, m, v)


def _pad_lanes(a, width=LANES):
    return jnp.pad(a, ((0, 0), (0, width - a.shape[1])))


def _rows_of(a):
    flat = a.reshape(-1)
    n = -(-flat.shape[0] // LANES) * LANES
    return jnp.pad(flat, (0, n - flat.shape[0])).reshape(-1, LANES)


def _columns_to_slots(full, n_rows):
    return full.reshape(n_rows, N_DEV, -1).transpose(1, 0, 2)


def _slots_to_columns(slots):
    return slots.transpose(1, 0, 2).reshape(slots.shape[1], -1)


def kernel(x, meta_tokens, norm_mix, w_in, b_fgate, b_gate, q_norm, k_norm, conv_w, w_attn_out, w_conv_out, w_o, norm_mlp, w_up, w_down, loss_target, m_meta_tokens, m_norm_mix, m_w_in, m_b_fgate, m_b_gate, m_q_norm, m_k_norm, m_conv_w, m_w_attn_out, m_w_conv_out, m_w_o, m_norm_mlp, m_w_up, m_w_down, v_meta_tokens, v_norm_mix, v_w_in, v_b_fgate, v_b_gate, v_q_norm, v_k_norm, v_conv_w, v_w_attn_out, v_w_conv_out, v_w_o, v_norm_mlp, v_w_up, v_w_down):
    seq, d = x.shape[1], x.shape[2]
    heads = b_fgate.shape[1]
    aw = heads * HEAD_DIM
    cwid = conv_w.shape[2] * N_DEV
    dff = w_up.shape[2] * N_DEV
    n_valid = N_META + seq
    t = -(-n_valid // LANES) * LANES
    me = _flat(*_my_place())
    off_cb, off_gl = 3 * aw, 3 * aw + 3 * cwid

    conv_shard = jnp.pad(conv_w[0], ((0, SUBLANES - conv_w.shape[1]), (0, 0)))
    w_in_t, m_in_t, v_in_t = (jnp.swapaxes(p, 1, 2)[0] for p in (w_in, m_w_in, v_w_in))
    g_in, g_meta, g_cw = _run_job(_Gather([w_in_t.astype(BF16), meta_tokens, conv_shard]), "gather_first")
    n_in = N_DEV * g_in.shape[1]
    w_all_t = g_in.reshape(n_in, d)
    w_main_t = jnp.concatenate([w_all_t[:3 * aw], w_all_t[3 * aw + heads:]], axis=0)
    w_fg_t = jnp.pad(w_all_t[3 * aw:3 * aw + heads], ((0, LANES - heads), (0, 0)))
    meta_full, cw_full = _slots_to_columns(g_meta), _slots_to_columns(g_cw)

    pad_rows = t - n_valid
    h0 = jnp.concatenate([meta_full, x[0], jnp.zeros((pad_rows, d), F32)], axis=0)
    target = jnp.concatenate([jnp.zeros((N_META, d), F32), loss_target[0], jnp.zeros((pad_rows, d), F32)], axis=0)
    b_f = _pad_lanes(b_fgate)

    xn = _rmsnorm_fwd(h0, norm_mix, "norm_mix_fwd")
    proj, (g_ao, g_co, g_o) = _matmul(
        xn, w_main_t, name="in_proj", trans_b=True, mid_at=0.5, out_dtypes=(BF16,),
        job=_Gather([w_attn_out[0].astype(BF16), w_conv_out[0].astype(BF16), w_o[0].astype(BF16)]))
    w_ao, w_co, w_o_f = _slots_to_columns(g_ao), _slots_to_columns(g_co), g_o.reshape(d, d)
    fg = _matmul(xn, w_fg_t, name="in_proj_fgate", trans_b=True)
    qn, kn, vb = _qk_prep(proj, q_norm, k_norm, aw, "qk_norm_fwd")
    cum = _forget_fwd(fg, b_f, "forget_cumsum")
    cum_heads = cum[:, :heads].T
    cum_row = cum_heads[:, None, :]
    t_attn = _tile(t, _attn_tile())
    (o, lse), (g_up, g_down) = _attn_fwd(
        qn, kn, vb, cum_heads.reshape(heads, t // t_attn, 1, t_attn), "attention_fwd", mid_at=0.55,
        job=_Gather([w_up[0].astype(BF16), w_down[0].astype(BF16)]))
    w_up_f, w_down_f = _slots_to_columns(g_up), g_down.reshape(dff, d)
    a = _matmul(o, w_ao, name="attn_out_proj", out_dtypes=(BF16,))
    cpre = _conv_fwd(proj, cw_full, off_cb, "short_conv_fwd")
    c = _matmul(cpre, w_co, name="conv_out_proj", out_dtypes=(BF16,))
    merged = _gate_fwd(a, c, proj, b_gate, off_gl, "gate_merge_fwd")
    h1 = _matmul(merged, w_o_f, name="out_proj", extras=(h0,), epilogue=lambda acc, i, j, r: (r + acc,))
    hn = _rmsnorm_fwd(h1, norm_mlp, "norm_mlp_fwd")
    z, u = _matmul(hn, w_up_f, name="mlp_up", out_dtypes=(F32, BF16),
                   epilogue=lambda acc, i, j: (acc, jnp.square(jnp.maximum(acc, 0.0))))

    tm_down = _tile(t, (1408, 1024, 512, 256, 128))

    def loss_grad(acc, i, j, h1_tile, tgt_tile):
        rows = i * tm_down + lax.broadcasted_iota(jnp.int32, acc.shape, 0)
        valid = (rows >= N_META) & (rows < n_valid)
        dy = jnp.where(valid, ((h1_tile + acc) - tgt_tile) / d, 0.0)
        return dy, dy

    dh2, dh2b = _matmul(u, w_down_f, name="mlp_down_loss", extras=(h1, target), epilogue=loss_grad,
                        out_dtypes=(F32, BF16), tm=tm_down)
    loss_part = _sum_squares(dh2, "loss_sum") * (0.5 * d)

    wide = lambda n_cols: _tile(n_cols, (1024, 512, 256, 128))
    dw_down = _matmul(u, dh2b, name="mlp_down_wgrad", trans_a=True, tn=wide(d), out_dtypes=(BF16,))
    s_down = dw_down.reshape(N_DEV, dff // N_DEV, d)
    half_down = dff // N_DEV // 2
    dz, l_down0 = _matmul(dh2b, w_down_f, name="mlp_down_bwd", trans_b=True, extras=(z,), out_dtypes=(BF16,),
                          epilogue=lambda acc, i, j, zt: (acc * (2.0 * jnp.maximum(zt, 0.0)),),
                          job=_Scatter([(s_down, 0, half_down)]))
    s_up, l_down1 = _matmul(hn, dz, name="mlp_up_wgrad", trans_a=True, slots=True, out_dtypes=(BF16,),
                            tn=wide(dff // N_DEV), job=_Scatter([(s_down, half_down, half_down)]))
    dhn, l_up0 = _matmul(dz, w_up_f, name="mlp_up_bwd", trans_b=True, tn=wide(d),
                         job=_Scatter([(s_up, 0, d // 2)]))
    dh1, dh1b, dg_mlp = _rmsnorm_bwd(h1, dhn, norm_mlp, dh2, "norm_mlp_bwd")
    dmerged = _matmul(dh1b, w_o_f, name="out_proj_bwd", trans_b=True)
    dw_o = _matmul(merged, dh1b, name="out_proj_wgrad", trans_a=True, tn=wide(d), out_dtypes=(BF16,))
    da, dc, dgl0, dgl1, dbg0, dbg1 = _gate_bwd(dmerged, a, c, proj, b_gate, off_gl, "gate_merge_bwd")
    do = _matmul(da, w_ao, name="attn_out_bwd", trans_b=True)
    s_ao = _matmul(o, da, name="attn_out_wgrad", trans_a=True, slots=True, out_dtypes=(BF16,), tk=t)
    dcp = _matmul(dc, w_co, name="conv_out_bwd", trans_b=True)
    s_co = _matmul(cpre, dc, name="conv_out_wgrad", trans_a=True, slots=True, out_dtypes=(BF16,), tk=t)
    dcb, dcc, dcx, dcw = _conv_bwd(dcp, proj, cw_full, off_cb, "short_conv_bwd")
    delta = _attn_stats(do, o, "attention_stats")
    (dqn, dkn, dv, dck, drow), (l_up1, l_o, l_ao, l_co) = _attn_bwd(
        qn, kn, vb, do, lse, delta, cum_row, "attention_bwd",
        job=_Scatter([(s_up, d // 2, d // 2), dw_o.reshape(N_DEV, d // N_DEV, d), s_ao, s_co]))
    dq_raw, dk_raw, dg_q, dg_k = _qk_bwd(dqn, dkn, proj, q_norm, k_norm, aw, "qk_norm_bwd")
    dcum = _pad_lanes((dck.reshape(heads, t) + drow.reshape(heads, t)).T)
    dfg, db_f = _forget_bwd(dcum, fg, b_f, "forget_bwd")
    dproj = jnp.concatenate([dq_raw, dk_raw, dv, dcb, dcc, dcx, dgl0, dgl1], axis=1)
    dwt_fg = _matmul(dfg, xn, name="in_proj_fgate_wgrad", trans_a=True, out_dtypes=(BF16,))
    range_ends = [3 * d // 16, d // 2, d]

    def in_slots(dwt, first):
        width = dwt.shape[1]
        parts = ((0, 3 * aw, dwt, 0), (3 * aw, 3 * aw + heads, dwt_fg[:heads, first:first + width], 3 * aw),
                 (3 * aw + heads, n_in, dwt, heads))
        slots = []
        for j in range(N_DEV):
            lo, hi = j * n_in // N_DEV, (j + 1) * n_in // N_DEV
            rows = [src[max(lo, a) - shift:min(hi, b) - shift] for a, b, src, shift in parts
                    if max(lo, a) < min(hi, b)]
            slots.append(rows[0] if len(rows) == 1 else jnp.concatenate(rows, axis=0))
        return jnp.stack(slots)

    def in_wgrad(idx, job):
        lo, hi = ([0] + range_ends)[idx], range_ends[idx]
        return _matmul(dproj, xn[:, lo:hi], name="in_proj_wgrad_%d" % idx, trans_a=True, tn=hi - lo,
                       out_dtypes=(BF16,), job=job)

    dwt0 = in_wgrad(0, None)
    dwt1, l_in0 = in_wgrad(1, _Scatter([in_slots(dwt0, 0)]))
    dwt2, l_in1 = in_wgrad(2, _Scatter([in_slots(dwt1, range_ends[0])]))
    dxn_fg = _matmul(dfg, w_fg_t, name="in_proj_fgate_bwd")
    dxn, l_in2 = _matmul(dproj, w_main_t, name="in_proj_bwd", extras=(dxn_fg,),
                         epilogue=lambda acc, i, j, r: (r + acc,), job=_Scatter([in_slots(dwt2, range_ends[1])]))
    dh0, _, dg_mix = _rmsnorm_bwd(h0, dxn, norm_mix, dh1, "norm_mix_bwd")

    small = [dg_mix, dbg0, dbg1, dg_mlp, dg_q, dg_k, db_f, loss_part, dcw, dh0[:N_META]]
    small_rows = [_rows_of(s) for s in small]
    pack = jnp.concatenate(small_rows, axis=0)
    pack = jnp.pad(pack, ((0, -pack.shape[0] % SUBLANES), (0, 0)))
    (pack_all,) = _run_job(_Scatter([], [pack]), "gather_small")

    landed = {"w_attn_out": [l_ao], "w_conv_out": [l_co], "w_o": [l_o],
              "w_up": l_up0 + [l_up1], "w_down": l_down0 + l_down1}
    shards = {"w_attn_out": (w_attn_out, m_w_attn_out, v_w_attn_out),
              "w_conv_out": (w_conv_out, m_w_conv_out, v_w_conv_out), "w_o": (w_o, m_w_o, v_w_o),
              "w_up": (w_up, m_w_up, v_w_up), "w_down": (w_down, m_w_down, v_w_down)}
    out = {}
    for nm, chunks in landed.items():
        w_, m_, v_ = shards[nm]
        res = _adamw(list(chunks), w_[0], m_[0], v_[0], "adamw_" + nm)
        out[nm] = [r[None] for r in res]
    res = _adamw_cols(l_in0 + l_in1 + l_in2, w_in_t, m_in_t, v_in_t, "adamw_w_in")
    out["w_in"] = [r.T[None] for r in res]

    total = _sum_parts(pack_all, "sum_small")
    pieces, at = [], 0
    for s, rows in zip(small, small_rows):
        n_el = 1
        for dim in s.shape:
            n_el *= dim
        pieces.append(total[at:at + rows.shape[0]].reshape(-1)[:n_el].reshape(s.shape))
        at += rows.shape[0]
    g_mix, g_bg0, g_bg1, g_mlp, g_q, g_k, g_bf, loss_row, g_cw_full, g_meta_full = pieces
    loss = loss_row[0, 0]
    cshard = conv_w.shape[2]
    g_small = {
        "norm_mix": g_mix, "b_gate": jnp.concatenate([g_bg0, g_bg1], axis=1), "norm_mlp": g_mlp,
        "q_norm": g_q, "k_norm": g_k, "b_fgate": g_bf[:, :heads],
        "conv_w": lax.dynamic_slice_in_dim(g_cw_full[:conv_w.shape[1]], me * cshard, cshard, axis=1)[None],
        "meta_tokens": lax.dynamic_slice_in_dim(g_meta_full, me * (d // N_DEV), d // N_DEV, axis=1),
    }
    small_w = {"norm_mix": (norm_mix, m_norm_mix, v_norm_mix), "b_gate": (b_gate, m_b_gate, v_b_gate),
               "norm_mlp": (norm_mlp, m_norm_mlp, v_norm_mlp), "q_norm": (q_norm, m_q_norm, v_q_norm),
               "k_norm": (k_norm, m_k_norm, v_k_norm), "b_fgate": (b_fgate, m_b_fgate, v_b_fgate),
               "conv_w": (conv_w, m_conv_w, v_conv_w), "meta_tokens": (meta_tokens, m_meta_tokens, v_meta_tokens)}
    order = list(small_w)
    packed = []
    for idx in range(4):
        cols = [g_small[nm] if idx == 0 else small_w[nm][idx - 1] for nm in order]
        rows = jnp.concatenate([_rows_of(c_) for c_ in cols], axis=0)
        packed.append(jnp.pad(rows, ((0, -rows.shape[0] % SUBLANES), (0, 0))))
    res = _adamw([packed[0][None]], packed[1], packed[2], packed[3], "adamw_small")
    at = 0
    for nm in order:
        shape = small_w[nm][0].shape
        n_el = 1
        for dim in shape:
            n_el *= dim
        n_rows = -(-n_el // LANES)
        out[nm] = [r[at:at + n_rows].reshape(-1)[:n_el].reshape(shape) for r in res]
        at += n_rows

    weights = ["meta_tokens", "norm_mix", "w_in", "b_fgate", "b_gate", "q_norm", "k_norm", "conv_w",
               "w_attn_out", "w_conv_out", "w_o", "norm_mlp", "w_up", "w_down"]
    grad_x = dh0[N_META:n_valid][None]
    return (loss, grad_x, *[out[nm][0] for nm in weights], *[out[nm][1] for nm in weights],
            *[out[nm][2] for nm in weights], *[out[nm][3] for nm in weights])
```

```python
import functools
import math

import jax
import jax.numpy as jnp
from jax import lax
from jax.experimental import pallas as pl
from jax.experimental.pallas import tpu as pltpu

F32 = jnp.float32
BF16 = jnp.bfloat16

N_DEV = 8
N_META = 16
HEAD_DIM = 128
LANES = 128
SUBLANES = 8
EPS = 1e-6
VMEM_LIMIT = 56 * 1024 * 1024

ADAM_LR = 0.001
ADAM_B1 = 0.9
ADAM_B2 = 0.999
ADAM_EPS = 1e-08
ADAM_WD = 0.01
ADAM_STEP = 10

MESH = pl.DeviceIdType.MESH
HBM_SPEC = pl.BlockSpec(memory_space=pltpu.HBM)
RELATIONS = tuple((r >> 2 & 1, r >> 1 & 1, r & 1) for r in range(1, N_DEV))


def _params(semantics=None):
    return pltpu.CompilerParams(dimension_semantics=semantics, vmem_limit_bytes=VMEM_LIMIT)


def _tile(n, prefs):
    for p in prefs:
        if n % p == 0:
            return p
    return n


def _sds(shape, dtype):
    return jax.ShapeDtypeStruct(shape, dtype)


def _my_place():
    return lax.axis_index("x"), lax.axis_index("y"), lax.axis_index("c")


def _flat(px, py, pc):
    return 4 * px + 2 * py + pc


class _Gather:
    def __init__(self, arrays):
        self.operands = list(arrays)
        self.n = len(arrays)
        self.out_shape = [_sds((N_DEV,) + a.shape, a.dtype) for a in arrays]
        self.split = [(a.shape[0] // 2 // 16 * 16) or a.shape[0] for a in arrays]

    def _copy(self, srcs, outs, sems, a, k, block, to, from_src=False, rows=None):
        slot = outs[a].at[_flat(*block)]
        if rows is not None:
            slot = slot.at[pl.ds(*rows)]
        return pltpu.make_async_remote_copy(
            src_ref=srcs[a] if from_src else slot, dst_ref=slot,
            send_sem=sems[0].at[a, k], recv_sem=sems[1].at[a, k],
            device_id=to, device_id_type=MESH)

    def _places(self):
        x, y, c = _my_place()
        return {"me": (x, y, c), "sib": (x, y, 1 - c), "x": (1 - x, y, c), "y": (x, 1 - y, c),
                "diag": (1 - x, 1 - y, c)}

    def _parts(self, a):
        n_rows, first = self.operands[a].shape[0], self.split[a]
        return (0, first), ((first, n_rows - first) if first < n_rows else None)

    def start(self, srcs, outs, sems):
        at = self._places()
        for a in range(self.n):
            pltpu.make_async_copy(srcs[a], outs[a].at[_flat(*at["me"])], sems[2].at[a]).start()
            self._copy(srcs, outs, sems, a, 1, at["me"], at["x"], from_src=True).start()
            self._copy(srcs, outs, sems, a, 2, at["me"], at["y"], from_src=True).start()
            self._copy(srcs, outs, sems, a, 0, at["me"], at["sib"], from_src=True).start()

    def mid(self, srcs, outs, sems):
        at = self._places()
        for a in range(self.n):
            first, rest = self._parts(a)
            self._copy(srcs, outs, sems, a, 1, at["x"], at["me"]).wait_recv()
            self._copy(srcs, outs, sems, a, 3, at["x"], at["y"], rows=first).start()
            self._copy(srcs, outs, sems, a, 5, at["x"], at["sib"]).start()
            self._copy(srcs, outs, sems, a, 2, at["y"], at["me"]).wait_recv()
            if rest:
                self._copy(srcs, outs, sems, a, 4, at["y"], at["x"], rows=rest).start()
            self._copy(srcs, outs, sems, a, 6, at["y"], at["sib"]).start()

    def finish(self, srcs, outs, sems):
        at = self._places()
        x, y, c = at["me"]
        for a in range(self.n):
            first, rest = self._parts(a)
            self._copy(srcs, outs, sems, a, 3, at["diag"], at["me"], rows=first).wait_recv()
            if rest:
                self._copy(srcs, outs, sems, a, 4, at["diag"], at["me"], rows=rest).wait_recv()
            self._copy(srcs, outs, sems, a, 7, at["diag"], at["sib"]).start()
        for a in range(self.n):
            first, rest = self._parts(a)
            self._copy(srcs, outs, sems, a, 0, at["sib"], at["me"]).wait_recv()
            for k, chip in ((5, (1 - x, y)), (6, (x, 1 - y)), (7, (1 - x, 1 - y))):
                self._copy(srcs, outs, sems, a, k, (*chip, 1 - c), at["me"]).wait_recv()
            for k in (0, 1, 2, 5, 6, 7):
                self._copy(srcs, outs, sems, a, k, at["me"], at["sib"]).wait_send()
            self._copy(srcs, outs, sems, a, 3, at["me"], at["sib"], rows=first).wait_send()
            if rest:
                self._copy(srcs, outs, sems, a, 4, at["me"], at["sib"], rows=rest).wait_send()
            pltpu.make_async_copy(srcs[a], outs[a].at[_flat(*at["me"])], sems[2].at[a]).wait()


class _Scatter:
    def __init__(self, scatter, gather=()):
        scatter = [s if isinstance(s, tuple) else (s, 0, s.shape[1]) for s in scatter]
        self.ranges = [(lo, cnt) for _, lo, cnt in scatter]
        self.operands = [s[0] for s in scatter] + list(gather)
        self.ns, self.n = len(scatter), len(scatter) + len(gather)
        self.out_shape = ([_sds((N_DEV, cnt, arr.shape[2]), arr.dtype) for arr, _, cnt in scatter]
                          + [_sds((N_DEV,) + a.shape, a.dtype) for a in gather])

    def _peer(self, rel):
        return tuple(1 - p if r else p for p, r in zip(_my_place(), rel))

    def _src(self, srcs, a, place):
        if a >= self.ns:
            return srcs[a]
        lo, cnt = self.ranges[a]
        return srcs[a].at[_flat(*place), pl.ds(lo, cnt)]

    def _send(self, srcs, outs, sems, a, k, rel):
        peer = self._peer(rel)
        return pltpu.make_async_remote_copy(
            src_ref=self._src(srcs, a, peer), dst_ref=outs[a].at[_flat(*_my_place())],
            send_sem=sems[0].at[a, k], recv_sem=sems[1].at[a, k],
            device_id=peer, device_id_type=MESH)

    def _landed(self, outs, sems, a, k, rel):
        peer = self._peer(rel)
        slot = outs[a].at[_flat(*peer)]
        return pltpu.make_async_remote_copy(
            src_ref=slot, dst_ref=slot, send_sem=sems[0].at[a, k], recv_sem=sems[1].at[a, k],
            device_id=peer, device_id_type=MESH)

    def _own(self, srcs, outs, sems, a):
        me = _my_place()
        return pltpu.make_async_copy(self._src(srcs, a, me), outs[a].at[_flat(*me)], sems[2].at[a])

    def start(self, srcs, outs, sems):
        for a in range(self.n):
            self._own(srcs, outs, sems, a).start()
            for k, rel in enumerate(RELATIONS):
                self._send(srcs, outs, sems, a, k, rel).start()

    def mid(self, srcs, outs, sems):
        pass

    def finish(self, srcs, outs, sems):
        for a in range(self.n):
            for k, rel in enumerate(RELATIONS):
                self._landed(outs, sems, a, k, rel).wait_recv()
            for k, rel in enumerate(RELATIONS):
                self._send(srcs, outs, sems, a, k, rel).wait_send()
            self._own(srcs, outs, sems, a).wait()


def _job_sems(job):
    return [pltpu.SemaphoreType.DMA((job.n, 8)), pltpu.SemaphoreType.DMA((job.n, 8)),
            pltpu.SemaphoreType.DMA((job.n,))]


def _run_job(job, name):
    n = job.n

    def body(*refs):
        srcs, outs, sems = refs[:n], refs[n:2 * n], refs[2 * n:]
        job.start(srcs, outs, sems)
        job.mid(srcs, outs, sems)
        job.finish(srcs, outs, sems)

    return pl.pallas_call(
        body, name=name, out_shape=job.out_shape,
        in_specs=[HBM_SPEC] * n, out_specs=[HBM_SPEC] * n, scratch_shapes=_job_sems(job),
    )(*job.operands)


def _call(body, *, name, grid, in_specs, out_specs, out_shape, scratch_shapes, semantics,
          operands, job=None, mid_at=0.5):
    if job is None:
        res = pl.pallas_call(
            body, name=name, grid=grid, in_specs=in_specs, out_specs=out_specs, out_shape=out_shape,
            scratch_shapes=scratch_shapes, compiler_params=_params(semantics))(*operands)
        return res, []
    n_in, n_out, n_scr = len(in_specs), len(out_specs), len(scratch_shapes)
    total = 1
    for g in grid:
        total *= g
    mid_step = min(int(total * mid_at), total - 1)

    def carried(*refs):
        c_in, j_in = refs[:n_in], refs[n_in:n_in + job.n]
        o0 = n_in + job.n
        c_out, j_out = refs[o0:o0 + n_out], refs[o0 + n_out:o0 + n_out + job.n]
        s0 = o0 + n_out + job.n
        c_scr, sems = refs[s0:s0 + n_scr], refs[s0 + n_scr:]
        step = pl.program_id(0)
        for ax in range(1, len(grid)):
            step = step * grid[ax] + pl.program_id(ax)

        @pl.when(step == 0)
        def _():
            job.start(j_in, j_out, sems)

        body(*c_in, *c_out, *c_scr)

        @pl.when(step == mid_step)
        def _():
            job.mid(j_in, j_out, sems)

        @pl.when(step == total - 1)
        def _():
            job.finish(j_in, j_out, sems)

    res = pl.pallas_call(
        carried, name=name, grid=grid,
        in_specs=list(in_specs) + [HBM_SPEC] * job.n,
        out_specs=list(out_specs) + [HBM_SPEC] * job.n,
        out_shape=list(out_shape) + job.out_shape,
        scratch_shapes=list(scratch_shapes) + _job_sems(job),
        compiler_params=_params(("arbitrary",) * len(grid)),
    )(*operands, *job.operands)
    return list(res[:n_out]), list(res[n_out:])


def _matmul(a, b, *, name, trans_b=False, extras=(), epilogue=None, out_dtypes=(F32,),
            tm=None, tn=None, tk=None, rows=None, cols=None, trans_a=False, slots=False, job=None,
            mid_at=0.5):
    k, m = a.shape if trans_a else a.shape[::-1]
    n = b.shape[0] if trans_b else b.shape[1]
    tm = tm or _tile(m, (1408, 1024, 512, 256, 128))
    tn = tn or _tile(n // N_DEV if slots else n, (512, 256, 128))
    tk = tk or _tile(k, (2048, 1408, 1024, 512, 256, 128))
    nk = k // tk
    row0, n_rows = rows or (0, m // tm)
    m = n_rows * tm
    col0, n_cols = cols or (0, n // tn)
    n = n_cols * tn
    n_ex, n_out = len(extras), len(out_dtypes)
    dims = (((0,) if trans_a else (1,), (1,) if trans_b else (0,)), ((), ()))

    def body(*refs):
        a_ref, b_ref = refs[:2]
        ex_refs = refs[2:2 + n_ex]
        out_refs = refs[2 + n_ex:2 + n_ex + n_out]
        part = lax.dot_general(a_ref[...].astype(BF16), b_ref[...].astype(BF16), dims,
                               preferred_element_type=F32)

        def finish(acc):
            if epilogue is None:
                res = (acc,)
            else:
                res = epilogue(acc, pl.program_id(0), pl.program_id(1), *[e[...] for e in ex_refs])
            for o_ref, r in zip(out_refs, res):
                o_ref[...] = r.astype(o_ref.dtype)

        if nk == 1:
            finish(part)
        else:
            acc_ref = refs[-1]
            kk = pl.program_id(2)

            @pl.when(kk == 0)
            def _():
                acc_ref[...] = part

            @pl.when(kk > 0)
            def _():
                acc_ref[...] += part

            @pl.when(kk == nk - 1)
            def _():
                finish(acc_ref[...])

    in_specs = [pl.BlockSpec((tk, tm), lambda i, j, kk: (kk, row0 + i)) if trans_a
                else pl.BlockSpec((tm, tk), lambda i, j, kk: (row0 + i, kk)),
                pl.BlockSpec((tn, tk), lambda i, j, kk: (col0 + j, kk)) if trans_b
                else pl.BlockSpec((tk, tn), lambda i, j, kk: (kk, col0 + j))]
    for e in extras:
        if e.shape[0] == 1:
            in_specs.append(pl.BlockSpec((1, tn), lambda i, j, kk: (0, j)))
        else:
            in_specs.append(pl.BlockSpec((tm, tn), lambda i, j, kk: (i, j)))
    if slots:
        per_slot = n // N_DEV // tn
        out_spec = pl.BlockSpec((None, tm, tn), lambda i, j, kk: (j // per_slot, i, j % per_slot))
        out_shape = [_sds((N_DEV, m, n // N_DEV), d) for d in out_dtypes]
    else:
        out_spec = pl.BlockSpec((tm, tn), lambda i, j, kk: (i, j))
        out_shape = [_sds((m, n), d) for d in out_dtypes]
    res, moved = _call(
        body, name=name, grid=(n_rows, n // tn, nk),
        in_specs=in_specs,
        out_specs=[out_spec] * n_out,
        out_shape=out_shape,
        scratch_shapes=[pltpu.VMEM((tm, tn), F32)] if nk > 1 else [],
        semantics=("parallel", "parallel", "arbitrary"),
        operands=(a, b, *extras), job=job, mid_at=mid_at)
    res = res[0] if n_out == 1 else tuple(res)
    return res if job is None else (res, moved)


def _rstd(x):
    return lax.rsqrt(jnp.mean(x * x, axis=-1, keepdims=True) + EPS)


def _norm_bwd(x, dy, g):
    r = _rstd(x)
    u = dy * g
    dx = r * u - x * (r * r * r) * jnp.mean(u * x, axis=-1, keepdims=True)
    return dx, dy * (x * r)


def _rmsnorm_fwd(h, g, name):
    t, d = h.shape
    tr = _tile(t, (384, 256, 128))

    def body(h_ref, g_ref, o_ref):
        x = h_ref[...]
        o_ref[...] = ((x * _rstd(x)) * g_ref[...]).astype(o_ref.dtype)

    row = pl.BlockSpec((tr, d), lambda i: (i, 0))
    return pl.pallas_call(
        body, name=name, grid=(t // tr,),
        in_specs=[row, pl.BlockSpec((1, d), lambda i: (0, 0))], out_specs=row,
        out_shape=_sds((t, d), BF16), compiler_params=_params(("parallel",)),
    )(h, g)


def _rmsnorm_bwd(h, dy, g, res, name):
    t, d = h.shape
    tr = _tile(t, (384, 256, 128))

    def body(h_ref, dy_ref, g_ref, res_ref, dh_ref, dhb_ref, dg_ref):
        dx, dg_rows = _norm_bwd(h_ref[...], dy_ref[...], g_ref[...])
        dh = res_ref[...] + dx
        dh_ref[...] = dh
        dhb_ref[...] = dh.astype(BF16)

        @pl.when(pl.program_id(0) == 0)
        def _():
            dg_ref[...] = jnp.zeros_like(dg_ref)

        dg_ref[...] += jnp.sum(dg_rows, axis=0, keepdims=True)

    row = pl.BlockSpec((tr, d), lambda i: (i, 0))
    vec = pl.BlockSpec((1, d), lambda i: (0, 0))
    return pl.pallas_call(
        body, name=name, grid=(t // tr,),
        in_specs=[row, row, vec, row], out_specs=[row, row, vec],
        out_shape=[_sds((t, d), F32), _sds((t, d), BF16), _sds((1, d), F32)],
        compiler_params=_params(("arbitrary",)),
    )(h, dy, g, res)


def _qk_prep(proj, gq, gk, aw, name):
    t = proj.shape[0]
    heads = aw // HEAD_DIM
    tr = _tile(t, (384, 256, 128))

    def body(q_ref, k_ref, v_ref, gq_ref, gk_ref, qo_ref, ko_ref, vo_ref):
        for h in range(heads):
            sl = slice(h * HEAD_DIM, (h + 1) * HEAD_DIM)
            xq, xk = q_ref[:, sl].astype(F32), k_ref[:, sl].astype(F32)
            qo_ref[:, sl] = ((xq * _rstd(xq)) * gq_ref[...]).astype(BF16)
            ko_ref[:, sl] = ((xk * _rstd(xk)) * gk_ref[...]).astype(BF16)
        vo_ref[...] = v_ref[...].astype(BF16)

    vec = pl.BlockSpec((1, HEAD_DIM), lambda i: (0, 0))
    out = pl.BlockSpec((tr, aw), lambda i: (i, 0))
    return pl.pallas_call(
        body, name=name, grid=(t // tr,),
        in_specs=[pl.BlockSpec((tr, aw), lambda i: (i, 0)), pl.BlockSpec((tr, aw), lambda i: (i, 1)),
                  pl.BlockSpec((tr, aw), lambda i: (i, 2)), vec, vec],
        out_specs=[out, out, out], out_shape=[_sds((t, aw), BF16)] * 3,
        compiler_params=_params(("parallel",)),
    )(proj, proj, proj, gq, gk)


def _qk_bwd(dqn, dkn, proj, gq, gk, aw, name):
    t = proj.shape[0]
    heads = aw // HEAD_DIM
    tr = _tile(t, (384, 256, 128))

    def body(dq_ref, dk_ref, q_ref, k_ref, gq_ref, gk_ref, dqo_ref, dko_ref, dgq_ref, dgk_ref):
        @pl.when(pl.program_id(0) == 0)
        def _():
            dgq_ref[...] = jnp.zeros_like(dgq_ref)
            dgk_ref[...] = jnp.zeros_like(dgk_ref)

        for h in range(heads):
            sl = slice(h * HEAD_DIM, (h + 1) * HEAD_DIM)
            dx, dg_rows = _norm_bwd(q_ref[:, sl].astype(F32), dq_ref[:, sl], gq_ref[...])
            dqo_ref[:, sl] = dx.astype(BF16)
            dgq_ref[...] += jnp.sum(dg_rows, axis=0, keepdims=True)
            dx, dg_rows = _norm_bwd(k_ref[:, sl].astype(F32), dk_ref[:, sl], gk_ref[...])
            dko_ref[:, sl] = dx.astype(BF16)
            dgk_ref[...] += jnp.sum(dg_rows, axis=0, keepdims=True)

    vec = pl.BlockSpec((1, HEAD_DIM), lambda i: (0, 0))
    row = pl.BlockSpec((tr, aw), lambda i: (i, 0))
    return pl.pallas_call(
        body, name=name, grid=(t // tr,),
        in_specs=[row, row, row, pl.BlockSpec((tr, aw), lambda i: (i, 1)), vec, vec],
        out_specs=[row, row, vec, vec],
        out_shape=[_sds((t, aw), BF16), _sds((t, aw), BF16), _sds((1, HEAD_DIM), F32), _sds((1, HEAD_DIM), F32)],
        compiler_params=_params(("arbitrary",)),
    )(dqn, dkn, proj, proj, gq, gk)


def _triangle(lower):
    r = lax.broadcasted_iota(jnp.int32, (LANES, LANES), 0)
    c = lax.broadcasted_iota(jnp.int32, (LANES, LANES), 1)
    return ((c <= r) if lower else (c >= r)).astype(F32)


def _forget_fwd(fg, b, name):
    t = fg.shape[0]

    def body(fg_ref, b_ref, cum_ref, carry):
        @pl.when(pl.program_id(0) == 0)
        def _():
            carry[...] = jnp.zeros_like(carry)

        z = fg_ref[...] + b_ref[...]
        log_f = jnp.minimum(z, 0.0) - jnp.log1p(jnp.exp(-jnp.abs(z)))
        cs = jnp.dot(_triangle(True), log_f, precision=lax.Precision.HIGHEST,
                     preferred_element_type=F32) + carry[0:1, :]
        cum_ref[...] = cs
        carry[...] = jnp.broadcast_to(cs[LANES - 1:LANES, :], carry.shape)

    row = pl.BlockSpec((LANES, LANES), lambda i: (i, 0))
    return pl.pallas_call(
        body, name=name, grid=(t // LANES,),
        in_specs=[row, pl.BlockSpec((1, LANES), lambda i: (0, 0))], out_specs=row,
        out_shape=_sds((t, LANES), F32), scratch_shapes=[pltpu.VMEM((SUBLANES, LANES), F32)],
        compiler_params=_params(("arbitrary",)),
    )(fg, b)


def _forget_bwd(dcum, fg, b, name):
    t = fg.shape[0]
    nt = t // LANES

    def body(dc_ref, fg_ref, b_ref, dfg_ref, db_ref, carry):
        @pl.when(pl.program_id(0) == 0)
        def _():
            carry[...] = jnp.zeros_like(carry)
            db_ref[...] = jnp.zeros_like(db_ref)

        d_log_f = jnp.dot(_triangle(False), dc_ref[...], precision=lax.Precision.HIGHEST,
                          preferred_element_type=F32) + carry[0:1, :]
        carry[...] = jnp.broadcast_to(d_log_f[0:1, :], carry.shape)
        dz = d_log_f * jax.nn.sigmoid(-(fg_ref[...] + b_ref[...]))
        dfg_ref[...] = dz.astype(BF16)
        db_ref[...] += jnp.sum(dz, axis=0, keepdims=True)

    row = pl.BlockSpec((LANES, LANES), lambda i: (nt - 1 - i, 0))
    vec = pl.BlockSpec((1, LANES), lambda i: (0, 0))
    return pl.pallas_call(
        body, name=name, grid=(nt,),
        in_specs=[row, row, vec], out_specs=[row, vec],
        out_shape=[_sds((t, LANES), BF16), _sds((1, LANES), F32)],
        scratch_shapes=[pltpu.VMEM((SUBLANES, LANES), F32)],
        compiler_params=_params(("arbitrary",)),
    )(dcum, fg, b)


def _causal(qi, kj, tq):
    rows = qi * tq + lax.broadcasted_iota(jnp.int32, (tq, tq), 0)
    cols = kj * tq + lax.broadcasted_iota(jnp.int32, (tq, tq), 1)
    return cols <= rows


NT_DIMS = (((1,), (1,)), ((), ()))
TN_DIMS = (((0,), (0,)), ((), ()))


def _attn_tile():
    return (384, 256, 128)


def _attn_fwd(q, k, v, cum_row, name, job=None, mid_at=0.5):
    t, aw = q.shape
    heads = aw // HEAD_DIM
    tq = _tile(t, _attn_tile())
    nq = t // tq
    rq = tq
    scale = HEAD_DIM ** -0.5

    def body(q_ref, k_ref, v_ref, ck_ref, o_ref, of_ref, lse_ref):
        qi = pl.program_id(1)
        qv = q_ref[...]
        n_full = (qi * rq) // tq

        def tile(kj, carry, masked):
            m_prev, l_prev, acc = carry
            ks = pl.ds(pl.multiple_of(kj * tq, tq), tq)
            s = lax.dot_general(qv, k_ref[ks, :], NT_DIMS, preferred_element_type=F32) * scale - ck_ref[kj]
            if masked:
                rows = qi * rq + lax.broadcasted_iota(jnp.int32, (rq, tq), 0)
                cols = kj * tq + lax.broadcasted_iota(jnp.int32, (rq, tq), 1)
                s = jnp.where(cols <= rows, s, -jnp.inf)
            m_new = jnp.maximum(m_prev, jnp.max(s, axis=-1, keepdims=True))
            alpha = jnp.exp(m_prev - m_new)
            p = jnp.exp(s - m_new)
            return (m_new, alpha * l_prev + jnp.sum(p, axis=-1, keepdims=True),
                    alpha * acc + jnp.dot(p.astype(BF16), v_ref[ks, :], preferred_element_type=F32))

        init = (jnp.full((rq, 1), -jnp.inf, F32), jnp.zeros((rq, 1), F32), jnp.zeros((rq, HEAD_DIM), F32))
        carry = lax.fori_loop(0, n_full, lambda kj, c: tile(kj, c, False), init)
        m_fin, l_fin, acc = tile(n_full, carry, True)
        out = acc / l_fin
        o_ref[...] = out.astype(o_ref.dtype)
        of_ref[...] = out
        lse_ref[...] = m_fin + jnp.log(l_fin)

    q_spec = pl.BlockSpec((rq, HEAD_DIM), lambda h, i: (i, h))
    head = pl.BlockSpec((t, HEAD_DIM), lambda h, i: (0, h))
    return _call(
        body, name=name, grid=(heads, t // rq),
        in_specs=[q_spec, head, head, pl.BlockSpec((None, nq, 1, tq), lambda h, i: (h, 0, 0, 0))],
        out_specs=[q_spec, q_spec, pl.BlockSpec((None, rq, 1), lambda h, i: (h, i, 0))],
        out_shape=[_sds((t, aw), BF16), _sds((t, aw), F32), _sds((heads, t, 1), F32)],
        scratch_shapes=[], semantics=("parallel", "arbitrary"),
        operands=(q, k, v, cum_row), job=job, mid_at=mid_at)


def _attn_stats(do, o, name):
    t, aw = o.shape
    heads = aw // HEAD_DIM
    tr = _tile(t, (384, 256, 128))

    def body(do_ref, o_ref, delta_ref):
        for h in range(heads):
            sl = slice(h * HEAD_DIM, (h + 1) * HEAD_DIM)
            do_seen = do_ref[:, sl].astype(BF16).astype(F32)
            delta_ref[h] = jnp.sum(do_seen * o_ref[:, sl], axis=-1, keepdims=True)

    row = pl.BlockSpec((tr, aw), lambda i: (i, 0))
    return pl.pallas_call(
        body, name=name, grid=(t // tr,),
        in_specs=[row, row], out_specs=pl.BlockSpec((heads, tr, 1), lambda i: (0, i, 0)),
        out_shape=_sds((heads, t, 1), F32), compiler_params=_params(("parallel",)),
    )(do, o)


def _attn_bwd(q, k, v, do, lse, delta, cum_row, name, job=None):
    t, aw = q.shape
    heads = aw // HEAD_DIM
    tq = _tile(t, _attn_tile())
    nq = t // tq
    scale = HEAD_DIM ** -0.5

    def body(q_ref, k_ref, v_ref, do_ref, lse_ref, delta_ref, ck_ref, dq_ref, dk_ref, dv_ref, dck_ref,
             drow_ref):
        kj = pl.program_id(1)

        @pl.when(kj == 0)
        def _():
            dq_ref[...] = jnp.zeros_like(dq_ref)
            drow_ref[...] = jnp.zeros_like(drow_ref)

        kv, vv, ck = k_ref[...], v_ref[...], ck_ref[...]

        def tile(qi, carry, masked):
            dk_acc, dv_acc, dck_acc = carry
            rows = pl.ds(pl.multiple_of(qi * tq, tq), tq)
            qv, dov = q_ref[rows, :], do_ref[rows, :].astype(BF16)
            s = lax.dot_general(qv, kv, NT_DIMS, preferred_element_type=F32) * scale - ck - lse_ref[rows, :]
            p = jnp.exp(s)
            if masked:
                p = jnp.where(_causal(0, 0, tq), p, 0.0)
            dp = lax.dot_general(dov, vv, NT_DIMS, preferred_element_type=F32)
            ds = p * (dp - delta_ref[rows, :])
            dsb = ds.astype(BF16)
            dq_ref[rows, :] += jnp.dot(dsb, kv, preferred_element_type=F32) * scale
            drow_ref[rows, :] += jnp.sum(ds, axis=1, keepdims=True)
            return (dk_acc + lax.dot_general(dsb, qv, TN_DIMS, preferred_element_type=F32),
                    dv_acc + lax.dot_general(p.astype(BF16), dov, TN_DIMS, preferred_element_type=F32),
                    dck_acc + jnp.sum(ds, axis=0, keepdims=True))

        init = (jnp.zeros((tq, HEAD_DIM), F32), jnp.zeros((tq, HEAD_DIM), F32), jnp.zeros((1, tq), F32))
        carry = tile(kj, init, True)
        dk_acc, dv_acc, dck_acc = lax.fori_loop(kj + 1, nq, lambda qi, c: tile(qi, c, False), carry)
        dk_ref[...] = dk_acc * scale
        dv_ref[...] = dv_acc.astype(dv_ref.dtype)
        dck_ref[...] = -dck_acc

    head = pl.BlockSpec((t, HEAD_DIM), lambda h, j: (0, h))
    k_spec = pl.BlockSpec((tq, HEAD_DIM), lambda h, j: (j, h))
    col = pl.BlockSpec((None, t, 1), lambda h, j: (h, 0, 0))
    row = pl.BlockSpec((None, 1, tq), lambda h, j: (h, 0, j))
    return _call(
        body, name=name, grid=(heads, nq),
        in_specs=[head, k_spec, k_spec, head, col, col, row],
        out_specs=[head, k_spec, k_spec, row, col],
        out_shape=[_sds((t, aw), F32), _sds((t, aw), F32), _sds((t, aw), BF16), _sds((heads, 1, t), F32),
                   _sds((heads, t, 1), F32)],
        scratch_shapes=[], semantics=("parallel", "arbitrary"),
        operands=(q, k, v, do, lse, delta, cum_row), job=job)


def _shift_down(u, by):
    rows = lax.broadcasted_iota(jnp.int32, u.shape, 0)
    return jnp.where(rows >= by, pltpu.roll(u, by, 0), 0.0)


def _shift_up(u, by):
    t = u.shape[0]
    rows = lax.broadcasted_iota(jnp.int32, u.shape, 0)
    return jnp.where(rows < t - by, pltpu.roll(u, t - by, 0), 0.0)


def _conv_specs(t, off_b, cw_width):
    nb = cw_width // LANES
    base = off_b // LANES
    return [pl.BlockSpec((t, LANES), lambda j, s=s: (0, base + s * nb + j)) for s in range(3)]


def _conv_fwd(proj, cw, off_b, name):
    t = proj.shape[0]
    width = cw.shape[1]

    def body(cb_ref, cc_ref, cx_ref, w_ref, o_ref):
        u = cc_ref[...].astype(F32) * cx_ref[...]
        y = w_ref[0:1, :] * _shift_down(u, 2) + w_ref[1:2, :] * _shift_down(u, 1) + w_ref[2:3, :] * u
        o_ref[...] = (cb_ref[...] * y).astype(BF16)

    return pl.pallas_call(
        body, name=name, grid=(width // LANES,),
        in_specs=_conv_specs(t, off_b, width) + [pl.BlockSpec((SUBLANES, LANES), lambda j: (0, j))],
        out_specs=pl.BlockSpec((t, LANES), lambda j: (0, j)),
        out_shape=_sds((t, width), BF16), compiler_params=_params(("parallel",)),
    )(proj, proj, proj, cw)


def _conv_bwd(dcp, proj, cw, off_b, name):
    t = proj.shape[0]
    width = cw.shape[1]

    def body(d_ref, cb_ref, cc_ref, cx_ref, w_ref, dcb_ref, dcc_ref, dcx_ref, dw_ref):
        cc, cx = cc_ref[...].astype(F32), cx_ref[...].astype(F32)
        u = cc * cx
        u1, u2 = _shift_down(u, 1), _shift_down(u, 2)
        w0, w1, w2 = w_ref[0:1, :], w_ref[1:2, :], w_ref[2:3, :]
        d = d_ref[...]
        dcb_ref[...] = (d * (w0 * u2 + w1 * u1 + w2 * u)).astype(BF16)
        dy = d * cb_ref[...]
        du = w2 * dy + w1 * _shift_up(dy, 1) + w0 * _shift_up(dy, 2)
        dcc_ref[...] = (du * cx).astype(BF16)
        dcx_ref[...] = (du * cc).astype(BF16)
        dw = [jnp.sum(dy * s, axis=0, keepdims=True) for s in (u2, u1, u)]
        dw_ref[...] = jnp.concatenate(dw + [jnp.zeros((SUBLANES - 3, LANES), F32)], axis=0)

    col = pl.BlockSpec((t, LANES), lambda j: (0, j))
    wspec = pl.BlockSpec((SUBLANES, LANES), lambda j: (0, j))
    return pl.pallas_call(
        body, name=name, grid=(width // LANES,),
        in_specs=[col] + _conv_specs(t, off_b, width) + [wspec],
        out_specs=[col, col, col, wspec],
        out_shape=[_sds((t, width), BF16)] * 3 + [_sds((SUBLANES, width), F32)],
        compiler_params=_params(("parallel",)),
    )(dcp, proj, proj, proj, cw)


def _gate_specs(t, d, off_g, tr, tc, rows_first):
    nb = d // tc
    base = off_g // tc
    if rows_first:
        tile = lambda s: pl.BlockSpec((tr, tc), lambda i, j: (i, base + s * nb + j))
        vec = lambda s: pl.BlockSpec((1, tc), lambda i, j: (0, s * nb + j))
        plain = pl.BlockSpec((tr, tc), lambda i, j: (i, j))
    else:
        tile = lambda s: pl.BlockSpec((tr, tc), lambda j, i: (i, base + s * nb + j))
        vec = lambda s: pl.BlockSpec((1, tc), lambda j, i: (0, s * nb + j))
        plain = pl.BlockSpec((tr, tc), lambda j, i: (i, j))
    return tile, vec, plain


def _gate_fwd(a, c, proj, bg, off_g, name):
    t, d = a.shape
    tr, tc = _tile(t, (384, 256, 128)), _tile(d, (512, 256, 128))
    tile, vec, plain = _gate_specs(t, d, off_g, tr, tc, True)

    def body(a_ref, c_ref, g0_ref, g1_ref, b0_ref, b1_ref, o_ref):
        g0 = jax.nn.sigmoid(g0_ref[...] + b0_ref[...])
        g1 = jax.nn.sigmoid(g1_ref[...] + b1_ref[...])
        o_ref[...] = (g0 * a_ref[...] + g1 * c_ref[...]).astype(BF16)

    return pl.pallas_call(
        body, name=name, grid=(t // tr, d // tc),
        in_specs=[plain, plain, tile(0), tile(1), vec(0), vec(1)], out_specs=plain,
        out_shape=_sds((t, d), BF16), compiler_params=_params(("parallel", "parallel")),
    )(a, c, proj, proj, bg, bg)


def _gate_bwd(dm, a, c, proj, bg, off_g, name):
    t, d = a.shape
    tr, tc = _tile(t, (384, 256, 128)), _tile(d, (512, 256, 128))
    tile, vec, plain = _gate_specs(t, d, off_g, tr, tc, False)

    def body(dm_ref, a_ref, c_ref, g0_ref, g1_ref, b0_ref, b1_ref,
             da_ref, dc_ref, dg0_ref, dg1_ref, db0_ref, db1_ref):
        @pl.when(pl.program_id(1) == 0)
        def _():
            db0_ref[...] = jnp.zeros_like(db0_ref)
            db1_ref[...] = jnp.zeros_like(db1_ref)

        dm = dm_ref[...]
        g0 = jax.nn.sigmoid(g0_ref[...] + b0_ref[...])
        g1 = jax.nn.sigmoid(g1_ref[...] + b1_ref[...])
        da_ref[...] = (dm * g0).astype(BF16)
        dc_ref[...] = (dm * g1).astype(BF16)
        dz0 = dm * a_ref[...] * (g0 * (1.0 - g0))
        dz1 = dm * c_ref[...] * (g1 * (1.0 - g1))
        dg0_ref[...] = dz0.astype(BF16)
        dg1_ref[...] = dz1.astype(BF16)
        db0_ref[...] += jnp.sum(dz0, axis=0, keepdims=True)
        db1_ref[...] += jnp.sum(dz1, axis=0, keepdims=True)

    bvec = pl.BlockSpec((1, tc), lambda j, i: (0, j))
    return pl.pallas_call(
        body, name=name, grid=(d // tc, t // tr),
        in_specs=[plain, plain, plain, tile(0), tile(1), vec(0), vec(1)],
        out_specs=[plain] * 4 + [bvec, bvec],
        out_shape=[_sds((t, d), BF16)] * 4 + [_sds((1, d), F32)] * 2,
        compiler_params=_params(("parallel", "arbitrary")),
    )(dm, a, c, proj, proj, bg, bg)


def _sum_squares(x, name):
    t, d = x.shape
    tr = _tile(t, (384, 256, 128))

    def body(x_ref, o_ref):
        @pl.when(pl.program_id(0) == 0)
        def _():
            o_ref[...] = jnp.zeros_like(o_ref)

        v = x_ref[...]
        o_ref[...] += jnp.sum(jnp.sum(v * v, axis=0, keepdims=True), axis=1, keepdims=True)

    return pl.pallas_call(
        body, name=name, grid=(t // tr,),
        in_specs=[pl.BlockSpec((tr, d), lambda i: (i, 0))],
        out_specs=pl.BlockSpec((1, LANES), lambda i: (0, 0)),
        out_shape=_sds((1, LANES), F32), compiler_params=_params(("arbitrary",)),
    )(x)


def _row_tile(r, c):
    return r if r * c <= 128 * 1024 else _tile(r, (128, 64, 32, 16))


def _sum_parts(parts, name):
    n, r, c = parts.shape
    tr = _row_tile(r, c)

    def body(p_ref, o_ref):
        acc = p_ref[0].astype(F32)
        for i in range(1, n):
            acc = acc + p_ref[i].astype(F32)
        o_ref[...] = acc

    return pl.pallas_call(
        body, name=name, grid=(r // tr,),
        in_specs=[pl.BlockSpec((n, tr, c), lambda i: (0, i, 0))],
        out_specs=pl.BlockSpec((tr, c), lambda i: (i, 0)),
        out_shape=_sds((r, c), F32), compiler_params=_params(("parallel",)),
    )(parts)


def _adamw(chunks, w, m, v, name):
    n, _, c = chunks[0].shape
    heights = [ch.shape[1] for ch in chunks]
    r = sum(heights)
    tr = _row_tile(functools.reduce(math.gcd, heights), c)
    firsts = [sum(heights[:ci]) // tr for ci in range(len(chunks) + 1)]

    def body(*refs):
        p_refs = refs[:len(chunks)]
        w_ref, m_ref, v_ref, g_ref, d_ref, nm_ref, nv_ref = refs[len(chunks):]
        i = pl.program_id(0)

        def update(p_ref):
            g = p_ref[0].astype(F32)
            for s in range(1, n):
                g = g + p_ref[s].astype(F32)
            nm = ADAM_B1 * m_ref[...] + (1.0 - ADAM_B1) * g
            nv = ADAM_B2 * v_ref[...] + (1.0 - ADAM_B2) * (g * g)
            m_hat = nm / (1.0 - ADAM_B1 ** ADAM_STEP)
            v_hat = nv / (1.0 - ADAM_B2 ** ADAM_STEP)
            g_ref[...] = g
            d_ref[...] = -ADAM_LR * (m_hat / (jnp.sqrt(v_hat) + ADAM_EPS) + ADAM_WD * w_ref[...])
            nm_ref[...] = nm
            nv_ref[...] = nv

        if len(chunks) == 1:
            update(p_refs[0])
        else:
            for ci, p_ref in enumerate(p_refs):
                pl.when((i >= firsts[ci]) & (i < firsts[ci + 1]))(functools.partial(update, p_ref))

    row = pl.BlockSpec((tr, c), lambda i: (i, 0))
    part_specs = [pl.BlockSpec((n, tr, c),
                               lambda i, lo=firsts[ci], hi=firsts[ci + 1]: (0, jnp.clip(i - lo, 0, hi - lo - 1), 0))
                  for ci in range(len(chunks))]
    return pl.pallas_call(
        body, name=name, grid=(r // tr,),
        in_specs=part_specs + [row, row, row],
        out_specs=[row] * 4, out_shape=[_sds((r, c), F32)] * 4,
        compiler_params=_params(("parallel",)),
    )(*chunks, w, m, v)


def _adamw_cols(chunks, w, m, v, name):
    n, r, _ = chunks[0].shape
    widths = [ch.shape[2] for ch in chunks]
    tc = functools.reduce(math.gcd, widths, LANES)
    firsts = [sum(widths[:ci]) // tc for ci in range(len(chunks) + 1)]

    def body(*refs):
        p_refs = refs[:len(chunks)]
        w_ref, m_ref, v_ref, g_ref, d_ref, nm_ref, nv_ref = refs[len(chunks):]
        j = pl.program_id(0)

        def update(p_ref):
            g = p_ref[0].astype(F32)
            for s in range(1, n):
                g = g + p_ref[s].astype(F32)
            nm = ADAM_B1 * m_ref[...] + (1.0 - ADAM_B1) * g
            nv = ADAM_B2 * v_ref[...] + (1.0 - ADAM_B2) * (g * g)
            m_hat = nm / (1.0 - ADAM_B1 ** ADAM_STEP)
            v_hat = nv / (1.0 - ADAM_B2 ** ADAM_STEP)
            g_ref[...] = g
            d_ref[...] = -ADAM_LR * (m_hat / (jnp.sqrt(v_hat) + ADAM_EPS) + ADAM_WD * w_ref[...])
            nm_ref[...] = nm
            nv_ref[...] = nv

        for ci, p_ref in enumerate(p_refs):
            pl.when((j >= firsts[ci]) & (j < firsts[ci + 1]))(functools.partial(update, p_ref))

    col = pl.BlockSpec((r, tc), lambda j: (0, j))
    part_specs = [pl.BlockSpec((n, r, tc),
                               lambda j, lo=firsts[ci], hi=firsts[ci + 1]: (0, 0, jnp.clip(j - lo, 0, hi - lo - 1)))
                  for ci in range(len(chunks))]
    return pl.pallas_call(
        body, name=name, grid=(firsts[-1],),
        in_specs=part_specs + [col, col, col],
        out_specs=[col] * 4, out_shape=[_sds((r, firsts[-1] * tc), F32)] * 4,
        compiler_params=_params(("parallel",)),
    )(*chunks, w, m, v)


def _pad_lanes(a, width=LANES):
    return jnp.pad(a, ((0, 0), (0, width - a.shape[1])))


def _rows_of(a):
    flat = a.reshape(-1)
    n = -(-flat.shape[0] // LANES) * LANES
    return jnp.pad(flat, (0, n - flat.shape[0])).reshape(-1, LANES)


def _columns_to_slots(full, n_rows):
    return full.reshape(n_rows, N_DEV, -1).transpose(1, 0, 2)


def _slots_to_columns(slots):
    return slots.transpose(1, 0, 2).reshape(slots.shape[1], -1)


def kernel(x, meta_tokens, norm_mix, w_in, b_fgate, b_gate, q_norm, k_norm, conv_w, w_attn_out, w_conv_out, w_o, norm_mlp, w_up, w_down, loss_target, m_meta_tokens, m_norm_mix, m_w_in, m_b_fgate, m_b_gate, m_q_norm, m_k_norm, m_conv_w, m_w_attn_out, m_w_conv_out, m_w_o, m_norm_mlp, m_w_up, m_w_down, v_meta_tokens, v_norm_mix, v_w_in, v_b_fgate, v_b_gate, v_q_norm, v_k_norm, v_conv_w, v_w_attn_out, v_w_conv_out, v_w_o, v_norm_mlp, v_w_up, v_w_down):
    seq, d = x.shape[1], x.shape[2]
    heads = b_fgate.shape[1]
    aw = heads * HEAD_DIM
    cwid = conv_w.shape[2] * N_DEV
    dff = w_up.shape[2] * N_DEV
    n_valid = N_META + seq
    t = -(-n_valid // LANES) * LANES
    me = _flat(*_my_place())
    off_cb, off_gl = 3 * aw, 3 * aw + 3 * cwid

    conv_shard = jnp.pad(conv_w[0], ((0, SUBLANES - conv_w.shape[1]), (0, 0)))
    w_in_t, m_in_t, v_in_t = (jnp.swapaxes(p, 1, 2)[0] for p in (w_in, m_w_in, v_w_in))
    g_in, g_meta, g_cw = _run_job(_Gather([w_in_t.astype(BF16), meta_tokens, conv_shard]), "gather_first")
    n_in = N_DEV * g_in.shape[1]
    w_all_t = g_in.reshape(n_in, d)
    w_main_t = jnp.concatenate([w_all_t[:3 * aw], w_all_t[3 * aw + heads:]], axis=0)
    w_fg_t = jnp.pad(w_all_t[3 * aw:3 * aw + heads], ((0, LANES - heads), (0, 0)))
    meta_full, cw_full = _slots_to_columns(g_meta), _slots_to_columns(g_cw)

    pad_rows = t - n_valid
    h0 = jnp.concatenate([meta_full, x[0], jnp.zeros((pad_rows, d), F32)], axis=0)
    target = jnp.concatenate([jnp.zeros((N_META, d), F32), loss_target[0], jnp.zeros((pad_rows, d), F32)], axis=0)
    b_f = _pad_lanes(b_fgate)

    xn = _rmsnorm_fwd(h0, norm_mix, "norm_mix_fwd")
    proj, (g_ao, g_co, g_o) = _matmul(
        xn, w_main_t, name="in_proj", trans_b=True, mid_at=0.5, out_dtypes=(BF16,),
        job=_Gather([w_attn_out[0].astype(BF16), w_conv_out[0].astype(BF16), w_o[0].astype(BF16)]))
    w_ao, w_co, w_o_f = _slots_to_columns(g_ao), _slots_to_columns(g_co), g_o.reshape(d, d)
    fg = _matmul(xn, w_fg_t, name="in_proj_fgate", trans_b=True)
    qn, kn, vb = _qk_prep(proj, q_norm, k_norm, aw, "qk_norm_fwd")
    cum = _forget_fwd(fg, b_f, "forget_cumsum")
    cum_heads = cum[:, :heads].T
    cum_row = cum_heads[:, None, :]
    t_attn = _tile(t, _attn_tile())
    (o, o_fine, lse), (g_up, g_down) = _attn_fwd(
        qn, kn, vb, cum_heads.reshape(heads, t // t_attn, 1, t_attn), "attention_fwd", mid_at=0.55,
        job=_Gather([w_up[0].astype(BF16), w_down[0].astype(BF16)]))
    w_up_f, w_down_f = _slots_to_columns(g_up), g_down.reshape(dff, d)
    a = _matmul(o, w_ao, name="attn_out_proj", out_dtypes=(BF16,))
    cpre = _conv_fwd(proj, cw_full, off_cb, "short_conv_fwd")
    c = _matmul(cpre, w_co, name="conv_out_proj", out_dtypes=(BF16,))
    merged = _gate_fwd(a, c, proj, b_gate, off_gl, "gate_merge_fwd")
    h1 = _matmul(merged, w_o_f, name="out_proj", extras=(h0,), epilogue=lambda acc, i, j, r: (r + acc,))
    hn = _rmsnorm_fwd(h1, norm_mlp, "norm_mlp_fwd")
    z, u = _matmul(hn, w_up_f, name="mlp_up", out_dtypes=(F32, BF16),
                   epilogue=lambda acc, i, j: (acc, jnp.square(jnp.maximum(acc, 0.0))))

    tm_down = _tile(t, (1408, 1024, 512, 256, 128))

    def loss_grad(acc, i, j, h1_tile, tgt_tile):
        rows = i * tm_down + lax.broadcasted_iota(jnp.int32, acc.shape, 0)
        valid = (rows >= N_META) & (rows < n_valid)
        dy = jnp.where(valid, ((h1_tile + acc) - tgt_tile) / d, 0.0)
        return dy, dy

    dh2, dh2b = _matmul(u, w_down_f, name="mlp_down_loss", extras=(h1, target), epilogue=loss_grad,
                        out_dtypes=(F32, BF16), tm=tm_down)
    loss_part = _sum_squares(dh2, "loss_sum") * (0.5 * d)

    wide = lambda n_cols: _tile(n_cols, (1024, 512, 256, 128))
    dw_down = _matmul(u, dh2b, name="mlp_down_wgrad", trans_a=True, tn=wide(d), out_dtypes=(BF16,))
    s_down = dw_down.reshape(N_DEV, dff // N_DEV, d)
    e_down, e_up = dff // N_DEV // 8, d // 8
    dz, l_down0 = _matmul(dh2b, w_down_f, name="mlp_down_bwd", trans_b=True, extras=(z,), out_dtypes=(BF16,),
                          epilogue=lambda acc, i, j, zt: (acc * (2.0 * jnp.maximum(zt, 0.0)),),
                          job=_Scatter([(s_down, 0, 3 * e_down)]))
    s_up, l_down1 = _matmul(hn, dz, name="mlp_up_wgrad", trans_a=True, slots=True, out_dtypes=(BF16,),
                            tn=wide(dff // N_DEV), job=_Scatter([(s_down, 3 * e_down, 3 * e_down)]))
    dhn, (l_down2, l_up0) = _matmul(dz, w_up_f, name="mlp_up_bwd", trans_b=True, tn=wide(d),
                                    job=_Scatter([(s_down, 6 * e_down, 2 * e_down), (s_up, 0, e_up)]))
    dh1, dh1b, dg_mlp = _rmsnorm_bwd(h1, dhn, norm_mlp, dh2, "norm_mlp_bwd")
    dmerged, (l_up1,) = _matmul(dh1b, w_o_f, name="out_proj_bwd", trans_b=True,
                                job=_Scatter([(s_up, e_up, e_up)]))
    dw_o, (l_up2,) = _matmul(merged, dh1b, name="out_proj_wgrad", trans_a=True, tn=wide(d), out_dtypes=(BF16,),
                             job=_Scatter([(s_up, 2 * e_up, e_up)]))
    da, dc, dgl0, dgl1, dbg0, dbg1 = _gate_bwd(dmerged, a, c, proj, b_gate, off_gl, "gate_merge_bwd")
    do = _matmul(da, w_ao, name="attn_out_bwd", trans_b=True)
    s_ao = _matmul(o, da, name="attn_out_wgrad", trans_a=True, slots=True, out_dtypes=(BF16,), tk=t)
    dcp = _matmul(dc, w_co, name="conv_out_bwd", trans_b=True)
    s_co = _matmul(cpre, dc, name="conv_out_wgrad", trans_a=True, slots=True, out_dtypes=(BF16,), tk=t)
    dcb, dcc, dcx, dcw = _conv_bwd(dcp, proj, cw_full, off_cb, "short_conv_bwd")
    delta = _attn_stats(do, o_fine, "attention_stats")
    (dqn, dkn, dv, dck, drow), (l_up3, l_o, l_ao, l_co) = _attn_bwd(
        qn, kn, vb, do, lse, delta, cum_row, "attention_bwd",
        job=_Scatter([(s_up, 3 * e_up, 5 * e_up), dw_o.reshape(N_DEV, d // N_DEV, d), s_ao, s_co]))
    dq_raw, dk_raw, dg_q, dg_k = _qk_bwd(dqn, dkn, proj, q_norm, k_norm, aw, "qk_norm_bwd")
    dcum = _pad_lanes((dck.reshape(heads, t) + drow.reshape(heads, t)).T)
    dfg, db_f = _forget_bwd(dcum, fg, b_f, "forget_bwd")
    dproj = jnp.concatenate([dq_raw, dk_raw, dv, dcb, dcc, dcx, dgl0, dgl1], axis=1)
    dwt_fg = _matmul(dfg, xn, name="in_proj_fgate_wgrad", trans_a=True, out_dtypes=(BF16,))
    range_ends = [3 * d // 16, d // 2, d]

    def in_slots(dwt, first):
        width = dwt.shape[1]
        parts = ((0, 3 * aw, dwt, 0), (3 * aw, 3 * aw + heads, dwt_fg[:heads, first:first + width], 3 * aw),
                 (3 * aw + heads, n_in, dwt, heads))
        slots = []
        for j in range(N_DEV):
            lo, hi = j * n_in // N_DEV, (j + 1) * n_in // N_DEV
            rows = [src[max(lo, a) - shift:min(hi, b) - shift] for a, b, src, shift in parts
                    if max(lo, a) < min(hi, b)]
            slots.append(rows[0] if len(rows) == 1 else jnp.concatenate(rows, axis=0))
        return jnp.stack(slots)

    def in_wgrad(idx, job):
        lo, hi = ([0] + range_ends)[idx], range_ends[idx]
        return _matmul(dproj, xn[:, lo:hi], name="in_proj_wgrad_%d" % idx, trans_a=True, tn=hi - lo,
                       out_dtypes=(BF16,), job=job)

    dwt0 = in_wgrad(0, None)
    dwt1, l_in0 = in_wgrad(1, _Scatter([in_slots(dwt0, 0)]))
    dwt2, l_in1 = in_wgrad(2, _Scatter([in_slots(dwt1, range_ends[0])]))
    dxn_fg = _matmul(dfg, w_fg_t, name="in_proj_fgate_bwd")
    dxn, l_in2 = _matmul(dproj, w_main_t, name="in_proj_bwd", extras=(dxn_fg,),
                         epilogue=lambda acc, i, j, r: (r + acc,), job=_Scatter([in_slots(dwt2, range_ends[1])]))
    dh0, _, dg_mix = _rmsnorm_bwd(h0, dxn, norm_mix, dh1, "norm_mix_bwd")

    small = [dg_mix, dbg0, dbg1, dg_mlp, dg_q, dg_k, db_f, loss_part, dcw, dh0[:N_META]]
    small_rows = [_rows_of(s) for s in small]
    pack = jnp.concatenate(small_rows, axis=0)
    pack = jnp.pad(pack, ((0, -pack.shape[0] % SUBLANES), (0, 0)))
    (pack_all,) = _run_job(_Scatter([], [pack]), "gather_small")

    landed = {"w_attn_out": [l_ao], "w_conv_out": [l_co], "w_o": [l_o],
              "w_up": [l_up0, l_up1, l_up2, l_up3], "w_down": l_down0 + l_down1 + [l_down2]}
    shards = {"w_attn_out": (w_attn_out, m_w_attn_out, v_w_attn_out),
              "w_conv_out": (w_conv_out, m_w_conv_out, v_w_conv_out), "w_o": (w_o, m_w_o, v_w_o),
              "w_up": (w_up, m_w_up, v_w_up), "w_down": (w_down, m_w_down, v_w_down)}
    out = {}
    for nm, chunks in landed.items():
        w_, m_, v_ = shards[nm]
        res = _adamw(list(chunks), w_[0], m_[0], v_[0], "adamw_" + nm)
        out[nm] = [r[None] for r in res]
    res = _adamw_cols(l_in0 + l_in1 + l_in2, w_in_t, m_in_t, v_in_t, "adamw_w_in")
    out["w_in"] = [r.T[None] for r in res]

    total = _sum_parts(pack_all, "sum_small")
    pieces, at = [], 0
    for s, rows in zip(small, small_rows):
        n_el = 1
        for dim in s.shape:
            n_el *= dim
        pieces.append(total[at:at + rows.shape[0]].reshape(-1)[:n_el].reshape(s.shape))
        at += rows.shape[0]
    g_mix, g_bg0, g_bg1, g_mlp, g_q, g_k, g_bf, loss_row, g_cw_full, g_meta_full = pieces
    loss = loss_row[0, 0]
    cshard = conv_w.shape[2]
    g_small = {
        "norm_mix": g_mix, "b_gate": jnp.concatenate([g_bg0, g_bg1], axis=1), "norm_mlp": g_mlp,
        "q_norm": g_q, "k_norm": g_k, "b_fgate": g_bf[:, :heads],
        "conv_w": lax.dynamic_slice_in_dim(g_cw_full[:conv_w.shape[1]], me * cshard, cshard, axis=1)[None],
        "meta_tokens": lax.dynamic_slice_in_dim(g_meta_full, me * (d // N_DEV), d // N_DEV, axis=1),
    }
    small_w = {"norm_mix": (norm_mix, m_norm_mix, v_norm_mix), "b_gate": (b_gate, m_b_gate, v_b_gate),
               "norm_mlp": (norm_mlp, m_norm_mlp, v_norm_mlp), "q_norm": (q_norm, m_q_norm, v_q_norm),
               "k_norm": (k_norm, m_k_norm, v_k_norm), "b_fgate": (b_fgate, m_b_fgate, v_b_fgate),
               "conv_w": (conv_w, m_conv_w, v_conv_w), "meta_tokens": (meta_tokens, m_meta_tokens, v_meta_tokens)}
    order = list(small_w)
    packed = []
    for idx in range(4):
        cols = [g_small[nm] if idx == 0 else small_w[nm][idx - 1] for nm in order]
        rows = jnp.concatenate([_rows_of(c_) for c_ in cols], axis=0)
        packed.append(jnp.pad(rows, ((0, -rows.shape[0] % SUBLANES), (0, 0))))
    res = _adamw([packed[0][None]], packed[1], packed[2], packed[3], "adamw_small")
    at = 0
    for nm in order:
        shape = small_w[nm][0].shape
        n_el = 1
        for dim in shape:
            n_el *= dim
        n_rows = -(-n_el // LANES)
        out[nm] = [r[at:at + n_rows].reshape(-1)[:n_el].reshape(shape) for r in res]
        at += n_rows

    weights = ["meta_tokens", "norm_mix", "w_in", "b_fgate", "b_gate", "q_norm", "k_norm", "conv_w",
               "w_attn_out", "w_conv_out", "w_o", "norm_mlp", "w_up", "w_down"]
    grad_x = dh0[N_META:n_valid][None]
    return (loss, grad_x, *[out[nm][0] for nm in weights], *[out[nm][1] for nm in weights],
            *[out[nm][2] for nm in weights], *[out[nm][3] for nm in weights])
```
